```python
import math
import jax, jax.numpy as jnp
from jax import lax
import numpy as np

D_MODEL = 1024
BATCH = 8
SEQ = 2048
DEPTH = 2
DEC_BATCH = 128
DEC_SEQ = 4
PAST_LEN = 16384
PAGE_SIZE = 128

N_BRANCH = 4
BRANCH_W = D_MODEL // N_BRANCH
GDN_H = 4
GDN_DK = 64
GDN_DV = BRANCH_W // GDN_H
HG_H = 4
HG_DK = 64
HG_DV = BRANCH_W // HG_H
SSD_H = 8
SSD_P = BRANCH_W // SSD_H
SSD_N = 64
SSD_G = 2
ML_H = 4
ML_DK = 64
ML_DV = BRANCH_W // ML_H
CONV_K = 4
CHUNK = 64
D_FF = -(-8 * D_MODEL // (3 * 256)) * 256
EPS = 1e-6
NEG = -1e30
GDN_CONV_W = 2 * GDN_H * GDN_DK + GDN_H * GDN_DV
SSD_CONV_W = SSD_H * SSD_P + 2 * SSD_G * SSD_N
IN_SPLITS = (GDN_CONV_W, GDN_H * GDN_DV, GDN_H, GDN_H,
             HG_H * HG_DK, HG_H * HG_DK, HG_H * HG_DV, HG_H * HG_DV,
             SSD_H * SSD_P, SSD_CONV_W, SSD_H,
             ML_H * ML_DK, ML_H * ML_DK, ML_H * ML_DV, ML_H * ML_DV, ML_H, ML_H,
             N_BRANCH * D_MODEL)

kernel_name = 'hybrid_gdn_hgrn2_ssd_mlstm_step'


def _rms(x, w):
    xf = x.astype(jnp.float32)
    return xf * lax.rsqrt(jnp.mean(xf * xf, axis=-1, keepdims=True) + EPS) * w.astype(jnp.float32)


def _l2n(x):
    return x * lax.rsqrt(jnp.sum(x * x, axis=-1, keepdims=True) + EPS)


def _masked_exp(d, mask):
    return jnp.where(mask, jnp.exp(jnp.where(mask, d, 0.0)), 0.0)


def _chunk_len(T):
    return CHUNK if T % CHUNK == 0 else T


def _chunk_heads(a, L):
    B, T, H = a.shape[:3]
    a = a.reshape((B, T // L, L, H) + a.shape[3:])
    return jnp.moveaxis(jnp.moveaxis(a, 1, 0), 3, 2)


def _unchunk_heads(o):
    o = jnp.moveaxis(jnp.moveaxis(o, 2, 3), 0, 1)
    return o.reshape((o.shape[0], -1) + o.shape[3:])


def _causal_conv(u, buf, w, b):
    T = u.shape[1]
    up = jnp.concatenate([buf.astype(u.dtype), u], axis=1)
    y = b
    for i in range(CONV_K):
        y = y + up[:, i:i + T] * w[i]
    return jax.nn.silu(y), up[:, T:]


def _gated_delta(q, k, v, beta, logd, S0):
    L = _chunk_len(q.shape[1])
    incl = jnp.tril(jnp.ones((L, L), bool))
    strict = jnp.tril(jnp.ones((L, L), bool), -1)
    eye = jnp.eye(L, dtype=jnp.float32)
    dv = v.shape[-1]
    xs = tuple(_chunk_heads(a, L) for a in (q, k, v, beta, logd))

    def step(S, inp):
        qc, kc, vc, bc, lc = inp
        g = jnp.cumsum(lc, axis=-1)
        dec = _masked_exp(g[..., :, None] - g[..., None, :], incl)
        kk = jnp.einsum('bhld,bhmd->bhlm', kc, kc)
        lhs = eye + jnp.where(strict, kk * dec * bc[..., :, None], 0.0)
        rhs = jnp.concatenate([vc * bc[..., None], kc * (bc * jnp.exp(g))[..., None]], axis=-1)
        sol = lax.linalg.triangular_solve(lhs, rhs, left_side=True, lower=True, unit_diagonal=True)
        u = sol[..., :dv] - jnp.einsum('bhlk,bhkv->bhlv', sol[..., dv:], S)
        qk = jnp.einsum('bhld,bhmd->bhlm', qc, kc) * dec
        o = (jnp.einsum('bhlk,bhkv->bhlv', qc * jnp.exp(g)[..., None], S)
             + jnp.einsum('bhlm,bhmv->bhlv', qk, u))
        gl = g[..., -1:]
        S = S * jnp.exp(gl)[..., None] + jnp.einsum('bhlk,bhlv->bhkv', kc * jnp.exp(gl - g)[..., None], u)
        return S, o

    S, o = lax.scan(step, S0, xs)
    return _unchunk_heads(o), S


def _hgrn2(q, k, v, logf, S0):
    L = _chunk_len(q.shape[1])
    incl = jnp.tril(jnp.ones((L, L), bool))[..., None]
    xs = tuple(_chunk_heads(a, L) for a in (q, k, v, logf))

    def step(S, inp):
        qc, kc, vc, fc = inp
        g = jnp.cumsum(fc, axis=2)
        dec = _masked_exp(g[:, :, :, None, :] - g[:, :, None, :, :], incl)
        a = jnp.einsum('bhlc,bhmc,bhlmc->bhlm', qc, kc, dec)
        o = (jnp.einsum('bhlc,bhcv->bhlv', qc * jnp.exp(g), S)
             + jnp.einsum('bhlm,bhmv->bhlv', a, vc))
        gl = g[:, :, -1:, :]
        S = S * jnp.exp(gl[:, :, 0, :, None]) + jnp.einsum('bhlc,bhlv->bhcv', kc * jnp.exp(gl - g), vc)
        return S, o

    S, o = lax.scan(step, S0, xs)
    return _unchunk_heads(o), S


def _ssd_scan(c, b, v, da, S0):
    L = _chunk_len(c.shape[1])
    incl = jnp.tril(jnp.ones((L, L), bool))
    xs = tuple(_chunk_heads(a, L) for a in (c, b, v, da))

    def step(S, inp):
        cc, bc, vc, dc = inp
        g = jnp.cumsum(dc, axis=-1)
        dec = _masked_exp(g[..., :, None] - g[..., None, :], incl)
        a = jnp.einsum('bhln,bhmn->bhlm', cc, bc) * dec
        o = (jnp.einsum('bhln,bhnp->bhlp', cc * jnp.exp(g)[..., None], S)
             + jnp.einsum('bhlm,bhmp->bhlp', a, vc))
        gl = g[..., -1:]
        S = S * jnp.exp(gl)[..., None] + jnp.einsum('bhln,bhlp->bhnp', bc * jnp.exp(gl - g)[..., None], vc)
        return S, o

    S, o = lax.scan(step, S0, xs)
    return _unchunk_heads(o), S


def _mlstm(q, k, v, ig, lf, C0, n0, m0):
    L = _chunk_len(q.shape[1])
    incl = jnp.tril(jnp.ones((L, L), bool))
    xs = tuple(_chunk_heads(a, L) for a in (q, k, v, ig, lf))

    def step(carry, inp):
        C, n, m = carry
        qc, kc, vc, ic, fc = inp
        b = jnp.cumsum(fc, axis=-1)
        logw = jnp.where(incl, b[..., :, None] - b[..., None, :] + ic[..., None, :], NEG)
        log0 = b + m[..., None]
        mt = jnp.maximum(log0, jnp.max(logw, axis=-1))
        w = jnp.where(incl, jnp.exp(logw - mt[..., None]), 0.0)
        w0 = jnp.exp(log0 - mt)
        qk = jnp.einsum('bhld,bhmd->bhlm', qc, kc) * w
        num = w0[..., None] * jnp.einsum('bhld,bhdv->bhlv', qc, C) + jnp.einsum('bhlm,bhmv->bhlv', qk, vc)
        den = w0 * jnp.einsum('bhld,bhd->bhl', qc, n) + jnp.sum(qk, axis=-1)
        h = num / jnp.maximum(jnp.abs(den), jnp.exp(-mt))[..., None]
        mL = mt[..., -1]
        wL0 = jnp.exp(b[..., -1] + m - mL)
        wL = jnp.exp(b[..., -1:] - b + ic - mL[..., None])
        C = wL0[..., None, None] * C + jnp.einsum('bhl,bhld,bhlv->bhdv', wL, kc, vc)
        n = wL0[..., None] * n + jnp.einsum('bhl,bhld->bhd', wL, kc)
        return (C, n, mL), h

    (C, n, m), h = lax.scan(step, (C0, n0, m0), xs)
    return _unchunk_heads(h), C, n, m


def _mixer(h, st, p):
    Bsz, T, _ = h.shape
    proj = (h @ p['w_in']).astype(jnp.float32)
    (g_qkv, g_z, g_b, g_a, h_q, h_f, h_i, h_g, s_z, s_xbc, s_dt,
     m_q, m_k, m_v, m_o, m_i, m_f, gate) = jnp.split(proj, np.cumsum(IN_SPLITS)[:-1].tolist(), axis=-1)
    S_gdn, buf_gdn, S_hg, S_ssd, buf_ssd, C_ml, n_ml, m_ml = st

    qkv, buf_gdn = _causal_conv(g_qkv, buf_gdn, p['gdn_conv_w'], p['gdn_conv_b'])
    q, k, v = jnp.split(qkv, [GDN_H * GDN_DK, 2 * GDN_H * GDN_DK], axis=-1)
    q = _l2n(q.reshape(Bsz, T, GDN_H, GDN_DK)) * GDN_DK ** -0.5
    k = _l2n(k.reshape(Bsz, T, GDN_H, GDN_DK))
    v = v.reshape(Bsz, T, GDN_H, GDN_DV)
    beta = jax.nn.sigmoid(g_b)
    logd = -jnp.exp(p['gdn_a_log']) * jax.nn.softplus(g_a + p['gdn_dt_bias'])
    o, S_gdn = _gated_delta(q, k, v, beta, logd, S_gdn)
    out_a = (_rms(o, p['gdn_norm_w']) * jax.nn.silu(g_z.reshape(Bsz, T, GDN_H, GDN_DV))).reshape(Bsz, T, -1)

    lb = p['hg_lb'].reshape(HG_H, HG_DK)
    fz = h_f.reshape(Bsz, T, HG_H, HG_DK)
    lb_pos = lb > 0
    log_lb = jnp.log(jnp.where(lb_pos, lb, 1.0))
    ls = jax.nn.log_sigmoid(fz)
    logf = jnp.where(lb_pos, jnp.logaddexp(log_lb, jnp.log1p(-lb) + ls), ls)
    kg = (1.0 - lb) * jax.nn.sigmoid(-fz)
    o, S_hg = _hgrn2(jax.nn.silu(h_q.reshape(Bsz, T, HG_H, HG_DK)), kg,
                     h_i.reshape(Bsz, T, HG_H, HG_DV), logf, S_hg)
    out_b = (_rms(o, p['hg_norm_w']) * jax.nn.silu(h_g.reshape(Bsz, T, HG_H, HG_DV))).reshape(Bsz, T, -1)

    xbc, buf_ssd = _causal_conv(s_xbc, buf_ssd, p['ssd_conv_w'], p['ssd_conv_b'])
    xs_, Bs, Cs = jnp.split(xbc, [SSD_H * SSD_P, SSD_H * SSD_P + SSD_G * SSD_N], axis=-1)
    dt = jax.nn.softplus(s_dt + p['ssd_dt_bias'])
    xh = xs_.reshape(Bsz, T, SSD_H, SSD_P)
    Bh = jnp.repeat(Bs.reshape(Bsz, T, SSD_G, SSD_N), SSD_H // SSD_G, axis=2)
    Ch = jnp.repeat(Cs.reshape(Bsz, T, SSD_G, SSD_N), SSD_H // SSD_G, axis=2)
    y, S_ssd = _ssd_scan(Ch, Bh, xh * dt[..., None], dt * -jnp.exp(p['ssd_a_log']), S_ssd)
    y = (y + p['ssd_d'][:, None] * xh).reshape(Bsz, T, -1) * jax.nn.silu(s_z)
    out_c = _rms(y.reshape(Bsz, T, SSD_G, -1), p['ssd_norm_w'].reshape(SSD_G, -1)).reshape(Bsz, T, -1)

    qm = m_q.reshape(Bsz, T, ML_H, ML_DK) * ML_DK ** -0.5
    km = m_k.reshape(Bsz, T, ML_H, ML_DK)
    vm = m_v.reshape(Bsz, T, ML_H, ML_DV)
    ig = m_i + p['ml_b_i']
    lf = jax.nn.log_sigmoid(m_f + p['ml_b_f'])
    hh, C_ml, n_ml, m_ml = _mlstm(qm, km, vm, ig, lf, C_ml, n_ml, m_ml)
    out_d = (_rms(hh, p['ml_norm_w']) * jax.nn.sigmoid(m_o.reshape(Bsz, T, ML_H, ML_DV))).reshape(Bsz, T, -1)

    branches = jnp.stack([out_a, out_b, out_c, out_d], axis=2)
    up = jnp.einsum('btnw,nwd->btnd', branches, p['w_branch'].astype(jnp.float32))
    gates = jax.nn.sigmoid(gate.reshape(Bsz, T, N_BRANCH, D_MODEL))
    merged = jnp.einsum('btnd,btnd->btd', gates, up)
    y = merged.astype(h.dtype) @ p['w_out']
    return y, (S_gdn, buf_gdn, S_hg, S_ssd, buf_ssd, C_ml, n_ml, m_ml)


def _swiglu(h, w_in, w_out):
    a, b = jnp.split(h @ w_in, 2, axis=-1)
    return (jax.nn.silu(a) * b) @ w_out


def _trunk(x, c, states, P, final_norm_w):
    new = [[] for _ in states]
    cs = jax.nn.silu(c)
    for l in range(DEPTH):
        p = {name: arr[l] for name, arr in P.items()}
        mod = cs @ p['ada_w'] + p['ada_b']
        sh1, sc1, gt1, sh2, sc2, gt2 = (m[:, None, :] for m in jnp.split(mod, 6, axis=-1))
        hmix = (_rms(x, p['norm1_w']) * (1 + sc1) + sh1).astype(x.dtype)
        y, st = _mixer(hmix, tuple(s[l].astype(jnp.float32) for s in states), p)
        x = x + gt1 * y
        hffn = (_rms(x, p['norm2_w']) * (1 + sc2) + sh2).astype(x.dtype)
        x = x + gt2 * _swiglu(hffn, p['ffn_w_in'], p['ffn_w_out'])
        for lst, s in zip(new, st):
            lst.append(s)
    return _rms(x, final_norm_w).astype(x.dtype), tuple(jnp.stack(s) for s in new)


def setup_inputs(seed: int = 0) -> dict:
    key = jax.random.key(seed)
    keys = iter(jax.random.split(key, 64))

    def nrm(shape, scale):
        return scale * jax.random.normal(next(keys), shape, jnp.float32)

    def dt_bias(shape):
        dt = jnp.exp(jax.random.uniform(next(keys), shape, jnp.float32, math.log(1e-3), math.log(1e-1)))
        return dt + jnp.log(-jnp.expm1(-dt))

    def a_log(shape):
        return jnp.log(jax.random.uniform(next(keys), shape, jnp.float32, 1.0, 16.0))

    n_in = sum(IN_SPLITS)
    return {
        'x_prompt': nrm((BATCH, SEQ, D_MODEL), 1.0),
        'x_sample': nrm((DEC_BATCH, DEC_SEQ, D_MODEL), 1.0),
        'c_prompt': nrm((BATCH, D_MODEL), 1.0),
        'c_sample': nrm((DEC_BATCH, D_MODEL), 1.0),
        'state_gdn': nrm((DEPTH, DEC_BATCH, GDN_H, GDN_DK, GDN_DV), 0.1),
        'state_gdn_conv': nrm((DEPTH, DEC_BATCH, CONV_K - 1, GDN_CONV_W), 1.0),
        'state_hgrn': nrm((DEPTH, DEC_BATCH, HG_H, HG_DK, HG_DV), 0.5),
        'state_ssd': nrm((DEPTH, DEC_BATCH, SSD_H, SSD_N, SSD_P), 0.1),
        'state_ssd_conv': nrm((DEPTH, DEC_BATCH, CONV_K - 1, SSD_CONV_W), 1.0),
        'state_mlstm_c': nrm((DEPTH, DEC_BATCH, ML_H, ML_DK, ML_DV), 0.1),
        'state_mlstm_n': nrm((DEPTH, DEC_BATCH, ML_H, ML_DK), 0.1),
        'state_mlstm_m': nrm((DEPTH, DEC_BATCH, ML_H), 1.0),
        'ada_w': nrm((DEPTH, D_MODEL, 6 * D_MODEL), 0.5 * D_MODEL ** -0.5),
        'ada_b': nrm((DEPTH, 6 * D_MODEL), 0.02),
        'norm1_w': 1.0 + nrm((DEPTH, D_MODEL), 0.1),
        'norm2_w': 1.0 + nrm((DEPTH, D_MODEL), 0.1),
        'w_in': nrm((DEPTH, D_MODEL, n_in), D_MODEL ** -0.5),
        'gdn_conv_w': nrm((DEPTH, CONV_K, GDN_CONV_W), CONV_K ** -0.5),
        'gdn_conv_b': nrm((DEPTH, GDN_CONV_W), 0.02),
        'gdn_a_log': a_log((DEPTH, GDN_H)),
        'gdn_dt_bias': dt_bias((DEPTH, GDN_H)),
        'gdn_norm_w': 1.0 + nrm((DEPTH, GDN_DV), 0.1),
        'hg_lb_logits': nrm((DEPTH, HG_H * HG_DK), 1.0),
        'hg_norm_w': 1.0 + nrm((DEPTH, HG_DV), 0.1),
        'ssd_conv_w': nrm((DEPTH, CONV_K, SSD_CONV_W), CONV_K ** -0.5),
        'ssd_conv_b': nrm((DEPTH, SSD_CONV_W), 0.02),
        'ssd_a_log': a_log((DEPTH, SSD_H)),
        'ssd_dt_bias': dt_bias((DEPTH, SSD_H)),
        'ssd_d': 1.0 + nrm((DEPTH, SSD_H), 0.1),
        'ssd_norm_w': 1.0 + nrm((DEPTH, SSD_H * SSD_P), 0.1),
        'ml_b_i': nrm((DEPTH, ML_H), 0.1),
        'ml_b_f': 3.0 + nrm((DEPTH, ML_H), 0.5),
        'ml_norm_w': 1.0 + nrm((DEPTH, ML_DV), 0.1),
        'w_branch': nrm((DEPTH, N_BRANCH, BRANCH_W, D_MODEL), BRANCH_W ** -0.5),
        'w_out': nrm((DEPTH, D_MODEL, D_MODEL), D_MODEL ** -0.5),
        'ffn_w_in': nrm((DEPTH, D_MODEL, 2 * D_FF), D_MODEL ** -0.5),
        'ffn_w_out': nrm((DEPTH, D_FF, D_MODEL), D_FF ** -0.5),
        'final_norm_w': 1.0 + nrm((D_MODEL,), 0.1),
    }


def reference(x_prompt, x_sample, c_prompt, c_sample, state_gdn, state_gdn_conv, state_hgrn,
              state_ssd, state_ssd_conv, state_mlstm_c, state_mlstm_n, state_mlstm_m,
              ada_w, ada_b, norm1_w, norm2_w, w_in, gdn_conv_w, gdn_conv_b, gdn_a_log,
              gdn_dt_bias, gdn_norm_w, hg_lb_logits, hg_norm_w, ssd_conv_w, ssd_conv_b,
              ssd_a_log, ssd_dt_bias, ssd_d, ssd_norm_w, ml_b_i, ml_b_f, ml_norm_w,
              w_branch, w_out, ffn_w_in, ffn_w_out, final_norm_w):
    sm = jax.nn.softmax(hg_lb_logits.astype(jnp.float32), axis=0)
    hg_lb = jnp.cumsum(sm, axis=0) - sm[0]
    P = dict(ada_w=ada_w, ada_b=ada_b, norm1_w=norm1_w, norm2_w=norm2_w, w_in=w_in,
             gdn_conv_w=gdn_conv_w, gdn_conv_b=gdn_conv_b, gdn_a_log=gdn_a_log,
             gdn_dt_bias=gdn_dt_bias, gdn_norm_w=gdn_norm_w, hg_lb=hg_lb, hg_norm_w=hg_norm_w,
             ssd_conv_w=ssd_conv_w, ssd_conv_b=ssd_conv_b, ssd_a_log=ssd_a_log,
             ssd_dt_bias=ssd_dt_bias, ssd_d=ssd_d, ssd_norm_w=ssd_norm_w, ml_b_i=ml_b_i,
             ml_b_f=ml_b_f, ml_norm_w=ml_norm_w, w_branch=w_branch, w_out=w_out,
             ffn_w_in=ffn_w_in, ffn_w_out=ffn_w_out)
    sample_states = (state_gdn, state_gdn_conv, state_hgrn, state_ssd, state_ssd_conv,
                     state_mlstm_c, state_mlstm_n, state_mlstm_m)
    prompt_init = tuple(jnp.zeros((DEPTH, x_prompt.shape[0]) + s.shape[2:], jnp.float32)
                        for s in sample_states)
    y_prompt, new_p = _trunk(x_prompt, c_prompt, prompt_init, P, final_norm_w)
    y_sample, new_s = _trunk(x_sample, c_sample, sample_states, P, final_norm_w)
    p_gdn, p_gdn_conv, p_hgrn, p_ssd, p_ssd_conv, p_mc, p_mn, p_mm = new_p
    s_gdn, s_gdn_conv, s_hgrn, s_ssd, s_ssd_conv, s_mc, s_mn, s_mm = new_s
    return (y_prompt, y_sample, p_gdn, p_gdn_conv, p_hgrn, p_ssd, p_ssd_conv, p_mc, p_mn, p_mm,
            s_gdn, s_gdn_conv, s_hgrn, s_ssd, s_ssd_conv, s_mc, s_mn, s_mm)
```

```python
import functools
import math

import numpy as np
import jax
import jax.numpy as jnp
from jax import lax
from jax.experimental import pallas as pl
from jax.experimental.pallas import tpu as pltpu

F32 = jnp.float32
BF16 = jnp.bfloat16

D_MODEL = 1024
DEPTH = 2
N_BRANCH = 4
BRANCH_W = 256
N_HEAD = 4
HEAD_D = 64
SSD_H = 8
SSD_P = 32
SSD_N = 64
SSD_G = 2
CONV_K = 4
CHUNK = 64
D_FF = 2816
EPS = 1e-6
NEG = -1e30
GDN_CONV_W = 768
SSD_CONV_W = 512
MIX_W = 4096
GATE_W = 4096
PROJ_W = MIX_W + GATE_W
SMALL_OFF = 3840
LANE = 128
VMEM_LIMIT = 56 * 1024 * 1024

ST_BETA, ST_GA, ST_DT, ST_MI, ST_MF = 0, 4, 8, 16, 20

_NN = (((1,), (0,)), ((), ()))
_NT = (((1,), (1,)), ((), ()))
_TN = (((0,), (0,)), ((), ()))


def _mm(a, b, dims=_NN):
    return lax.dot_general(a.astype(BF16), b.astype(BF16), dims, preferred_element_type=F32)


def _split(x, n):
    parts, r = [], x
    for i in range(n):
        p = r.astype(BF16)
        parts.append(p)
        if i < n - 1:
            r = r - p.astype(F32)
    return parts


def _mm01(x, m, n=3, dims=_NN):
    out = None
    m = m.astype(BF16)
    for p in _split(x, n):
        t = lax.dot_general(p, m, dims, preferred_element_type=F32)
        out = t if out is None else out + t
    return out


def _m01m(m, x, n=3, dims=_NN):
    out = None
    m = m.astype(BF16)
    for p in _split(x, n):
        t = lax.dot_general(m, p, dims, preferred_element_type=F32)
        out = t if out is None else out + t
    return out


def _sigmoid(x):
    return jax.nn.sigmoid(x)


def _silu(x):
    return x * jax.nn.sigmoid(x)


def _softplus(x):
    return jnp.maximum(x, 0.0) + jnp.log1p(jnp.exp(-jnp.abs(x)))


def _logsig(x):
    return jnp.minimum(x, 0.0) - jnp.log1p(jnp.exp(-jnp.abs(x)))


def _expand(base, heads, width):
    e = np.zeros((LANE, heads * width), np.float32)
    for h in range(heads):
        e[base + h, h * width:(h + 1) * width] = 1.0
    return e


def _cat_segments(segs):
    cols, off, pos = [], {}, 0
    for name, m in segs:
        w = m.shape[1]
        wp = -(-w // LANE) * LANE
        mp = np.zeros((m.shape[0], wp), np.float32)
        mp[:, :w] = m
        cols.append(mp)
        off[name] = (pos, w)
        pos += wp
    return np.concatenate(cols, axis=1), off


@functools.lru_cache(maxsize=None)
def _mixer_consts(L, nv):
    J = int(round(math.log2(L)))
    assert 1 << J == L
    r = np.arange(L)
    tri = (r[None, :] <= r[:, None])
    rev = (r[:, None] < r[None, :]) & (r[None, :] <= nv - 1)
    trirev = np.concatenate([tri, rev], axis=0).astype(np.float32)

    def wide(m, heads):
        return np.tile(m.astype(np.float32), (1, heads))

    eye = np.eye(L, dtype=bool)
    strict = (r[None, :] < r[:, None])
    c = dict(
        trirev=trirev,
        ones_ll=np.ones((L, L), np.float32),
        tril4=wide(tri, 4), strict4=wide(strict, 4), eye4=wide(eye, 4),
        tril8=wide(tri, 8), eye8=wide(eye, 8),
        rowvalid=(r[:, None] <= nv - 1).astype(np.float32) * np.ones((1, LANE), np.float32),
    )
    ch = np.arange(256)
    row4 = np.repeat(np.arange(4), L)
    row8 = np.repeat(np.arange(8), L)
    c['stk4'] = (row4[:, None] == ch[None, :] // 64).astype(np.float32)
    c['stk8'] = (row8[:, None] == ch[None, :] // 32).astype(np.float32)
    c['bstk8'] = (row8[:, None] // 4 == np.arange(128)[None, :] // 64).astype(np.float32)
    c['bdl4'] = (row4[:, None] == row4[None, :]).astype(np.float32)
    c['eye4l'] = np.eye(4 * L, dtype=np.float32)
    c['bd64'] = (ch[:, None] // 64 == ch[None, :] // 64).astype(np.float32)
    c['bd128'] = (ch[:, None] // 128 == ch[None, :] // 128).astype(np.float32)
    c['gbd'] = (np.arange(128)[:, None] // 64 == ch[None, :] // 128).astype(np.float32)
    c['eye64'] = np.eye(64, dtype=np.float32)
    lr = np.tile(r, 4)
    lv = []
    for j in range(J):
        same = (row4[:, None] == row4[None, :]) & ((lr[:, None] >> (j + 1)) == (lr[None, :] >> (j + 1)))
        lv.append(same & (((lr >> j) & 1) == 1)[:, None] & (((lr >> j) & 1) == 0)[None, :])
    c['lvl'] = np.stack(lv).astype(np.float32)

    mats, masks = [], [wide(eye, 4)]
    a_list, b_list = [], []
    for j in range(J):
        bnd = ((r >> (j + 1)) << (j + 1)) + (1 << j) - 1
        low = ((r >> j) & 1) == 1
        a = low[:, None] & (bnd[:, None] < r[None, :]) & (r[None, :] <= r[:, None])
        b = (~low)[:, None] & (r[:, None] < r[None, :]) & (r[None, :] <= bnd[:, None])
        a_list.append(a)
        b_list.append(b)
        same = (r[:, None] >> (j + 1)) == (r[None, :] >> (j + 1))
        masks.append(wide(same & low[:, None] & (~low)[None, :], 4))
    mats = a_list + b_list + [tri, rev]
    c['mhg'] = np.concatenate(mats, axis=0).astype(np.float32)
    c['hmask'] = np.stack(masks).astype(np.float32)

    e_act, off_act = _cat_segments([('b4L', _expand(ST_BETA, 4, L)), ('bch', _expand(ST_BETA, 4, 64)),
                                    ('dtch', _expand(ST_DT, 8, 32)), ('i4L', _expand(ST_MI, 4, L)),
                                    ('ich', _expand(ST_MI, 4, 64))])
    e_cs, off_cs = _cat_segments([('g4L', _expand(ST_GA, 4, L)), ('gch', _expand(ST_GA, 4, 64)),
                                  ('s8L', _expand(ST_DT, 8, L)), ('sch', _expand(ST_DT, 8, 32)),
                                  ('m4L', _expand(ST_MF, 4, L))])
    e_rev, off_rev = _cat_segments([('gch', _expand(ST_GA, 4, 64)), ('sch', _expand(ST_DT, 8, 32)),
                                    ('mch', _expand(ST_MF, 4, 64))])
    c['e_act'], c['e_cs'], c['e_rev'] = e_act, e_cs, e_rev
    offs = dict(act=off_act, cs=off_cs, rev=off_rev)
    return c, offs, J


_F32_CONSTS = ('rowvalid', 'hmask')
_CONST_ORDER = ('trirev', 'ones_ll', 'tril4', 'strict4', 'eye4', 'tril8', 'eye8', 'rowvalid', 'stk4', 'stk8',
                'bstk8', 'bdl4', 'eye4l', 'bd64', 'bd128', 'gbd', 'eye64', 'mhg', 'hmask', 'e_act', 'e_cs', 'e_rev',
                'lvl')
_MXU_CONSTS = ('trirev', 'ones_ll', 'bd64', 'bd128', 'eye64', 'mhg', 'e_act', 'e_cs', 'e_rev')


def _seg(x, offs, name):
    o, w = offs[name]
    return x[:, o:o + w]


def _conv_silu(ext, w, b, L):
    y = b
    for i in range(CONV_K):
        y = y + ext[5 + i:5 + i + L, :] * w[i:i + 1, :]
    return _silu(y)


def _stack(x, n, mask):
    return jnp.concatenate([x] * n, axis=0) * mask


def _unit_lower_inverse(nbd, eye, lvl, J):
    t = eye - nbd * lvl[0]
    for j in range(1, J):
        t = t - _mm(_mm(t, nbd * lvl[j]), t)
    return t


def _mixer_chunk(pm, ext_g, ext_s, st, P, K, offs, L, nv, J, layer):
    lane = lax.broadcasted_iota(jnp.int32, (1, LANE), 1)
    rowvalid = K['rowvalid'][:, 0:1]

    z = pm[:, SMALL_OFF:SMALL_OFF + LANE] + P['sp'][0:1, :]
    act = jnp.where(lane < ST_GA, _sigmoid(z),
                    jnp.where(lane < ST_MI, _softplus(z), jnp.where(lane < ST_MF, z, _logsig(z))))
    neg_a = -jnp.exp(P['sp'][1:2, :])
    dec_in = jnp.where((lane >= ST_GA) & (lane < ST_MI), neg_a * act,
                       jnp.where((lane >= ST_MF) & (lane < ST_MF + 4), act, 0.0))
    cr = _m01m(K['trirev'], dec_in)
    cs, rev = cr[:L], cr[L:]
    act_e = _mm01(act, K['e_act'])
    cs_e = _mm01(cs, K['e_cs'])
    rev_e = _mm01(rev, K['e_rev'])
    oa, oc, orv = offs['act'], offs['cs'], offs['rev']

    def row_bcast(x_wide, eye_wide):
        return _m01m(K['ones_ll'], x_wide * eye_wide, dims=_TN)

    def seg_norm(x, seg, width):
        return x * lax.rsqrt(_mm01(x * x, seg, n=2) * (1.0 / width) + EPS)

    qkv = _conv_silu(ext_g, P['cwg'], P['cbg'], L)
    q, k, v = qkv[:, 0:256], qkv[:, 256:512], qkv[:, 512:768]
    q = q * lax.rsqrt(_mm01(q * q, K['bd64'], n=2) + EPS) * (HEAD_D ** -0.5)
    k = k * lax.rsqrt(_mm01(k * k, K['bd64'], n=2) + EPS)
    g4 = _seg(cs_e, oc, 'g4L')
    b4 = _seg(act_e, oa, 'b4L')
    d = g4 - row_bcast(g4, K['eye4'])
    tril4 = K['tril4'] > 0
    dec = jnp.where(tril4, jnp.exp(jnp.where(tril4, d, 0.0)), 0.0)
    kst = _stack(k, 4, K['stk4'])
    kk = _mm(k, kst, _NT)
    nw = jnp.where(K['strict4'] > 0, kk * dec * b4, 0.0)
    nbd = _stack(nw, 4, K['bdl4'])
    tinv = _unit_lower_inverse(nbd, K['eye4l'], K['lvl'], J)
    bch = _seg(act_e, oa, 'bch')
    gch = _seg(cs_e, oc, 'gch')
    eg = jnp.exp(gch)
    solv_st = _mm(tinv, _stack(v * bch, 4, K['stk4']))
    solk_st = _mm(tinv, _stack(k * (bch * eg), 4, K['stk4']))
    solv = solv_st[0:L] + solv_st[L:2 * L] + solv_st[2 * L:3 * L] + solv_st[3 * L:4 * L]
    solk = solk_st[0:L] + solk_st[L:2 * L] + solk_st[2 * L:3 * L] + solk_st[3 * L:4 * L]
    s_g = st['gdn']
    u = solv - _mm(solk, s_g)
    qk = _mm(q, kst, _NT) * dec
    o_gdn = _mm(q * eg, s_g) + _mm(qk, _stack(u, 4, K['stk4']))
    kw = k * (jnp.exp(_seg(rev_e, orv, 'gch')) * rowvalid)
    gdn_new = s_g * jnp.exp(gch[nv - 1:nv, :]) + K['bd64'] * _mm(kw, u, _TN)
    gz = pm[:, 768:1024]
    out_a = seg_norm(o_gdn, K['bd64'], HEAD_D) * P['vec'][0:1, :] * _silu(gz)

    lg = P['lg']
    mx = lg[0:1, :]
    for i in range(1, DEPTH):
        mx = jnp.maximum(mx, lg[i:i + 1, :])
    ex = [jnp.exp(lg[i:i + 1, :] - mx) for i in range(DEPTH)]
    tot = ex[0]
    for i in range(1, DEPTH):
        tot = tot + ex[i]
    sm = [e / tot for e in ex]
    cum = sm[0]
    for i in range(1, layer + 1):
        cum = cum + sm[i]
    lb = cum - sm[0]
    lb_pos = lb > 0
    log_lb = jnp.log(jnp.where(lb_pos, lb, 1.0))
    hq = _silu(pm[:, 1024:1280])
    fz = pm[:, 1280:1536]
    hv = pm[:, 1536:1792]
    hgate = pm[:, 1792:2048]
    ls = _logsig(fz)
    t2 = jnp.log1p(-lb) + ls
    la = jnp.maximum(log_lb, t2) + jnp.log1p(jnp.exp(-jnp.abs(log_lb - t2)))
    logf = jnp.where(lb_pos, la, ls)
    kg = (1.0 - lb) * _sigmoid(-fz)
    y = _m01m(K['mhg'], logf)
    ey = jnp.exp(y)
    a_w = K['hmask'][0] * _mm(hq, _stack(kg, 4, K['stk4']), _NT)
    for j in range(J):
        ea = ey[j * L:(j + 1) * L]
        eb = ey[(J + j) * L:(J + j + 1) * L]
        a_w = a_w + K['hmask'][j + 1] * _mm(hq * ea, _stack(kg * eb, 4, K['stk4']), _NT)
    g_h = y[2 * J * L:(2 * J + 1) * L]
    e_rev_h = ey[(2 * J + 1) * L:(2 * J + 2) * L]
    st_h = st['hg']
    o_hg = _mm(hq * jnp.exp(g_h), st_h, _NT) + _mm(a_w, _stack(hv, 4, K['stk4']))
    hg_new = st_h * jnp.exp(g_h[nv - 1:nv, :]) + K['bd64'] * _mm(hv, kg * (e_rev_h * rowvalid), _TN)
    out_b = seg_norm(o_hg, K['bd64'], HEAD_D) * P['vec'][1:2, :] * _silu(hgate)

    sz = pm[:, 2048:2304]
    xbc = _conv_silu(ext_s, P['cws'], P['cbs'], L)
    xs, bs, cc = xbc[:, 0:256], xbc[:, 256:384], xbc[:, 384:512]
    vs = xs * _seg(act_e, oa, 'dtch')
    g8 = _seg(cs_e, oc, 's8L')
    d8 = g8 - row_bcast(g8, K['eye8'])
    tril8 = K['tril8'] > 0
    dec8 = jnp.where(tril8, jnp.exp(jnp.where(tril8, d8, 0.0)), 0.0)
    a8 = _mm(cc, _stack(bs, 8, K['bstk8']), _NT) * dec8
    gs = _seg(cs_e, oc, 'sch')
    s_s = st['ssd']
    o_ssd = jnp.exp(gs) * _mm(cc, s_s) + _mm(a8, _stack(vs, 8, K['stk8']))
    ssd_new = s_s * jnp.exp(gs[nv - 1:nv, :]) + K['gbd'] * _mm(bs * rowvalid, vs * jnp.exp(_seg(rev_e, orv, 'sch')), _TN)
    ys = (o_ssd + P['vec'][4:5, :] * xs) * _silu(sz)
    out_c = seg_norm(ys, K['bd128'], 2 * HEAD_D) * P['vec'][2:3, :]

    mq = pm[:, 2816:3072] * (HEAD_D ** -0.5)
    mk = pm[:, 3072:3328]
    mv = pm[:, 3328:3584]
    mo = pm[:, 3584:3840]
    bm4 = _seg(cs_e, oc, 'm4L')
    ig4 = _seg(act_e, oa, 'i4L')
    logw = jnp.where(tril4, bm4 + row_bcast(ig4 - bm4, K['eye4']), NEG)
    hid4 = lax.broadcasted_iota(jnp.int32, (1, 4 * L), 1) // L
    hidc = lax.broadcasted_iota(jnp.int32, (1, 256), 1) // HEAD_D
    m_prev = st['mlm']
    mt4 = jnp.zeros((L, 4 * L), F32)
    mtc = jnp.zeros((L, 256), F32)
    l0c = jnp.zeros((L, 256), F32)
    mlc_row = jnp.zeros((1, 256), F32)
    wl0_row = jnp.zeros((1, 256), F32)
    m_new = jnp.zeros((1, LANE), F32)
    for h in range(N_HEAD):
        mp_h = m_prev[:, h:h + 1]
        b_h = cs[:, ST_MF + h:ST_MF + h + 1]
        l0_h = b_h + mp_h
        mx_h = jnp.max(jnp.where(hid4 == h, logw, NEG), axis=-1, keepdims=True)
        mt_h = jnp.maximum(l0_h, mx_h)
        ml_h = mt_h[nv - 1:nv, :]
        wl0_h = jnp.exp(b_h[nv - 1:nv, :] + mp_h - ml_h)
        mt4 = jnp.where(hid4 == h, mt_h, mt4)
        mtc = jnp.where(hidc == h, mt_h, mtc)
        l0c = jnp.where(hidc == h, l0_h, l0c)
        mlc_row = jnp.where(hidc == h, ml_h, mlc_row)
        wl0_row = jnp.where(hidc == h, wl0_h, wl0_row)
        m_new = jnp.where(lane == h, ml_h, m_new)
    w = jnp.where(tril4, jnp.exp(logw - mt4), 0.0)
    w0c = jnp.exp(l0c - mtc)
    qkm = _mm(mq, _stack(mk, 4, K['stk4']), _NT) * w
    c_m = st['mlc']
    n_m = st['mln']
    num = w0c * _mm(mq, c_m) + _mm(qkm, _stack(mv, 4, K['stk4']))
    qn = _mm01(mq * n_m, K['bd64'], n=2)
    den = w0c * qn + _mm01(qkm, K['stk4'], n=2)
    hh = num / jnp.maximum(jnp.abs(den), jnp.exp(-mtc))
    wlc = jnp.exp(_seg(rev_e, orv, 'mch') + _seg(act_e, oa, 'ich') - mlc_row) * rowvalid
    kwl = mk * wlc
    mlc_new = c_m * wl0_row + K['bd64'] * _mm(kwl, mv, _TN)
    mln_new = n_m * wl0_row + jnp.sum(kwl, axis=0, keepdims=True)
    out_d = seg_norm(hh, K['bd64'], HEAD_D) * P['vec'][3:4, :] * _sigmoid(mo)

    branches = jnp.concatenate([out_a, out_b, out_c, out_d], axis=1)
    new = dict(gdn=gdn_new, hg=hg_new, ssd=ssd_new, mlc=mlc_new, mln=mln_new, mlm=m_new)
    return branches, new


def _mixer_kernel(L, nv, J, layer, offs, n_const, *refs):
    it = iter(refs)
    pm_ref = next(it)
    gdn0, cg0, hg0, ssd0, cs0, mc0, mn0, mm0 = (next(it) for _ in range(8))
    sp_ref, cwg_ref, cbg_ref, cws_ref, cbs_ref, vec_ref, lg_ref = (next(it) for _ in range(7))
    kref = {name: next(it) for name in _CONST_ORDER[:n_const]}
    br_ref = next(it)
    gdn1, cg1, hg1, ssd1, cs1, mc1, mn1, mm1 = (next(it) for _ in range(8))
    sg, sh, ss, sc, sn, sm, extg, exts = (next(it) for _ in range(8))

    c = pl.program_id(1)
    nc = pl.num_programs(1)

    @pl.when(c == 0)
    def _init():
        sg[...] = jnp.zeros_like(sg)
        sh[...] = jnp.zeros_like(sh)
        ss[...] = jnp.zeros_like(ss)
        sc[...] = jnp.zeros_like(sc)
        sm[...] = jnp.zeros_like(sm)
        eye = kref['eye64'][...]
        for h in range(N_HEAD):
            lo = h * HEAD_D
            sg[lo:lo + HEAD_D, lo:lo + HEAD_D] = gdn0[0, h]
            sh[lo:lo + HEAD_D, lo:lo + HEAD_D] = _m01m(eye, hg0[0, h], dims=_NT)
            sc[lo:lo + HEAD_D, lo:lo + HEAD_D] = mc0[0, h]
            sn[0:1, lo:lo + HEAD_D] = mn0[0, h:h + 1, :]
        for h in range(SSD_H):
            g = h // (SSD_H // SSD_G)
            ss[g * SSD_N:(g + 1) * SSD_N, h * SSD_P:(h + 1) * SSD_P] = ssd0[0, h]
        sm[0:1, 0:N_HEAD] = mm0[0]
        extg[0:8, :] = jnp.zeros((8, GDN_CONV_W), F32)
        exts[0:8, :] = jnp.zeros((8, SSD_CONV_W), F32)
        extg[5:8, :] = cg0[0]
        exts[5:8, :] = cs0[0]

    pm = pm_ref[0]
    extg[8:8 + L, :] = pm[:, 0:GDN_CONV_W]
    exts[8:8 + L, :] = pm[:, 2304:2304 + SSD_CONV_W]

    P = dict(sp=sp_ref[...], cwg=cwg_ref[...], cbg=cbg_ref[...], cws=cws_ref[...], cbs=cbs_ref[...],
             vec=vec_ref[...], lg=lg_ref[...])
    K = {name: r[...] for name, r in kref.items()}
    st = dict(gdn=sg[...], hg=sh[...], ssd=ss[...], mlc=sc[...], mln=sn[...], mlm=sm[...])
    branches, new = _mixer_chunk(pm, extg, exts, st, P, K, offs, L, nv, J, layer)
    br_ref[0] = branches.astype(br_ref.dtype)
    sg[...] = new['gdn']
    sh[...] = new['hg']
    ss[...] = new['ssd']
    sc[...] = new['mlc']
    sn[...] = new['mln']
    sm[...] = new['mlm']
    tail_g = extg[8 + nv - 3:8 + nv, :]
    tail_s = exts[8 + nv - 3:8 + nv, :]
    extg[5:8, :] = tail_g
    exts[5:8, :] = tail_s

    @pl.when(c == nc - 1)
    def _fin():
        eye = kref['eye64'][...]
        for h in range(N_HEAD):
            lo = h * HEAD_D
            gdn1[0, h] = sg[lo:lo + HEAD_D, lo:lo + HEAD_D]
            hg1[0, h] = _m01m(eye, sh[lo:lo + HEAD_D, lo:lo + HEAD_D], dims=_NT)
            mc1[0, h] = sc[lo:lo + HEAD_D, lo:lo + HEAD_D]
            mn1[0, h:h + 1, :] = sn[0:1, lo:lo + HEAD_D]
        for h in range(SSD_H):
            g = h // (SSD_H // SSD_G)
            ssd1[0, h] = ss[g * SSD_N:(g + 1) * SSD_N, h * SSD_P:(h + 1) * SSD_P]
        mm1[0] = sm[0:1, 0:N_HEAD]
        cg1[0] = tail_g
        cs1[0] = tail_s


def _full_spec(a):
    nd = a.ndim
    return pl.BlockSpec(a.shape, lambda b, c, _nd=nd: (0,) * _nd)


def _mixer_call(proj, states, params, L, nv, layer):
    bg, t, _ = proj.shape
    nchunk = t // L
    consts, offs, J = _mixer_consts(L, nv)
    const_arrays = [jnp.asarray(consts[n], F32 if n in _F32_CONSTS else (BF16 if n in _MXU_CONSTS else F32))
                    for n in _CONST_ORDER]
    gdn0, cg0, hg0, ssd0, cs0, mc0, mn0, mm0 = states
    mm0 = mm0.reshape(bg, 1, N_HEAD)
    st_in = [gdn0, cg0, hg0, ssd0, cs0, mc0, mn0, mm0]

    def st_spec(a):
        nd = a.ndim
        return pl.BlockSpec((1,) + a.shape[1:], lambda b, c, _nd=nd: (b,) + (0,) * (_nd - 1))

    in_specs = ([pl.BlockSpec((1, L, MIX_W), lambda b, c: (b, c, 0))]
                + [st_spec(a) for a in st_in]
                + [_full_spec(a) for a in params]
                + [_full_spec(a) for a in const_arrays])
    out_shape = ([jax.ShapeDtypeStruct((bg, t, N_BRANCH * BRANCH_W), BF16)]
                 + [jax.ShapeDtypeStruct(a.shape, F32) for a in st_in])
    out_specs = ([pl.BlockSpec((1, L, N_BRANCH * BRANCH_W), lambda b, c: (b, c, 0))]
                 + [st_spec(a) for a in st_in])
    scratch = [pltpu.VMEM((256, 256), F32), pltpu.VMEM((256, 256), F32), pltpu.VMEM((SSD_G * SSD_N, 256), F32),
               pltpu.VMEM((256, 256), F32), pltpu.VMEM((1, 256), F32), pltpu.VMEM((1, LANE), F32),
               pltpu.VMEM((8 + L, GDN_CONV_W), F32), pltpu.VMEM((8 + L, SSD_CONV_W), F32)]
    outs = pl.pallas_call(
        functools.partial(_mixer_kernel, L, nv, J, layer, offs, len(_CONST_ORDER)),
        grid=(bg, nchunk),
        in_specs=in_specs, out_specs=out_specs, out_shape=out_shape, scratch_shapes=scratch,
        compiler_params=pltpu.CompilerParams(dimension_semantics=("arbitrary", "arbitrary"),
                                             vmem_limit_bytes=VMEM_LIMIT),
        name=f"mixer_L{L}",
    )(proj, *st_in, *params, *const_arrays)
    br = outs[0]
    new = list(outs[1:])
    new[7] = new[7].reshape(bg, N_HEAD)
    return br, new


def _ada_kernel(c_ref, w_ref, b_ref, o_ref):
    cs = _silu(c_ref[...])
    o_ref[0] = _mm(cs, w_ref[0]) + b_ref[0]


def _ada_call(c_all, ada_w, ada_b):
    rows = c_all.shape[0]
    n = ada_w.shape[-1]
    tn = 1536
    return pl.pallas_call(
        _ada_kernel,
        grid=(DEPTH, n // tn),
        in_specs=[pl.BlockSpec((rows, D_MODEL), lambda l, j: (0, 0)),
                  pl.BlockSpec((1, D_MODEL, tn), lambda l, j: (l, 0, j)),
                  pl.BlockSpec((1, 1, tn), lambda l, j: (l, 0, j))],
        out_specs=pl.BlockSpec((1, rows, tn), lambda l, j: (l, 0, j)),
        out_shape=jax.ShapeDtypeStruct((DEPTH, rows, n), F32),
        compiler_params=pltpu.CompilerParams(dimension_semantics=("arbitrary", "arbitrary"),
                                             vmem_limit_bytes=VMEM_LIMIT),
        name="ada",
    )(c_all, ada_w, ada_b.reshape(DEPTH, 1, n))


def _rms_mod(x, nw, sc, sh):
    ms = jnp.mean(x * x, axis=-1, keepdims=True)
    return x * lax.rsqrt(ms + EPS) * nw * (1.0 + sc) + sh


def _inproj_kernel(x_ref, nw_ref, sc_ref, sh_ref, w_ref, o_ref):
    h = _rms_mod(x_ref[0], nw_ref[...], sc_ref[0], sh_ref[0])
    o_ref[0] = _mm(h, w_ref[...])


def _mod_spec(a, tm):
    if a.shape[1] == 1:
        return pl.BlockSpec((1, 1, D_MODEL), lambda g, i: (g, 0, 0))
    return pl.BlockSpec((1, tm, D_MODEL), lambda g, i: (g, i, 0))


def _resident(shape):
    nd = len(shape)
    return pl.BlockSpec(shape, lambda g, i, _nd=nd: (0,) * _nd, pipeline_mode=pl.Buffered(1))


def _inproj_call(x, nw, sc, sh, w, tm):
    g, r, _ = x.shape
    return pl.pallas_call(
        _inproj_kernel,
        grid=(g, r // tm),
        in_specs=[pl.BlockSpec((1, tm, D_MODEL), lambda g_, i: (g_, i, 0)),
                  _resident((1, D_MODEL)), _mod_spec(sc, tm), _mod_spec(sh, tm),
                  _resident((D_MODEL, PROJ_W))],
        out_specs=pl.BlockSpec((1, tm, PROJ_W), lambda g_, i: (g_, i, 0)),
        out_shape=jax.ShapeDtypeStruct((g, r, PROJ_W), F32),
        compiler_params=pltpu.CompilerParams(dimension_semantics=("arbitrary", "arbitrary"),
                                             vmem_limit_bytes=VMEM_LIMIT),
        name="inproj",
    )(x, nw, sc, sh, w)


def _merge_kernel(br_ref, gate_ref, x_ref, gt_ref, wb_ref, wo_ref, o_ref):
    br = br_ref[0]
    merged = None
    for n in range(N_BRANCH):
        up = _mm(br[:, n * BRANCH_W:(n + 1) * BRANCH_W], wb_ref[n])
        t = _sigmoid(gate_ref[0, :, n * D_MODEL:(n + 1) * D_MODEL]) * up
        merged = t if merged is None else merged + t
    y = _mm(merged, wo_ref[...])
    o_ref[0] = x_ref[0] + gt_ref[0] * y


def _merge_call(br, proj, x, gt, wb, wo, tm):
    g, r, _ = x.shape
    return pl.pallas_call(
        _merge_kernel,
        grid=(g, r // tm),
        in_specs=[pl.BlockSpec((1, tm, N_BRANCH * BRANCH_W), lambda g_, i: (g_, i, 0)),
                  pl.BlockSpec((1, tm, GATE_W), lambda g_, i: (g_, i, 1)),
                  pl.BlockSpec((1, tm, D_MODEL), lambda g_, i: (g_, i, 0)),
                  _mod_spec(gt, tm),
                  _resident((N_BRANCH, BRANCH_W, D_MODEL)), _resident((D_MODEL, D_MODEL))],
        out_specs=pl.BlockSpec((1, tm, D_MODEL), lambda g_, i: (g_, i, 0)),
        out_shape=jax.ShapeDtypeStruct((g, r, D_MODEL), F32),
        compiler_params=pltpu.CompilerParams(dimension_semantics=("arbitrary", "arbitrary"),
                                             vmem_limit_bytes=VMEM_LIMIT),
        name="merge",
    )(br, proj, x, gt, wb, wo)


def _ffn_kernel(final, x_ref, nw_ref, sc_ref, sh_ref, gt_ref, w1_ref, w2_ref, fw_ref, o_ref):
    x = x_ref[0]
    h = _rms_mod(x, nw_ref[...], sc_ref[0], sh_ref[0]).astype(BF16)
    a = _mm(h, w1_ref[:, 0:D_FF])
    b = _mm(h, w1_ref[:, D_FF:2 * D_FF])
    y = _mm(_silu(a) * b, w2_ref[...])
    x2 = x + gt_ref[0] * y
    if final:
        ms = jnp.mean(x2 * x2, axis=-1, keepdims=True)
        x2 = x2 * lax.rsqrt(ms + EPS) * fw_ref[...]
    o_ref[0] = x2


def _ffn_call(x, nw, sc, sh, gt, w1, w2, fw, tm, final):
    g, r, _ = x.shape
    return pl.pallas_call(
        functools.partial(_ffn_kernel, final),
        grid=(g, r // tm),
        in_specs=[pl.BlockSpec((1, tm, D_MODEL), lambda g_, i: (g_, i, 0)),
                  _resident((1, D_MODEL)), _mod_spec(sc, tm), _mod_spec(sh, tm), _mod_spec(gt, tm),
                  _resident((D_MODEL, 2 * D_FF)), _resident((D_FF, D_MODEL)), _resident((1, D_MODEL))],
        out_specs=pl.BlockSpec((1, tm, D_MODEL), lambda g_, i: (g_, i, 0)),
        out_shape=jax.ShapeDtypeStruct((g, r, D_MODEL), F32),
        compiler_params=pltpu.CompilerParams(dimension_semantics=("arbitrary", "arbitrary"),
                                             vmem_limit_bytes=VMEM_LIMIT),
        name="ffn",
    )(x, nw, sc, sh, gt, w1, w2, fw)


def _reorder_w_in(w):
    small = jnp.concatenate([w[:, 1024:1032], w[:, 2824:2832], w[:, 3856:3864]], axis=1)
    pad = jnp.zeros((w.shape[0], MIX_W - SMALL_OFF - small.shape[1]), w.dtype)
    return jnp.concatenate([w[:, 0:1024], w[:, 1032:2824], w[:, 2832:3856], small, pad, w[:, 3864:7960]],
                           axis=1).astype(BF16)


def _small_params(gdn_dt_bias, ssd_dt_bias, ml_b_i, ml_b_f, gdn_a_log, ssd_a_log):
    z4 = jnp.zeros((4,), F32)
    bias = jnp.concatenate([z4, gdn_dt_bias, ssd_dt_bias, ml_b_i, ml_b_f, jnp.zeros((LANE - 24,), F32)])
    alog = jnp.concatenate([z4, gdn_a_log, ssd_a_log, jnp.zeros((LANE - 16,), F32)])
    return jnp.concatenate([bias[None], alog[None], jnp.zeros((6, LANE), F32)], axis=0)


def _layer_params(l, gdn_conv_w, gdn_conv_b, gdn_a_log, gdn_dt_bias, gdn_norm_w, hg_lb_logits, hg_norm_w,
                  ssd_conv_w, ssd_conv_b, ssd_a_log, ssd_dt_bias, ssd_d, ssd_norm_w, ml_b_i, ml_b_f, ml_norm_w):
    sp = _small_params(gdn_dt_bias[l], ssd_dt_bias[l], ml_b_i[l], ml_b_f[l], gdn_a_log[l], ssd_a_log[l])
    vec = jnp.stack([jnp.tile(gdn_norm_w[l], N_HEAD), jnp.tile(hg_norm_w[l], N_HEAD), ssd_norm_w[l],
                     jnp.tile(ml_norm_w[l], N_HEAD), jnp.repeat(ssd_d[l], SSD_P)]
                    + [jnp.zeros((256,), F32)] * 3)
    return [sp, gdn_conv_w[l], gdn_conv_b[l][None], ssd_conv_w[l], ssd_conv_b[l][None], vec,
            hg_lb_logits.astype(F32)]


def _trunk(x, mods, states, L, nv, tm, W, mixer_params, norm1_w, norm2_w, final_norm_w, per_token):
    bg, t, _ = x.shape
    tm = min(tm, bg * t if per_token else t)
    new_states = [[] for _ in range(8)]
    for l in range(DEPTH):
        sh1, sc1, gt1, sh2, sc2, gt2 = jnp.split(mods[l], 6, axis=-1)
        if per_token:
            xd = x.reshape(1, bg * t, D_MODEL)
            m = [jnp.repeat(a, t, axis=0)[None] for a in (sh1, sc1, gt1, sh2, sc2, gt2)]
        else:
            xd = x
            m = [a[:, None, :] for a in (sh1, sc1, gt1, sh2, sc2, gt2)]
        sh1, sc1, gt1, sh2, sc2, gt2 = m
        proj = _inproj_call(xd, norm1_w[l][None], sc1, sh1, W['w_in'][l], tm)
        br, st = _mixer_call(proj.reshape(bg, t, PROJ_W), [s[l] for s in states], mixer_params[l], L, nv, l)
        x1 = _merge_call(br.reshape(xd.shape[0], xd.shape[1], -1), proj, xd, gt1, W['w_branch'][l], W['w_out'][l], tm)
        x2 = _ffn_call(x1, norm2_w[l][None], sc2, sh2, gt2, W['ffn_w_in'][l], W['ffn_w_out'][l],
                       final_norm_w[None], tm, final=(l == DEPTH - 1))
        x = x2.reshape(bg, t, D_MODEL)
        for lst, s in zip(new_states, st):
            lst.append(s)
    return x, [jnp.stack(s) for s in new_states]


def kernel(x_prompt, x_sample, c_prompt, c_sample, state_gdn, state_gdn_conv, state_hgrn, state_ssd, state_ssd_conv, state_mlstm_c, state_mlstm_n, state_mlstm_m, ada_w, ada_b, norm1_w, norm2_w, w_in, gdn_conv_w, gdn_conv_b, gdn_a_log, gdn_dt_bias, gdn_norm_w, hg_lb_logits, hg_norm_w, ssd_conv_w, ssd_conv_b, ssd_a_log, ssd_dt_bias, ssd_d, ssd_norm_w, ml_b_i, ml_b_f, ml_norm_w, w_branch, w_out, ffn_w_in, ffn_w_out, final_norm_w):
    bp, tp, _ = x_prompt.shape
    bs, ts, _ = x_sample.shape
    ls = 8
    assert tp % CHUNK == 0 and ts <= ls

    W = dict(w_in=jnp.stack([_reorder_w_in(w_in[l]) for l in range(DEPTH)]),
             w_branch=w_branch.astype(BF16), w_out=w_out.astype(BF16),
             ffn_w_in=ffn_w_in.astype(BF16), ffn_w_out=ffn_w_out.astype(BF16))
    mixer_params = [_layer_params(l, gdn_conv_w, gdn_conv_b, gdn_a_log, gdn_dt_bias, gdn_norm_w, hg_lb_logits,
                                  hg_norm_w, ssd_conv_w, ssd_conv_b, ssd_a_log, ssd_dt_bias, ssd_d, ssd_norm_w,
                                  ml_b_i, ml_b_f, ml_norm_w) for l in range(DEPTH)]

    mods = _ada_call(jnp.concatenate([c_prompt, c_sample], axis=0), ada_w, ada_b)
    sample_states = [state_gdn, state_gdn_conv, state_hgrn, state_ssd, state_ssd_conv,
                     state_mlstm_c, state_mlstm_n, state_mlstm_m]
    prompt_states = [jnp.zeros((DEPTH, bp) + s.shape[2:], F32) for s in sample_states]

    y_p, new_p = _trunk(x_prompt, mods[:, :bp], prompt_states, CHUNK, CHUNK, 256, W, mixer_params,
                        norm1_w, norm2_w, final_norm_w, per_token=False)
    xs_pad = jnp.pad(x_sample, ((0, 0), (0, ls - ts), (0, 0)))
    y_s, new_s = _trunk(xs_pad, mods[:, bp:], sample_states, ls, ts, 256, W, mixer_params,
                        norm1_w, norm2_w, final_norm_w, per_token=True)
    return (y_p, y_s[:, :ts]) + tuple(new_p) + tuple(new_s)
```

```python
import functools
import math

import numpy as np
import jax
import jax.numpy as jnp
from jax import lax
from jax.experimental import pallas as pl
from jax.experimental.pallas import tpu as pltpu

F32 = jnp.float32
BF16 = jnp.bfloat16

D_MODEL = 1024
DEPTH = 2
N_BRANCH = 4
BRANCH_W = 256
N_HEAD = 4
HEAD_D = 64
SSD_H = 8
SSD_P = 32
SSD_N = 64
SSD_G = 2
CONV_K = 4
CHUNK = 64
D_FF = 2816
EPS = 1e-6
NEG = -1e30
GDN_CONV_W = 768
SSD_CONV_W = 512
SSD_XBC_OFF = 2304
MIX_W = 4096
GATE_W = 4096
PROJ_W = MIX_W + GATE_W
SMALL_OFF = 3840
LANE = 128
VMEM_LIMIT = 56 * 1024 * 1024

ST_BETA, ST_GA, ST_DT, ST_MI, ST_MF = 0, 4, 8, 16, 20

_NN = (((1,), (0,)), ((), ()))
_NT = (((1,), (1,)), ((), ()))
_TN = (((0,), (0,)), ((), ()))


def _mm(a, b, dims=_NN):
    return lax.dot_general(a.astype(BF16), b.astype(BF16), dims, preferred_element_type=F32)


def _split(x, n):
    parts, r = [], x
    for i in range(n):
        p = r.astype(BF16)
        parts.append(p)
        if i < n - 1:
            r = r - p.astype(F32)
    return parts


def _mm01(x, m, n):
    out = None
    for p in _split(x, n):
        t = lax.dot_general(p, m.astype(BF16), _NN, preferred_element_type=F32)
        out = t if out is None else out + t
    return out


def _m01m(m, x, n):
    out = None
    for p in _split(x, n):
        t = lax.dot_general(m, p, _NN, preferred_element_type=F32)
        out = t if out is None else out + t
    return out


def _sigmoid(x):
    return jax.nn.sigmoid(x)


def _silu(x):
    return x * jax.nn.sigmoid(x)


def _softplus(x):
    return jnp.maximum(x, 0.0) + jnp.log1p(jnp.exp(-jnp.abs(x)))


def _logsig(x):
    return jnp.minimum(x, 0.0) - jnp.log1p(jnp.exp(-jnp.abs(x)))


def _expand(base, heads, width):
    e = np.zeros((LANE, heads * width), np.float32)
    for h in range(heads):
        e[base + h, h * width:(h + 1) * width] = 1.0
    return e


def _cat_segments(segs):
    cols, off, pos, seen = [], {}, 0, {}
    for name, m in segs:
        key = m.tobytes() + bytes(str(m.shape), 'ascii')
        if key in seen:
            off[name] = seen[key]
            continue
        w = m.shape[1]
        wp = -(-w // LANE) * LANE
        mp = np.zeros((m.shape[0], wp), np.float32)
        mp[:, :w] = m
        cols.append(mp)
        off[name] = seen[key] = (pos, w)
        pos += wp
    return np.concatenate(cols, axis=1), off


@functools.lru_cache(maxsize=None)
def _mixer_consts(L, nv):
    J = int(round(math.log2(L)))
    assert 1 << J == L
    f = np.float32
    r = np.arange(L)
    tri = (r[None, :] <= r[:, None])
    rev = (r[:, None] < r[None, :]) & (r[None, :] <= nv - 1)
    strict = (r[None, :] < r[:, None])
    eye = np.eye(L, dtype=bool)
    lvl, ab = [], []
    for j in range(J):
        bnd = ((r >> (j + 1)) << (j + 1)) + (1 << j) - 1
        low = ((r >> j) & 1) == 1
        a = low[:, None] & (bnd[:, None] < r[None, :]) & (r[None, :] <= r[:, None])
        b = (~low)[:, None] & (r[:, None] < r[None, :]) & (r[None, :] <= bnd[:, None])
        ab.append(a | b)
        same = (r[:, None] >> (j + 1)) == (r[None, :] >> (j + 1))
        lvl.append(same & low[:, None] & (~low)[None, :])

    def tile(m, n):
        return np.tile(m.astype(f), (1, n))

    pr = np.repeat(np.arange(2), L)
    lr2 = np.tile(r, 2)
    ch = np.arange(LANE)
    ch2 = np.arange(2 * LANE)
    c = dict(
        trirev=np.concatenate([tri, rev], axis=0).astype(f),
        mhg=np.concatenate(ab + [tri, rev], axis=0).astype(f),
        ones_ll=np.ones((L, L), f),
        tril2=tile(tri, 2), strict2=tile(strict, 2), tril4=tile(tri, 4),
        eye4=tile(eye, 4), eye8=tile(eye, 8),
        hmaskp=np.stack([tile(eye, 2)] + [tile(m, 2) for m in lvl]),
        eyep=np.eye(2 * L, dtype=f),
        lvlp=np.stack([(pr[:, None] == pr[None, :]) & m[lr2[:, None], lr2[None, :]] for m in lvl]).astype(f),
        stkp=(pr[:, None] == ch[None, :] // HEAD_D).astype(f),
        bdp=(ch[:, None] // HEAD_D == ch[None, :] // HEAD_D).astype(f),
        pm2=np.stack([ch < HEAD_D, ch >= HEAD_D]).astype(f),
        stk4s=(np.repeat(np.arange(4), L)[:, None] == ch[None, :] // SSD_P).astype(f),
        gbd=(ch[:, None] // SSD_N == ch2[None, :] // LANE).astype(f),
        rowvalid=(r[:, None] <= nv - 1).astype(f) * np.ones((1, LANE), f),
        bd64=(ch2[:, None] // 64 == ch2[None, :] // 64).astype(f),
        bd128=(ch2[:, None] // 128 == ch2[None, :] // 128).astype(f),
    )
    c['e_act'], off_act = _cat_segments([('b_r', _expand(ST_BETA, 4, L)), ('b_c', _expand(ST_BETA, 4, HEAD_D)),
                                         ('dt_c', _expand(ST_DT, SSD_H, SSD_P)),
                                         ('i_r', _expand(ST_MI, 4, L)), ('i_c', _expand(ST_MI, 4, HEAD_D))])
    c['e_cs'], off_cs = _cat_segments([('g_r', _expand(ST_GA, 4, L)), ('g_c', _expand(ST_GA, 4, HEAD_D)),
                                       ('s_r', _expand(ST_DT, SSD_H, L)), ('s_c', _expand(ST_DT, SSD_H, SSD_P)),
                                       ('m_r', _expand(ST_MF, 4, L)), ('m_c', _expand(ST_MF, 4, HEAD_D))])
    c['e_rev'], off_rev = _cat_segments([('g_c', _expand(ST_GA, 4, HEAD_D)), ('s_c', _expand(ST_DT, SSD_H, SSD_P)),
                                         ('m_c', _expand(ST_MF, 4, HEAD_D))])
    return c, dict(act=off_act, cs=off_cs, rev=off_rev), J


_CONST_ORDER = ('trirev', 'mhg', 'ones_ll', 'tril2', 'strict2', 'tril4', 'eye4', 'eye8', 'hmaskp', 'eyep', 'lvlp',
                'stkp', 'bdp', 'pm2', 'stk4s', 'gbd', 'rowvalid', 'bd64', 'bd128', 'e_act', 'e_cs', 'e_rev')
_BF16_CONSTS = ('trirev', 'mhg', 'ones_ll', 'lvlp', 'stkp', 'stk4s', 'bd64', 'bd128', 'e_act', 'e_cs', 'e_rev')


def _conv_silu(ext, w, b, L):
    y = b
    for i in range(CONV_K):
        y = y + ext[5 + i:5 + i + L, :] * w[i:i + 1, :]
    return _silu(y)


def _run_interleaved(tasks):
    tasks = list(tasks)
    while tasks:
        alive = []
        for t in tasks:
            try:
                next(t)
                alive.append(t)
            except StopIteration:
                pass
        tasks = alive


def _mixer_chunk(pm, ext_g, ext_s, st, P, K, offs, L, nv, J, layer):
    lane = lax.broadcasted_iota(jnp.int32, (1, LANE), 1)
    rowvalid = K['rowvalid'][:, 0:1]
    tril2 = K['tril2'] > 0.0
    tril4 = K['tril4'] > 0.0
    stkp, bdp = K['stkp'], K['bdp']
    upper_c = lane >= HEAD_D
    upper_r = lax.broadcasted_iota(jnp.int32, (1, 2 * L), 1) >= L

    def seg(x, which, name):
        o, w = offs[which][name]
        return x[:, o:o + w]

    def pair_c(x, p):
        return x[:, p * LANE:(p + 1) * LANE]

    def pair_r(x, p):
        return x[:, p * 2 * L:(p + 1) * 2 * L]

    def stack(x):
        xb = x.astype(BF16)
        return jnp.concatenate([xb, xb], axis=0) * stkp

    z = pm[:, SMALL_OFF:SMALL_OFF + LANE] + P['sp'][0:1, :]
    act = jnp.where(lane < ST_GA, _sigmoid(z),
                    jnp.where(lane < ST_MI, _softplus(z), jnp.where(lane < ST_MF, z, _logsig(z))))
    neg_a = -jnp.exp(P['sp'][1:2, :])
    dec_in = jnp.where((lane >= ST_GA) & (lane < ST_MI), neg_a * act,
                       jnp.where((lane >= ST_MF) & (lane < ST_MF + 4), act, 0.0))
    cr = _m01m(K['trirev'], dec_in, 2)
    cs, rev = cr[:L], cr[L:]
    ea = _mm01(act, K['e_act'], 2)
    ec = _mm01(cs, K['e_cs'], 2)
    er = _mm01(rev, K['e_rev'], 2)
    g_r, s_r, m_r = seg(ec, 'cs', 'g_r'), seg(ec, 'cs', 's_r'), seg(ec, 'cs', 'm_r')
    g_c, s_c, m_c = seg(ec, 'cs', 'g_c'), seg(ec, 'cs', 's_c'), seg(ec, 'cs', 'm_c')
    b_r, b_c, dt_c = seg(ea, 'act', 'b_r'), seg(ea, 'act', 'b_c'), seg(ea, 'act', 'dt_c')
    i_r, i_c = seg(ea, 'act', 'i_r'), seg(ea, 'act', 'i_c')
    rg_c, rs_c, rm_c = seg(er, 'rev', 'g_c'), seg(er, 'rev', 's_c'), seg(er, 'rev', 'm_c')
    row_g = _m01m(K['ones_ll'], g_r * K['eye4'], 2)
    row_s = _m01m(K['ones_ll'], s_r * K['eye8'], 2)
    row_m = _m01m(K['ones_ll'], (i_r - m_r) * K['eye4'], 2)

    res = dict(o_gdn=[None, None], gdn=[None, None], o_hg=[None, None], hg=[None, None],
               o_ml=[None, None], mlc=[None, None], mln=[None, None], mlm=[None] * N_HEAD)

    qkv = _conv_silu(ext_g, P['cwg'], P['cbg'], L)
    q, k, v = qkv[:, 0:256], qkv[:, 256:512], qkv[:, 512:768]
    ss = _mm(jnp.concatenate([q * q, k * k], axis=0), K['bd64'])
    q = q * (lax.rsqrt(ss[:L] + EPS) * (HEAD_D ** -0.5))
    k = k * lax.rsqrt(ss[L:] + EPS)

    def gdn_task(p):
        qt, kt, vt = pair_c(q, p), pair_c(k, p), pair_c(v, p)
        beta_c, gc = pair_c(b_c, p), pair_c(g_c, p)
        eg = jnp.exp(gc)
        d = pair_r(g_r, p) - pair_r(row_g, p)
        dec = jnp.where(tril2, jnp.exp(jnp.where(tril2, d, 0.0)), 0.0)
        kkqk = _mm(jnp.concatenate([kt, qt], axis=0), stack(kt), _NT)
        yield
        n = (kkqk[:L] * dec * pair_r(b_r, p) * K['strict2']).astype(BF16)
        nbd = jnp.concatenate([n, n], axis=0)
        t = K['eyep'] - (nbd * K['lvlp'][0]).astype(F32)
        for j in range(1, J):
            x = _mm(t, nbd * K['lvlp'][j])
            yield
            t = t - _mm(x, t)
            yield
        sv = _mm(t, stack(vt * beta_c))
        sk = _mm(t, stack(kt * (beta_c * eg)))
        yield
        s_p = st['gdn'][p]
        r = _mm(jnp.concatenate([sk[:L] + sk[L:], qt * eg], axis=0), s_p)
        yield
        u = sv[:L] + sv[L:] - r[:L]
        res['o_gdn'][p] = r[L:] + _mm(kkqk[L:] * dec, stack(u))
        kw = kt * (jnp.exp(pair_c(rg_c, p)) * rowvalid)
        res['gdn'][p] = s_p * jnp.exp(gc[nv - 1:nv, :]) + bdp * _mm(kw, u, _TN)
        yield

    lg = P['lg']
    mx = lg[0:1, :]
    for i in range(1, DEPTH):
        mx = jnp.maximum(mx, lg[i:i + 1, :])
    ex = [jnp.exp(lg[i:i + 1, :] - mx) for i in range(DEPTH)]
    tot = ex[0]
    for i in range(1, DEPTH):
        tot = tot + ex[i]
    sm = [e / tot for e in ex]
    cum = sm[0]
    for i in range(1, layer + 1):
        cum = cum + sm[i]
    lb = cum - sm[0]
    lb_pos = lb > 0
    log_lb = jnp.log(jnp.where(lb_pos, lb, 1.0))
    hq = _silu(pm[:, 1024:1280])
    fz = pm[:, 1280:1536]
    hv = pm[:, 1536:1792]
    ls = _logsig(fz)
    t2 = jnp.log1p(-lb) + ls
    la = jnp.maximum(log_lb, t2) + jnp.log1p(jnp.exp(-jnp.abs(log_lb - t2)))
    logf = jnp.where(lb_pos, la, ls)
    kg = (1.0 - lb) * _sigmoid(-fz)
    ey = jnp.exp(_m01m(K['mhg'], logf, 2))
    eg_h = ey[J * L:(J + 1) * L]
    qs = [hq] + [hq * ey[j * L:(j + 1) * L] for j in range(J)]
    ks = [kg] + [kg * ey[j * L:(j + 1) * L] for j in range(J)]
    qe = hq * eg_h
    kwr = kg * (ey[(J + 1) * L:(J + 2) * L] * rowvalid)

    def hg_task(p):
        a_p = None
        for i in range(J + 1):
            t_i = K['hmaskp'][i] * _mm(pair_c(qs[i], p), stack(pair_c(ks[i], p)), _NT)
            a_p = t_i if a_p is None else a_p + t_i
            yield
        st_p = st['hg'][p]
        res['o_hg'][p] = _mm(pair_c(qe, p), st_p, _NT) + _mm(a_p, stack(pair_c(hv, p)))
        yield
        res['hg'][p] = (st_p * pair_c(eg_h[nv - 1:nv, :], p)
                        + bdp * _mm(pair_c(hv, p), pair_c(kwr, p), _TN))
        yield

    xbc = _conv_silu(ext_s, P['cws'], P['cbs'], L)
    xs, bs, cc = xbc[:, 0:256], xbc[:, 256:384], xbc[:, 384:512]
    vs = xs * dt_c
    bsb = bs.astype(BF16)
    bs4 = jnp.concatenate([bsb] * 4, axis=0)

    def ssd_task():
        o_intra = []
        for g in range(SSD_G):
            cbw = _mm(cc * K['pm2'][g:g + 1, :], bs4, _NT)
            d = s_r[:, g * 4 * L:(g + 1) * 4 * L] - row_s[:, g * 4 * L:(g + 1) * 4 * L]
            dec = jnp.where(tril4, jnp.exp(jnp.where(tril4, d, 0.0)), 0.0)
            vb = pair_c(vs, g).astype(BF16)
            o_intra.append(_mm(cbw * dec, jnp.concatenate([vb] * 4, axis=0) * K['stk4s']))
            yield
        s_all = st['ssd']
        res['o_ssd'] = jnp.exp(s_c) * _mm(cc, s_all) + jnp.concatenate(o_intra, axis=1)
        res['ssd'] = (s_all * jnp.exp(s_c[nv - 1:nv, :])
                      + K['gbd'] * _mm(bs * rowvalid, vs * jnp.exp(rs_c), _TN))
        yield

    mq = pm[:, 2816:3072] * (HEAD_D ** -0.5)
    mk = pm[:, 3072:3328]
    mv = pm[:, 3328:3584]

    def ml_task(p, delay):
        for _ in range(delay):
            yield
        qt, kt, vt = pair_c(mq, p), pair_c(mk, p), pair_c(mv, p)
        bm_r, bm_c = pair_r(m_r, p), pair_c(m_c, p)
        mp0 = st['mlm'][:, 2 * p:2 * p + 1]
        mp1 = st['mlm'][:, 2 * p + 1:2 * p + 2]
        logw = jnp.where(tril2, bm_r + pair_r(row_m, p), NEG)
        mx0 = jnp.max(jnp.where(upper_r, NEG, logw), axis=-1, keepdims=True)
        mx1 = jnp.max(jnp.where(upper_r, logw, NEG), axis=-1, keepdims=True)
        l0_r = bm_r + jnp.where(upper_r, mp1, mp0)
        l0_c = bm_c + jnp.where(upper_c, mp1, mp0)
        mt_r = jnp.maximum(l0_r, jnp.where(upper_r, mx1, mx0))
        mt_c = jnp.maximum(l0_c, jnp.where(upper_c, mx1, mx0))
        w = jnp.where(tril2, jnp.exp(logw - mt_r), 0.0)
        w0 = jnp.exp(l0_c - mt_c)
        qk = _mm(qt, stack(kt), _NT) * w
        yield
        c_p = st['mlc'][p]
        n_p = st['mln'][p]
        num = w0 * _mm(qt, c_p) + _mm(qk, stack(vt))
        den = w0 * _mm01(qt * n_p, bdp, 2) + _mm01(qk, stkp, 2)
        res['o_ml'][p] = num * (1.0 / jnp.maximum(jnp.abs(den), jnp.exp(-mt_c)))
        yield
        m_l = mt_c[nv - 1:nv, :]
        wl0 = jnp.exp(bm_c[nv - 1:nv, :] + jnp.where(upper_c, mp1, mp0) - m_l)
        kwl = kt * (jnp.exp(pair_c(rm_c, p) + pair_c(i_c, p) - m_l) * rowvalid)
        res['mlc'][p] = c_p * wl0 + bdp * _mm(kwl, vt, _TN)
        res['mln'][p] = n_p * wl0 + jnp.sum(kwl, axis=0, keepdims=True)
        res['mlm'][2 * p] = m_l[:, 0:1]
        res['mlm'][2 * p + 1] = m_l[:, HEAD_D:HEAD_D + 1]
        yield

    def finish():
        o_gdn = jnp.concatenate(res['o_gdn'], axis=1)
        o_hg = jnp.concatenate(res['o_hg'], axis=1)
        hh = jnp.concatenate(res['o_ml'], axis=1)
        ms = _mm(jnp.concatenate([o_gdn * o_gdn, o_hg * o_hg, hh * hh], axis=0), K['bd64']) * (1.0 / HEAD_D)
        out_a = o_gdn * lax.rsqrt(ms[0:L] + EPS) * P['vec'][0:1, :] * _silu(pm[:, 768:1024])
        out_b = o_hg * lax.rsqrt(ms[L:2 * L] + EPS) * P['vec'][1:2, :] * _silu(pm[:, 1792:2048])
        out_d = hh * lax.rsqrt(ms[2 * L:3 * L] + EPS) * P['vec'][3:4, :] * _sigmoid(pm[:, 3584:3840])
        ys = (res['o_ssd'] + P['vec'][4:5, :] * xs) * _silu(pm[:, 2048:2304])
        out_c = ys * lax.rsqrt(_mm(ys * ys, K['bd128']) * (1.0 / (2 * HEAD_D)) + EPS) * P['vec'][2:3, :]
        branches = jnp.concatenate([out_a, out_b, out_c, out_d], axis=1)
        new = dict(gdn=res['gdn'], hg=res['hg'], ssd=res['ssd'], mlc=res['mlc'], mln=res['mln'], mlm=res['mlm'])
        return branches, new

    tasks = ([gdn_task(p) for p in range(2)] + [hg_task(p) for p in range(2)]
             + [ssd_task()] + [ml_task(p, 3 + 2 * p) for p in range(2)])
    return tasks, finish


def _mixer_kernel(L, nv, J, layer, bb, offs, *refs):
    it = iter(refs)
    pm_ref = next(it)
    gdn0, cg0, hg0, ssd0, cs0, mc0, mn0, mm0 = (next(it) for _ in range(8))
    sp_ref, cwg_ref, cbg_ref, cws_ref, cbs_ref, vec_ref, lg_ref = (next(it) for _ in range(7))
    kref = {name: next(it) for name in _CONST_ORDER}
    br_ref = next(it)
    gdn1, cg1, hg1, ssd1, cs1, mc1, mn1, mm1 = (next(it) for _ in range(8))
    sg, sh, sc, sn, ss, extg, exts = (next(it) for _ in range(7))

    c = pl.program_id(1)
    nc = pl.num_programs(1)
    n_sub = SSD_H // SSD_G

    @pl.when(c == 0)
    def _init():
        sg[...] = jnp.zeros_like(sg)
        sh[...] = jnp.zeros_like(sh)
        sc[...] = jnp.zeros_like(sc)
        ss[...] = jnp.zeros_like(ss)
        for s in range(bb):
            for h in range(N_HEAD):
                p, lo = h // 2, (h % 2) * HEAD_D
                sg[s, p, lo:lo + HEAD_D, lo:lo + HEAD_D] = gdn0[s, h]
                sh[s, p, lo:lo + HEAD_D, lo:lo + HEAD_D] = hg0[s, h].T
                sc[s, p, lo:lo + HEAD_D, lo:lo + HEAD_D] = mc0[s, h]
                sn[s, p, :, lo:lo + HEAD_D] = mn0[s, h:h + 1, :]
            for h in range(SSD_H):
                g = h // n_sub
                ss[s, g * SSD_N:(g + 1) * SSD_N, h * SSD_P:(h + 1) * SSD_P] = ssd0[s, h]
        mm1[...] = mm0[...]
        extg[:, 0:8, :] = jnp.zeros((bb, 8, GDN_CONV_W), F32)
        exts[:, 0:8, :] = jnp.zeros((bb, 8, SSD_CONV_W), F32)
        extg[:, 5:8, :] = cg0[...]
        exts[:, 5:8, :] = cs0[...]

    P = dict(sp=sp_ref[...], cwg=cwg_ref[...], cbg=cbg_ref[...], cws=cws_ref[...], cbs=cbs_ref[...],
             vec=vec_ref[...], lg=lg_ref[...])
    K = {name: r[...] for name, r in kref.items()}
    tasks, finishers = [], []
    for s in range(bb):
        pm = pm_ref[s]
        extg[s, 8:8 + L, :] = pm[:, 0:GDN_CONV_W]
        exts[s, 8:8 + L, :] = pm[:, SSD_XBC_OFF:SSD_XBC_OFF + SSD_CONV_W]
        st = dict(gdn=sg[s], hg=sh[s], ssd=ss[s], mlc=sc[s], mln=sn[s], mlm=mm1[s])
        t, fin = _mixer_chunk(pm, extg.at[s], exts.at[s], st, P, K, offs, L, nv, J, layer)
        tasks.append(t)
        finishers.append(fin)
    _run_interleaved([t for group in zip(*tasks) for t in group])
    for s in range(bb):
        branches, new = finishers[s]()
        br_ref[s] = branches.astype(br_ref.dtype)
        for p in range(2):
            sg[s, p] = new['gdn'][p]
            sh[s, p] = new['hg'][p]
            sc[s, p] = new['mlc'][p]
            sn[s, p] = new['mln'][p]
        for h in range(N_HEAD):
            mm1[s, :, h:h + 1] = new['mlm'][h]
        ss[s] = new['ssd']
        tail_g = extg[s, 8 + nv - 3:8 + nv, :]
        tail_s = exts[s, 8 + nv - 3:8 + nv, :]
        extg[s, 5:8, :] = tail_g
        exts[s, 5:8, :] = tail_s
        cg1[s] = tail_g
        cs1[s] = tail_s

    @pl.when(c == nc - 1)
    def _fin():
        for s in range(bb):
            for h in range(N_HEAD):
                p, lo = h // 2, (h % 2) * HEAD_D
                gdn1[s, h] = sg[s, p, lo:lo + HEAD_D, lo:lo + HEAD_D]
                hg1[s, h] = sh[s, p, lo:lo + HEAD_D, lo:lo + HEAD_D].T
                mc1[s, h] = sc[s, p, lo:lo + HEAD_D, lo:lo + HEAD_D]
                mn1[s, h:h + 1, :] = sn[s, p, :, lo:lo + HEAD_D]
            for h in range(SSD_H):
                g = h // n_sub
                ssd1[s, h] = ss[s, g * SSD_N:(g + 1) * SSD_N, h * SSD_P:(h + 1) * SSD_P]


def _full_spec(a):
    nd = a.ndim
    return pl.BlockSpec(a.shape, lambda b, c, _nd=nd: (0,) * _nd)


def _mixer_call(proj, states, params, L, nv, layer, bb):
    bg, t, _ = proj.shape
    nchunk = t // L
    assert bg % bb == 0
    consts, offs, J = _mixer_consts(L, nv)
    const_arrays = [jnp.asarray(consts[n], BF16 if n in _BF16_CONSTS else F32) for n in _CONST_ORDER]
    gdn0, cg0, hg0, ssd0, cs0, mc0, mn0, mm0 = states
    mm0 = mm0.reshape(bg, 1, N_HEAD)
    st_in = [gdn0, cg0, hg0, ssd0, cs0, mc0, mn0, mm0]

    def st_spec(a):
        nd = a.ndim
        return pl.BlockSpec((bb,) + a.shape[1:], lambda b, c, _nd=nd: (b,) + (0,) * (_nd - 1))

    in_specs = ([pl.BlockSpec((bb, L, MIX_W), lambda b, c: (b, c, 0))]
                + [st_spec(a) for a in st_in]
                + [_full_spec(a) for a in params]
                + [_full_spec(a) for a in const_arrays])
    out_shape = ([jax.ShapeDtypeStruct((bg, t, N_BRANCH * BRANCH_W), BF16)]
                 + [jax.ShapeDtypeStruct(a.shape, F32) for a in st_in])
    out_specs = ([pl.BlockSpec((bb, L, N_BRANCH * BRANCH_W), lambda b, c: (b, c, 0))]
                 + [st_spec(a) for a in st_in])
    pair = (bb, 2, LANE, LANE)
    scratch = [pltpu.VMEM(pair, F32), pltpu.VMEM(pair, F32), pltpu.VMEM(pair, F32),
               pltpu.VMEM((bb, 2, 1, LANE), F32), pltpu.VMEM((bb, SSD_G * SSD_N, SSD_H * SSD_P), F32),
               pltpu.VMEM((bb, 8 + L, GDN_CONV_W), F32), pltpu.VMEM((bb, 8 + L, SSD_CONV_W), F32)]
    outs = pl.pallas_call(
        functools.partial(_mixer_kernel, L, nv, J, layer, bb, offs),
        grid=(bg // bb, nchunk),
        in_specs=in_specs, out_specs=out_specs, out_shape=out_shape, scratch_shapes=scratch,
        compiler_params=pltpu.CompilerParams(dimension_semantics=("parallel", "arbitrary"),
                                             vmem_limit_bytes=VMEM_LIMIT),
        name=f"mixer_L{L}",
    )(proj, *st_in, *params, *const_arrays)
    br = outs[0]
    new = list(outs[1:])
    new[7] = new[7].reshape(bg, N_HEAD)
    return br, new


def _ada_kernel(c_ref, w_ref, b_ref, o_ref):
    cs = _silu(c_ref[...])
    o_ref[0] = _mm(cs, w_ref[0]) + b_ref[0]


def _ada_call(c_all, ada_w, ada_b):
    rows = c_all.shape[0]
    n = ada_w.shape[-1]
    tn = 1536
    return pl.pallas_call(
        _ada_kernel,
        grid=(DEPTH, n // tn),
        in_specs=[pl.BlockSpec((rows, D_MODEL), lambda l, j: (0, 0)),
                  pl.BlockSpec((1, D_MODEL, tn), lambda l, j: (l, 0, j)),
                  pl.BlockSpec((1, 1, tn), lambda l, j: (l, 0, j))],
        out_specs=pl.BlockSpec((1, rows, tn), lambda l, j: (l, 0, j)),
        out_shape=jax.ShapeDtypeStruct((DEPTH, rows, n), F32),
        compiler_params=pltpu.CompilerParams(dimension_semantics=("arbitrary", "arbitrary"),
                                             vmem_limit_bytes=VMEM_LIMIT),
        name="ada",
    )(c_all, ada_w, ada_b.reshape(DEPTH, 1, n))


def _rms_mod(x, nw, sc, sh):
    ms = jnp.mean(x * x, axis=-1, keepdims=True)
    return x * lax.rsqrt(ms + EPS) * nw * (1.0 + sc) + sh


def _inproj_kernel(x_ref, nw_ref, sc_ref, sh_ref, w_ref, o_ref):
    h = _rms_mod(x_ref[0], nw_ref[...], sc_ref[0], sh_ref[0])
    o_ref[0] = _mm(h, w_ref[...])


def _mod_spec(a, tm):
    if a.shape[1] == 1:
        return pl.BlockSpec((1, 1, D_MODEL), lambda g, i: (g, 0, 0))
    return pl.BlockSpec((1, tm, D_MODEL), lambda g, i: (g, i, 0))


def _resident(shape):
    nd = len(shape)
    return pl.BlockSpec(shape, lambda g, i, _nd=nd: (0,) * _nd, pipeline_mode=pl.Buffered(1))


def _inproj_call(x, nw, sc, sh, w, tm):
    g, r, _ = x.shape
    return pl.pallas_call(
        _inproj_kernel,
        grid=(g, r // tm),
        in_specs=[pl.BlockSpec((1, tm, D_MODEL), lambda g_, i: (g_, i, 0)),
                  _resident((1, D_MODEL)), _mod_spec(sc, tm), _mod_spec(sh, tm),
                  _resident((D_MODEL, PROJ_W))],
        out_specs=pl.BlockSpec((1, tm, PROJ_W), lambda g_, i: (g_, i, 0)),
        out_shape=jax.ShapeDtypeStruct((g, r, PROJ_W), F32),
        compiler_params=pltpu.CompilerParams(dimension_semantics=("arbitrary", "arbitrary"),
                                             vmem_limit_bytes=VMEM_LIMIT),
        name="inproj",
    )(x, nw, sc, sh, w)


def _merge_kernel(br_ref, gate_ref, x_ref, gt_ref, wb_ref, wo_ref, o_ref):
    br = br_ref[0]
    merged = None
    for n in range(N_BRANCH):
        up = _mm(br[:, n * BRANCH_W:(n + 1) * BRANCH_W], wb_ref[n])
        t = _sigmoid(gate_ref[0, :, n * D_MODEL:(n + 1) * D_MODEL]) * up
        merged = t if merged is None else merged + t
    y = _mm(merged, wo_ref[...])
    o_ref[0] = x_ref[0] + gt_ref[0] * y


def _merge_call(br, proj, x, gt, wb, wo, tm):
    g, r, _ = x.shape
    return pl.pallas_call(
        _merge_kernel,
        grid=(g, r // tm),
        in_specs=[pl.BlockSpec((1, tm, N_BRANCH * BRANCH_W), lambda g_, i: (g_, i, 0)),
                  pl.BlockSpec((1, tm, GATE_W), lambda g_, i: (g_, i, 1)),
                  pl.BlockSpec((1, tm, D_MODEL), lambda g_, i: (g_, i, 0)),
                  _mod_spec(gt, tm),
                  _resident((N_BRANCH, BRANCH_W, D_MODEL)), _resident((D_MODEL, D_MODEL))],
        out_specs=pl.BlockSpec((1, tm, D_MODEL), lambda g_, i: (g_, i, 0)),
        out_shape=jax.ShapeDtypeStruct((g, r, D_MODEL), F32),
        compiler_params=pltpu.CompilerParams(dimension_semantics=("arbitrary", "arbitrary"),
                                             vmem_limit_bytes=VMEM_LIMIT),
        name="merge",
    )(br, proj, x, gt, wb, wo)


def _ffn_kernel(final, x_ref, nw_ref, sc_ref, sh_ref, gt_ref, w1_ref, w2_ref, fw_ref, o_ref):
    x = x_ref[0]
    h = _rms_mod(x, nw_ref[...], sc_ref[0], sh_ref[0]).astype(BF16)
    a = _mm(h, w1_ref[:, 0:D_FF])
    b = _mm(h, w1_ref[:, D_FF:2 * D_FF])
    y = _mm(_silu(a) * b, w2_ref[...])
    x2 = x + gt_ref[0] * y
    if final:
        ms = jnp.mean(x2 * x2, axis=-1, keepdims=True)
        x2 = x2 * lax.rsqrt(ms + EPS) * fw_ref[...]
    o_ref[0] = x2


def _ffn_call(x, nw, sc, sh, gt, w1, w2, fw, tm, final):
    g, r, _ = x.shape
    return pl.pallas_call(
        functools.partial(_ffn_kernel, final),
        grid=(g, r // tm),
        in_specs=[pl.BlockSpec((1, tm, D_MODEL), lambda g_, i: (g_, i, 0)),
                  _resident((1, D_MODEL)), _mod_spec(sc, tm), _mod_spec(sh, tm), _mod_spec(gt, tm),
                  _resident((D_MODEL, 2 * D_FF)), _resident((D_FF, D_MODEL)), _resident((1, D_MODEL))],
        out_specs=pl.BlockSpec((1, tm, D_MODEL), lambda g_, i: (g_, i, 0)),
        out_shape=jax.ShapeDtypeStruct((g, r, D_MODEL), F32),
        compiler_params=pltpu.CompilerParams(dimension_semantics=("arbitrary", "arbitrary"),
                                             vmem_limit_bytes=VMEM_LIMIT),
        name="ffn",
    )(x, nw, sc, sh, gt, w1, w2, fw)


def _reorder_w_in(w):
    small = jnp.concatenate([w[:, 1024:1032], w[:, 2824:2832], w[:, 3856:3864]], axis=1)
    pad = jnp.zeros((w.shape[0], MIX_W - SMALL_OFF - small.shape[1]), w.dtype)
    return jnp.concatenate([w[:, 0:1024], w[:, 1032:2824], w[:, 2832:3856], small, pad, w[:, 3864:7960]],
                           axis=1).astype(BF16)


def _small_params(gdn_dt_bias, ssd_dt_bias, ml_b_i, ml_b_f, gdn_a_log, ssd_a_log):
    z4 = jnp.zeros((4,), F32)
    bias = jnp.concatenate([z4, gdn_dt_bias, ssd_dt_bias, ml_b_i, ml_b_f, jnp.zeros((LANE - 24,), F32)])
    alog = jnp.concatenate([z4, gdn_a_log, ssd_a_log, jnp.zeros((LANE - 16,), F32)])
    return jnp.concatenate([bias[None], alog[None], jnp.zeros((6, LANE), F32)], axis=0)


def _layer_params(l, gdn_conv_w, gdn_conv_b, gdn_a_log, gdn_dt_bias, gdn_norm_w, hg_lb_logits, hg_norm_w,
                  ssd_conv_w, ssd_conv_b, ssd_a_log, ssd_dt_bias, ssd_d, ssd_norm_w, ml_b_i, ml_b_f, ml_norm_w):
    sp = _small_params(gdn_dt_bias[l], ssd_dt_bias[l], ml_b_i[l], ml_b_f[l], gdn_a_log[l], ssd_a_log[l])
    vec = jnp.stack([jnp.tile(gdn_norm_w[l], N_HEAD), jnp.tile(hg_norm_w[l], N_HEAD), ssd_norm_w[l],
                     jnp.tile(ml_norm_w[l], N_HEAD), jnp.repeat(ssd_d[l], SSD_P)]
                    + [jnp.zeros((256,), F32)] * 3)
    return [sp, gdn_conv_w[l], gdn_conv_b[l][None], ssd_conv_w[l], ssd_conv_b[l][None], vec,
            hg_lb_logits.astype(F32)]


def _trunk(x, mods, states, L, nv, tm, bb, W, mixer_params, norm1_w, norm2_w, final_norm_w, per_token):
    bg, t, _ = x.shape
    tm = min(tm, bg * t if per_token else t)
    new_states = [[] for _ in range(8)]
    for l in range(DEPTH):
        sh1, sc1, gt1, sh2, sc2, gt2 = jnp.split(mods[l], 6, axis=-1)
        if per_token:
            xd = x.reshape(1, bg * t, D_MODEL)
            m = [jnp.repeat(a, t, axis=0)[None] for a in (sh1, sc1, gt1, sh2, sc2, gt2)]
        else:
            xd = x
            m = [a[:, None, :] for a in (sh1, sc1, gt1, sh2, sc2, gt2)]
        sh1, sc1, gt1, sh2, sc2, gt2 = m
        proj = _inproj_call(xd, norm1_w[l][None], sc1, sh1, W['w_in'][l], tm)
        br, st = _mixer_call(proj.reshape(bg, t, PROJ_W), [s[l] for s in states], mixer_params[l], L, nv, l, bb)
        x1 = _merge_call(br.reshape(xd.shape[0], xd.shape[1], -1), proj, xd, gt1, W['w_branch'][l], W['w_out'][l], tm)
        x2 = _ffn_call(x1, norm2_w[l][None], sc2, sh2, gt2, W['ffn_w_in'][l], W['ffn_w_out'][l],
                       final_norm_w[None], tm, final=(l == DEPTH - 1))
        x = x2.reshape(bg, t, D_MODEL)
        for lst, s in zip(new_states, st):
            lst.append(s)
    return x, [jnp.stack(s) for s in new_states]


def kernel(x_prompt, x_sample, c_prompt, c_sample, state_gdn, state_gdn_conv, state_hgrn, state_ssd, state_ssd_conv, state_mlstm_c, state_mlstm_n, state_mlstm_m, ada_w, ada_b, norm1_w, norm2_w, w_in, gdn_conv_w, gdn_conv_b, gdn_a_log, gdn_dt_bias, gdn_norm_w, hg_lb_logits, hg_norm_w, ssd_conv_w, ssd_conv_b, ssd_a_log, ssd_dt_bias, ssd_d, ssd_norm_w, ml_b_i, ml_b_f, ml_norm_w, w_branch, w_out, ffn_w_in, ffn_w_out, final_norm_w):
    bp, tp, _ = x_prompt.shape
    bs, ts, _ = x_sample.shape
    ls = 8
    assert tp % CHUNK == 0 and ts <= ls

    W = dict(w_in=jnp.stack([_reorder_w_in(w_in[l]) for l in range(DEPTH)]),
             w_branch=w_branch.astype(BF16), w_out=w_out.astype(BF16),
             ffn_w_in=ffn_w_in.astype(BF16), ffn_w_out=ffn_w_out.astype(BF16))
    mixer_params = [_layer_params(l, gdn_conv_w, gdn_conv_b, gdn_a_log, gdn_dt_bias, gdn_norm_w, hg_lb_logits,
                                  hg_norm_w, ssd_conv_w, ssd_conv_b, ssd_a_log, ssd_dt_bias, ssd_d, ssd_norm_w,
                                  ml_b_i, ml_b_f, ml_norm_w) for l in range(DEPTH)]

    mods = _ada_call(jnp.concatenate([c_prompt, c_sample], axis=0), ada_w, ada_b)
    sample_states = [state_gdn, state_gdn_conv, state_hgrn, state_ssd, state_ssd_conv,
                     state_mlstm_c, state_mlstm_n, state_mlstm_m]
    prompt_states = [jnp.zeros((DEPTH, bp) + s.shape[2:], F32) for s in sample_states]

    y_p, new_p = _trunk(x_prompt, mods[:, :bp], prompt_states, CHUNK, CHUNK, 256, 2, W, mixer_params,
                        norm1_w, norm2_w, final_norm_w, per_token=False)
    xs_pad = jnp.pad(x_sample, ((0, 0), (0, ls - ts), (0, 0)))
    y_s, new_s = _trunk(xs_pad, mods[:, bp:], sample_states, ls, ts, 256, 2, W, mixer_params,
                        norm1_w, norm2_w, final_norm_w, per_token=True)
    return (y_p, y_s[:, :ts]) + tuple(new_p) + tuple(new_s)
```

```python
import functools
import math

import numpy as np
import jax
import jax.numpy as jnp
from jax import lax
from jax.experimental import pallas as pl
from jax.experimental.pallas import tpu as pltpu

F32 = jnp.float32
BF16 = jnp.bfloat16

D_MODEL = 1024
DEPTH = 2
N_BRANCH = 4
BRANCH_W = 256
N_HEAD = 4
HEAD_D = 64
SSD_H = 8
SSD_P = 32
SSD_N = 64
SSD_G = 2
CONV_K = 4
CHUNK = 64
D_FF = 2816
EPS = 1e-6
NEG = -1e30
GDN_CONV_W = 768
SSD_CONV_W = 512
SSD_XBC_OFF = 2304
MIX_W = 4096
GATE_W = 4096
PROJ_W = MIX_W + GATE_W
SMALL_OFF = 3840
LANE = 128
VMEM_LIMIT = 56 * 1024 * 1024

ST_BETA, ST_GA, ST_DT, ST_MI, ST_MF = 0, 4, 8, 16, 20

_NN = (((1,), (0,)), ((), ()))
_NT = (((1,), (1,)), ((), ()))
_TN = (((0,), (0,)), ((), ()))


def _mm(a, b, dims=_NN):
    return lax.dot_general(a.astype(BF16), b.astype(BF16), dims, preferred_element_type=F32)


def _split(x, n):
    parts, r = [], x
    for i in range(n):
        p = r.astype(BF16)
        parts.append(p)
        if i < n - 1:
            r = r - p.astype(F32)
    return parts


def _mm01(x, m, n):
    out = None
    for p in _split(x, n):
        t = lax.dot_general(p, m.astype(BF16), _NN, preferred_element_type=F32)
        out = t if out is None else out + t
    return out


def _m01m(m, x, n):
    out = None
    for p in _split(x, n):
        t = lax.dot_general(m, p, _NN, preferred_element_type=F32)
        out = t if out is None else out + t
    return out


def _sigmoid(x):
    return jax.nn.sigmoid(x)


def _silu(x):
    return x * jax.nn.sigmoid(x)


def _softplus(x):
    return jnp.maximum(x, 0.0) + jnp.log(1.0 + jnp.exp(-jnp.abs(x)))


def _logsig(x):
    return jnp.minimum(x, 0.0) - jnp.log(1.0 + jnp.exp(-jnp.abs(x)))


def _expand(base, heads, width):
    e = np.zeros((LANE, heads * width), np.float32)
    for h in range(heads):
        e[base + h, h * width:(h + 1) * width] = 1.0
    return e


def _cat_segments(segs):
    cols, off, pos, seen = [], {}, 0, {}
    for name, m in segs:
        key = m.tobytes() + bytes(str(m.shape), 'ascii')
        if key in seen:
            off[name] = seen[key]
            continue
        w = m.shape[1]
        wp = -(-w // LANE) * LANE
        mp = np.zeros((m.shape[0], wp), np.float32)
        mp[:, :w] = m
        cols.append(mp)
        off[name] = seen[key] = (pos, w)
        pos += wp
    return np.concatenate(cols, axis=1), off


@functools.lru_cache(maxsize=None)
def _mixer_consts(L, nv):
    J = int(round(math.log2(L)))
    assert 1 << J == L
    f = np.float32
    r = np.arange(L)
    tri = (r[None, :] <= r[:, None])
    rev = (r[:, None] < r[None, :]) & (r[None, :] <= nv - 1)
    strict = (r[None, :] < r[:, None])
    eye = np.eye(L, dtype=bool)
    lvl, ab = [], []
    for j in range(J):
        bnd = ((r >> (j + 1)) << (j + 1)) + (1 << j) - 1
        low = ((r >> j) & 1) == 1
        a = low[:, None] & (bnd[:, None] < r[None, :]) & (r[None, :] <= r[:, None])
        b = (~low)[:, None] & (r[:, None] < r[None, :]) & (r[None, :] <= bnd[:, None])
        ab.append(a | b)
        same = (r[:, None] >> (j + 1)) == (r[None, :] >> (j + 1))
        lvl.append(same & low[:, None] & (~low)[None, :])

    def tile(m, n):
        return np.tile(m.astype(f), (1, n))

    pr = np.repeat(np.arange(2), L)
    lr2 = np.tile(r, 2)
    ch = np.arange(LANE)
    ch2 = np.arange(2 * LANE)
    c = dict(
        trirev=np.concatenate([tri, rev], axis=0).astype(f),
        mhg=np.concatenate(ab + [tri, rev], axis=0).astype(f),
        ones_ll=np.ones((L, L), f), tri=tri.astype(f),
        tril2=tile(tri, 2), strict2=tile(strict, 2), tril4=tile(tri, 4),
        eye4=tile(eye, 4), supper16=tile(strict, 16),
        e_dr=np.concatenate([_expand(ST_GA, 4, L), _expand(ST_DT, SSD_H, L), _expand(ST_MF, 4, L)], axis=1),
        hmaskp=np.stack([tile(eye, 2)] + [tile(m, 2) for m in lvl]),
        eyep=np.eye(2 * L, dtype=f),
        lvlp=np.stack([(pr[:, None] == pr[None, :]) & m[lr2[:, None], lr2[None, :]] for m in lvl]).astype(f),
        stkp=(pr[:, None] == ch[None, :] // HEAD_D).astype(f),
        bdp=(ch[:, None] // HEAD_D == ch[None, :] // HEAD_D).astype(f),
        pm2=np.stack([ch < HEAD_D, ch >= HEAD_D]).astype(f),
        stk4s=(np.repeat(np.arange(4), L)[:, None] == ch[None, :] // SSD_P).astype(f),
        gbd=(ch[:, None] // SSD_N == ch2[None, :] // LANE).astype(f),
        rowvalid=(r[:, None] <= nv - 1).astype(f) * np.ones((1, LANE), f),
        bd64=(ch2[:, None] // 64 == ch2[None, :] // 64).astype(f),
        bd128=(ch2[:, None] // 128 == ch2[None, :] // 128).astype(f),
    )
    c['e_act'], off_act = _cat_segments([('b_r', _expand(ST_BETA, 4, L)), ('b_c', _expand(ST_BETA, 4, HEAD_D)),
                                         ('dt_c', _expand(ST_DT, SSD_H, SSD_P)),
                                         ('i_r', _expand(ST_MI, 4, L)), ('i_c', _expand(ST_MI, 4, HEAD_D))])
    c['e_cs'], off_cs = _cat_segments([('g_c', _expand(ST_GA, 4, HEAD_D)), ('s_c', _expand(ST_DT, SSD_H, SSD_P)),
                                       ('m_r', _expand(ST_MF, 4, L)), ('m_c', _expand(ST_MF, 4, HEAD_D))])
    c['e_rev'], off_rev = _cat_segments([('g_c', _expand(ST_GA, 4, HEAD_D)), ('s_c', _expand(ST_DT, SSD_H, SSD_P)),
                                         ('m_c', _expand(ST_MF, 4, HEAD_D))])
    return c, dict(act=off_act, cs=off_cs, rev=off_rev), J


_CONST_ORDER = ('trirev', 'mhg', 'ones_ll', 'tri', 'tril2', 'strict2', 'tril4', 'eye4', 'supper16', 'e_dr', 'hmaskp',
                'eyep', 'lvlp', 'stkp', 'bdp', 'pm2', 'stk4s', 'gbd', 'rowvalid', 'bd64', 'bd128', 'e_act', 'e_cs',
                'e_rev')
_BF16_CONSTS = ('trirev', 'mhg', 'ones_ll', 'tri', 'supper16', 'e_dr', 'lvlp', 'stkp', 'stk4s', 'bd64', 'bd128',
                'e_act', 'e_cs', 'e_rev')


def _conv_silu(ext, w, b, L):
    y = b
    for i in range(CONV_K):
        y = y + ext[5 + i:5 + i + L, :] * w[i:i + 1, :]
    return _silu(y)


def _run_interleaved(tasks):
    tasks = list(tasks)
    while tasks:
        alive = []
        for t in tasks:
            try:
                next(t)
                alive.append(t)
            except StopIteration:
                pass
        tasks = alive


def _mixer_chunk(pm, ext_g, ext_s, st, P, K, offs, L, nv, J, layer):
    lane = lax.broadcasted_iota(jnp.int32, (1, LANE), 1)
    rowvalid = K['rowvalid'][:, 0:1]
    tril2 = K['tril2'] > 0.0
    tril4 = K['tril4'] > 0.0
    stkp, bdp = K['stkp'], K['bdp']
    upper_c = lane >= HEAD_D
    upper_r = lax.broadcasted_iota(jnp.int32, (1, 2 * L), 1) >= L

    def seg(x, which, name):
        o, w = offs[which][name]
        return x[:, o:o + w]

    def pair_c(x, p):
        return x[:, p * LANE:(p + 1) * LANE]

    def pair_r(x, p):
        return x[:, p * 2 * L:(p + 1) * 2 * L]

    def stack(x):
        xb = x.astype(BF16)
        return jnp.concatenate([xb, xb], axis=0) * stkp

    z = pm[:, SMALL_OFF:SMALL_OFF + LANE] + P['sp'][0:1, :]
    act = jnp.where(lane < ST_GA, _sigmoid(z),
                    jnp.where(lane < ST_MI, _softplus(z), jnp.where(lane < ST_MF, z, _logsig(z))))
    neg_a = -jnp.exp(P['sp'][1:2, :])
    dec_in = jnp.where((lane >= ST_GA) & (lane < ST_MI), neg_a * act,
                       jnp.where((lane >= ST_MF) & (lane < ST_MF + 4), act, 0.0))
    cr = _m01m(K['trirev'], dec_in, 2)
    cs, rev = cr[:L], cr[L:]
    ea = _mm01(act, K['e_act'], 2)
    ec = _mm01(cs, K['e_cs'], 2)
    er = _mm01(rev, K['e_rev'], 2)
    m_r = seg(ec, 'cs', 'm_r')
    g_c, s_c, m_c = seg(ec, 'cs', 'g_c'), seg(ec, 'cs', 's_c'), seg(ec, 'cs', 'm_c')
    b_r, b_c, dt_c = seg(ea, 'act', 'b_r'), seg(ea, 'act', 'b_c'), seg(ea, 'act', 'dt_c')
    i_r, i_c = seg(ea, 'act', 'i_r'), seg(ea, 'act', 'i_c')
    rg_c, rs_c, rm_c = seg(er, 'rev', 'g_c'), seg(er, 'rev', 's_c'), seg(er, 'rev', 'm_c')
    ym = _mm(dec_in, K['e_dr']).astype(BF16) * K['supper16']
    d_all = lax.dot_general(K['tri'], ym, _NN, preferred_element_type=F32)
    d_g, d_s, d_m = d_all[:, 0:4 * L], d_all[:, 4 * L:12 * L], d_all[:, 12 * L:16 * L]
    row_i = _m01m(K['ones_ll'], i_r * K['eye4'], 2)

    res = dict(o_gdn=[None, None], gdn=[None, None], o_hg=[None, None], hg=[None, None],
               o_ml=[None, None], mlc=[None, None], mln=[None, None], mlm=[None] * N_HEAD)

    qkv = _conv_silu(ext_g, P['cwg'], P['cbg'], L)
    q, k, v = qkv[:, 0:256], qkv[:, 256:512], qkv[:, 512:768]
    ss = _mm(jnp.concatenate([q * q, k * k], axis=0), K['bd64'])
    q = q * (lax.rsqrt(ss[:L] + EPS) * (HEAD_D ** -0.5))
    k = k * lax.rsqrt(ss[L:] + EPS)

    def gdn_task(p):
        qt, kt, vt = pair_c(q, p), pair_c(k, p), pair_c(v, p)
        beta_c, gc = pair_c(b_c, p), pair_c(g_c, p)
        eg = jnp.exp(gc)
        dec = jnp.where(tril2, jnp.exp(jnp.where(tril2, pair_r(d_g, p), 0.0)), 0.0)
        kkqk = _mm(jnp.concatenate([kt, qt], axis=0), stack(kt), _NT)
        yield
        n = (kkqk[:L] * dec * pair_r(b_r, p) * K['strict2']).astype(BF16)
        nbd = jnp.concatenate([n, n], axis=0)
        t = K['eyep'] - (nbd * K['lvlp'][0]).astype(F32)
        for j in range(1, J):
            x = _mm(t, nbd * K['lvlp'][j])
            yield
            t = t - _mm(x, t)
            yield
        sv = _mm(t, stack(vt * beta_c))
        sk = _mm(t, stack(kt * (beta_c * eg)))
        yield
        s_p = st['gdn'][p]
        r = _mm(jnp.concatenate([sk[:L] + sk[L:], qt * eg], axis=0), s_p)
        yield
        u = sv[:L] + sv[L:] - r[:L]
        res['o_gdn'][p] = r[L:] + _mm(kkqk[L:] * dec, stack(u))
        kw = kt * (jnp.exp(pair_c(rg_c, p)) * rowvalid)
        res['gdn'][p] = s_p * jnp.exp(gc[nv - 1:nv, :]) + bdp * _mm(kw, u, _TN)
        yield

    lg = P['lg']
    mx = lg[0:1, :]
    for i in range(1, DEPTH):
        mx = jnp.maximum(mx, lg[i:i + 1, :])
    ex = [jnp.exp(lg[i:i + 1, :] - mx) for i in range(DEPTH)]
    tot = ex[0]
    for i in range(1, DEPTH):
        tot = tot + ex[i]
    sm = [e / tot for e in ex]
    cum = sm[0]
    for i in range(1, layer + 1):
        cum = cum + sm[i]
    lb = cum - sm[0]
    lb_pos = lb > 0
    log_lb = jnp.log(jnp.where(lb_pos, lb, 1.0))
    hq = _silu(pm[:, 1024:1280])
    fz = pm[:, 1280:1536]
    hv = pm[:, 1536:1792]
    ls = _logsig(fz)
    t2 = jnp.log1p(-lb) + ls
    la = jnp.maximum(log_lb, t2) + jnp.log1p(jnp.exp(-jnp.abs(log_lb - t2)))
    logf = jnp.where(lb_pos, la, ls)
    kg = (1.0 - lb) * _sigmoid(-fz)
    ey = jnp.exp(_m01m(K['mhg'], logf, 2))
    eg_h = ey[J * L:(J + 1) * L]
    qs = [hq] + [hq * ey[j * L:(j + 1) * L] for j in range(J)]
    ks = [kg] + [kg * ey[j * L:(j + 1) * L] for j in range(J)]
    qe = hq * eg_h
    kwr = kg * (ey[(J + 1) * L:(J + 2) * L] * rowvalid)

    def hg_task(p):
        a_p = None
        for i in range(J + 1):
            t_i = K['hmaskp'][i] * _mm(pair_c(qs[i], p), stack(pair_c(ks[i], p)), _NT)
            a_p = t_i if a_p is None else a_p + t_i
            yield
        st_p = st['hg'][p]
        res['o_hg'][p] = _mm(pair_c(qe, p), st_p, _NT) + _mm(a_p, stack(pair_c(hv, p)))
        yield
        res['hg'][p] = (st_p * pair_c(eg_h[nv - 1:nv, :], p)
                        + bdp * _mm(pair_c(hv, p), pair_c(kwr, p), _TN))
        yield

    xbc = _conv_silu(ext_s, P['cws'], P['cbs'], L)
    xs, bs, cc = xbc[:, 0:256], xbc[:, 256:384], xbc[:, 384:512]
    vs = xs * dt_c
    bsb = bs.astype(BF16)
    bs4 = jnp.concatenate([bsb] * 4, axis=0)

    def ssd_task():
        o_intra = []
        for g in range(SSD_G):
            cbw = _mm(cc * K['pm2'][g:g + 1, :], bs4, _NT)
            dec = jnp.where(tril4, jnp.exp(jnp.where(tril4, d_s[:, g * 4 * L:(g + 1) * 4 * L], 0.0)), 0.0)
            vb = pair_c(vs, g).astype(BF16)
            o_intra.append(_mm(cbw * dec, jnp.concatenate([vb] * 4, axis=0) * K['stk4s']))
            yield
        s_all = st['ssd']
        res['o_ssd'] = jnp.exp(s_c) * _mm(cc, s_all) + jnp.concatenate(o_intra, axis=1)
        res['ssd'] = (s_all * jnp.exp(s_c[nv - 1:nv, :])
                      + K['gbd'] * _mm(bs * rowvalid, vs * jnp.exp(rs_c), _TN))
        yield

    mq = pm[:, 2816:3072] * (HEAD_D ** -0.5)
    mk = pm[:, 3072:3328]
    mv = pm[:, 3328:3584]

    def ml_task(p, delay):
        for _ in range(delay):
            yield
        qt, kt, vt = pair_c(mq, p), pair_c(mk, p), pair_c(mv, p)
        bm_r, bm_c = pair_r(m_r, p), pair_c(m_c, p)
        mp0 = st['mlm'][:, 2 * p:2 * p + 1]
        mp1 = st['mlm'][:, 2 * p + 1:2 * p + 2]
        logw = jnp.where(tril2, pair_r(d_m, p) + pair_r(row_i, p), NEG)
        mx0 = jnp.max(jnp.where(upper_r, NEG, logw), axis=-1, keepdims=True)
        mx1 = jnp.max(jnp.where(upper_r, logw, NEG), axis=-1, keepdims=True)
        l0_r = bm_r + jnp.where(upper_r, mp1, mp0)
        l0_c = bm_c + jnp.where(upper_c, mp1, mp0)
        mt_r = jnp.maximum(l0_r, jnp.where(upper_r, mx1, mx0))
        mt_c = jnp.maximum(l0_c, jnp.where(upper_c, mx1, mx0))
        w = jnp.where(tril2, jnp.exp(logw - mt_r), 0.0)
        w0 = jnp.exp(l0_c - mt_c)
        qk = _mm(qt, stack(kt), _NT) * w
        yield
        c_p = st['mlc'][p]
        n_p = st['mln'][p]
        num = w0 * _mm(qt, c_p) + _mm(qk, stack(vt))
        den = w0 * _mm01(qt * n_p, bdp, 2) + _mm01(qk, stkp, 2)
        res['o_ml'][p] = num * (1.0 / jnp.maximum(jnp.abs(den), jnp.exp(-mt_c)))
        yield
        m_l = mt_c[nv - 1:nv, :]
        wl0 = jnp.exp(bm_c[nv - 1:nv, :] + jnp.where(upper_c, mp1, mp0) - m_l)
        kwl = kt * (jnp.exp(pair_c(rm_c, p) + pair_c(i_c, p) - m_l) * rowvalid)
        res['mlc'][p] = c_p * wl0 + bdp * _mm(kwl, vt, _TN)
        res['mln'][p] = n_p * wl0 + jnp.sum(kwl, axis=0, keepdims=True)
        res['mlm'][2 * p] = m_l[:, 0:1]
        res['mlm'][2 * p + 1] = m_l[:, HEAD_D:HEAD_D + 1]
        yield

    def finish():
        o_gdn = jnp.concatenate(res['o_gdn'], axis=1)
        o_hg = jnp.concatenate(res['o_hg'], axis=1)
        hh = jnp.concatenate(res['o_ml'], axis=1)
        ms = _mm(jnp.concatenate([o_gdn * o_gdn, o_hg * o_hg, hh * hh], axis=0), K['bd64']) * (1.0 / HEAD_D)
        out_a = o_gdn * lax.rsqrt(ms[0:L] + EPS) * P['vec'][0:1, :] * _silu(pm[:, 768:1024])
        out_b = o_hg * lax.rsqrt(ms[L:2 * L] + EPS) * P['vec'][1:2, :] * _silu(pm[:, 1792:2048])
        out_d = hh * lax.rsqrt(ms[2 * L:3 * L] + EPS) * P['vec'][3:4, :] * _sigmoid(pm[:, 3584:3840])
        ys = (res['o_ssd'] + P['vec'][4:5, :] * xs) * _silu(pm[:, 2048:2304])
        out_c = ys * lax.rsqrt(_mm(ys * ys, K['bd128']) * (1.0 / (2 * HEAD_D)) + EPS) * P['vec'][2:3, :]
        branches = jnp.concatenate([out_a, out_b, out_c, out_d], axis=1)
        new = dict(gdn=res['gdn'], hg=res['hg'], ssd=res['ssd'], mlc=res['mlc'], mln=res['mln'], mlm=res['mlm'])
        return branches, new

    tasks = ([gdn_task(p) for p in range(2)] + [hg_task(p) for p in range(2)]
             + [ssd_task()] + [ml_task(p, 3 + 2 * p) for p in range(2)])
    return tasks, finish


def _mixer_kernel(L, nv, J, layer, bb, offs, has_init, n_alias, *refs):
    it = iter(refs)
    pm_ref = next(it)
    if has_init:
        gdn0, cg0, hg0, ssd0, cs0, mc0, mn0, mm0 = (next(it) for _ in range(8))
    sp_ref, cwg_ref, cbg_ref, cws_ref, cbs_ref, vec_ref, lg_ref = (next(it) for _ in range(7))
    kref = {name: next(it) for name in _CONST_ORDER}
    for _ in range(n_alias):
        next(it)
    br_ref = next(it)
    gdn1, cg1, hg1, ssd1, cs1, mc1, mn1, mm1 = (next(it) for _ in range(8))
    sg, sh, sc, sn, ss, extg, exts = (next(it) for _ in range(7))

    c = pl.program_id(1)
    nc = pl.num_programs(1)
    n_sub = SSD_H // SSD_G

    @pl.when(c == 0)
    def _init():
        sg[...] = jnp.zeros_like(sg)
        sh[...] = jnp.zeros_like(sh)
        sc[...] = jnp.zeros_like(sc)
        ss[...] = jnp.zeros_like(ss)
        extg[:, 0:8, :] = jnp.zeros((bb, 8, GDN_CONV_W), F32)
        exts[:, 0:8, :] = jnp.zeros((bb, 8, SSD_CONV_W), F32)
        if not has_init:
            sn[...] = jnp.zeros_like(sn)
            mm1[...] = jnp.zeros_like(mm1)
            return
        for s in range(bb):
            for h in range(N_HEAD):
                p, lo = h // 2, (h % 2) * HEAD_D
                sg[s, p, lo:lo + HEAD_D, lo:lo + HEAD_D] = gdn0[s, h]
                sh[s, p, lo:lo + HEAD_D, lo:lo + HEAD_D] = hg0[s, h].T
                sc[s, p, lo:lo + HEAD_D, lo:lo + HEAD_D] = mc0[s, h]
                sn[s, p, :, lo:lo + HEAD_D] = mn0[s, h:h + 1, :]
            for h in range(SSD_H):
                g = h // n_sub
                ss[s, g * SSD_N:(g + 1) * SSD_N, h * SSD_P:(h + 1) * SSD_P] = ssd0[s, h]
        mm1[...] = mm0[...]
        extg[:, 5:8, :] = cg0[...]
        exts[:, 5:8, :] = cs0[...]

    P = dict(sp=sp_ref[...], cwg=cwg_ref[...], cbg=cbg_ref[...], cws=cws_ref[...], cbs=cbs_ref[...],
             vec=vec_ref[...], lg=lg_ref[...])
    K = {name: r[...] for name, r in kref.items()}
    tasks, finishers = [], []
    for s in range(bb):
        pm = pm_ref[s]
        extg[s, 8:8 + L, :] = pm[:, 0:GDN_CONV_W]
        exts[s, 8:8 + L, :] = pm[:, SSD_XBC_OFF:SSD_XBC_OFF + SSD_CONV_W]
        st = dict(gdn=sg[s], hg=sh[s], ssd=ss[s], mlc=sc[s], mln=sn[s], mlm=mm1[s])
        t, fin = _mixer_chunk(pm, extg.at[s], exts.at[s], st, P, K, offs, L, nv, J, layer)
        tasks.append(t)
        finishers.append(fin)
    _run_interleaved([t for group in zip(*tasks) for t in group])
    for s in range(bb):
        branches, new = finishers[s]()
        br_ref[s] = branches.astype(br_ref.dtype)
        for p in range(2):
            sg[s, p] = new['gdn'][p]
            sh[s, p] = new['hg'][p]
            sc[s, p] = new['mlc'][p]
            sn[s, p] = new['mln'][p]
        for h in range(N_HEAD):
            mm1[s, :, h:h + 1] = new['mlm'][h]
        ss[s] = new['ssd']
        tail_g = extg[s, 8 + nv - 3:8 + nv, :]
        tail_s = exts[s, 8 + nv - 3:8 + nv, :]
        extg[s, 5:8, :] = tail_g
        exts[s, 5:8, :] = tail_s
        cg1[s] = tail_g
        cs1[s] = tail_s

    @pl.when(c == nc - 1)
    def _fin():
        for s in range(bb):
            for h in range(N_HEAD):
                p, lo = h // 2, (h % 2) * HEAD_D
                gdn1[s, h] = sg[s, p, lo:lo + HEAD_D, lo:lo + HEAD_D]
                hg1[s, h] = sh[s, p, lo:lo + HEAD_D, lo:lo + HEAD_D].T
                mc1[s, h] = sc[s, p, lo:lo + HEAD_D, lo:lo + HEAD_D]
                mn1[s, h:h + 1, :] = sn[s, p, :, lo:lo + HEAD_D]
            for h in range(SSD_H):
                g = h // n_sub
                ssd1[s, h] = ss[s, g * SSD_N:(g + 1) * SSD_N, h * SSD_P:(h + 1) * SSD_P]


def _full_spec(a):
    nd = a.ndim
    return pl.BlockSpec(a.shape, lambda b, c, _nd=nd: (0,) * _nd)


def _mixer_call(proj, state_shapes, states_in, prev_out, params, L, nv, layer, bb):
    bg, t, _ = proj.shape
    nchunk = t // L
    assert bg % bb == 0
    consts, offs, J = _mixer_consts(L, nv)
    const_arrays = [jnp.asarray(consts[n], BF16 if n in _BF16_CONSTS else F32) for n in _CONST_ORDER]

    def st_spec(shape):
        nd = len(shape)
        return pl.BlockSpec((None, bb) + tuple(shape[2:]), lambda b, c, _nd=nd: (layer, b) + (0,) * (_nd - 2))

    has_init = states_in is not None
    n_alias = 0 if prev_out is None else len(prev_out)
    inputs = [proj] + (list(states_in) if has_init else []) + list(params) + const_arrays + list(prev_out or [])
    in_specs = ([pl.BlockSpec((bb, L, MIX_W), lambda b, c: (b, c, 0))]
                + ([st_spec(s) for s in state_shapes] if has_init else [])
                + [_full_spec(a) for a in params]
                + [_full_spec(a) for a in const_arrays]
                + [pl.BlockSpec(memory_space=pl.ANY)] * n_alias)
    br_dtype = BF16 if L % 16 == 0 else F32
    out_shape = ([jax.ShapeDtypeStruct((bg, t, N_BRANCH * BRANCH_W), br_dtype)]
                 + [jax.ShapeDtypeStruct(s, F32) for s in state_shapes])
    out_specs = ([pl.BlockSpec((bb, L, N_BRANCH * BRANCH_W), lambda b, c: (b, c, 0))]
                 + [st_spec(s) for s in state_shapes])
    first_alias = len(inputs) - n_alias
    pair = (bb, 2, LANE, LANE)
    scratch = [pltpu.VMEM(pair, F32), pltpu.VMEM(pair, F32), pltpu.VMEM(pair, F32),
               pltpu.VMEM((bb, 2, 1, LANE), F32), pltpu.VMEM((bb, SSD_G * SSD_N, SSD_H * SSD_P), F32),
               pltpu.VMEM((bb, 8 + L, GDN_CONV_W), F32), pltpu.VMEM((bb, 8 + L, SSD_CONV_W), F32)]
    outs = pl.pallas_call(
        functools.partial(_mixer_kernel, L, nv, J, layer, bb, offs, has_init, n_alias),
        grid=(bg // bb, nchunk),
        in_specs=in_specs, out_specs=out_specs, out_shape=out_shape, scratch_shapes=scratch,
        input_output_aliases={first_alias + k: 1 + k for k in range(n_alias)},
        compiler_params=pltpu.CompilerParams(dimension_semantics=("parallel", "arbitrary"),
                                             vmem_limit_bytes=VMEM_LIMIT),
        name=f"mixer_L{L}",
    )(*inputs)
    return outs[0], list(outs[1:])


def _ada_kernel(c_ref, w_ref, b_ref, o_ref):
    o_ref[...] = _mm(_silu(c_ref[...]), w_ref[...]) + b_ref[...]


def _ada_call(c_all, ada_w, ada_b):
    rows = c_all.shape[0]
    n = ada_w.shape[-1]
    tn = 1536
    return pl.pallas_call(
        _ada_kernel,
        grid=(DEPTH, n // tn),
        in_specs=[pl.BlockSpec((rows, D_MODEL), lambda l, j: (0, 0)),
                  pl.BlockSpec((None, D_MODEL, tn), lambda l, j: (l, 0, j)),
                  pl.BlockSpec((None, 1, tn), lambda l, j: (l, 0, j))],
        out_specs=pl.BlockSpec((None, rows, tn), lambda l, j: (l, 0, j)),
        out_shape=jax.ShapeDtypeStruct((DEPTH, rows, n), F32),
        compiler_params=pltpu.CompilerParams(dimension_semantics=("arbitrary", "arbitrary"),
                                             vmem_limit_bytes=VMEM_LIMIT),
        name="ada",
    )(c_all, ada_w, ada_b.reshape(DEPTH, 1, n))


def _prep_w_in_kernel(w_ref, o_ref):
    rows = w_ref.shape[0]
    o_ref[:, 0:1024] = w_ref[:, 0:1024].astype(BF16)
    o_ref[:, 1024:2816] = w_ref[:, 1032:2824].astype(BF16)
    o_ref[:, 2816:SMALL_OFF] = w_ref[:, 2832:3856].astype(BF16)
    o_ref[:, SMALL_OFF:SMALL_OFF + 8] = w_ref[:, 1024:1032].astype(BF16)
    o_ref[:, SMALL_OFF + 8:SMALL_OFF + 16] = w_ref[:, 2824:2832].astype(BF16)
    o_ref[:, SMALL_OFF + 16:SMALL_OFF + 24] = w_ref[:, 3856:3864].astype(BF16)
    o_ref[:, SMALL_OFF + 24:MIX_W] = jnp.zeros((rows, MIX_W - SMALL_OFF - 24), BF16)
    o_ref[:, MIX_W:PROJ_W] = w_ref[:, 3864:3864 + GATE_W].astype(BF16)


def _prep_w_in_call(w_in):
    depth, d, n = w_in.shape
    tr = 128
    return pl.pallas_call(
        _prep_w_in_kernel,
        grid=(depth, d // tr),
        in_specs=[pl.BlockSpec((None, tr, n), lambda l, i: (l, i, 0))],
        out_specs=pl.BlockSpec((None, tr, PROJ_W), lambda l, i: (l, i, 0)),
        out_shape=jax.ShapeDtypeStruct((depth, d, PROJ_W), BF16),
        compiler_params=pltpu.CompilerParams(dimension_semantics=("arbitrary", "arbitrary"),
                                             vmem_limit_bytes=VMEM_LIMIT),
        name="prep_w_in",
    )(w_in)


def _rms_mod(x, nw, sc, sh):
    ms = jnp.mean(x * x, axis=-1, keepdims=True)
    return x * lax.rsqrt(ms + EPS) * nw * (1.0 + sc) + sh


def _tok_spec(gblk, width, col=0):
    g, r = gblk
    return pl.BlockSpec((g, r, width), lambda i, j: (i, j, col))


def _mod_spec(gblk, layer, k):
    return pl.BlockSpec((None, gblk[0], 1, D_MODEL), lambda i, j: (layer, i, 0, k))


def _layer_spec(shape, layer):
    nd = len(shape)
    return pl.BlockSpec((None,) + tuple(shape[1:]), lambda i, j: (layer,) + (0,) * (nd - 1),
                        pipeline_mode=pl.Buffered(1))


def _dense_params():
    return pltpu.CompilerParams(dimension_semantics=("arbitrary", "arbitrary"), vmem_limit_bytes=VMEM_LIMIT)


def _inproj_kernel(x_ref, nw_ref, sc_ref, sh_ref, w_ref, o_ref):
    g, r, _ = x_ref.shape
    h = _rms_mod(x_ref[...], nw_ref[...], sc_ref[...], sh_ref[...])
    o_ref[...] = _mm(h.reshape(g * r, D_MODEL), w_ref[...]).reshape(g, r, PROJ_W)


def _inproj_call(x, layer, nw, mods, w, gblk):
    bg, t, _ = x.shape
    return pl.pallas_call(
        _inproj_kernel,
        grid=(bg // gblk[0], t // gblk[1]),
        in_specs=[_tok_spec(gblk, D_MODEL), _layer_spec(nw.shape, layer),
                  _mod_spec(gblk, layer, 1), _mod_spec(gblk, layer, 0), _layer_spec(w.shape, layer)],
        out_specs=_tok_spec(gblk, PROJ_W),
        out_shape=jax.ShapeDtypeStruct((bg, t, PROJ_W), F32),
        compiler_params=_dense_params(), name="inproj",
    )(x, nw, mods, mods, w)


def _merge_kernel(br_ref, gate_ref, x_ref, gt_ref, wb_ref, wo_ref, o_ref):
    g, r, _ = x_ref.shape
    br = br_ref[...].astype(F32).reshape(g * r, N_BRANCH * BRANCH_W)
    merged = None
    for n in range(N_BRANCH):
        up = _mm(br[:, n * BRANCH_W:(n + 1) * BRANCH_W], wb_ref[n])
        gate = gate_ref[:, :, n * D_MODEL:(n + 1) * D_MODEL].reshape(g * r, D_MODEL)
        t = _sigmoid(gate) * up
        merged = t if merged is None else merged + t
    y = _mm(merged, wo_ref[...]).reshape(g, r, D_MODEL)
    o_ref[...] = x_ref[...] + gt_ref[...] * y


def _merge_call(br, proj, x, layer, mods, wb, wo, gblk):
    bg, t, _ = x.shape
    return pl.pallas_call(
        _merge_kernel,
        grid=(bg // gblk[0], t // gblk[1]),
        in_specs=[_tok_spec(gblk, N_BRANCH * BRANCH_W), _tok_spec(gblk, GATE_W, col=1), _tok_spec(gblk, D_MODEL),
                  _mod_spec(gblk, layer, 2), _layer_spec(wb.shape, layer), _layer_spec(wo.shape, layer)],
        out_specs=_tok_spec(gblk, D_MODEL),
        out_shape=jax.ShapeDtypeStruct((bg, t, D_MODEL), F32),
        compiler_params=_dense_params(), name="merge",
    )(br, proj, x, mods, wb, wo)


def _ffn_kernel(final, x_ref, nw_ref, sc_ref, sh_ref, gt_ref, w1_ref, w2_ref, fw_ref, o_ref):
    g, r, _ = x_ref.shape
    x = x_ref[...]
    h = _rms_mod(x, nw_ref[...], sc_ref[...], sh_ref[...]).astype(BF16).reshape(g * r, D_MODEL)
    a = _mm(h, w1_ref[:, 0:D_FF])
    b = _mm(h, w1_ref[:, D_FF:2 * D_FF])
    y = _mm(_silu(a) * b, w2_ref[...]).reshape(g, r, D_MODEL)
    x2 = x + gt_ref[...] * y
    if final:
        ms = jnp.mean(x2 * x2, axis=-1, keepdims=True)
        x2 = x2 * lax.rsqrt(ms + EPS) * fw_ref[...]
    o_ref[...] = x2


def _ffn_call(x, layer, nw, mods, w1, w2, fw, gblk, final):
    bg, t, _ = x.shape
    return pl.pallas_call(
        functools.partial(_ffn_kernel, final),
        grid=(bg // gblk[0], t // gblk[1]),
        in_specs=[_tok_spec(gblk, D_MODEL), _layer_spec(nw.shape, layer),
                  _mod_spec(gblk, layer, 4), _mod_spec(gblk, layer, 3), _mod_spec(gblk, layer, 5),
                  _layer_spec(w1.shape, layer), _layer_spec(w2.shape, layer),
                  pl.BlockSpec(fw.shape, lambda i, j: (0, 0), pipeline_mode=pl.Buffered(1))],
        out_specs=_tok_spec(gblk, D_MODEL),
        out_shape=jax.ShapeDtypeStruct((bg, t, D_MODEL), F32),
        compiler_params=_dense_params(), name="ffn",
    )(x, nw, mods, mods, mods, w1, w2, fw)


def _small_params(gdn_dt_bias, ssd_dt_bias, ml_b_i, ml_b_f, gdn_a_log, ssd_a_log):
    z4 = jnp.zeros((4,), F32)
    bias = jnp.concatenate([z4, gdn_dt_bias, ssd_dt_bias, ml_b_i, ml_b_f, jnp.zeros((LANE - 24,), F32)])
    alog = jnp.concatenate([z4, gdn_a_log, ssd_a_log, jnp.zeros((LANE - 16,), F32)])
    return jnp.concatenate([bias[None], alog[None], jnp.zeros((6, LANE), F32)], axis=0)


def _layer_params(l, gdn_conv_w, gdn_conv_b, gdn_a_log, gdn_dt_bias, gdn_norm_w, hg_lb_logits, hg_norm_w,
                  ssd_conv_w, ssd_conv_b, ssd_a_log, ssd_dt_bias, ssd_d, ssd_norm_w, ml_b_i, ml_b_f, ml_norm_w):
    sp = _small_params(gdn_dt_bias[l], ssd_dt_bias[l], ml_b_i[l], ml_b_f[l], gdn_a_log[l], ssd_a_log[l])
    vec = jnp.stack([jnp.tile(gdn_norm_w[l], N_HEAD), jnp.tile(hg_norm_w[l], N_HEAD), ssd_norm_w[l],
                     jnp.tile(ml_norm_w[l], N_HEAD), jnp.repeat(ssd_d[l], SSD_P)]
                    + [jnp.zeros((256,), F32)] * 3)
    return [sp, gdn_conv_w[l], gdn_conv_b[l][None], ssd_conv_w[l], ssd_conv_b[l][None], vec,
            hg_lb_logits.astype(F32)]


def _trunk(x, mods, state_shapes, states_in, L, nv, gblk, bb, W, mixer_params):
    new_states = None
    for l in range(DEPTH):
        proj = _inproj_call(x, l, W['norm1'], mods, W['w_in'], gblk)
        br, new_states = _mixer_call(proj, state_shapes, states_in, new_states, mixer_params[l], L, nv, l, bb)
        x1 = _merge_call(br, proj, x, l, mods, W['w_branch'], W['w_out'], gblk)
        x = _ffn_call(x1, l, W['norm2'], mods, W['ffn_w_in'], W['ffn_w_out'], W['final'], gblk,
                      final=(l == DEPTH - 1))
    return x, new_states


def kernel(x_prompt, x_sample, c_prompt, c_sample, state_gdn, state_gdn_conv, state_hgrn, state_ssd, state_ssd_conv, state_mlstm_c, state_mlstm_n, state_mlstm_m, ada_w, ada_b, norm1_w, norm2_w, w_in, gdn_conv_w, gdn_conv_b, gdn_a_log, gdn_dt_bias, gdn_norm_w, hg_lb_logits, hg_norm_w, ssd_conv_w, ssd_conv_b, ssd_a_log, ssd_dt_bias, ssd_d, ssd_norm_w, ml_b_i, ml_b_f, ml_norm_w, w_branch, w_out, ffn_w_in, ffn_w_out, final_norm_w):
    bp, tp, _ = x_prompt.shape
    bs, ts, _ = x_sample.shape
    ls = 8
    assert tp % CHUNK == 0 and ts <= ls

    W = dict(w_in=_prep_w_in_call(w_in), w_branch=w_branch.astype(BF16), w_out=w_out.astype(BF16),
             ffn_w_in=ffn_w_in.astype(BF16), ffn_w_out=ffn_w_out.astype(BF16),
             norm1=norm1_w[:, None, :], norm2=norm2_w[:, None, :], final=final_norm_w[None])
    mixer_params = [_layer_params(l, gdn_conv_w, gdn_conv_b, gdn_a_log, gdn_dt_bias, gdn_norm_w, hg_lb_logits,
                                  hg_norm_w, ssd_conv_w, ssd_conv_b, ssd_a_log, ssd_dt_bias, ssd_d, ssd_norm_w,
                                  ml_b_i, ml_b_f, ml_norm_w) for l in range(DEPTH)]

    mods = _ada_call(jnp.concatenate([c_prompt, c_sample], axis=0), ada_w, ada_b)
    mods_p = mods[:, :bp].reshape(DEPTH, bp, 1, 6 * D_MODEL)
    mods_s = mods[:, bp:].reshape(DEPTH, bs, 1, 6 * D_MODEL)
    sample_states = [state_gdn, state_gdn_conv, state_hgrn, state_ssd, state_ssd_conv,
                     state_mlstm_c, state_mlstm_n, state_mlstm_m.reshape(DEPTH, bs, 1, N_HEAD)]
    shapes_s = [s.shape for s in sample_states]
    shapes_p = [(DEPTH, bp) + tuple(s[2:]) for s in shapes_s]

    y_p, new_p = _trunk(x_prompt, mods_p, shapes_p, None, CHUNK, CHUNK, (1, 256), 4, W, mixer_params)
    xs_pad = jnp.pad(x_sample, ((0, 0), (0, ls - ts), (0, 0)))
    y_s, new_s = _trunk(xs_pad, mods_s, shapes_s, sample_states, ls, ts, (32, 8), 8, W, mixer_params)
    new_p[7] = new_p[7].reshape(DEPTH, bp, N_HEAD)
    new_s[7] = new_s[7].reshape(DEPTH, bs, N_HEAD)
    return (y_p, y_s[:, :ts]) + tuple(new_p) + tuple(new_s)
```

```python
import functools
import math

import numpy as np
import jax
import jax.numpy as jnp
from jax import lax
from jax.experimental import pallas as pl
from jax.experimental.pallas import tpu as pltpu

F32 = jnp.float32
BF16 = jnp.bfloat16

D_MODEL = 1024
DEPTH = 2
N_BRANCH = 4
BRANCH_W = 256
N_HEAD = 4
HEAD_D = 64
SSD_H = 8
SSD_P = 32
SSD_N = 64
SSD_G = 2
CONV_K = 4
CHUNK = 64
D_FF = 2816
EPS = 1e-6
NEG = -1e30
GDN_CONV_W = 768
SSD_CONV_W = 512
SSD_XBC_OFF = 2304
MIX_W = 4096
GATE_W = 4096
PROJ_W = MIX_W + GATE_W
SMALL_OFF = 3840
LANE = 128
VMEM_LIMIT = 56 * 1024 * 1024

ST_BETA, ST_GA, ST_DT, ST_MI, ST_MF = 0, 4, 8, 16, 20

_NN = (((1,), (0,)), ((), ()))
_NT = (((1,), (1,)), ((), ()))
_TN = (((0,), (0,)), ((), ()))


def _mm(a, b, dims=_NN):
    return lax.dot_general(a.astype(BF16), b.astype(BF16), dims, preferred_element_type=F32)


def _split(x, n):
    parts, r = [], x
    for i in range(n):
        p = r.astype(BF16)
        parts.append(p)
        if i < n - 1:
            r = r - p.astype(F32)
    return parts


def _mm01(x, m, n):
    out = None
    for p in _split(x, n):
        t = lax.dot_general(p, m.astype(BF16), _NN, preferred_element_type=F32)
        out = t if out is None else out + t
    return out


def _m01m(m, x, n):
    out = None
    for p in _split(x, n):
        t = lax.dot_general(m, p, _NN, preferred_element_type=F32)
        out = t if out is None else out + t
    return out


def _sigmoid(x):
    return jax.nn.sigmoid(x)


def _silu(x):
    return x * jax.nn.sigmoid(x)


def _softplus(x):
    return jnp.maximum(x, 0.0) + jnp.log(1.0 + jnp.exp(-jnp.abs(x)))


def _logsig(x):
    return jnp.minimum(x, 0.0) - jnp.log(1.0 + jnp.exp(-jnp.abs(x)))


def _row_dtype(rows):
    return BF16 if rows % 16 == 0 else F32


def _expand(base, heads, width):
    e = np.zeros((LANE, heads * width), np.float32)
    for h in range(heads):
        e[base + h, h * width:(h + 1) * width] = 1.0
    return e


def _cat_segments(segs):
    cols, off, pos, seen = [], {}, 0, {}
    for name, m in segs:
        key = m.tobytes() + bytes(str(m.shape), 'ascii')
        if key in seen:
            off[name] = seen[key]
            continue
        w = m.shape[1]
        wp = -(-w // LANE) * LANE
        mp = np.zeros((m.shape[0], wp), np.float32)
        mp[:, :w] = m
        cols.append(mp)
        off[name] = seen[key] = (pos, w)
        pos += wp
    return np.concatenate(cols, axis=1), off


@functools.lru_cache(maxsize=None)
def _mixer_consts(L, nv):
    J = int(round(math.log2(L)))
    assert 1 << J == L
    f = np.float32
    r = np.arange(L)
    tri = (r[None, :] <= r[:, None])
    rev = (r[:, None] < r[None, :]) & (r[None, :] <= nv - 1)
    strict = (r[None, :] < r[:, None])
    eye = np.eye(L, dtype=bool)
    lvl, ab = [], []
    for j in range(J):
        bnd = ((r >> (j + 1)) << (j + 1)) + (1 << j) - 1
        low = ((r >> j) & 1) == 1
        a = low[:, None] & (bnd[:, None] < r[None, :]) & (r[None, :] <= r[:, None])
        b = (~low)[:, None] & (r[:, None] < r[None, :]) & (r[None, :] <= bnd[:, None])
        ab.append(a | b)
        same = (r[:, None] >> (j + 1)) == (r[None, :] >> (j + 1))
        lvl.append(same & low[:, None] & (~low)[None, :])

    def tile(m, n):
        return np.tile(m.astype(f), (1, n))

    pr = np.repeat(np.arange(2), L)
    lr2 = np.tile(r, 2)
    ch = np.arange(LANE)
    ch2 = np.arange(2 * LANE)
    c = dict(
        trirev=np.concatenate([tri, rev], axis=0).astype(f),
        mhg=np.concatenate(ab + [tri, rev], axis=0).astype(f),
        ones_ll=np.ones((L, L), f), tri=tri.astype(f),
        tril2=tile(tri, 2), strict2=tile(strict, 2), tril4=tile(tri, 4),
        eye4=tile(eye, 4), supper16=tile(strict, 16),
        e_dr=np.concatenate([_expand(ST_GA, 4, L), _expand(ST_DT, SSD_H, L), _expand(ST_MF, 4, L)], axis=1),
        hmaskp=np.stack([tile(eye, 2)] + [tile(m, 2) for m in lvl]),
        eyep=np.eye(2 * L, dtype=f),
        lvlp=np.stack([(pr[:, None] == pr[None, :]) & m[lr2[:, None], lr2[None, :]] for m in lvl]).astype(f),
        stkp=(pr[:, None] == ch[None, :] // HEAD_D).astype(f),
        bdp=(ch[:, None] // HEAD_D == ch[None, :] // HEAD_D).astype(f),
        pm2=np.stack([ch < HEAD_D, ch >= HEAD_D]).astype(f),
        stk4s=(np.repeat(np.arange(4), L)[:, None] == ch[None, :] // SSD_P).astype(f),
        gbd=(ch[:, None] // SSD_N == ch2[None, :] // LANE).astype(f),
        rowvalid=(r[:, None] <= nv - 1).astype(f) * np.ones((1, LANE), f),
        bd64=(ch2[:, None] // 64 == ch2[None, :] // 64).astype(f),
        bd128=(ch2[:, None] // 128 == ch2[None, :] // 128).astype(f),
    )
    c['e_act'], off_act = _cat_segments([('b_r', _expand(ST_BETA, 4, L)), ('b_c', _expand(ST_BETA, 4, HEAD_D)),
                                         ('dt_c', _expand(ST_DT, SSD_H, SSD_P)),
                                         ('i_r', _expand(ST_MI, 4, L)), ('i_c', _expand(ST_MI, 4, HEAD_D))])
    c['e_cs'], off_cs = _cat_segments([('g_c', _expand(ST_GA, 4, HEAD_D)), ('s_c', _expand(ST_DT, SSD_H, SSD_P)),
                                       ('m_r', _expand(ST_MF, 4, L)), ('m_c', _expand(ST_MF, 4, HEAD_D))])
    c['e_rev'], off_rev = _cat_segments([('g_c', _expand(ST_GA, 4, HEAD_D)), ('s_c', _expand(ST_DT, SSD_H, SSD_P)),
                                         ('m_c', _expand(ST_MF, 4, HEAD_D))])
    return c, dict(act=off_act, cs=off_cs, rev=off_rev), J


_CONST_ORDER = ('trirev', 'mhg', 'ones_ll', 'tri', 'tril2', 'strict2', 'tril4', 'eye4', 'supper16', 'e_dr', 'hmaskp',
                'eyep', 'lvlp', 'stkp', 'bdp', 'pm2', 'stk4s', 'gbd', 'rowvalid', 'bd64', 'bd128', 'e_act', 'e_cs',
                'e_rev')
_BF16_CONSTS = ('trirev', 'mhg', 'ones_ll', 'tri', 'supper16', 'e_dr', 'lvlp', 'stkp', 'stk4s', 'bd64', 'bd128',
                'e_act', 'e_cs', 'e_rev')


def _conv_silu(ext, w, b, L):
    y = b
    for i in range(CONV_K):
        y = y + ext[5 + i:5 + i + L, :] * w[i:i + 1, :]
    return _silu(y)


def _run_interleaved(tasks):
    tasks = list(tasks)
    while tasks:
        alive = []
        for t in tasks:
            try:
                next(t)
                alive.append(t)
            except StopIteration:
                pass
        tasks = alive


def _mixer_chunk(pm, ext_g, ext_s, st, P, K, offs, L, nv, J, layer):
    lane = lax.broadcasted_iota(jnp.int32, (1, LANE), 1)
    rowvalid = K['rowvalid'][:, 0:1]
    tril2 = K['tril2'] > 0.0
    tril4 = K['tril4'] > 0.0
    stkp, bdp = K['stkp'], K['bdp']
    upper_c = lane >= HEAD_D
    upper_r = lax.broadcasted_iota(jnp.int32, (1, 2 * L), 1) >= L

    def seg(x, which, name):
        o, w = offs[which][name]
        return x[:, o:o + w]

    def pair_c(x, p):
        return x[:, p * LANE:(p + 1) * LANE]

    def pair_r(x, p):
        return x[:, p * 2 * L:(p + 1) * 2 * L]

    def stack(x):
        xb = x.astype(BF16)
        return jnp.concatenate([xb, xb], axis=0) * stkp

    z = pm[:, SMALL_OFF:SMALL_OFF + LANE] + P['sp'][0:1, :]
    act = jnp.where(lane < ST_GA, _sigmoid(z),
                    jnp.where(lane < ST_MI, _softplus(z), jnp.where(lane < ST_MF, z, _logsig(z))))
    neg_a = -jnp.exp(P['sp'][1:2, :])
    dec_in = jnp.where((lane >= ST_GA) & (lane < ST_MI), neg_a * act,
                       jnp.where((lane >= ST_MF) & (lane < ST_MF + 4), act, 0.0))
    cr = _m01m(K['trirev'], dec_in, 2)
    cs, rev = cr[:L], cr[L:]
    ea = _mm01(act, K['e_act'], 2)
    ec = _mm01(cs, K['e_cs'], 2)
    er = _mm01(rev, K['e_rev'], 2)
    m_r = seg(ec, 'cs', 'm_r')
    g_c, s_c, m_c = seg(ec, 'cs', 'g_c'), seg(ec, 'cs', 's_c'), seg(ec, 'cs', 'm_c')
    b_r, b_c, dt_c = seg(ea, 'act', 'b_r'), seg(ea, 'act', 'b_c'), seg(ea, 'act', 'dt_c')
    i_r, i_c = seg(ea, 'act', 'i_r'), seg(ea, 'act', 'i_c')
    rg_c, rs_c, rm_c = seg(er, 'rev', 'g_c'), seg(er, 'rev', 's_c'), seg(er, 'rev', 'm_c')
    ym = _mm(dec_in, K['e_dr']).astype(BF16) * K['supper16']
    d_all = lax.dot_general(K['tri'], ym, _NN, preferred_element_type=F32)
    d_g, d_s, d_m = d_all[:, 0:4 * L], d_all[:, 4 * L:12 * L], d_all[:, 12 * L:16 * L]
    row_i = _m01m(K['ones_ll'], i_r * K['eye4'], 2)

    res = dict(o_gdn=[None, None], gdn=[None, None], o_hg=[None, None], hg=[None, None],
               o_ml=[None, None], mlc=[None, None], mln=[None, None], mlm=[None] * N_HEAD)

    qkv = _conv_silu(ext_g, P['cwg'], P['cbg'], L)
    q, k, v = qkv[:, 0:256], qkv[:, 256:512], qkv[:, 512:768]
    ss = _mm(jnp.concatenate([q * q, k * k], axis=0), K['bd64'])
    q = q * (lax.rsqrt(ss[:L] + EPS) * (HEAD_D ** -0.5))
    k = k * lax.rsqrt(ss[L:] + EPS)

    def gdn_task(p):
        qt, kt, vt = pair_c(q, p), pair_c(k, p), pair_c(v, p)
        beta_c, gc = pair_c(b_c, p), pair_c(g_c, p)
        eg = jnp.exp(gc)
        dec = jnp.where(tril2, jnp.exp(jnp.where(tril2, pair_r(d_g, p), 0.0)), 0.0)
        kkqk = _mm(jnp.concatenate([kt, qt], axis=0), stack(kt), _NT)
        yield
        n = (kkqk[:L] * dec * pair_r(b_r, p) * K['strict2']).astype(BF16)
        nbd = jnp.concatenate([n, n], axis=0)
        t = K['eyep'] - (nbd * K['lvlp'][0]).astype(F32)
        for j in range(1, J):
            x = _mm(t, nbd * K['lvlp'][j])
            yield
            t = t - _mm(x, t)
            yield
        sv = _mm(t, stack(vt * beta_c))
        sk = _mm(t, stack(kt * (beta_c * eg)))
        yield
        s_p = st['gdn'][p]
        r = _mm(jnp.concatenate([sk[:L] + sk[L:], qt * eg], axis=0), s_p)
        yield
        u = sv[:L] + sv[L:] - r[:L]
        res['o_gdn'][p] = r[L:] + _mm(kkqk[L:] * dec, stack(u))
        kw = kt * (jnp.exp(pair_c(rg_c, p)) * rowvalid)
        res['gdn'][p] = s_p * jnp.exp(gc[nv - 1:nv, :]) + bdp * _mm(kw, u, _TN)
        yield

    lg = P['lg']
    mx = lg[0:1, :]
    for i in range(1, DEPTH):
        mx = jnp.maximum(mx, lg[i:i + 1, :])
    ex = [jnp.exp(lg[i:i + 1, :] - mx) for i in range(DEPTH)]
    tot = ex[0]
    for i in range(1, DEPTH):
        tot = tot + ex[i]
    sm = [e / tot for e in ex]
    cum = sm[0]
    for i in range(1, layer + 1):
        cum = cum + sm[i]
    lb = cum - sm[0]
    lb_pos = lb > 0
    log_lb = jnp.log(jnp.where(lb_pos, lb, 1.0))
    hq = _silu(pm[:, 1024:1280])
    fz = pm[:, 1280:1536]
    hv = pm[:, 1536:1792]
    ls = _logsig(fz)
    t2 = jnp.log1p(-lb) + ls
    la = jnp.maximum(log_lb, t2) + jnp.log1p(jnp.exp(-jnp.abs(log_lb - t2)))
    logf = jnp.where(lb_pos, la, ls)
    kg = (1.0 - lb) * _sigmoid(-fz)
    ey = jnp.exp(_m01m(K['mhg'], logf, 2))
    eg_h = ey[J * L:(J + 1) * L]
    qs = [hq] + [hq * ey[j * L:(j + 1) * L] for j in range(J)]
    ks = [kg] + [kg * ey[j * L:(j + 1) * L] for j in range(J)]
    qe = hq * eg_h
    kwr = kg * (ey[(J + 1) * L:(J + 2) * L] * rowvalid)

    def hg_task(p):
        a_p = None
        for i in range(J + 1):
            t_i = K['hmaskp'][i] * _mm(pair_c(qs[i], p), stack(pair_c(ks[i], p)), _NT)
            a_p = t_i if a_p is None else a_p + t_i
            yield
        st_p = st['hg'][p]
        res['o_hg'][p] = _mm(pair_c(qe, p), st_p, _NT) + _mm(a_p, stack(pair_c(hv, p)))
        yield
        res['hg'][p] = (st_p * pair_c(eg_h[nv - 1:nv, :], p)
                        + bdp * _mm(pair_c(hv, p), pair_c(kwr, p), _TN))
        yield

    xbc = _conv_silu(ext_s, P['cws'], P['cbs'], L)
    xs, bs, cc = xbc[:, 0:256], xbc[:, 256:384], xbc[:, 384:512]
    vs = xs * dt_c
    bsb = bs.astype(BF16)
    bs4 = jnp.concatenate([bsb] * 4, axis=0)

    def ssd_task():
        o_intra = []
        for g in range(SSD_G):
            cbw = _mm(cc * K['pm2'][g:g + 1, :], bs4, _NT)
            dec = jnp.where(tril4, jnp.exp(jnp.where(tril4, d_s[:, g * 4 * L:(g + 1) * 4 * L], 0.0)), 0.0)
            vb = pair_c(vs, g).astype(BF16)
            o_intra.append(_mm(cbw * dec, jnp.concatenate([vb] * 4, axis=0) * K['stk4s']))
            yield
        s_all = st['ssd']
        res['o_ssd'] = jnp.exp(s_c) * _mm(cc, s_all) + jnp.concatenate(o_intra, axis=1)
        res['ssd'] = (s_all * jnp.exp(s_c[nv - 1:nv, :])
                      + K['gbd'] * _mm(bs * rowvalid, vs * jnp.exp(rs_c), _TN))
        yield

    mq = pm[:, 2816:3072] * (HEAD_D ** -0.5)
    mk = pm[:, 3072:3328]
    mv = pm[:, 3328:3584]

    def ml_task(p, delay):
        for _ in range(delay):
            yield
        qt, kt, vt = pair_c(mq, p), pair_c(mk, p), pair_c(mv, p)
        bm_r, bm_c = pair_r(m_r, p), pair_c(m_c, p)
        mp0 = st['mlm'][:, 2 * p:2 * p + 1]
        mp1 = st['mlm'][:, 2 * p + 1:2 * p + 2]
        logw = jnp.where(tril2, pair_r(d_m, p) + pair_r(row_i, p), NEG)
        mx0 = jnp.max(jnp.where(upper_r, NEG, logw), axis=-1, keepdims=True)
        mx1 = jnp.max(jnp.where(upper_r, logw, NEG), axis=-1, keepdims=True)
        l0_r = bm_r + jnp.where(upper_r, mp1, mp0)
        l0_c = bm_c + jnp.where(upper_c, mp1, mp0)
        mt_r = jnp.maximum(l0_r, jnp.where(upper_r, mx1, mx0))
        mt_c = jnp.maximum(l0_c, jnp.where(upper_c, mx1, mx0))
        w = jnp.where(tril2, jnp.exp(logw - mt_r), 0.0)
        w0 = jnp.exp(l0_c - mt_c)
        qk = _mm(qt, stack(kt), _NT) * w
        yield
        c_p = st['mlc'][p]
        n_p = st['mln'][p]
        num = w0 * _mm(qt, c_p) + _mm(qk, stack(vt))
        den = w0 * _mm01(qt * n_p, bdp, 2) + _mm01(qk, stkp, 2)
        res['o_ml'][p] = num * (1.0 / jnp.maximum(jnp.abs(den), jnp.exp(-mt_c)))
        yield
        m_l = mt_c[nv - 1:nv, :]
        wl0 = jnp.exp(bm_c[nv - 1:nv, :] + jnp.where(upper_c, mp1, mp0) - m_l)
        kwl = kt * (jnp.exp(pair_c(rm_c, p) + pair_c(i_c, p) - m_l) * rowvalid)
        res['mlc'][p] = c_p * wl0 + bdp * _mm(kwl, vt, _TN)
        res['mln'][p] = n_p * wl0 + jnp.sum(kwl, axis=0, keepdims=True)
        res['mlm'][2 * p] = m_l[:, 0:1]
        res['mlm'][2 * p + 1] = m_l[:, HEAD_D:HEAD_D + 1]
        yield

    def finish():
        o_gdn = jnp.concatenate(res['o_gdn'], axis=1)
        o_hg = jnp.concatenate(res['o_hg'], axis=1)
        hh = jnp.concatenate(res['o_ml'], axis=1)
        ms = _mm(jnp.concatenate([o_gdn * o_gdn, o_hg * o_hg, hh * hh], axis=0), K['bd64']) * (1.0 / HEAD_D)
        out_a = o_gdn * lax.rsqrt(ms[0:L] + EPS) * P['vec'][0:1, :] * _silu(pm[:, 768:1024])
        out_b = o_hg * lax.rsqrt(ms[L:2 * L] + EPS) * P['vec'][1:2, :] * _silu(pm[:, 1792:2048])
        out_d = hh * lax.rsqrt(ms[2 * L:3 * L] + EPS) * P['vec'][3:4, :] * _sigmoid(pm[:, 3584:3840])
        ys = (res['o_ssd'] + P['vec'][4:5, :] * xs) * _silu(pm[:, 2048:2304])
        out_c = ys * lax.rsqrt(_mm(ys * ys, K['bd128']) * (1.0 / (2 * HEAD_D)) + EPS) * P['vec'][2:3, :]
        branches = jnp.concatenate([out_a, out_b, out_c, out_d], axis=1)
        new = dict(gdn=res['gdn'], hg=res['hg'], ssd=res['ssd'], mlc=res['mlc'], mln=res['mln'], mlm=res['mlm'])
        return branches, new

    tasks = ([gdn_task(p) for p in range(2)] + [hg_task(p) for p in range(2)]
             + [ssd_task()] + [ml_task(p, 3 + 2 * p) for p in range(2)])
    return tasks, finish


def _mixer_kernel(L, nv, J, layer, bb, offs, has_init, n_alias, *refs):
    it = iter(refs)
    pm_ref = next(it)
    if has_init:
        gdn0, cg0, hg0, ssd0, cs0, mc0, mn0, mm0 = (next(it) for _ in range(8))
    sp_ref, cwg_ref, cbg_ref, cws_ref, cbs_ref, vec_ref, lg_ref = (next(it) for _ in range(7))
    kref = {name: next(it) for name in _CONST_ORDER}
    for _ in range(n_alias):
        next(it)
    br_ref = next(it)
    gdn1, cg1, hg1, ssd1, cs1, mc1, mn1, mm1 = (next(it) for _ in range(8))
    sg, sh, sc, sn, ss, extg, exts = (next(it) for _ in range(7))

    c = pl.program_id(1)
    nc = pl.num_programs(1)
    n_sub = SSD_H // SSD_G

    @pl.when(c == 0)
    def _init():
        sg[...] = jnp.zeros_like(sg)
        sh[...] = jnp.zeros_like(sh)
        sc[...] = jnp.zeros_like(sc)
        ss[...] = jnp.zeros_like(ss)
        extg[:, 0:8, :] = jnp.zeros((bb, 8, GDN_CONV_W), F32)
        exts[:, 0:8, :] = jnp.zeros((bb, 8, SSD_CONV_W), F32)
        if not has_init:
            sn[...] = jnp.zeros_like(sn)
            mm1[...] = jnp.zeros_like(mm1)
            return
        for s in range(bb):
            for h in range(N_HEAD):
                p, lo = h // 2, (h % 2) * HEAD_D
                sg[s, p, lo:lo + HEAD_D, lo:lo + HEAD_D] = gdn0[s, h]
                sh[s, p, lo:lo + HEAD_D, lo:lo + HEAD_D] = hg0[s, h].T
                sc[s, p, lo:lo + HEAD_D, lo:lo + HEAD_D] = mc0[s, h]
                sn[s, p, :, lo:lo + HEAD_D] = mn0[s, h:h + 1, :]
            for h in range(SSD_H):
                g = h // n_sub
                ss[s, g * SSD_N:(g + 1) * SSD_N, h * SSD_P:(h + 1) * SSD_P] = ssd0[s, h]
        mm1[...] = mm0[...]
        extg[:, 5:8, :] = cg0[...]
        exts[:, 5:8, :] = cs0[...]

    P = dict(sp=sp_ref[...], cwg=cwg_ref[...], cbg=cbg_ref[...], cws=cws_ref[...], cbs=cbs_ref[...],
             vec=vec_ref[...], lg=lg_ref[...])
    K = {name: r[...] for name, r in kref.items()}
    tasks, finishers = [], []
    for s in range(bb):
        pm = pm_ref[s]
        extg[s, 8:8 + L, :] = pm[:, 0:GDN_CONV_W]
        exts[s, 8:8 + L, :] = pm[:, SSD_XBC_OFF:SSD_XBC_OFF + SSD_CONV_W]
        st = dict(gdn=sg[s], hg=sh[s], ssd=ss[s], mlc=sc[s], mln=sn[s], mlm=mm1[s])
        t, fin = _mixer_chunk(pm, extg.at[s], exts.at[s], st, P, K, offs, L, nv, J, layer)
        tasks.append(t)
        finishers.append(fin)
    _run_interleaved([t for group in zip(*tasks) for t in group])
    for s in range(bb):
        branches, new = finishers[s]()
        br_ref[s] = branches.astype(br_ref.dtype)
        for p in range(2):
            sg[s, p] = new['gdn'][p]
            sh[s, p] = new['hg'][p]
            sc[s, p] = new['mlc'][p]
            sn[s, p] = new['mln'][p]
        for h in range(N_HEAD):
            mm1[s, :, h:h + 1] = new['mlm'][h]
        ss[s] = new['ssd']
        tail_g = extg[s, 8 + nv - 3:8 + nv, :]
        tail_s = exts[s, 8 + nv - 3:8 + nv, :]
        extg[s, 5:8, :] = tail_g
        exts[s, 5:8, :] = tail_s
        cg1[s] = tail_g
        cs1[s] = tail_s

    @pl.when(c == nc - 1)
    def _fin():
        for s in range(bb):
            for h in range(N_HEAD):
                p, lo = h // 2, (h % 2) * HEAD_D
                gdn1[s, h] = sg[s, p, lo:lo + HEAD_D, lo:lo + HEAD_D]
                hg1[s, h] = sh[s, p, lo:lo + HEAD_D, lo:lo + HEAD_D].T
                mc1[s, h] = sc[s, p, lo:lo + HEAD_D, lo:lo + HEAD_D]
                mn1[s, h:h + 1, :] = sn[s, p, :, lo:lo + HEAD_D]
            for h in range(SSD_H):
                g = h // n_sub
                ssd1[s, h] = ss[s, g * SSD_N:(g + 1) * SSD_N, h * SSD_P:(h + 1) * SSD_P]


def _full_spec(a):
    nd = a.ndim
    return pl.BlockSpec(a.shape, lambda b, c, _nd=nd: (0,) * _nd)


def _mixer_call(proj, state_shapes, states_in, prev_out, params, L, nv, layer, bb):
    bg, t, _ = proj.shape
    nchunk = t // L
    assert bg % bb == 0
    consts, offs, J = _mixer_consts(L, nv)
    const_arrays = [jnp.asarray(consts[n], BF16 if n in _BF16_CONSTS else F32) for n in _CONST_ORDER]

    def st_spec(shape):
        nd = len(shape)
        return pl.BlockSpec((None, bb) + tuple(shape[2:]), lambda b, c, _nd=nd: (layer, b) + (0,) * (_nd - 2))

    has_init = states_in is not None
    n_alias = 0 if prev_out is None else len(prev_out)
    inputs = [proj] + (list(states_in) if has_init else []) + list(params) + const_arrays + list(prev_out or [])
    in_specs = ([pl.BlockSpec((bb, L, MIX_W), lambda b, c: (b, c, 0))]
                + ([st_spec(s) for s in state_shapes] if has_init else [])
                + [_full_spec(a) for a in params]
                + [_full_spec(a) for a in const_arrays]
                + [pl.BlockSpec(memory_space=pl.ANY)] * n_alias)
    out_shape = ([jax.ShapeDtypeStruct((bg, t, N_BRANCH * BRANCH_W), _row_dtype(L))]
                 + [jax.ShapeDtypeStruct(s, F32) for s in state_shapes])
    out_specs = ([pl.BlockSpec((bb, L, N_BRANCH * BRANCH_W), lambda b, c: (b, c, 0))]
                 + [st_spec(s) for s in state_shapes])
    first_alias = len(inputs) - n_alias
    pair = (bb, 2, LANE, LANE)
    scratch = [pltpu.VMEM(pair, F32), pltpu.VMEM(pair, F32), pltpu.VMEM(pair, F32),
               pltpu.VMEM((bb, 2, 1, LANE), F32), pltpu.VMEM((bb, SSD_G * SSD_N, SSD_H * SSD_P), F32),
               pltpu.VMEM((bb, 8 + L, GDN_CONV_W), F32), pltpu.VMEM((bb, 8 + L, SSD_CONV_W), F32)]
    outs = pl.pallas_call(
        functools.partial(_mixer_kernel, L, nv, J, layer, bb, offs, has_init, n_alias),
        grid=(bg // bb, nchunk),
        in_specs=in_specs, out_specs=out_specs, out_shape=out_shape, scratch_shapes=scratch,
        input_output_aliases={first_alias + k: 1 + k for k in range(n_alias)},
        compiler_params=pltpu.CompilerParams(dimension_semantics=("parallel", "arbitrary"),
                                             vmem_limit_bytes=VMEM_LIMIT),
        name=f"mixer_L{L}",
    )(*inputs)
    return outs[0], list(outs[1:])


def _ada_kernel(c_ref, w_ref, b_ref, o_ref):
    o_ref[...] = _mm(_silu(c_ref[...]), w_ref[...]) + b_ref[...]


def _ada_call(c_all, ada_w, ada_b):
    rows = c_all.shape[0]
    n = ada_w.shape[-1]
    tn = 1536
    return pl.pallas_call(
        _ada_kernel,
        grid=(DEPTH, n // tn),
        in_specs=[pl.BlockSpec((rows, D_MODEL), lambda l, j: (0, 0)),
                  pl.BlockSpec((None, D_MODEL, tn), lambda l, j: (l, 0, j)),
                  pl.BlockSpec((None, 1, tn), lambda l, j: (l, 0, j))],
        out_specs=pl.BlockSpec((None, rows, tn), lambda l, j: (l, 0, j)),
        out_shape=jax.ShapeDtypeStruct((DEPTH, rows, n), F32),
        compiler_params=pltpu.CompilerParams(dimension_semantics=("arbitrary", "arbitrary"),
                                             vmem_limit_bytes=VMEM_LIMIT),
        name="ada",
    )(c_all, ada_w, ada_b.reshape(DEPTH, 1, n))


PREP_ROWS = 128


def _prep_w_in_kernel(w_ref, s0_ref, s1_ref, s2_ref, o_ref):
    j = pl.program_id(1)
    j_small = SMALL_OFF // PREP_ROWS

    @pl.when((j != j_small) & (j != j_small + 1))
    def _copy():
        o_ref[...] = w_ref[0].astype(BF16)

    @pl.when(j == j_small)
    def _small():
        o_ref[...] = jnp.zeros(o_ref.shape, BF16)
        o_ref[0:8, :] = s0_ref[0].astype(BF16)
        o_ref[8:16, :] = s1_ref[0].astype(BF16)
        o_ref[16:24, :] = s2_ref[0].astype(BF16)

    @pl.when(j == j_small + 1)
    def _pad():
        o_ref[...] = jnp.zeros(o_ref.shape, BF16)


def _prep_src_row(j):
    r = PREP_ROWS
    return jnp.where(j < 1024 // r, r * j,
                     jnp.where(j < 2816 // r, r * j + 8,
                               jnp.where(j < SMALL_OFF // r, r * j + 16,
                                         jnp.where(j < MIX_W // r, 0, r * j - MIX_W + 3864))))


def _prep_w_in_call(w_in):
    wt = jnp.swapaxes(w_in, 1, 2)
    depth, _, d = wt.shape

    def rows(n, index):
        return pl.BlockSpec((pl.Element(1), pl.Element(n), pl.Element(d)), index)

    return pl.pallas_call(
        _prep_w_in_kernel,
        grid=(depth, PROJ_W // PREP_ROWS),
        in_specs=[rows(PREP_ROWS, lambda l, j: (l, pl.multiple_of(_prep_src_row(j), 8), 0)),
                  rows(8, lambda l, j: (l, 1024, 0)), rows(8, lambda l, j: (l, 2824, 0)),
                  rows(8, lambda l, j: (l, 3856, 0))],
        out_specs=pl.BlockSpec((None, PREP_ROWS, d), lambda l, j: (l, j, 0)),
        out_shape=jax.ShapeDtypeStruct((depth, PROJ_W, d), BF16),
        compiler_params=pltpu.CompilerParams(dimension_semantics=("arbitrary", "arbitrary"),
                                             vmem_limit_bytes=VMEM_LIMIT),
        name="prep_w_in",
    )(wt, wt, wt, wt)


def _rms_mod(x, nw, sc, sh):
    ms = jnp.mean(x * x, axis=-1, keepdims=True)
    return x * lax.rsqrt(ms + EPS) * nw * (1.0 + sc) + sh


def _tok_spec(gblk, width):
    g, r = gblk
    return pl.BlockSpec((g, r, width), lambda i, j: (i, j, 0))


def _mod_spec(gblk, layer, k):
    return pl.BlockSpec((None, gblk[0], 1, D_MODEL), lambda i, j: (layer, i, 0, k))


def _layer_spec(shape, layer):
    nd = len(shape)
    return pl.BlockSpec((None,) + tuple(shape[1:]), lambda i, j: (layer,) + (0,) * (nd - 1),
                        pipeline_mode=pl.Buffered(1))


def _dense_params():
    return pltpu.CompilerParams(dimension_semantics=("arbitrary", "arbitrary"), vmem_limit_bytes=VMEM_LIMIT)


def _inproj_kernel(x_ref, nw_ref, sc_ref, sh_ref, w_ref, pm_ref, gate_ref):
    g, r, _ = x_ref.shape
    h = _rms_mod(x_ref[...], nw_ref[...], sc_ref[...], sh_ref[...]).astype(BF16).reshape(g * r, D_MODEL)
    pm_ref[...] = _mm(h, w_ref[0:MIX_W, :], _NT).reshape(g, r, MIX_W)
    gates = _sigmoid(_mm(h, w_ref[MIX_W:PROJ_W, :], _NT))
    gate_ref[...] = gates.astype(gate_ref.dtype).reshape(g, r, GATE_W)


def _inproj_call(x, layer, nw, mods, w, gblk):
    bg, t, _ = x.shape
    return pl.pallas_call(
        _inproj_kernel,
        grid=(bg // gblk[0], t // gblk[1]),
        in_specs=[_tok_spec(gblk, D_MODEL), _layer_spec(nw.shape, layer),
                  _mod_spec(gblk, layer, 1), _mod_spec(gblk, layer, 0), _layer_spec(w.shape, layer)],
        out_specs=[_tok_spec(gblk, MIX_W), _tok_spec(gblk, GATE_W)],
        out_shape=[jax.ShapeDtypeStruct((bg, t, MIX_W), F32),
                   jax.ShapeDtypeStruct((bg, t, GATE_W), _row_dtype(gblk[1]))],
        compiler_params=_dense_params(), name="inproj",
    )(x, nw, mods, mods, w)


def _merge_kernel(br_ref, gate_ref, x_ref, gt_ref, wb_ref, wo_ref, o_ref):
    g, r, _ = x_ref.shape
    br = br_ref[...].astype(F32).reshape(g * r, N_BRANCH * BRANCH_W)
    merged = None
    for n in range(N_BRANCH):
        up = _mm(br[:, n * BRANCH_W:(n + 1) * BRANCH_W], wb_ref[n])
        gate = gate_ref[:, :, n * D_MODEL:(n + 1) * D_MODEL].astype(F32).reshape(g * r, D_MODEL)
        t = gate * up
        merged = t if merged is None else merged + t
    y = _mm(merged, wo_ref[...]).reshape(g, r, D_MODEL)
    o_ref[...] = x_ref[...] + gt_ref[...] * y


def _merge_call(br, gates, x, layer, mods, wb, wo, gblk):
    bg, t, _ = x.shape
    return pl.pallas_call(
        _merge_kernel,
        grid=(bg // gblk[0], t // gblk[1]),
        in_specs=[_tok_spec(gblk, N_BRANCH * BRANCH_W), _tok_spec(gblk, GATE_W), _tok_spec(gblk, D_MODEL),
                  _mod_spec(gblk, layer, 2), _layer_spec(wb.shape, layer), _layer_spec(wo.shape, layer)],
        out_specs=_tok_spec(gblk, D_MODEL),
        out_shape=jax.ShapeDtypeStruct((bg, t, D_MODEL), F32),
        compiler_params=_dense_params(), name="merge",
    )(br, gates, x, mods, wb, wo)


def _ffn_kernel(final, x_ref, nw_ref, sc_ref, sh_ref, gt_ref, w1_ref, w2_ref, fw_ref, o_ref):
    g, r, _ = x_ref.shape
    x = x_ref[...]
    h = _rms_mod(x, nw_ref[...], sc_ref[...], sh_ref[...]).astype(BF16).reshape(g * r, D_MODEL)
    a = _mm(h, w1_ref[:, 0:D_FF])
    b = _mm(h, w1_ref[:, D_FF:2 * D_FF])
    y = _mm(_silu(a) * b, w2_ref[...]).reshape(g, r, D_MODEL)
    x2 = x + gt_ref[...] * y
    if final:
        ms = jnp.mean(x2 * x2, axis=-1, keepdims=True)
        x2 = x2 * lax.rsqrt(ms + EPS) * fw_ref[...]
    o_ref[...] = x2


def _ffn_call(x, layer, nw, mods, w1, w2, fw, gblk, final):
    bg, t, _ = x.shape
    return pl.pallas_call(
        functools.partial(_ffn_kernel, final),
        grid=(bg // gblk[0], t // gblk[1]),
        in_specs=[_tok_spec(gblk, D_MODEL), _layer_spec(nw.shape, layer),
                  _mod_spec(gblk, layer, 4), _mod_spec(gblk, layer, 3), _mod_spec(gblk, layer, 5),
                  _layer_spec(w1.shape, layer), _layer_spec(w2.shape, layer),
                  pl.BlockSpec(fw.shape, lambda i, j: (0, 0), pipeline_mode=pl.Buffered(1))],
        out_specs=_tok_spec(gblk, D_MODEL),
        out_shape=jax.ShapeDtypeStruct((bg, t, D_MODEL), F32),
        compiler_params=_dense_params(), name="ffn",
    )(x, nw, mods, mods, mods, w1, w2, fw)


def _small_params(gdn_dt_bias, ssd_dt_bias, ml_b_i, ml_b_f, gdn_a_log, ssd_a_log):
    z4 = jnp.zeros((4,), F32)
    bias = jnp.concatenate([z4, gdn_dt_bias, ssd_dt_bias, ml_b_i, ml_b_f, jnp.zeros((LANE - 24,), F32)])
    alog = jnp.concatenate([z4, gdn_a_log, ssd_a_log, jnp.zeros((LANE - 16,), F32)])
    return jnp.concatenate([bias[None], alog[None], jnp.zeros((6, LANE), F32)], axis=0)


def _layer_params(l, gdn_conv_w, gdn_conv_b, gdn_a_log, gdn_dt_bias, gdn_norm_w, hg_lb_logits, hg_norm_w,
                  ssd_conv_w, ssd_conv_b, ssd_a_log, ssd_dt_bias, ssd_d, ssd_norm_w, ml_b_i, ml_b_f, ml_norm_w):
    sp = _small_params(gdn_dt_bias[l], ssd_dt_bias[l], ml_b_i[l], ml_b_f[l], gdn_a_log[l], ssd_a_log[l])
    vec = jnp.stack([jnp.tile(gdn_norm_w[l], N_HEAD), jnp.tile(hg_norm_w[l], N_HEAD), ssd_norm_w[l],
                     jnp.tile(ml_norm_w[l], N_HEAD), jnp.repeat(ssd_d[l], SSD_P)]
                    + [jnp.zeros((256,), F32)] * 3)
    return [sp, gdn_conv_w[l], gdn_conv_b[l][None], ssd_conv_w[l], ssd_conv_b[l][None], vec,
            hg_lb_logits.astype(F32)]


def _trunk(x, mods, state_shapes, states_in, L, nv, gblk, bb, W, mixer_params):
    new_states = None
    for l in range(DEPTH):
        pm, gates = _inproj_call(x, l, W['norm1'], mods, W['w_in'], gblk)
        br, new_states = _mixer_call(pm, state_shapes, states_in, new_states, mixer_params[l], L, nv, l, bb)
        x1 = _merge_call(br, gates, x, l, mods, W['w_branch'], W['w_out'], gblk)
        x = _ffn_call(x1, l, W['norm2'], mods, W['ffn_w_in'], W['ffn_w_out'], W['final'], gblk,
                      final=(l == DEPTH - 1))
    return x, new_states


def kernel(x_prompt, x_sample, c_prompt, c_sample, state_gdn, state_gdn_conv, state_hgrn, state_ssd, state_ssd_conv, state_mlstm_c, state_mlstm_n, state_mlstm_m, ada_w, ada_b, norm1_w, norm2_w, w_in, gdn_conv_w, gdn_conv_b, gdn_a_log, gdn_dt_bias, gdn_norm_w, hg_lb_logits, hg_norm_w, ssd_conv_w, ssd_conv_b, ssd_a_log, ssd_dt_bias, ssd_d, ssd_norm_w, ml_b_i, ml_b_f, ml_norm_w, w_branch, w_out, ffn_w_in, ffn_w_out, final_norm_w):
    bp, tp, _ = x_prompt.shape
    bs, ts, _ = x_sample.shape
    ls = 8
    assert tp % CHUNK == 0 and ts <= ls

    W = dict(w_in=_prep_w_in_call(w_in), w_branch=w_branch.astype(BF16), w_out=w_out.astype(BF16),
             ffn_w_in=ffn_w_in.astype(BF16), ffn_w_out=ffn_w_out.astype(BF16),
             norm1=norm1_w[:, None, :], norm2=norm2_w[:, None, :], final=final_norm_w[None])
    mixer_params = [_layer_params(l, gdn_conv_w, gdn_conv_b, gdn_a_log, gdn_dt_bias, gdn_norm_w, hg_lb_logits,
                                  hg_norm_w, ssd_conv_w, ssd_conv_b, ssd_a_log, ssd_dt_bias, ssd_d, ssd_norm_w,
                                  ml_b_i, ml_b_f, ml_norm_w) for l in range(DEPTH)]

    mods = _ada_call(jnp.concatenate([c_prompt, c_sample], axis=0), ada_w, ada_b)
    mods_p = mods[:, :bp].reshape(DEPTH, bp, 1, 6 * D_MODEL)
    mods_s = mods[:, bp:].reshape(DEPTH, bs, 1, 6 * D_MODEL)
    sample_states = [state_gdn, state_gdn_conv, state_hgrn, state_ssd, state_ssd_conv,
                     state_mlstm_c, state_mlstm_n, state_mlstm_m.reshape(DEPTH, bs, 1, N_HEAD)]
    shapes_s = [s.shape for s in sample_states]
    shapes_p = [(DEPTH, bp) + tuple(s[2:]) for s in shapes_s]

    y_p, new_p = _trunk(x_prompt, mods_p, shapes_p, None, CHUNK, CHUNK, (1, 256), 4, W, mixer_params)
    xs_pad = jnp.pad(x_sample, ((0, 0), (0, ls - ts), (0, 0)))
    y_s, new_s = _trunk(xs_pad, mods_s, shapes_s, sample_states, ls, ts, (32, 8), 8, W, mixer_params)
    new_p[7] = new_p[7].reshape(DEPTH, bp, N_HEAD)
    new_s[7] = new_s[7].reshape(DEPTH, bs, N_HEAD)
    return (y_p, y_s[:, :ts]) + tuple(new_p) + tuple(new_s)
```

```python
import functools
import math

import numpy as np
import jax
import jax.numpy as jnp
from jax import lax
from jax.experimental import pallas as pl
from jax.experimental.pallas import tpu as pltpu

F32 = jnp.float32
BF16 = jnp.bfloat16

D_MODEL = 1024
DEPTH = 2
N_BRANCH = 4
BRANCH_W = 256
N_HEAD = 4
HEAD_D = 64
SSD_H = 8
SSD_P = 32
SSD_N = 64
SSD_G = 2
CONV_K = 4
CHUNK = 64
D_FF = 2816
EPS = 1e-6
NEG = -1e30
GDN_CONV_W = 768
SSD_CONV_W = 512
SSD_XBC_OFF = 2304
MIX_W = 4096
GATE_W = 4096
PROJ_W = MIX_W + GATE_W
SMALL_OFF = 3840
LANE = 128
VMEM_LIMIT = 56 * 1024 * 1024

ST_BETA, ST_GA, ST_DT, ST_MI, ST_MF = 0, 4, 8, 16, 20

_NN = (((1,), (0,)), ((), ()))
_NT = (((1,), (1,)), ((), ()))
_TN = (((0,), (0,)), ((), ()))


def _mm(a, b, dims=_NN):
    return lax.dot_general(a.astype(BF16), b.astype(BF16), dims, preferred_element_type=F32)


def _split(x, n):
    parts, r = [], x
    for i in range(n):
        p = r.astype(BF16)
        parts.append(p)
        if i < n - 1:
            r = r - p.astype(F32)
    return parts


def _mm01(x, m, n):
    out = None
    for p in _split(x, n):
        t = lax.dot_general(p, m.astype(BF16), _NN, preferred_element_type=F32)
        out = t if out is None else out + t
    return out


def _m01m(m, x, n):
    out = None
    for p in _split(x, n):
        t = lax.dot_general(m, p, _NN, preferred_element_type=F32)
        out = t if out is None else out + t
    return out


def _sigmoid(x):
    return jax.nn.sigmoid(x)


def _silu(x):
    return x * jax.nn.sigmoid(x)


def _softplus(x):
    return jnp.maximum(x, 0.0) + jnp.log(1.0 + jnp.exp(-jnp.abs(x)))


def _logsig(x):
    return jnp.minimum(x, 0.0) - jnp.log(1.0 + jnp.exp(-jnp.abs(x)))


def _row_dtype(rows):
    return BF16 if rows % 16 == 0 else F32


def _expand(base, heads, width):
    e = np.zeros((LANE, heads * width), np.float32)
    for h in range(heads):
        e[base + h, h * width:(h + 1) * width] = 1.0
    return e


def _cat_segments(segs):
    cols, off, pos, seen = [], {}, 0, {}
    for name, m in segs:
        key = m.tobytes() + bytes(str(m.shape), 'ascii')
        if key in seen:
            off[name] = seen[key]
            continue
        w = m.shape[1]
        wp = -(-w // LANE) * LANE
        mp = np.zeros((m.shape[0], wp), np.float32)
        mp[:, :w] = m
        cols.append(mp)
        off[name] = seen[key] = (pos, w)
        pos += wp
    return np.concatenate(cols, axis=1), off


@functools.lru_cache(maxsize=None)
def _mixer_consts(L, nv):
    J = int(round(math.log2(L)))
    assert 1 << J == L
    f = np.float32
    r = np.arange(L)
    tri = (r[None, :] <= r[:, None])
    rev = (r[:, None] < r[None, :]) & (r[None, :] <= nv - 1)
    strict = (r[None, :] < r[:, None])
    eye = np.eye(L, dtype=bool)
    lvl, ab = [], []
    for j in range(J):
        bnd = ((r >> (j + 1)) << (j + 1)) + (1 << j) - 1
        low = ((r >> j) & 1) == 1
        a = low[:, None] & (bnd[:, None] < r[None, :]) & (r[None, :] <= r[:, None])
        b = (~low)[:, None] & (r[:, None] < r[None, :]) & (r[None, :] <= bnd[:, None])
        ab.append(a | b)
        same = (r[:, None] >> (j + 1)) == (r[None, :] >> (j + 1))
        lvl.append(same & low[:, None] & (~low)[None, :])

    def tile(m, n):
        return np.tile(m.astype(f), (1, n))

    pr = np.repeat(np.arange(2), L)
    lr2 = np.tile(r, 2)
    ch = np.arange(LANE)
    ch2 = np.arange(2 * LANE)
    c = dict(
        trirev=np.concatenate([tri, rev], axis=0).astype(f),
        mhg=np.concatenate(ab + [tri, rev], axis=0).astype(f),
        ones_ll=np.ones((L, L), f), tri=tri.astype(f),
        tril2=tile(tri, 2), strict2=tile(strict, 2), tril4=tile(tri, 4),
        eye4=tile(eye, 4), supper16=tile(strict, 16),
        e_dr=np.concatenate([_expand(ST_GA, 4, L), _expand(ST_DT, SSD_H, L), _expand(ST_MF, 4, L)], axis=1),
        hmaskp=np.stack([tile(eye, 2)] + [tile(m, 2) for m in lvl]),
        eyep=np.eye(2 * L, dtype=f),
        lvlp=np.stack([(pr[:, None] == pr[None, :]) & m[lr2[:, None], lr2[None, :]] for m in lvl]).astype(f),
        stkp=(pr[:, None] == ch[None, :] // HEAD_D).astype(f),
        bdp=(ch[:, None] // HEAD_D == ch[None, :] // HEAD_D).astype(f),
        pm2=np.stack([ch < HEAD_D, ch >= HEAD_D]).astype(f),
        stk4s=(np.repeat(np.arange(4), L)[:, None] == ch[None, :] // SSD_P).astype(f),
        gbd=(ch[:, None] // SSD_N == ch2[None, :] // LANE).astype(f),
        rowvalid=(r[:, None] <= nv - 1).astype(f) * np.ones((1, LANE), f),
        bd64=(ch2[:, None] // 64 == ch2[None, :] // 64).astype(f),
        bd128=(ch2[:, None] // 128 == ch2[None, :] // 128).astype(f),
    )
    c['e_act'], off_act = _cat_segments([('b_r', _expand(ST_BETA, 4, L)), ('b_c', _expand(ST_BETA, 4, HEAD_D)),
                                         ('dt_c', _expand(ST_DT, SSD_H, SSD_P)),
                                         ('i_r', _expand(ST_MI, 4, L)), ('i_c', _expand(ST_MI, 4, HEAD_D))])
    c['e_cs'], off_cs = _cat_segments([('g_c', _expand(ST_GA, 4, HEAD_D)), ('s_c', _expand(ST_DT, SSD_H, SSD_P)),
                                       ('m_r', _expand(ST_MF, 4, L)), ('m_c', _expand(ST_MF, 4, HEAD_D))])
    c['e_rev'], off_rev = _cat_segments([('g_c', _expand(ST_GA, 4, HEAD_D)), ('s_c', _expand(ST_DT, SSD_H, SSD_P)),
                                         ('m_c', _expand(ST_MF, 4, HEAD_D))])
    return c, dict(act=off_act, cs=off_cs, rev=off_rev), J


_CONST_ORDER = ('trirev', 'mhg', 'ones_ll', 'tri', 'tril2', 'strict2', 'tril4', 'eye4', 'supper16', 'e_dr', 'hmaskp',
                'eyep', 'lvlp', 'stkp', 'bdp', 'pm2', 'stk4s', 'gbd', 'rowvalid', 'bd64', 'bd128', 'e_act', 'e_cs',
                'e_rev')
_BF16_CONSTS = ('trirev', 'mhg', 'ones_ll', 'tri', 'supper16', 'e_dr', 'lvlp', 'stkp', 'stk4s', 'bd64', 'bd128',
                'e_act', 'e_cs', 'e_rev')


def _conv_silu(ext, w, b, L):
    y = b
    for i in range(CONV_K):
        y = y + ext[5 + i:5 + i + L, :] * w[i:i + 1, :]
    return _silu(y)


def _run_interleaved(tasks):
    tasks = list(tasks)
    while tasks:
        alive = []
        for t in tasks:
            try:
                next(t)
                alive.append(t)
            except StopIteration:
                pass
        tasks = alive


def _mixer_chunk(pm, ext_g, ext_s, st, P, K, offs, L, nv, J, layer):
    lane = lax.broadcasted_iota(jnp.int32, (1, LANE), 1)
    rowvalid = K['rowvalid'][:, 0:1]
    tril2 = K['tril2'] > 0.0
    tril4 = K['tril4'] > 0.0
    stkp, bdp = K['stkp'], K['bdp']
    upper_c = lane >= HEAD_D
    upper_r = lax.broadcasted_iota(jnp.int32, (1, 2 * L), 1) >= L

    def seg(x, which, name):
        o, w = offs[which][name]
        return x[:, o:o + w]

    def pair_c(x, p):
        return x[:, p * LANE:(p + 1) * LANE]

    def pair_r(x, p):
        return x[:, p * 2 * L:(p + 1) * 2 * L]

    def stack(x):
        xb = x.astype(BF16)
        return jnp.concatenate([xb, xb], axis=0) * stkp

    z = pm[:, SMALL_OFF:SMALL_OFF + LANE] + P['sp'][0:1, :]
    act = jnp.where(lane < ST_GA, _sigmoid(z),
                    jnp.where(lane < ST_MI, _softplus(z), jnp.where(lane < ST_MF, z, _logsig(z))))
    neg_a = -jnp.exp(P['sp'][1:2, :])
    dec_in = jnp.where((lane >= ST_GA) & (lane < ST_MI), neg_a * act,
                       jnp.where((lane >= ST_MF) & (lane < ST_MF + 4), act, 0.0))
    cr = _m01m(K['trirev'], dec_in, 2)
    cs, rev = cr[:L], cr[L:]
    ea = _mm01(act, K['e_act'], 2)
    ec = _mm01(cs, K['e_cs'], 2)
    er = _mm01(rev, K['e_rev'], 2)
    m_r = seg(ec, 'cs', 'm_r')
    g_c, s_c, m_c = seg(ec, 'cs', 'g_c'), seg(ec, 'cs', 's_c'), seg(ec, 'cs', 'm_c')
    b_r, b_c, dt_c = seg(ea, 'act', 'b_r'), seg(ea, 'act', 'b_c'), seg(ea, 'act', 'dt_c')
    i_r, i_c = seg(ea, 'act', 'i_r'), seg(ea, 'act', 'i_c')
    rg_c, rs_c, rm_c = seg(er, 'rev', 'g_c'), seg(er, 'rev', 's_c'), seg(er, 'rev', 'm_c')
    ym = _mm(dec_in, K['e_dr']).astype(BF16) * K['supper16']
    d_all = lax.dot_general(K['tri'], ym, _NN, preferred_element_type=F32)
    d_g, d_s, d_m = d_all[:, 0:4 * L], d_all[:, 4 * L:12 * L], d_all[:, 12 * L:16 * L]
    row_i = _m01m(K['ones_ll'], i_r * K['eye4'], 2)

    res = dict(o_gdn=[None, None], gdn=[None, None], o_hg=[None, None], hg=[None, None],
               o_ml=[None, None], mlc=[None, None], mln=[None, None], mlm=[None] * N_HEAD)

    qkv = _conv_silu(ext_g, P['cwg'], P['cbg'], L)
    q, k, v = qkv[:, 0:256], qkv[:, 256:512], qkv[:, 512:768]
    ss = _mm(jnp.concatenate([q * q, k * k], axis=0), K['bd64'])
    q = q * (lax.rsqrt(ss[:L] + EPS) * (HEAD_D ** -0.5))
    k = k * lax.rsqrt(ss[L:] + EPS)

    def gdn_task(p):
        qt, kt, vt = pair_c(q, p), pair_c(k, p), pair_c(v, p)
        beta_c, gc = pair_c(b_c, p), pair_c(g_c, p)
        eg = jnp.exp(gc)
        dec = jnp.where(tril2, jnp.exp(jnp.where(tril2, pair_r(d_g, p), 0.0)), 0.0)
        kkqk = _mm(jnp.concatenate([kt, qt], axis=0), stack(kt), _NT)
        yield
        n = (kkqk[:L] * dec * pair_r(b_r, p) * K['strict2']).astype(BF16)
        nbd = jnp.concatenate([n, n], axis=0)
        t = K['eyep'] - (nbd * K['lvlp'][0]).astype(F32)
        for j in range(1, J):
            x = _mm(t, nbd * K['lvlp'][j])
            yield
            t = t - _mm(x, t)
            yield
        sv = _mm(t, stack(vt * beta_c))
        sk = _mm(t, stack(kt * (beta_c * eg)))
        yield
        s_p = st['gdn'][p]
        r = _mm(jnp.concatenate([sk[:L] + sk[L:], qt * eg], axis=0), s_p)
        yield
        u = sv[:L] + sv[L:] - r[:L]
        res['o_gdn'][p] = r[L:] + _mm(kkqk[L:] * dec, stack(u))
        kw = kt * (jnp.exp(pair_c(rg_c, p)) * rowvalid)
        res['gdn'][p] = s_p * jnp.exp(gc[nv - 1:nv, :]) + bdp * _mm(kw, u, _TN)
        yield

    lg = P['lg']
    mx = lg[0:1, :]
    for i in range(1, DEPTH):
        mx = jnp.maximum(mx, lg[i:i + 1, :])
    ex = [jnp.exp(lg[i:i + 1, :] - mx) for i in range(DEPTH)]
    tot = ex[0]
    for i in range(1, DEPTH):
        tot = tot + ex[i]
    sm = [e / tot for e in ex]
    cum = sm[0]
    for i in range(1, layer + 1):
        cum = cum + sm[i]
    lb = cum - sm[0]
    lb_pos = lb > 0
    log_lb = jnp.log(jnp.where(lb_pos, lb, 1.0))
    hq = _silu(pm[:, 1024:1280])
    fz = pm[:, 1280:1536]
    hv = pm[:, 1536:1792]
    ls = _logsig(fz)
    t2 = jnp.log1p(-lb) + ls
    la = jnp.maximum(log_lb, t2) + jnp.log1p(jnp.exp(-jnp.abs(log_lb - t2)))
    logf = jnp.where(lb_pos, la, ls)
    kg = (1.0 - lb) * _sigmoid(-fz)
    ey = jnp.exp(_m01m(K['mhg'], logf, 2))
    eg_h = ey[J * L:(J + 1) * L]
    qs = [hq] + [hq * ey[j * L:(j + 1) * L] for j in range(J)]
    ks = [kg] + [kg * ey[j * L:(j + 1) * L] for j in range(J)]
    qe = hq * eg_h
    kwr = kg * (ey[(J + 1) * L:(J + 2) * L] * rowvalid)

    def hg_task(p):
        a_p = None
        for i in range(J + 1):
            t_i = K['hmaskp'][i] * _mm(pair_c(qs[i], p), stack(pair_c(ks[i], p)), _NT)
            a_p = t_i if a_p is None else a_p + t_i
            yield
        st_p = st['hg'][p]
        res['o_hg'][p] = _mm(pair_c(qe, p), st_p, _NT) + _mm(a_p, stack(pair_c(hv, p)))
        yield
        res['hg'][p] = (st_p * pair_c(eg_h[nv - 1:nv, :], p)
                        + bdp * _mm(pair_c(hv, p), pair_c(kwr, p), _TN))
        yield

    xbc = _conv_silu(ext_s, P['cws'], P['cbs'], L)
    xs, bs, cc = xbc[:, 0:256], xbc[:, 256:384], xbc[:, 384:512]
    vs = xs * dt_c
    bsb = bs.astype(BF16)
    bs4 = jnp.concatenate([bsb] * 4, axis=0)

    def ssd_task():
        o_intra = []
        for g in range(SSD_G):
            cbw = _mm(cc * K['pm2'][g:g + 1, :], bs4, _NT)
            dec = jnp.where(tril4, jnp.exp(jnp.where(tril4, d_s[:, g * 4 * L:(g + 1) * 4 * L], 0.0)), 0.0)
            vb = pair_c(vs, g).astype(BF16)
            o_intra.append(_mm(cbw * dec, jnp.concatenate([vb] * 4, axis=0) * K['stk4s']))
            yield
        s_all = st['ssd']
        res['o_ssd'] = jnp.exp(s_c) * _mm(cc, s_all) + jnp.concatenate(o_intra, axis=1)
        res['ssd'] = (s_all * jnp.exp(s_c[nv - 1:nv, :])
                      + K['gbd'] * _mm(bs * rowvalid, vs * jnp.exp(rs_c), _TN))
        yield

    mq = pm[:, 2816:3072] * (HEAD_D ** -0.5)
    mk = pm[:, 3072:3328]
    mv = pm[:, 3328:3584]

    def ml_task(p, delay):
        for _ in range(delay):
            yield
        qt, kt, vt = pair_c(mq, p), pair_c(mk, p), pair_c(mv, p)
        bm_r, bm_c = pair_r(m_r, p), pair_c(m_c, p)
        mp0 = st['mlm'][:, 2 * p:2 * p + 1]
        mp1 = st['mlm'][:, 2 * p + 1:2 * p + 2]
        logw = jnp.where(tril2, pair_r(d_m, p) + pair_r(row_i, p), NEG)
        mx0 = jnp.max(jnp.where(upper_r, NEG, logw), axis=-1, keepdims=True)
        mx1 = jnp.max(jnp.where(upper_r, logw, NEG), axis=-1, keepdims=True)
        l0_r = bm_r + jnp.where(upper_r, mp1, mp0)
        l0_c = bm_c + jnp.where(upper_c, mp1, mp0)
        mt_r = jnp.maximum(l0_r, jnp.where(upper_r, mx1, mx0))
        mt_c = jnp.maximum(l0_c, jnp.where(upper_c, mx1, mx0))
        w = jnp.where(tril2, jnp.exp(logw - mt_r), 0.0)
        w0 = jnp.exp(l0_c - mt_c)
        qk = _mm(qt, stack(kt), _NT) * w
        yield
        c_p = st['mlc'][p]
        n_p = st['mln'][p]
        num = w0 * _mm(qt, c_p) + _mm(qk, stack(vt))
        den = w0 * _mm01(qt * n_p, bdp, 2) + _mm01(qk, stkp, 2)
        res['o_ml'][p] = num * (1.0 / jnp.maximum(jnp.abs(den), jnp.exp(-mt_c)))
        yield
        m_l = mt_c[nv - 1:nv, :]
        wl0 = jnp.exp(bm_c[nv - 1:nv, :] + jnp.where(upper_c, mp1, mp0) - m_l)
        kwl = kt * (jnp.exp(pair_c(rm_c, p) + pair_c(i_c, p) - m_l) * rowvalid)
        res['mlc'][p] = c_p * wl0 + bdp * _mm(kwl, vt, _TN)
        res['mln'][p] = n_p * wl0 + jnp.sum(kwl, axis=0, keepdims=True)
        res['mlm'][2 * p] = m_l[:, 0:1]
        res['mlm'][2 * p + 1] = m_l[:, HEAD_D:HEAD_D + 1]
        yield

    def finish():
        o_gdn = jnp.concatenate(res['o_gdn'], axis=1)
        o_hg = jnp.concatenate(res['o_hg'], axis=1)
        hh = jnp.concatenate(res['o_ml'], axis=1)
        ms = _mm(jnp.concatenate([o_gdn * o_gdn, o_hg * o_hg, hh * hh], axis=0), K['bd64']) * (1.0 / HEAD_D)
        out_a = o_gdn * lax.rsqrt(ms[0:L] + EPS) * P['vec'][0:1, :] * _silu(pm[:, 768:1024])
        out_b = o_hg * lax.rsqrt(ms[L:2 * L] + EPS) * P['vec'][1:2, :] * _silu(pm[:, 1792:2048])
        out_d = hh * lax.rsqrt(ms[2 * L:3 * L] + EPS) * P['vec'][3:4, :] * _sigmoid(pm[:, 3584:3840])
        ys = (res['o_ssd'] + P['vec'][4:5, :] * xs) * _silu(pm[:, 2048:2304])
        out_c = ys * lax.rsqrt(_mm(ys * ys, K['bd128']) * (1.0 / (2 * HEAD_D)) + EPS) * P['vec'][2:3, :]
        branches = jnp.concatenate([out_a, out_b, out_c, out_d], axis=1)
        new = dict(gdn=res['gdn'], hg=res['hg'], ssd=res['ssd'], mlc=res['mlc'], mln=res['mln'], mlm=res['mlm'])
        return branches, new

    tasks = ([gdn_task(p) for p in range(2)] + [hg_task(p) for p in range(2)]
             + [ssd_task()] + [ml_task(p, 3 + 2 * p) for p in range(2)])
    return tasks, finish


def _mixer_kernel(L, nv, J, layer, bb, offs, has_init, n_alias, *refs):
    it = iter(refs)
    pm_ref = next(it)
    if has_init:
        gdn0, cg0, hg0, ssd0, cs0, mc0, mn0, mm0 = (next(it) for _ in range(8))
    sp_ref, cwg_ref, cbg_ref, cws_ref, cbs_ref, vec_ref, lg_ref = (next(it) for _ in range(7))
    kref = {name: next(it) for name in _CONST_ORDER}
    for _ in range(n_alias):
        next(it)
    br_ref = next(it)
    gdn1, cg1, hg1, ssd1, cs1, mc1, mn1, mm1 = (next(it) for _ in range(8))
    sg, sh, sc, sn, ss, extg, exts = (next(it) for _ in range(7))

    c = pl.program_id(1)
    nc = pl.num_programs(1)
    n_sub = SSD_H // SSD_G

    @pl.when(c == 0)
    def _init():
        sg[...] = jnp.zeros_like(sg)
        sh[...] = jnp.zeros_like(sh)
        sc[...] = jnp.zeros_like(sc)
        ss[...] = jnp.zeros_like(ss)
        extg[:, 0:8, :] = jnp.zeros((bb, 8, GDN_CONV_W), F32)
        exts[:, 0:8, :] = jnp.zeros((bb, 8, SSD_CONV_W), F32)
        if not has_init:
            sn[...] = jnp.zeros_like(sn)
            mm1[...] = jnp.zeros_like(mm1)
            return
        for s in range(bb):
            for h in range(N_HEAD):
                p, lo = h // 2, (h % 2) * HEAD_D
                sg[s, p, lo:lo + HEAD_D, lo:lo + HEAD_D] = gdn0[s, h]
                sh[s, p, lo:lo + HEAD_D, lo:lo + HEAD_D] = hg0[s, h].T
                sc[s, p, lo:lo + HEAD_D, lo:lo + HEAD_D] = mc0[s, h]
                sn[s, p, :, lo:lo + HEAD_D] = mn0[s, h:h + 1, :]
            for h in range(SSD_H):
                g = h // n_sub
                ss[s, g * SSD_N:(g + 1) * SSD_N, h * SSD_P:(h + 1) * SSD_P] = ssd0[s, h]
        mm1[...] = mm0[...]
        extg[:, 5:8, :] = cg0[...]
        exts[:, 5:8, :] = cs0[...]

    P = dict(sp=sp_ref[...], cwg=cwg_ref[...], cbg=cbg_ref[...], cws=cws_ref[...], cbs=cbs_ref[...],
             vec=vec_ref[...], lg=lg_ref[...])
    K = {name: r[...] for name, r in kref.items()}
    tasks, finishers = [], []
    for s in range(bb):
        pm = pm_ref[s]
        extg[s, 8:8 + L, :] = pm[:, 0:GDN_CONV_W]
        exts[s, 8:8 + L, :] = pm[:, SSD_XBC_OFF:SSD_XBC_OFF + SSD_CONV_W]
        st = dict(gdn=sg[s], hg=sh[s], ssd=ss[s], mlc=sc[s], mln=sn[s], mlm=mm1[s])
        t, fin = _mixer_chunk(pm, extg.at[s], exts.at[s], st, P, K, offs, L, nv, J, layer)
        tasks.append(t)
        finishers.append(fin)
    _run_interleaved([t for group in zip(*tasks) for t in group])
    for s in range(bb):
        branches, new = finishers[s]()
        br_ref[s] = branches.astype(br_ref.dtype)
        for p in range(2):
            sg[s, p] = new['gdn'][p]
            sh[s, p] = new['hg'][p]
            sc[s, p] = new['mlc'][p]
            sn[s, p] = new['mln'][p]
        for h in range(N_HEAD):
            mm1[s, :, h:h + 1] = new['mlm'][h]
        ss[s] = new['ssd']
        tail_g = extg[s, 8 + nv - 3:8 + nv, :]
        tail_s = exts[s, 8 + nv - 3:8 + nv, :]
        extg[s, 5:8, :] = tail_g
        exts[s, 5:8, :] = tail_s
        cg1[s] = tail_g
        cs1[s] = tail_s

    @pl.when(c == nc - 1)
    def _fin():
        for s in range(bb):
            for h in range(N_HEAD):
                p, lo = h // 2, (h % 2) * HEAD_D
                gdn1[s, h] = sg[s, p, lo:lo + HEAD_D, lo:lo + HEAD_D]
                hg1[s, h] = sh[s, p, lo:lo + HEAD_D, lo:lo + HEAD_D].T
                mc1[s, h] = sc[s, p, lo:lo + HEAD_D, lo:lo + HEAD_D]
                mn1[s, h:h + 1, :] = sn[s, p, :, lo:lo + HEAD_D]
            for h in range(SSD_H):
                g = h // n_sub
                ssd1[s, h] = ss[s, g * SSD_N:(g + 1) * SSD_N, h * SSD_P:(h + 1) * SSD_P]


def _full_spec(a):
    nd = a.ndim
    return pl.BlockSpec(a.shape, lambda b, c, _nd=nd: (0,) * _nd)


def _mixer_call(proj, state_shapes, states_in, prev_out, params, L, nv, layer, bb):
    bg, t, _ = proj.shape
    nchunk = t // L
    assert bg % bb == 0
    consts, offs, J = _mixer_consts(L, nv)
    const_arrays = [jnp.asarray(consts[n], BF16 if n in _BF16_CONSTS else F32) for n in _CONST_ORDER]

    def st_spec(shape):
        nd = len(shape)
        return pl.BlockSpec((None, bb) + tuple(shape[2:]), lambda b, c, _nd=nd: (layer, b) + (0,) * (_nd - 2))

    has_init = states_in is not None
    n_alias = 0 if prev_out is None else len(prev_out)
    inputs = [proj] + (list(states_in) if has_init else []) + list(params) + const_arrays + list(prev_out or [])
    in_specs = ([pl.BlockSpec((bb, L, MIX_W), lambda b, c: (b, c, 0))]
                + ([st_spec(s) for s in state_shapes] if has_init else [])
                + [_full_spec(a) for a in params]
                + [_full_spec(a) for a in const_arrays]
                + [pl.BlockSpec(memory_space=pl.ANY)] * n_alias)
    out_shape = ([jax.ShapeDtypeStruct((bg, t, N_BRANCH * BRANCH_W), _row_dtype(L))]
                 + [jax.ShapeDtypeStruct(s, F32) for s in state_shapes])
    out_specs = ([pl.BlockSpec((bb, L, N_BRANCH * BRANCH_W), lambda b, c: (b, c, 0))]
                 + [st_spec(s) for s in state_shapes])
    first_alias = len(inputs) - n_alias
    pair = (bb, 2, LANE, LANE)
    scratch = [pltpu.VMEM(pair, F32), pltpu.VMEM(pair, F32), pltpu.VMEM(pair, F32),
               pltpu.VMEM((bb, 2, 1, LANE), F32), pltpu.VMEM((bb, SSD_G * SSD_N, SSD_H * SSD_P), F32),
               pltpu.VMEM((bb, 8 + L, GDN_CONV_W), F32), pltpu.VMEM((bb, 8 + L, SSD_CONV_W), F32)]
    outs = pl.pallas_call(
        functools.partial(_mixer_kernel, L, nv, J, layer, bb, offs, has_init, n_alias),
        grid=(bg // bb, nchunk),
        in_specs=in_specs, out_specs=out_specs, out_shape=out_shape, scratch_shapes=scratch,
        input_output_aliases={first_alias + k: 1 + k for k in range(n_alias)},
        compiler_params=pltpu.CompilerParams(dimension_semantics=("parallel", "arbitrary"),
                                             vmem_limit_bytes=VMEM_LIMIT),
        name=f"mixer_L{L}",
    )(*inputs)
    return outs[0], list(outs[1:])


def _ada_kernel(c_ref, w_ref, b_ref, o_ref):
    o_ref[...] = _mm(_silu(c_ref[...]), w_ref[...]) + b_ref[...]


def _ada_call(c_all, ada_w, ada_b):
    rows = c_all.shape[0]
    n = ada_w.shape[-1]
    tn = 1536
    return pl.pallas_call(
        _ada_kernel,
        grid=(DEPTH, n // tn),
        in_specs=[pl.BlockSpec((rows, D_MODEL), lambda l, j: (0, 0)),
                  pl.BlockSpec((None, D_MODEL, tn), lambda l, j: (l, 0, j)),
                  pl.BlockSpec((None, 1, tn), lambda l, j: (l, 0, j))],
        out_specs=pl.BlockSpec((None, rows, tn), lambda l, j: (l, 0, j)),
        out_shape=jax.ShapeDtypeStruct((DEPTH, rows, n), F32),
        compiler_params=pltpu.CompilerParams(dimension_semantics=("arbitrary", "arbitrary"),
                                             vmem_limit_bytes=VMEM_LIMIT),
        name="ada",
    )(c_all, ada_w, ada_b.reshape(DEPTH, 1, n))


PREP_ROWS = MIX_W - SMALL_OFF


def _prep_w_in_kernel(w_ref, s0_ref, s1_ref, s2_ref, o_ref):
    j = pl.program_id(1)
    j_small = SMALL_OFF // PREP_ROWS

    @pl.when(j != j_small)
    def _copy():
        o_ref[...] = w_ref[0].astype(BF16)

    @pl.when(j == j_small)
    def _small():
        o_ref[...] = jnp.zeros(o_ref.shape, BF16)
        o_ref[0:8, :] = s0_ref[0].astype(BF16)
        o_ref[8:16, :] = s1_ref[0].astype(BF16)
        o_ref[16:24, :] = s2_ref[0].astype(BF16)


def _prep_src_row(j):
    r = PREP_ROWS
    return jnp.where(j < 1024 // r, r * j,
                     jnp.where(j < 2816 // r, r * j + 8,
                               jnp.where(j < SMALL_OFF // r, r * j + 16,
                                         jnp.where(j < MIX_W // r, 0, r * j - MIX_W + 3864))))


def _prep_w_in_call(w_in):
    wt = jnp.swapaxes(w_in, 1, 2)
    depth, _, d = wt.shape

    def rows(n, index):
        return pl.BlockSpec((pl.Element(1), pl.Element(n), pl.Element(d)), index)

    return pl.pallas_call(
        _prep_w_in_kernel,
        grid=(depth, PROJ_W // PREP_ROWS),
        in_specs=[rows(PREP_ROWS, lambda l, j: (l, pl.multiple_of(_prep_src_row(j), 8), 0)),
                  rows(8, lambda l, j: (l, 1024, 0)), rows(8, lambda l, j: (l, 2824, 0)),
                  rows(8, lambda l, j: (l, 3856, 0))],
        out_specs=pl.BlockSpec((None, PREP_ROWS, d), lambda l, j: (l, j, 0)),
        out_shape=jax.ShapeDtypeStruct((depth, PROJ_W, d), BF16),
        compiler_params=pltpu.CompilerParams(dimension_semantics=("arbitrary", "arbitrary"),
                                             vmem_limit_bytes=VMEM_LIMIT),
        name="prep_w_in",
    )(wt, wt, wt, wt)


def _rms_mod(x, nw, sc, sh):
    ms = jnp.mean(x * x, axis=-1, keepdims=True)
    return x * lax.rsqrt(ms + EPS) * nw * (1.0 + sc) + sh


def _tok_spec(gblk, width):
    g, r = gblk
    return pl.BlockSpec((g, r, width), lambda i, j: (i, j, 0))


def _mod_spec(gblk, layer, k):
    return pl.BlockSpec((None, gblk[0], 1, D_MODEL), lambda i, j: (layer, i, 0, k))


def _layer_spec(shape, layer):
    nd = len(shape)
    return pl.BlockSpec((None,) + tuple(shape[1:]), lambda i, j: (layer,) + (0,) * (nd - 1),
                        pipeline_mode=pl.Buffered(1))


def _dense_params():
    return pltpu.CompilerParams(dimension_semantics=("arbitrary", "arbitrary"), vmem_limit_bytes=VMEM_LIMIT)


def _inproj_kernel(x_ref, nw_ref, sc_ref, sh_ref, w_ref, pm_ref, gate_ref):
    g, r, _ = x_ref.shape
    h = _rms_mod(x_ref[...], nw_ref[...], sc_ref[...], sh_ref[...]).astype(BF16).reshape(g * r, D_MODEL)
    pm_ref[...] = _mm(h, w_ref[0:MIX_W, :], _NT).reshape(g, r, MIX_W)
    gates = _sigmoid(_mm(h, w_ref[MIX_W:PROJ_W, :], _NT))
    gate_ref[...] = gates.astype(gate_ref.dtype).reshape(g, r, GATE_W)


def _inproj_call(x, layer, nw, mods, w, gblk):
    bg, t, _ = x.shape
    return pl.pallas_call(
        _inproj_kernel,
        grid=(bg // gblk[0], t // gblk[1]),
        in_specs=[_tok_spec(gblk, D_MODEL), _layer_spec(nw.shape, layer),
                  _mod_spec(gblk, layer, 1), _mod_spec(gblk, layer, 0), _layer_spec(w.shape, layer)],
        out_specs=[_tok_spec(gblk, MIX_W), _tok_spec(gblk, GATE_W)],
        out_shape=[jax.ShapeDtypeStruct((bg, t, MIX_W), F32),
                   jax.ShapeDtypeStruct((bg, t, GATE_W), _row_dtype(gblk[1]))],
        compiler_params=_dense_params(), name="inproj",
    )(x, nw, mods, mods, w)


def _merge_ffn_kernel(final, br_ref, gate_ref, x_ref, gt1_ref, nw_ref, sc_ref, sh_ref, gt2_ref,
                      wb_ref, wo_ref, w1_ref, w2_ref, fw_ref, o_ref):
    g, r, _ = x_ref.shape
    br = br_ref[...].reshape(g * r, N_BRANCH * BRANCH_W)
    merged = None
    for n in range(N_BRANCH):
        up = _mm(br[:, n * BRANCH_W:(n + 1) * BRANCH_W], wb_ref[n])
        gate = gate_ref[:, :, n * D_MODEL:(n + 1) * D_MODEL].astype(F32).reshape(g * r, D_MODEL)
        t = gate * up
        merged = t if merged is None else merged + t
    x1 = x_ref[...] + gt1_ref[...] * _mm(merged, wo_ref[...]).reshape(g, r, D_MODEL)
    h = _rms_mod(x1, nw_ref[...], sc_ref[...], sh_ref[...]).astype(BF16).reshape(g * r, D_MODEL)
    a = _mm(h, w1_ref[:, 0:D_FF])
    b = _mm(h, w1_ref[:, D_FF:2 * D_FF])
    x2 = x1 + gt2_ref[...] * _mm(_silu(a) * b, w2_ref[...]).reshape(g, r, D_MODEL)
    if final:
        ms = jnp.mean(x2 * x2, axis=-1, keepdims=True)
        x2 = x2 * lax.rsqrt(ms + EPS) * fw_ref[...]
    o_ref[...] = x2


def _merge_ffn_call(br, gates, x, layer, nw, mods, wb, wo, w1, w2, fw, gblk, final):
    bg, t, _ = x.shape
    return pl.pallas_call(
        functools.partial(_merge_ffn_kernel, final),
        grid=(bg // gblk[0], t // gblk[1]),
        in_specs=[_tok_spec(gblk, N_BRANCH * BRANCH_W), _tok_spec(gblk, GATE_W), _tok_spec(gblk, D_MODEL),
                  _mod_spec(gblk, layer, 2), _layer_spec(nw.shape, layer),
                  _mod_spec(gblk, layer, 4), _mod_spec(gblk, layer, 3), _mod_spec(gblk, layer, 5),
                  _layer_spec(wb.shape, layer), _layer_spec(wo.shape, layer),
                  _layer_spec(w1.shape, layer), _layer_spec(w2.shape, layer),
                  pl.BlockSpec(fw.shape, lambda i, j: (0, 0), pipeline_mode=pl.Buffered(1))],
        out_specs=_tok_spec(gblk, D_MODEL),
        out_shape=jax.ShapeDtypeStruct((bg, t, D_MODEL), F32),
        compiler_params=_dense_params(), name="merge_ffn",
    )(br, gates, x, mods, nw, mods, mods, mods, wb, wo, w1, w2, fw)


def _small_params(gdn_dt_bias, ssd_dt_bias, ml_b_i, ml_b_f, gdn_a_log, ssd_a_log):
    z4 = jnp.zeros((4,), F32)
    bias = jnp.concatenate([z4, gdn_dt_bias, ssd_dt_bias, ml_b_i, ml_b_f, jnp.zeros((LANE - 24,), F32)])
    alog = jnp.concatenate([z4, gdn_a_log, ssd_a_log, jnp.zeros((LANE - 16,), F32)])
    return jnp.concatenate([bias[None], alog[None], jnp.zeros((6, LANE), F32)], axis=0)


def _layer_params(l, gdn_conv_w, gdn_conv_b, gdn_a_log, gdn_dt_bias, gdn_norm_w, hg_lb_logits, hg_norm_w,
                  ssd_conv_w, ssd_conv_b, ssd_a_log, ssd_dt_bias, ssd_d, ssd_norm_w, ml_b_i, ml_b_f, ml_norm_w):
    sp = _small_params(gdn_dt_bias[l], ssd_dt_bias[l], ml_b_i[l], ml_b_f[l], gdn_a_log[l], ssd_a_log[l])
    vec = jnp.stack([jnp.tile(gdn_norm_w[l], N_HEAD), jnp.tile(hg_norm_w[l], N_HEAD), ssd_norm_w[l],
                     jnp.tile(ml_norm_w[l], N_HEAD), jnp.repeat(ssd_d[l], SSD_P)]
                    + [jnp.zeros((256,), F32)] * 3)
    return [sp, gdn_conv_w[l], gdn_conv_b[l][None], ssd_conv_w[l], ssd_conv_b[l][None], vec,
            hg_lb_logits.astype(F32)]


def _trunk(x, mods, state_shapes, states_in, L, nv, gblk, bb, W, mixer_params):
    new_states = None
    for l in range(DEPTH):
        pm, gates = _inproj_call(x, l, W['norm1'], mods, W['w_in'], gblk)
        br, new_states = _mixer_call(pm, state_shapes, states_in, new_states, mixer_params[l], L, nv, l, bb)
        x = _merge_ffn_call(br, gates, x, l, W['norm2'], mods, W['w_branch'], W['w_out'], W['ffn_w_in'],
                            W['ffn_w_out'], W['final'], gblk, final=(l == DEPTH - 1))
    return x, new_states


def kernel(x_prompt, x_sample, c_prompt, c_sample, state_gdn, state_gdn_conv, state_hgrn, state_ssd, state_ssd_conv, state_mlstm_c, state_mlstm_n, state_mlstm_m, ada_w, ada_b, norm1_w, norm2_w, w_in, gdn_conv_w, gdn_conv_b, gdn_a_log, gdn_dt_bias, gdn_norm_w, hg_lb_logits, hg_norm_w, ssd_conv_w, ssd_conv_b, ssd_a_log, ssd_dt_bias, ssd_d, ssd_norm_w, ml_b_i, ml_b_f, ml_norm_w, w_branch, w_out, ffn_w_in, ffn_w_out, final_norm_w):
    bp, tp, _ = x_prompt.shape
    bs, ts, _ = x_sample.shape
    ls = 8
    assert tp % CHUNK == 0 and ts <= ls

    W = dict(w_in=_prep_w_in_call(w_in), w_branch=w_branch.astype(BF16), w_out=w_out.astype(BF16),
             ffn_w_in=ffn_w_in.astype(BF16), ffn_w_out=ffn_w_out.astype(BF16),
             norm1=norm1_w[:, None, :], norm2=norm2_w[:, None, :], final=final_norm_w[None])
    mixer_params = [_layer_params(l, gdn_conv_w, gdn_conv_b, gdn_a_log, gdn_dt_bias, gdn_norm_w, hg_lb_logits,
                                  hg_norm_w, ssd_conv_w, ssd_conv_b, ssd_a_log, ssd_dt_bias, ssd_d, ssd_norm_w,
                                  ml_b_i, ml_b_f, ml_norm_w) for l in range(DEPTH)]

    mods = _ada_call(jnp.concatenate([c_prompt, c_sample], axis=0), ada_w, ada_b)
    mods_p = mods[:, :bp].reshape(DEPTH, bp, 1, 6 * D_MODEL)
    mods_s = mods[:, bp:].reshape(DEPTH, bs, 1, 6 * D_MODEL)
    sample_states = [state_gdn, state_gdn_conv, state_hgrn, state_ssd, state_ssd_conv,
                     state_mlstm_c, state_mlstm_n, state_mlstm_m.reshape(DEPTH, bs, 1, N_HEAD)]
    shapes_s = [s.shape for s in sample_states]
    shapes_p = [(DEPTH, bp) + tuple(s[2:]) for s in shapes_s]

    y_p, new_p = _trunk(x_prompt, mods_p, shapes_p, None, CHUNK, CHUNK, (1, 256), 4, W, mixer_params)
    xs_pad = jnp.pad(x_sample, ((0, 0), (0, ls - ts), (0, 0)))
    y_s, new_s = _trunk(xs_pad, mods_s, shapes_s, sample_states, ls, ts, (32, 8), 8, W, mixer_params)
    new_p[7] = new_p[7].reshape(DEPTH, bp, N_HEAD)
    new_s[7] = new_s[7].reshape(DEPTH, bs, N_HEAD)
    return (y_p, y_s[:, :ts]) + tuple(new_p) + tuple(new_s)
```

```python
import functools
import math

import numpy as np
import jax
import jax.numpy as jnp
from jax import lax
from jax.experimental import pallas as pl
from jax.experimental.pallas import tpu as pltpu

F32 = jnp.float32
BF16 = jnp.bfloat16

D_MODEL = 1024
DEPTH = 2
N_BRANCH = 4
BRANCH_W = 256
N_HEAD = 4
HEAD_D = 64
SSD_H = 8
SSD_P = 32
SSD_N = 64
SSD_G = 2
CONV_K = 4
CHUNK = 64
D_FF = 2816
EPS = 1e-6
NEG = -1e30
GDN_CONV_W = 768
SSD_CONV_W = 512
SSD_XBC_OFF = 2304
MIX_W = 4096
GATE_W = 4096
PROJ_W = MIX_W + GATE_W
SMALL_OFF = 3840
LANE = 128
VMEM_LIMIT = 56 * 1024 * 1024

ST_BETA, ST_GA, ST_DT, ST_MI, ST_MF = 0, 4, 8, 16, 20

_NN = (((1,), (0,)), ((), ()))
_NT = (((1,), (1,)), ((), ()))
_TN = (((0,), (0,)), ((), ()))


def _mm(a, b, dims=_NN):
    return lax.dot_general(a.astype(BF16), b.astype(BF16), dims, preferred_element_type=F32)


def _split(x, n):
    parts, r = [], x
    for i in range(n):
        p = r.astype(BF16)
        parts.append(p)
        if i < n - 1:
            r = r - p.astype(F32)
    return parts


def _mm01(x, m, n):
    out = None
    for p in _split(x, n):
        t = lax.dot_general(p, m.astype(BF16), _NN, preferred_element_type=F32)
        out = t if out is None else out + t
    return out


def _m01m(m, x, n):
    out = None
    for p in _split(x, n):
        t = lax.dot_general(m, p, _NN, preferred_element_type=F32)
        out = t if out is None else out + t
    return out


def _sigmoid(x):
    return jax.nn.sigmoid(x)


def _silu(x):
    return x * jax.nn.sigmoid(x)


def _softplus(x):
    return jnp.maximum(x, 0.0) + jnp.log(1.0 + jnp.exp(-jnp.abs(x)))


def _logsig(x):
    return jnp.minimum(x, 0.0) - jnp.log(1.0 + jnp.exp(-jnp.abs(x)))


def _row_dtype(rows):
    return BF16 if rows % 16 == 0 else F32


def _expand(base, heads, width):
    e = np.zeros((LANE, heads * width), np.float32)
    for h in range(heads):
        e[base + h, h * width:(h + 1) * width] = 1.0
    return e


def _cat_segments(segs):
    cols, off, pos, seen = [], {}, 0, {}
    for name, m in segs:
        key = m.tobytes() + bytes(str(m.shape), 'ascii')
        if key in seen:
            off[name] = seen[key]
            continue
        w = m.shape[1]
        wp = -(-w // LANE) * LANE
        mp = np.zeros((m.shape[0], wp), np.float32)
        mp[:, :w] = m
        cols.append(mp)
        off[name] = seen[key] = (pos, w)
        pos += wp
    return np.concatenate(cols, axis=1), off


@functools.lru_cache(maxsize=None)
def _mixer_consts(L, nv):
    J = int(round(math.log2(L)))
    assert 1 << J == L
    f = np.float32
    r = np.arange(L)
    tri = (r[None, :] <= r[:, None])
    rev = (r[:, None] < r[None, :]) & (r[None, :] <= nv - 1)
    strict = (r[None, :] < r[:, None])
    eye = np.eye(L, dtype=bool)
    lvl, ab = [], []
    for j in range(J):
        bnd = ((r >> (j + 1)) << (j + 1)) + (1 << j) - 1
        low = ((r >> j) & 1) == 1
        a = low[:, None] & (bnd[:, None] < r[None, :]) & (r[None, :] <= r[:, None])
        b = (~low)[:, None] & (r[:, None] < r[None, :]) & (r[None, :] <= bnd[:, None])
        ab.append(a | b)
        same = (r[:, None] >> (j + 1)) == (r[None, :] >> (j + 1))
        lvl.append(same & low[:, None] & (~low)[None, :])

    def tile(m, n):
        return np.tile(m.astype(f), (1, n))

    pr = np.repeat(np.arange(2), L)
    lr2 = np.tile(r, 2)
    ch = np.arange(LANE)
    ch2 = np.arange(2 * LANE)
    c = dict(
        trirev=np.concatenate([tri, rev], axis=0).astype(f),
        mhg=np.concatenate(ab + [tri, rev], axis=0).astype(f),
        ones_ll=np.ones((L, L), f), tri=tri.astype(f),
        tril2=tile(tri, 2), strict2=tile(strict, 2), tril4=tile(tri, 4),
        eye4=tile(eye, 4), supper16=tile(strict, 16),
        e_dr=np.concatenate([_expand(ST_GA, 4, L), _expand(ST_DT, SSD_H, L), _expand(ST_MF, 4, L)], axis=1),
        hmaskp=np.stack([tile(eye, 2)] + [tile(m, 2) for m in lvl]),
        eyep=np.eye(2 * L, dtype=f),
        lvlp=np.stack([(pr[:, None] == pr[None, :]) & m[lr2[:, None], lr2[None, :]] for m in lvl]).astype(f),
        stkp=(pr[:, None] == ch[None, :] // HEAD_D).astype(f),
        bdp=(ch[:, None] // HEAD_D == ch[None, :] // HEAD_D).astype(f),
        pm2=np.stack([ch < HEAD_D, ch >= HEAD_D]).astype(f),
        stk4s=(np.repeat(np.arange(4), L)[:, None] == ch[None, :] // SSD_P).astype(f),
        gbd=(ch[:, None] // SSD_N == ch2[None, :] // LANE).astype(f),
        rowvalid=(r[:, None] <= nv - 1).astype(f) * np.ones((1, LANE), f),
        bd64=(ch2[:, None] // 64 == ch2[None, :] // 64).astype(f),
        bd128=(ch2[:, None] // 128 == ch2[None, :] // 128).astype(f),
    )
    c['e_act'], off_act = _cat_segments([('b_r', _expand(ST_BETA, 4, L)), ('b_c', _expand(ST_BETA, 4, HEAD_D)),
                                         ('dt_c', _expand(ST_DT, SSD_H, SSD_P)),
                                         ('i_r', _expand(ST_MI, 4, L)), ('i_c', _expand(ST_MI, 4, HEAD_D))])
    c['e_cs'], off_cs = _cat_segments([('g_c', _expand(ST_GA, 4, HEAD_D)), ('s_c', _expand(ST_DT, SSD_H, SSD_P)),
                                       ('m_r', _expand(ST_MF, 4, L)), ('m_c', _expand(ST_MF, 4, HEAD_D))])
    c['e_rev'], off_rev = _cat_segments([('g_c', _expand(ST_GA, 4, HEAD_D)), ('s_c', _expand(ST_DT, SSD_H, SSD_P)),
                                         ('m_c', _expand(ST_MF, 4, HEAD_D))])
    return c, dict(act=off_act, cs=off_cs, rev=off_rev), J


_CONST_ORDER = ('trirev', 'mhg', 'ones_ll', 'tri', 'tril2', 'strict2', 'tril4', 'eye4', 'supper16', 'e_dr', 'hmaskp',
                'eyep', 'lvlp', 'stkp', 'bdp', 'pm2', 'stk4s', 'gbd', 'rowvalid', 'bd64', 'bd128', 'e_act', 'e_cs',
                'e_rev')
_BF16_CONSTS = ('trirev', 'mhg', 'ones_ll', 'tri', 'supper16', 'e_dr', 'lvlp', 'stkp', 'stk4s', 'bd64', 'bd128',
                'e_act', 'e_cs', 'e_rev')


def _conv_silu(ext, w, b, L):
    y = b
    for i in range(CONV_K):
        y = y + ext[5 + i:5 + i + L, :] * w[i:i + 1, :]
    return _silu(y)


def _run_interleaved(tasks):
    tasks = list(tasks)
    while tasks:
        alive = []
        for t in tasks:
            try:
                next(t)
                alive.append(t)
            except StopIteration:
                pass
        tasks = alive


def _mixer_chunk(pm, ext_g, ext_s, st, P, K, offs, L, nv, J, layer, out):
    lane = lax.broadcasted_iota(jnp.int32, (1, LANE), 1)
    rowvalid = K['rowvalid'][:, 0:1]
    tril2 = K['tril2'] > 0.0
    tril4 = K['tril4'] > 0.0
    stkp, bdp = K['stkp'], K['bdp']
    upper_c = lane >= HEAD_D
    upper_r = lax.broadcasted_iota(jnp.int32, (1, 2 * L), 1) >= L

    def seg(x, which, name):
        o, w = offs[which][name]
        return x[:, o:o + w]

    def pair_c(x, p):
        return x[:, p * LANE:(p + 1) * LANE]

    def pair_r(x, p):
        return x[:, p * 2 * L:(p + 1) * 2 * L]

    def stack(x):
        xb = x.astype(BF16)
        return jnp.concatenate([xb, xb], axis=0) * stkp

    z = pm[:, SMALL_OFF:SMALL_OFF + LANE] + P['sp'][0:1, :]
    act = jnp.where(lane < ST_GA, _sigmoid(z),
                    jnp.where(lane < ST_MI, _softplus(z), jnp.where(lane < ST_MF, z, _logsig(z))))
    neg_a = -jnp.exp(P['sp'][1:2, :])
    dec_in = jnp.where((lane >= ST_GA) & (lane < ST_MI), neg_a * act,
                       jnp.where((lane >= ST_MF) & (lane < ST_MF + 4), act, 0.0))
    cr = _m01m(K['trirev'], dec_in, 2)
    yield
    cs, rev = cr[:L], cr[L:]
    ea = _mm01(act, K['e_act'], 2)
    ec = _mm01(cs, K['e_cs'], 2)
    er = _mm01(rev, K['e_rev'], 2)
    yield
    m_r = seg(ec, 'cs', 'm_r')
    g_c, s_c, m_c = seg(ec, 'cs', 'g_c'), seg(ec, 'cs', 's_c'), seg(ec, 'cs', 'm_c')
    b_r, b_c, dt_c = seg(ea, 'act', 'b_r'), seg(ea, 'act', 'b_c'), seg(ea, 'act', 'dt_c')
    i_r, i_c = seg(ea, 'act', 'i_r'), seg(ea, 'act', 'i_c')
    rg_c, rs_c, rm_c = seg(er, 'rev', 'g_c'), seg(er, 'rev', 's_c'), seg(er, 'rev', 'm_c')
    ym = _mm(dec_in, K['e_dr']).astype(BF16) * K['supper16']
    yield
    d_all = lax.dot_general(K['tri'], ym, _NN, preferred_element_type=F32)
    d_g, d_s, d_m = d_all[:, 0:4 * L], d_all[:, 4 * L:12 * L], d_all[:, 12 * L:16 * L]
    row_i = _m01m(K['ones_ll'], i_r * K['eye4'], 2)

    res = dict(o_gdn=[None, None], gdn=[None, None], o_hg=[None, None], hg=[None, None],
               o_ml=[None, None], mlc=[None, None], mln=[None, None], mlm=[None] * N_HEAD)

    qkv = _conv_silu(ext_g, P['cwg'], P['cbg'], L)
    q, k, v = qkv[:, 0:256], qkv[:, 256:512], qkv[:, 512:768]
    ss = _mm(jnp.concatenate([q * q, k * k], axis=0), K['bd64'])
    yield
    q = q * (lax.rsqrt(ss[:L] + EPS) * (HEAD_D ** -0.5))
    k = k * lax.rsqrt(ss[L:] + EPS)

    def gdn_task(p):
        qt, kt, vt = pair_c(q, p), pair_c(k, p), pair_c(v, p)
        beta_c, gc = pair_c(b_c, p), pair_c(g_c, p)
        eg = jnp.exp(gc)
        dec = jnp.where(tril2, jnp.exp(jnp.where(tril2, pair_r(d_g, p), 0.0)), 0.0)
        kkqk = _mm(jnp.concatenate([kt, qt], axis=0), stack(kt), _NT)
        yield
        n = (kkqk[:L] * dec * pair_r(b_r, p) * K['strict2']).astype(BF16)
        nbd = jnp.concatenate([n, n], axis=0)
        t = K['eyep'] - (nbd * K['lvlp'][0]).astype(F32)
        for j in range(1, J):
            x = _mm(t, nbd * K['lvlp'][j])
            yield
            t = t - _mm(x, t)
            yield
        sv = _mm(t, stack(vt * beta_c))
        sk = _mm(t, stack(kt * (beta_c * eg)))
        yield
        s_p = st['gdn'][p]
        r = _mm(jnp.concatenate([sk[:L] + sk[L:], qt * eg], axis=0), s_p)
        yield
        u = sv[:L] + sv[L:] - r[:L]
        res['o_gdn'][p] = r[L:] + _mm(kkqk[L:] * dec, stack(u))
        kw = kt * (jnp.exp(pair_c(rg_c, p)) * rowvalid)
        res['gdn'][p] = s_p * jnp.exp(gc[nv - 1:nv, :]) + bdp * _mm(kw, u, _TN)
        yield

    lg = P['lg']
    mx = lg[0:1, :]
    for i in range(1, DEPTH):
        mx = jnp.maximum(mx, lg[i:i + 1, :])
    ex = [jnp.exp(lg[i:i + 1, :] - mx) for i in range(DEPTH)]
    tot = ex[0]
    for i in range(1, DEPTH):
        tot = tot + ex[i]
    sm = [e / tot for e in ex]
    cum = sm[0]
    for i in range(1, layer + 1):
        cum = cum + sm[i]
    lb = cum - sm[0]
    lb_pos = lb > 0
    log_lb = jnp.log(jnp.where(lb_pos, lb, 1.0))
    hq = _silu(pm[:, 1024:1280])
    fz = pm[:, 1280:1536]
    hv = pm[:, 1536:1792]
    ls = _logsig(fz)
    t2 = jnp.log1p(-lb) + ls
    la = jnp.maximum(log_lb, t2) + jnp.log1p(jnp.exp(-jnp.abs(log_lb - t2)))
    logf = jnp.where(lb_pos, la, ls)
    kg = (1.0 - lb) * _sigmoid(-fz)
    ey = jnp.exp(_m01m(K['mhg'], logf, 2))
    yield
    eg_h = ey[J * L:(J + 1) * L]
    qs = [hq] + [hq * ey[j * L:(j + 1) * L] for j in range(J)]
    ks = [kg] + [kg * ey[j * L:(j + 1) * L] for j in range(J)]
    qe = hq * eg_h
    kwr = kg * (ey[(J + 1) * L:(J + 2) * L] * rowvalid)

    def hg_task(p):
        a_p = None
        for i in range(J + 1):
            t_i = K['hmaskp'][i] * _mm(pair_c(qs[i], p), stack(pair_c(ks[i], p)), _NT)
            a_p = t_i if a_p is None else a_p + t_i
            yield
        st_p = st['hg'][p]
        res['o_hg'][p] = _mm(pair_c(qe, p), st_p, _NT) + _mm(a_p, stack(pair_c(hv, p)))
        yield
        res['hg'][p] = (st_p * pair_c(eg_h[nv - 1:nv, :], p)
                        + bdp * _mm(pair_c(hv, p), pair_c(kwr, p), _TN))
        yield

    xbc = _conv_silu(ext_s, P['cws'], P['cbs'], L)
    xs, bs, cc = xbc[:, 0:256], xbc[:, 256:384], xbc[:, 384:512]
    vs = xs * dt_c
    bsb = bs.astype(BF16)
    bs4 = jnp.concatenate([bsb] * 4, axis=0)

    def ssd_task():
        o_intra = []
        for g in range(SSD_G):
            cbw = _mm(cc * K['pm2'][g:g + 1, :], bs4, _NT)
            dec = jnp.where(tril4, jnp.exp(jnp.where(tril4, d_s[:, g * 4 * L:(g + 1) * 4 * L], 0.0)), 0.0)
            vb = pair_c(vs, g).astype(BF16)
            o_intra.append(_mm(cbw * dec, jnp.concatenate([vb] * 4, axis=0) * K['stk4s']))
            yield
        s_all = st['ssd']
        res['o_ssd'] = jnp.exp(s_c) * _mm(cc, s_all) + jnp.concatenate(o_intra, axis=1)
        res['ssd'] = (s_all * jnp.exp(s_c[nv - 1:nv, :])
                      + K['gbd'] * _mm(bs * rowvalid, vs * jnp.exp(rs_c), _TN))
        yield

    mq = pm[:, 2816:3072] * (HEAD_D ** -0.5)
    mk = pm[:, 3072:3328]
    mv = pm[:, 3328:3584]

    def ml_task(p, delay):
        for _ in range(delay):
            yield
        qt, kt, vt = pair_c(mq, p), pair_c(mk, p), pair_c(mv, p)
        bm_r, bm_c = pair_r(m_r, p), pair_c(m_c, p)
        mp0 = st['mlm'][:, 2 * p:2 * p + 1]
        mp1 = st['mlm'][:, 2 * p + 1:2 * p + 2]
        logw = jnp.where(tril2, pair_r(d_m, p) + pair_r(row_i, p), NEG)
        mx0 = jnp.max(jnp.where(upper_r, NEG, logw), axis=-1, keepdims=True)
        mx1 = jnp.max(jnp.where(upper_r, logw, NEG), axis=-1, keepdims=True)
        l0_r = bm_r + jnp.where(upper_r, mp1, mp0)
        l0_c = bm_c + jnp.where(upper_c, mp1, mp0)
        mt_r = jnp.maximum(l0_r, jnp.where(upper_r, mx1, mx0))
        mt_c = jnp.maximum(l0_c, jnp.where(upper_c, mx1, mx0))
        w = jnp.where(tril2, jnp.exp(logw - mt_r), 0.0)
        w0 = jnp.exp(l0_c - mt_c)
        qk = _mm(qt, stack(kt), _NT) * w
        yield
        c_p = st['mlc'][p]
        n_p = st['mln'][p]
        num = w0 * _mm(qt, c_p) + _mm(qk, stack(vt))
        den = w0 * _mm01(qt * n_p, bdp, 2) + _mm01(qk, stkp, 2)
        res['o_ml'][p] = num * (1.0 / jnp.maximum(jnp.abs(den), jnp.exp(-mt_c)))
        yield
        m_l = mt_c[nv - 1:nv, :]
        wl0 = jnp.exp(bm_c[nv - 1:nv, :] + jnp.where(upper_c, mp1, mp0) - m_l)
        kwl = kt * (jnp.exp(pair_c(rm_c, p) + pair_c(i_c, p) - m_l) * rowvalid)
        res['mlc'][p] = c_p * wl0 + bdp * _mm(kwl, vt, _TN)
        res['mln'][p] = n_p * wl0 + jnp.sum(kwl, axis=0, keepdims=True)
        res['mlm'][2 * p] = m_l[:, 0:1]
        res['mlm'][2 * p + 1] = m_l[:, HEAD_D:HEAD_D + 1]
        yield

    def finish():
        o_gdn = jnp.concatenate(res['o_gdn'], axis=1)
        o_hg = jnp.concatenate(res['o_hg'], axis=1)
        hh = jnp.concatenate(res['o_ml'], axis=1)
        ms = _mm(jnp.concatenate([o_gdn * o_gdn, o_hg * o_hg, hh * hh], axis=0), K['bd64']) * (1.0 / HEAD_D)
        yield
        out_a = o_gdn * lax.rsqrt(ms[0:L] + EPS) * P['vec'][0:1, :] * _silu(pm[:, 768:1024])
        out_b = o_hg * lax.rsqrt(ms[L:2 * L] + EPS) * P['vec'][1:2, :] * _silu(pm[:, 1792:2048])
        out_d = hh * lax.rsqrt(ms[2 * L:3 * L] + EPS) * P['vec'][3:4, :] * _sigmoid(pm[:, 3584:3840])
        ys = (res['o_ssd'] + P['vec'][4:5, :] * xs) * _silu(pm[:, 2048:2304])
        out_c = ys * lax.rsqrt(_mm(ys * ys, K['bd128']) * (1.0 / (2 * HEAD_D)) + EPS) * P['vec'][2:3, :]
        branches = jnp.concatenate([out_a, out_b, out_c, out_d], axis=1)
        new = dict(gdn=res['gdn'], hg=res['hg'], ssd=res['ssd'], mlc=res['mlc'], mln=res['mln'], mlm=res['mlm'])
        out['result'] = (branches, new)
        yield

    out['tasks'] = ([gdn_task(p) for p in range(2)] + [hg_task(p) for p in range(2)]
                    + [ssd_task()] + [ml_task(p, 3 + 2 * p) for p in range(2)])
    out['finish'] = finish


def _mixer_kernel(L, nv, J, layer, bb, offs, has_init, n_alias, *refs):
    it = iter(refs)
    pm_ref = next(it)
    if has_init:
        gdn0, cg0, hg0, ssd0, cs0, mc0, mn0, mm0 = (next(it) for _ in range(8))
    sp_ref, cwg_ref, cbg_ref, cws_ref, cbs_ref, vec_ref, lg_ref = (next(it) for _ in range(7))
    kref = {name: next(it) for name in _CONST_ORDER}
    for _ in range(n_alias):
        next(it)
    br_ref = next(it)
    gdn1, cg1, hg1, ssd1, cs1, mc1, mn1, mm1 = (next(it) for _ in range(8))
    sg, sh, sc, sn, ss, extg, exts = (next(it) for _ in range(7))

    c = pl.program_id(1)
    nc = pl.num_programs(1)
    n_sub = SSD_H // SSD_G

    @pl.when(c == 0)
    def _init():
        sg[...] = jnp.zeros_like(sg)
        sh[...] = jnp.zeros_like(sh)
        sc[...] = jnp.zeros_like(sc)
        ss[...] = jnp.zeros_like(ss)
        extg[:, 0:8, :] = jnp.zeros((bb, 8, GDN_CONV_W), F32)
        exts[:, 0:8, :] = jnp.zeros((bb, 8, SSD_CONV_W), F32)
        if not has_init:
            sn[...] = jnp.zeros_like(sn)
            mm1[...] = jnp.zeros_like(mm1)
            return
        for s in range(bb):
            for h in range(N_HEAD):
                p, lo = h // 2, (h % 2) * HEAD_D
                sg[s, p, lo:lo + HEAD_D, lo:lo + HEAD_D] = gdn0[s, h]
                sh[s, p, lo:lo + HEAD_D, lo:lo + HEAD_D] = hg0[s, h].T
                sc[s, p, lo:lo + HEAD_D, lo:lo + HEAD_D] = mc0[s, h]
                sn[s, p, :, lo:lo + HEAD_D] = mn0[s, h:h + 1, :]
            for h in range(SSD_H):
                g = h // n_sub
                ss[s, g * SSD_N:(g + 1) * SSD_N, h * SSD_P:(h + 1) * SSD_P] = ssd0[s, h]
        mm1[...] = mm0[...]
        extg[:, 5:8, :] = cg0[...]
        exts[:, 5:8, :] = cs0[...]

    P = dict(sp=sp_ref[...], cwg=cwg_ref[...], cbg=cbg_ref[...], cws=cws_ref[...], cbs=cbs_ref[...],
             vec=vec_ref[...], lg=lg_ref[...])
    K = {name: r[...] for name, r in kref.items()}
    stage1, outs = [], []
    for s in range(bb):
        pm = pm_ref[s]
        extg[s, 8:8 + L, :] = pm[:, 0:GDN_CONV_W]
        exts[s, 8:8 + L, :] = pm[:, SSD_XBC_OFF:SSD_XBC_OFF + SSD_CONV_W]
        st = dict(gdn=sg[s], hg=sh[s], ssd=ss[s], mlc=sc[s], mln=sn[s], mlm=mm1[s])
        outs.append({})
        stage1.append(_mixer_chunk(pm, extg.at[s], exts.at[s], st, P, K, offs, L, nv, J, layer, outs[s]))

    def stage2(seqs):
        return [t for group in zip(*[outs[s]['tasks'] for s in seqs]) for t in group]

    ga, gb = list(range(0, bb // 2)), list(range(bb // 2, bb))
    _run_interleaved([stage1[s] for s in ga])
    _run_interleaved(stage2(ga) + [stage1[s] for s in gb])
    _run_interleaved(stage2(gb) + [outs[s]['finish']() for s in ga])
    _run_interleaved([outs[s]['finish']() for s in gb])
    for s in range(bb):
        branches, new = outs[s]['result']
        br_ref[s] = branches.astype(br_ref.dtype)
        for p in range(2):
            sg[s, p] = new['gdn'][p]
            sh[s, p] = new['hg'][p]
            sc[s, p] = new['mlc'][p]
            sn[s, p] = new['mln'][p]
        for h in range(N_HEAD):
            mm1[s, :, h:h + 1] = new['mlm'][h]
        ss[s] = new['ssd']
        tail_g = extg[s, 8 + nv - 3:8 + nv, :]
        tail_s = exts[s, 8 + nv - 3:8 + nv, :]
        extg[s, 5:8, :] = tail_g
        exts[s, 5:8, :] = tail_s
        cg1[s] = tail_g
        cs1[s] = tail_s

    @pl.when(c == nc - 1)
    def _fin():
        for s in range(bb):
            for h in range(N_HEAD):
                p, lo = h // 2, (h % 2) * HEAD_D
                gdn1[s, h] = sg[s, p, lo:lo + HEAD_D, lo:lo + HEAD_D]
                hg1[s, h] = sh[s, p, lo:lo + HEAD_D, lo:lo + HEAD_D].T
                mc1[s, h] = sc[s, p, lo:lo + HEAD_D, lo:lo + HEAD_D]
                mn1[s, h:h + 1, :] = sn[s, p, :, lo:lo + HEAD_D]
            for h in range(SSD_H):
                g = h // n_sub
                ssd1[s, h] = ss[s, g * SSD_N:(g + 1) * SSD_N, h * SSD_P:(h + 1) * SSD_P]


def _full_spec(a):
    nd = a.ndim
    return pl.BlockSpec(a.shape, lambda b, c, _nd=nd: (0,) * _nd)


def _mixer_call(proj, state_shapes, states_in, prev_out, params, L, nv, layer, bb):
    bg, t, _ = proj.shape
    nchunk = t // L
    assert bg % bb == 0
    consts, offs, J = _mixer_consts(L, nv)
    const_arrays = [jnp.asarray(consts[n], BF16 if n in _BF16_CONSTS else F32) for n in _CONST_ORDER]

    def st_spec(shape):
        nd = len(shape)
        return pl.BlockSpec((None, bb) + tuple(shape[2:]), lambda b, c, _nd=nd: (layer, b) + (0,) * (_nd - 2))

    has_init = states_in is not None
    n_alias = 0 if prev_out is None else len(prev_out)
    inputs = [proj] + (list(states_in) if has_init else []) + list(params) + const_arrays + list(prev_out or [])
    in_specs = ([pl.BlockSpec((bb, L, MIX_W), lambda b, c: (b, c, 0))]
                + ([st_spec(s) for s in state_shapes] if has_init else [])
                + [_full_spec(a) for a in params]
                + [_full_spec(a) for a in const_arrays]
                + [pl.BlockSpec(memory_space=pl.ANY)] * n_alias)
    out_shape = ([jax.ShapeDtypeStruct((bg, t, N_BRANCH * BRANCH_W), _row_dtype(L))]
                 + [jax.ShapeDtypeStruct(s, F32) for s in state_shapes])
    out_specs = ([pl.BlockSpec((bb, L, N_BRANCH * BRANCH_W), lambda b, c: (b, c, 0))]
                 + [st_spec(s) for s in state_shapes])
    first_alias = len(inputs) - n_alias
    pair = (bb, 2, LANE, LANE)
    scratch = [pltpu.VMEM(pair, F32), pltpu.VMEM(pair, F32), pltpu.VMEM(pair, F32),
               pltpu.VMEM((bb, 2, 1, LANE), F32), pltpu.VMEM((bb, SSD_G * SSD_N, SSD_H * SSD_P), F32),
               pltpu.VMEM((bb, 8 + L, GDN_CONV_W), F32), pltpu.VMEM((bb, 8 + L, SSD_CONV_W), F32)]
    outs = pl.pallas_call(
        functools.partial(_mixer_kernel, L, nv, J, layer, bb, offs, has_init, n_alias),
        grid=(bg // bb, nchunk),
        in_specs=in_specs, out_specs=out_specs, out_shape=out_shape, scratch_shapes=scratch,
        input_output_aliases={first_alias + k: 1 + k for k in range(n_alias)},
        compiler_params=pltpu.CompilerParams(dimension_semantics=("parallel", "arbitrary"),
                                             vmem_limit_bytes=VMEM_LIMIT),
        name=f"mixer_L{L}",
    )(*inputs)
    return outs[0], list(outs[1:])


def _ada_kernel(c_ref, w_ref, b_ref, o_ref):
    o_ref[...] = _mm(_silu(c_ref[...]), w_ref[...]) + b_ref[...]


def _ada_call(c_all, ada_w, ada_b):
    rows = c_all.shape[0]
    n = ada_w.shape[-1]
    tn = 1536
    return pl.pallas_call(
        _ada_kernel,
        grid=(DEPTH, n // tn),
        in_specs=[pl.BlockSpec((rows, D_MODEL), lambda l, j: (0, 0)),
                  pl.BlockSpec((None, D_MODEL, tn), lambda l, j: (l, 0, j)),
                  pl.BlockSpec((None, 1, tn), lambda l, j: (l, 0, j))],
        out_specs=pl.BlockSpec((None, rows, tn), lambda l, j: (l, 0, j)),
        out_shape=jax.ShapeDtypeStruct((DEPTH, rows, n), F32),
        compiler_params=pltpu.CompilerParams(dimension_semantics=("arbitrary", "arbitrary"),
                                             vmem_limit_bytes=VMEM_LIMIT),
        name="ada",
    )(c_all, ada_w, ada_b.reshape(DEPTH, 1, n))


PREP_ROWS = MIX_W - SMALL_OFF


def _prep_w_in_kernel(w_ref, s0_ref, s1_ref, s2_ref, o_ref):
    j = pl.program_id(1)
    j_small = SMALL_OFF // PREP_ROWS

    @pl.when(j != j_small)
    def _copy():
        o_ref[...] = w_ref[0].astype(BF16)

    @pl.when(j == j_small)
    def _small():
        o_ref[...] = jnp.zeros(o_ref.shape, BF16)
        o_ref[0:8, :] = s0_ref[0].astype(BF16)
        o_ref[8:16, :] = s1_ref[0].astype(BF16)
        o_ref[16:24, :] = s2_ref[0].astype(BF16)


def _prep_src_row(j):
    r = PREP_ROWS
    return jnp.where(j < 1024 // r, r * j,
                     jnp.where(j < 2816 // r, r * j + 8,
                               jnp.where(j < SMALL_OFF // r, r * j + 16,
                                         jnp.where(j < MIX_W // r, 0, r * j - MIX_W + 3864))))


def _prep_w_in_call(w_in):
    wt = jnp.swapaxes(w_in, 1, 2)
    depth, _, d = wt.shape

    def rows(n, index):
        return pl.BlockSpec((pl.Element(1), pl.Element(n), pl.Element(d)), index)

    return pl.pallas_call(
        _prep_w_in_kernel,
        grid=(depth, PROJ_W // PREP_ROWS),
        in_specs=[rows(PREP_ROWS, lambda l, j: (l, pl.multiple_of(_prep_src_row(j), 8), 0)),
                  rows(8, lambda l, j: (l, 1024, 0)), rows(8, lambda l, j: (l, 2824, 0)),
                  rows(8, lambda l, j: (l, 3856, 0))],
        out_specs=pl.BlockSpec((None, PREP_ROWS, d), lambda l, j: (l, j, 0)),
        out_shape=jax.ShapeDtypeStruct((depth, PROJ_W, d), BF16),
        compiler_params=pltpu.CompilerParams(dimension_semantics=("arbitrary", "arbitrary"),
                                             vmem_limit_bytes=VMEM_LIMIT),
        name="prep_w_in",
    )(wt, wt, wt, wt)


def _rms_mod(x, nw, sc, sh):
    ms = jnp.mean(x * x, axis=-1, keepdims=True)
    return x * lax.rsqrt(ms + EPS) * nw * (1.0 + sc) + sh


def _tok_spec(gblk, width):
    g, r = gblk
    return pl.BlockSpec((g, r, width), lambda i, j: (i, j, 0))


def _mod_spec(gblk, layer, k):
    return pl.BlockSpec((None, gblk[0], 1, D_MODEL), lambda i, j: (layer, i, 0, k))


def _layer_spec(shape, layer):
    nd = len(shape)
    return pl.BlockSpec((None,) + tuple(shape[1:]), lambda i, j: (layer,) + (0,) * (nd - 1),
                        pipeline_mode=pl.Buffered(1))


def _dense_params():
    return pltpu.CompilerParams(dimension_semantics=("arbitrary", "arbitrary"), vmem_limit_bytes=VMEM_LIMIT)


def _inproj_kernel(x_ref, nw_ref, sc_ref, sh_ref, w_ref, pm_ref, gate_ref):
    g, r, _ = x_ref.shape
    h = _rms_mod(x_ref[...], nw_ref[...], sc_ref[...], sh_ref[...]).astype(BF16).reshape(g * r, D_MODEL)
    pm_ref[...] = _mm(h, w_ref[0:MIX_W, :], _NT).reshape(g, r, MIX_W)
    gates = _sigmoid(_mm(h, w_ref[MIX_W:PROJ_W, :], _NT))
    gate_ref[...] = gates.astype(gate_ref.dtype).reshape(g, r, GATE_W)


def _inproj_call(x, layer, nw, mods, w, gblk):
    bg, t, _ = x.shape
    return pl.pallas_call(
        _inproj_kernel,
        grid=(bg // gblk[0], t // gblk[1]),
        in_specs=[_tok_spec(gblk, D_MODEL), _layer_spec(nw.shape, layer),
                  _mod_spec(gblk, layer, 1), _mod_spec(gblk, layer, 0), _layer_spec(w.shape, layer)],
        out_specs=[_tok_spec(gblk, MIX_W), _tok_spec(gblk, GATE_W)],
        out_shape=[jax.ShapeDtypeStruct((bg, t, MIX_W), F32),
                   jax.ShapeDtypeStruct((bg, t, GATE_W), _row_dtype(gblk[1]))],
        compiler_params=_dense_params(), name="inproj",
    )(x, nw, mods, mods, w)


def _merge_ffn_kernel(final, br_ref, gate_ref, x_ref, gt1_ref, nw_ref, sc_ref, sh_ref, gt2_ref,
                      wb_ref, wo_ref, w1_ref, w2_ref, fw_ref, o_ref):
    g, r, _ = x_ref.shape
    br = br_ref[...].reshape(g * r, N_BRANCH * BRANCH_W)
    merged = None
    for n in range(N_BRANCH):
        up = _mm(br[:, n * BRANCH_W:(n + 1) * BRANCH_W], wb_ref[n])
        gate = gate_ref[:, :, n * D_MODEL:(n + 1) * D_MODEL].astype(F32).reshape(g * r, D_MODEL)
        t = gate * up
        merged = t if merged is None else merged + t
    x1 = x_ref[...] + gt1_ref[...] * _mm(merged, wo_ref[...]).reshape(g, r, D_MODEL)
    h = _rms_mod(x1, nw_ref[...], sc_ref[...], sh_ref[...]).astype(BF16).reshape(g * r, D_MODEL)
    a = _mm(h, w1_ref[:, 0:D_FF])
    b = _mm(h, w1_ref[:, D_FF:2 * D_FF])
    x2 = x1 + gt2_ref[...] * _mm(_silu(a) * b, w2_ref[...]).reshape(g, r, D_MODEL)
    if final:
        ms = jnp.mean(x2 * x2, axis=-1, keepdims=True)
        x2 = x2 * lax.rsqrt(ms + EPS) * fw_ref[...]
    o_ref[...] = x2


def _merge_ffn_call(br, gates, x, layer, nw, mods, wb, wo, w1, w2, fw, gblk, final):
    bg, t, _ = x.shape
    return pl.pallas_call(
        functools.partial(_merge_ffn_kernel, final),
        grid=(bg // gblk[0], t // gblk[1]),
        in_specs=[_tok_spec(gblk, N_BRANCH * BRANCH_W), _tok_spec(gblk, GATE_W), _tok_spec(gblk, D_MODEL),
                  _mod_spec(gblk, layer, 2), _layer_spec(nw.shape, layer),
                  _mod_spec(gblk, layer, 4), _mod_spec(gblk, layer, 3), _mod_spec(gblk, layer, 5),
                  _layer_spec(wb.shape, layer), _layer_spec(wo.shape, layer),
                  _layer_spec(w1.shape, layer), _layer_spec(w2.shape, layer),
                  pl.BlockSpec(fw.shape, lambda i, j: (0, 0), pipeline_mode=pl.Buffered(1))],
        out_specs=_tok_spec(gblk, D_MODEL),
        out_shape=jax.ShapeDtypeStruct((bg, t, D_MODEL), F32),
        compiler_params=_dense_params(), name="merge_ffn",
    )(br, gates, x, mods, nw, mods, mods, mods, wb, wo, w1, w2, fw)


def _small_params(gdn_dt_bias, ssd_dt_bias, ml_b_i, ml_b_f, gdn_a_log, ssd_a_log):
    z4 = jnp.zeros((4,), F32)
    bias = jnp.concatenate([z4, gdn_dt_bias, ssd_dt_bias, ml_b_i, ml_b_f, jnp.zeros((LANE - 24,), F32)])
    alog = jnp.concatenate([z4, gdn_a_log, ssd_a_log, jnp.zeros((LANE - 16,), F32)])
    return jnp.concatenate([bias[None], alog[None], jnp.zeros((6, LANE), F32)], axis=0)


def _layer_params(l, gdn_conv_w, gdn_conv_b, gdn_a_log, gdn_dt_bias, gdn_norm_w, hg_lb_logits, hg_norm_w,
                  ssd_conv_w, ssd_conv_b, ssd_a_log, ssd_dt_bias, ssd_d, ssd_norm_w, ml_b_i, ml_b_f, ml_norm_w):
    sp = _small_params(gdn_dt_bias[l], ssd_dt_bias[l], ml_b_i[l], ml_b_f[l], gdn_a_log[l], ssd_a_log[l])
    vec = jnp.stack([jnp.tile(gdn_norm_w[l], N_HEAD), jnp.tile(hg_norm_w[l], N_HEAD), ssd_norm_w[l],
                     jnp.tile(ml_norm_w[l], N_HEAD), jnp.repeat(ssd_d[l], SSD_P)]
                    + [jnp.zeros((256,), F32)] * 3)
    return [sp, gdn_conv_w[l], gdn_conv_b[l][None], ssd_conv_w[l], ssd_conv_b[l][None], vec,
            hg_lb_logits.astype(F32)]


def _trunk(x, mods, state_shapes, states_in, L, nv, gblk, bb, W, mixer_params):
    new_states = None
    for l in range(DEPTH):
        pm, gates = _inproj_call(x, l, W['norm1'], mods, W['w_in'], gblk)
        br, new_states = _mixer_call(pm, state_shapes, states_in, new_states, mixer_params[l], L, nv, l, bb)
        x = _merge_ffn_call(br, gates, x, l, W['norm2'], mods, W['w_branch'], W['w_out'], W['ffn_w_in'],
                            W['ffn_w_out'], W['final'], gblk, final=(l == DEPTH - 1))
    return x, new_states


def kernel(x_prompt, x_sample, c_prompt, c_sample, state_gdn, state_gdn_conv, state_hgrn, state_ssd, state_ssd_conv, state_mlstm_c, state_mlstm_n, state_mlstm_m, ada_w, ada_b, norm1_w, norm2_w, w_in, gdn_conv_w, gdn_conv_b, gdn_a_log, gdn_dt_bias, gdn_norm_w, hg_lb_logits, hg_norm_w, ssd_conv_w, ssd_conv_b, ssd_a_log, ssd_dt_bias, ssd_d, ssd_norm_w, ml_b_i, ml_b_f, ml_norm_w, w_branch, w_out, ffn_w_in, ffn_w_out, final_norm_w):
    bp, tp, _ = x_prompt.shape
    bs, ts, _ = x_sample.shape
    ls = 8
    assert tp % CHUNK == 0 and ts <= ls

    W = dict(w_in=_prep_w_in_call(w_in), w_branch=w_branch.astype(BF16), w_out=w_out.astype(BF16),
             ffn_w_in=ffn_w_in.astype(BF16), ffn_w_out=ffn_w_out.astype(BF16),
             norm1=norm1_w[:, None, :], norm2=norm2_w[:, None, :], final=final_norm_w[None])
    mixer_params = [_layer_params(l, gdn_conv_w, gdn_conv_b, gdn_a_log, gdn_dt_bias, gdn_norm_w, hg_lb_logits,
                                  hg_norm_w, ssd_conv_w, ssd_conv_b, ssd_a_log, ssd_dt_bias, ssd_d, ssd_norm_w,
                                  ml_b_i, ml_b_f, ml_norm_w) for l in range(DEPTH)]

    mods = _ada_call(jnp.concatenate([c_prompt, c_sample], axis=0), ada_w, ada_b)
    mods_p = mods[:, :bp].reshape(DEPTH, bp, 1, 6 * D_MODEL)
    mods_s = mods[:, bp:].reshape(DEPTH, bs, 1, 6 * D_MODEL)
    sample_states = [state_gdn, state_gdn_conv, state_hgrn, state_ssd, state_ssd_conv,
                     state_mlstm_c, state_mlstm_n, state_mlstm_m.reshape(DEPTH, bs, 1, N_HEAD)]
    shapes_s = [s.shape for s in sample_states]
    shapes_p = [(DEPTH, bp) + tuple(s[2:]) for s in shapes_s]

    y_p, new_p = _trunk(x_prompt, mods_p, shapes_p, None, CHUNK, CHUNK, (1, 256), 4, W, mixer_params)
    xs_pad = jnp.pad(x_sample, ((0, 0), (0, ls - ts), (0, 0)))
    y_s, new_s = _trunk(xs_pad, mods_s, shapes_s, sample_states, ls, ts, (32, 8), 8, W, mixer_params)
    new_p[7] = new_p[7].reshape(DEPTH, bp, N_HEAD)
    new_s[7] = new_s[7].reshape(DEPTH, bs, N_HEAD)
    return (y_p, y_s[:, :ts]) + tuple(new_p) + tuple(new_s)
```

```python
import functools
import math

import numpy as np
import jax
import jax.numpy as jnp
from jax import lax
from jax.experimental import pallas as pl
from jax.experimental.pallas import tpu as pltpu

F32 = jnp.float32
BF16 = jnp.bfloat16

D_MODEL = 1024
DEPTH = 2
N_BRANCH = 4
BRANCH_W = 256
N_HEAD = 4
HEAD_D = 64
SSD_H = 8
SSD_P = 32
SSD_N = 64
SSD_G = 2
CONV_K = 4
CHUNK = 64
LS = 8
D_FF = 2816
EPS = 1e-6
NEG = -1e30
GDN_CONV_W = 768
SSD_CONV_W = 512
SSD_XBC_OFF = 2304
MIX_W = 4096
GATE_W = 4096
PROJ_W = MIX_W + GATE_W
SMALL_OFF = 3840
LANE = 128
VMEM_LIMIT = 56 * 1024 * 1024

ST_BETA, ST_GA, ST_DT, ST_MI, ST_MF = 0, 4, 8, 16, 20

_NN = (((1,), (0,)), ((), ()))
_NT = (((1,), (1,)), ((), ()))
_TN = (((0,), (0,)), ((), ()))


def _mm(a, b, dims=_NN):
    return lax.dot_general(a.astype(BF16), b.astype(BF16), dims, preferred_element_type=F32)


def _split(x, n):
    parts, r = [], x
    for i in range(n):
        p = r.astype(BF16)
        parts.append(p)
        if i < n - 1:
            r = r - p.astype(F32)
    return parts


def _mm01(x, m, n):
    out = None
    for p in _split(x, n):
        t = lax.dot_general(p, m.astype(BF16), _NN, preferred_element_type=F32)
        out = t if out is None else out + t
    return out


def _m01m(m, x, n):
    out = None
    for p in _split(x, n):
        t = lax.dot_general(m, p, _NN, preferred_element_type=F32)
        out = t if out is None else out + t
    return out


def _sigmoid(x):
    return jax.nn.sigmoid(x)


def _silu(x):
    return x * jax.nn.sigmoid(x)


def _softplus(x):
    return jnp.maximum(x, 0.0) + jnp.log(1.0 + jnp.exp(-jnp.abs(x)))


def _logsig(x):
    return jnp.minimum(x, 0.0) - jnp.log(1.0 + jnp.exp(-jnp.abs(x)))


def _row_dtype(rows):
    return BF16 if rows % 16 == 0 else F32


def _expand(base, heads, width):
    e = np.zeros((LANE, heads * width), np.float32)
    for h in range(heads):
        e[base + h, h * width:(h + 1) * width] = 1.0
    return e


def _cat_segments(segs):
    cols, off, pos, seen = [], {}, 0, {}
    for name, m in segs:
        key = m.tobytes() + bytes(str(m.shape), 'ascii')
        if key in seen:
            off[name] = seen[key]
            continue
        w = m.shape[1]
        wp = -(-w // LANE) * LANE
        mp = np.zeros((m.shape[0], wp), np.float32)
        mp[:, :w] = m
        cols.append(mp)
        off[name] = seen[key] = (pos, w)
        pos += wp
    return np.concatenate(cols, axis=1), off


@functools.lru_cache(maxsize=None)
def _mixer_consts(L, nv):
    J = int(round(math.log2(L)))
    assert 1 << J == L
    f = np.float32
    r = np.arange(L)
    tri = (r[None, :] <= r[:, None])
    rev = (r[:, None] < r[None, :]) & (r[None, :] <= nv - 1)
    strict = (r[None, :] < r[:, None])
    eye = np.eye(L, dtype=bool)
    lvl, ab = [], []
    for j in range(J):
        bnd = ((r >> (j + 1)) << (j + 1)) + (1 << j) - 1
        low = ((r >> j) & 1) == 1
        a = low[:, None] & (bnd[:, None] < r[None, :]) & (r[None, :] <= r[:, None])
        b = (~low)[:, None] & (r[:, None] < r[None, :]) & (r[None, :] <= bnd[:, None])
        ab.append(a | b)
        same = (r[:, None] >> (j + 1)) == (r[None, :] >> (j + 1))
        lvl.append(same & low[:, None] & (~low)[None, :])

    def tile(m, n):
        return np.tile(m.astype(f), (1, n))

    pr = np.repeat(np.arange(2), L)
    lr2 = np.tile(r, 2)
    ch = np.arange(LANE)
    ch2 = np.arange(2 * LANE)
    c = dict(
        trirev=np.concatenate([tri, rev], axis=0).astype(f),
        mhg=np.concatenate(ab + [tri, rev], axis=0).astype(f),
        ones_ll=np.ones((L, L), f), tri=tri.astype(f),
        tril2=tile(tri, 2), strict2=tile(strict, 2), tril4=tile(tri, 4),
        eye4=tile(eye, 4), supper16=tile(strict, 16),
        e_dr=np.concatenate([_expand(ST_GA, 4, L), _expand(ST_DT, SSD_H, L), _expand(ST_MF, 4, L)], axis=1),
        hmaskp=np.stack([tile(eye, 2)] + [tile(m, 2) for m in lvl]),
        eyep=np.eye(2 * L, dtype=f),
        lvlp=np.stack([(pr[:, None] == pr[None, :]) & m[lr2[:, None], lr2[None, :]] for m in lvl]).astype(f),
        stkp=(pr[:, None] == ch[None, :] // HEAD_D).astype(f),
        bdp=(ch[:, None] // HEAD_D == ch[None, :] // HEAD_D).astype(f),
        pm2=np.stack([ch < HEAD_D, ch >= HEAD_D]).astype(f),
        stk4s=(np.repeat(np.arange(4), L)[:, None] == ch[None, :] // SSD_P).astype(f),
        gbd=(ch[:, None] // SSD_N == ch2[None, :] // LANE).astype(f),
        rowvalid=(r[:, None] <= nv - 1).astype(f) * np.ones((1, LANE), f),
        bd64=(ch2[:, None] // 64 == ch2[None, :] // 64).astype(f),
        bd128=(ch2[:, None] // 128 == ch2[None, :] // 128).astype(f),
    )
    c['e_act'], off_act = _cat_segments([('b_r', _expand(ST_BETA, 4, L)), ('b_c', _expand(ST_BETA, 4, HEAD_D)),
                                         ('dt_c', _expand(ST_DT, SSD_H, SSD_P)),
                                         ('i_r', _expand(ST_MI, 4, L)), ('i_c', _expand(ST_MI, 4, HEAD_D))])
    c['e_cs'], off_cs = _cat_segments([('g_c', _expand(ST_GA, 4, HEAD_D)), ('s_c', _expand(ST_DT, SSD_H, SSD_P)),
                                       ('m_r', _expand(ST_MF, 4, L)), ('m_c', _expand(ST_MF, 4, HEAD_D))])
    c['e_rev'], off_rev = _cat_segments([('g_c', _expand(ST_GA, 4, HEAD_D)), ('s_c', _expand(ST_DT, SSD_H, SSD_P)),
                                         ('m_c', _expand(ST_MF, 4, HEAD_D))])
    return c, dict(act=off_act, cs=off_cs, rev=off_rev), J


_CONST_ORDER = ('trirev', 'mhg', 'ones_ll', 'tri', 'tril2', 'strict2', 'tril4', 'eye4', 'supper16', 'e_dr', 'hmaskp',
                'eyep', 'lvlp', 'stkp', 'bdp', 'pm2', 'stk4s', 'gbd', 'rowvalid', 'bd64', 'bd128', 'e_act', 'e_cs',
                'e_rev')
_BF16_CONSTS = ('trirev', 'mhg', 'ones_ll', 'tri', 'supper16', 'e_dr', 'lvlp', 'stkp', 'stk4s', 'bd64', 'bd128',
                'e_act', 'e_cs', 'e_rev')


def _conv_silu(ext, w, b, L):
    y = b
    for i in range(CONV_K):
        y = y + ext[5 + i:5 + i + L, :] * w[i:i + 1, :]
    return _silu(y)


def _run_interleaved(tasks):
    tasks = list(tasks)
    while tasks:
        alive = []
        for t in tasks:
            try:
                next(t)
                alive.append(t)
            except StopIteration:
                pass
        tasks = alive


def _mixer_chunk(pm, ext_g, ext_s, st, P, K, offs, L, nv, J, layer, out):
    lane = lax.broadcasted_iota(jnp.int32, (1, LANE), 1)
    rowvalid = K['rowvalid'][:, 0:1]
    tril2 = K['tril2'] > 0.0
    tril4 = K['tril4'] > 0.0
    stkp, bdp = K['stkp'], K['bdp']
    upper_c = lane >= HEAD_D
    upper_r = lax.broadcasted_iota(jnp.int32, (1, 2 * L), 1) >= L

    def seg(x, which, name):
        o, w = offs[which][name]
        return x[:, o:o + w]

    def pair_c(x, p):
        return x[:, p * LANE:(p + 1) * LANE]

    def pair_r(x, p):
        return x[:, p * 2 * L:(p + 1) * 2 * L]

    def stack(x):
        xb = x.astype(BF16)
        return jnp.concatenate([xb, xb], axis=0) * stkp

    z = pm[:, SMALL_OFF:SMALL_OFF + LANE] + P['sp'][0:1, :]
    act = jnp.where(lane < ST_GA, _sigmoid(z),
                    jnp.where(lane < ST_MI, _softplus(z), jnp.where(lane < ST_MF, z, _logsig(z))))
    neg_a = -jnp.exp(P['sp'][1:2, :])
    dec_in = jnp.where((lane >= ST_GA) & (lane < ST_MI), neg_a * act,
                       jnp.where((lane >= ST_MF) & (lane < ST_MF + 4), act, 0.0))
    cr = _m01m(K['trirev'], dec_in, 2)
    yield
    cs, rev = cr[:L], cr[L:]
    ea = _mm01(act, K['e_act'], 2)
    ec = _mm01(cs, K['e_cs'], 2)
    er = _mm01(rev, K['e_rev'], 2)
    yield
    m_r = seg(ec, 'cs', 'm_r')
    g_c, s_c, m_c = seg(ec, 'cs', 'g_c'), seg(ec, 'cs', 's_c'), seg(ec, 'cs', 'm_c')
    b_r, b_c, dt_c = seg(ea, 'act', 'b_r'), seg(ea, 'act', 'b_c'), seg(ea, 'act', 'dt_c')
    i_r, i_c = seg(ea, 'act', 'i_r'), seg(ea, 'act', 'i_c')
    rg_c, rs_c, rm_c = seg(er, 'rev', 'g_c'), seg(er, 'rev', 's_c'), seg(er, 'rev', 'm_c')
    ym = _mm(dec_in, K['e_dr']).astype(BF16) * K['supper16']
    yield
    d_all = lax.dot_general(K['tri'], ym, _NN, preferred_element_type=F32)
    d_g, d_s, d_m = d_all[:, 0:4 * L], d_all[:, 4 * L:12 * L], d_all[:, 12 * L:16 * L]
    row_i = _m01m(K['ones_ll'], i_r * K['eye4'], 2)

    res = dict(o_gdn=[None, None], gdn=[None, None], o_hg=[None, None], hg=[None, None],
               o_ml=[None, None], mlc=[None, None], mln=[None, None], mlm=[None] * N_HEAD)

    qkv = _conv_silu(ext_g, P['cwg'], P['cbg'], L)
    q, k, v = qkv[:, 0:256], qkv[:, 256:512], qkv[:, 512:768]
    ss = _mm(jnp.concatenate([q * q, k * k], axis=0), K['bd64'])
    yield
    q = q * (lax.rsqrt(ss[:L] + EPS) * (HEAD_D ** -0.5))
    k = k * lax.rsqrt(ss[L:] + EPS)

    def gdn_task(p):
        qt, kt, vt = pair_c(q, p), pair_c(k, p), pair_c(v, p)
        beta_c, gc = pair_c(b_c, p), pair_c(g_c, p)
        eg = jnp.exp(gc)
        dec = jnp.where(tril2, jnp.exp(jnp.where(tril2, pair_r(d_g, p), 0.0)), 0.0)
        kkqk = _mm(jnp.concatenate([kt, qt], axis=0), stack(kt), _NT)
        yield
        n = (kkqk[:L] * dec * pair_r(b_r, p) * K['strict2']).astype(BF16)
        nbd = jnp.concatenate([n, n], axis=0)
        t = K['eyep'] - (nbd * K['lvlp'][0]).astype(F32)
        for j in range(1, J):
            x = _mm(t, nbd * K['lvlp'][j])
            yield
            t = t - _mm(x, t)
            yield
        sv = _mm(t, stack(vt * beta_c))
        sk = _mm(t, stack(kt * (beta_c * eg)))
        yield
        s_p = st['gdn'][p]
        r = _mm(jnp.concatenate([sk[:L] + sk[L:], qt * eg], axis=0), s_p)
        yield
        u = sv[:L] + sv[L:] - r[:L]
        res['o_gdn'][p] = r[L:] + _mm(kkqk[L:] * dec, stack(u))
        kw = kt * (jnp.exp(pair_c(rg_c, p)) * rowvalid)
        res['gdn'][p] = s_p * jnp.exp(gc[nv - 1:nv, :]) + bdp * _mm(kw, u, _TN)
        yield

    lg = P['lg']
    mx = lg[0:1, :]
    for i in range(1, DEPTH):
        mx = jnp.maximum(mx, lg[i:i + 1, :])
    ex = [jnp.exp(lg[i:i + 1, :] - mx) for i in range(DEPTH)]
    tot = ex[0]
    for i in range(1, DEPTH):
        tot = tot + ex[i]
    sm = [e / tot for e in ex]
    cum = sm[0]
    for i in range(1, layer + 1):
        cum = cum + sm[i]
    lb = cum - sm[0]
    lb_pos = lb > 0
    log_lb = jnp.log(jnp.where(lb_pos, lb, 1.0))
    hq = _silu(pm[:, 1024:1280])
    fz = pm[:, 1280:1536]
    hv = pm[:, 1536:1792]
    ls = _logsig(fz)
    t2 = jnp.log1p(-lb) + ls
    la = jnp.maximum(log_lb, t2) + jnp.log1p(jnp.exp(-jnp.abs(log_lb - t2)))
    logf = jnp.where(lb_pos, la, ls)
    kg = (1.0 - lb) * _sigmoid(-fz)
    ey = jnp.exp(_m01m(K['mhg'], logf, 2))
    yield
    eg_h = ey[J * L:(J + 1) * L]
    qs = [hq] + [hq * ey[j * L:(j + 1) * L] for j in range(J)]
    ks = [kg] + [kg * ey[j * L:(j + 1) * L] for j in range(J)]
    qe = hq * eg_h
    kwr = kg * (ey[(J + 1) * L:(J + 2) * L] * rowvalid)

    def hg_task(p):
        a_p = None
        for i in range(J + 1):
            t_i = K['hmaskp'][i] * _mm(pair_c(qs[i], p), stack(pair_c(ks[i], p)), _NT)
            a_p = t_i if a_p is None else a_p + t_i
            yield
        st_p = st['hg'][p]
        res['o_hg'][p] = _mm(pair_c(qe, p), st_p, _NT) + _mm(a_p, stack(pair_c(hv, p)))
        yield
        res['hg'][p] = (st_p * pair_c(eg_h[nv - 1:nv, :], p)
                        + bdp * _mm(pair_c(hv, p), pair_c(kwr, p), _TN))
        yield

    xbc = _conv_silu(ext_s, P['cws'], P['cbs'], L)
    xs, bs, cc = xbc[:, 0:256], xbc[:, 256:384], xbc[:, 384:512]
    vs = xs * dt_c
    bsb = bs.astype(BF16)
    bs4 = jnp.concatenate([bsb] * 4, axis=0)

    def ssd_task():
        o_intra = []
        for g in range(SSD_G):
            cbw = _mm(cc * K['pm2'][g:g + 1, :], bs4, _NT)
            dec = jnp.where(tril4, jnp.exp(jnp.where(tril4, d_s[:, g * 4 * L:(g + 1) * 4 * L], 0.0)), 0.0)
            vb = pair_c(vs, g).astype(BF16)
            o_intra.append(_mm(cbw * dec, jnp.concatenate([vb] * 4, axis=0) * K['stk4s']))
            yield
        s_all = st['ssd']
        res['o_ssd'] = jnp.exp(s_c) * _mm(cc, s_all) + jnp.concatenate(o_intra, axis=1)
        res['ssd'] = (s_all * jnp.exp(s_c[nv - 1:nv, :])
                      + K['gbd'] * _mm(bs * rowvalid, vs * jnp.exp(rs_c), _TN))
        yield

    mq = pm[:, 2816:3072] * (HEAD_D ** -0.5)
    mk = pm[:, 3072:3328]
    mv = pm[:, 3328:3584]

    def ml_task(p, delay):
        for _ in range(delay):
            yield
        qt, kt, vt = pair_c(mq, p), pair_c(mk, p), pair_c(mv, p)
        bm_r, bm_c = pair_r(m_r, p), pair_c(m_c, p)
        mp0 = st['mlm'][:, 2 * p:2 * p + 1]
        mp1 = st['mlm'][:, 2 * p + 1:2 * p + 2]
        logw = jnp.where(tril2, pair_r(d_m, p) + pair_r(row_i, p), NEG)
        mx0 = jnp.max(jnp.where(upper_r, NEG, logw), axis=-1, keepdims=True)
        mx1 = jnp.max(jnp.where(upper_r, logw, NEG), axis=-1, keepdims=True)
        l0_r = bm_r + jnp.where(upper_r, mp1, mp0)
        l0_c = bm_c + jnp.where(upper_c, mp1, mp0)
        mt_r = jnp.maximum(l0_r, jnp.where(upper_r, mx1, mx0))
        mt_c = jnp.maximum(l0_c, jnp.where(upper_c, mx1, mx0))
        w = jnp.where(tril2, jnp.exp(logw - mt_r), 0.0)
        w0 = jnp.exp(l0_c - mt_c)
        qk = _mm(qt, stack(kt), _NT) * w
        yield
        c_p = st['mlc'][p]
        n_p = st['mln'][p]
        num = w0 * _mm(qt, c_p) + _mm(qk, stack(vt))
        den = w0 * _mm01(qt * n_p, bdp, 2) + _mm01(qk, stkp, 2)
        res['o_ml'][p] = num * (1.0 / jnp.maximum(jnp.abs(den), jnp.exp(-mt_c)))
        yield
        m_l = mt_c[nv - 1:nv, :]
        wl0 = jnp.exp(bm_c[nv - 1:nv, :] + jnp.where(upper_c, mp1, mp0) - m_l)
        kwl = kt * (jnp.exp(pair_c(rm_c, p) + pair_c(i_c, p) - m_l) * rowvalid)
        res['mlc'][p] = c_p * wl0 + bdp * _mm(kwl, vt, _TN)
        res['mln'][p] = n_p * wl0 + jnp.sum(kwl, axis=0, keepdims=True)
        res['mlm'][2 * p] = m_l[:, 0:1]
        res['mlm'][2 * p + 1] = m_l[:, HEAD_D:HEAD_D + 1]
        yield

    def finish():
        o_gdn = jnp.concatenate(res['o_gdn'], axis=1)
        o_hg = jnp.concatenate(res['o_hg'], axis=1)
        hh = jnp.concatenate(res['o_ml'], axis=1)
        ms = _mm(jnp.concatenate([o_gdn * o_gdn, o_hg * o_hg, hh * hh], axis=0), K['bd64']) * (1.0 / HEAD_D)
        yield
        out_a = o_gdn * lax.rsqrt(ms[0:L] + EPS) * P['vec'][0:1, :] * _silu(pm[:, 768:1024])
        out_b = o_hg * lax.rsqrt(ms[L:2 * L] + EPS) * P['vec'][1:2, :] * _silu(pm[:, 1792:2048])
        out_d = hh * lax.rsqrt(ms[2 * L:3 * L] + EPS) * P['vec'][3:4, :] * _sigmoid(pm[:, 3584:3840])
        ys = (res['o_ssd'] + P['vec'][4:5, :] * xs) * _silu(pm[:, 2048:2304])
        out_c = ys * lax.rsqrt(_mm(ys * ys, K['bd128']) * (1.0 / (2 * HEAD_D)) + EPS) * P['vec'][2:3, :]
        branches = jnp.concatenate([out_a, out_b, out_c, out_d], axis=1)
        new = dict(gdn=res['gdn'], hg=res['hg'], ssd=res['ssd'], mlc=res['mlc'], mln=res['mln'], mlm=res['mlm'])
        out['result'] = (branches, new)
        yield

    out['tasks'] = ([gdn_task(p) for p in range(2)] + [hg_task(p) for p in range(2)]
                    + [ssd_task()] + [ml_task(p, 3 + 2 * p) for p in range(2)])
    out['finish'] = finish


def _mixer_kernel(L, nv, J, layer, bb, offs, has_init, n_alias, *refs):
    it = iter(refs)
    pm_ref = next(it)
    if has_init:
        gdn0, cg0, hg0, ssd0, cs0, mc0, mn0, mm0 = (next(it) for _ in range(8))
    sp_ref, cwg_ref, cbg_ref, cws_ref, cbs_ref, vec_ref, lg_ref = (next(it) for _ in range(7))
    kref = {name: next(it) for name in _CONST_ORDER}
    for _ in range(n_alias):
        next(it)
    br_ref = next(it)
    gdn1, cg1, hg1, ssd1, cs1, mc1, mn1, mm1 = (next(it) for _ in range(8))
    sg, sh, sc, sn, ss, extg, exts, pmpad = (next(it) for _ in range(8))

    c = pl.program_id(1)
    nc = pl.num_programs(1)
    n_sub = SSD_H // SSD_G

    @pl.when(c == 0)
    def _init():
        sg[...] = jnp.zeros_like(sg)
        sh[...] = jnp.zeros_like(sh)
        sc[...] = jnp.zeros_like(sc)
        ss[...] = jnp.zeros_like(ss)
        extg[:, 0:8, :] = jnp.zeros((bb, 8, GDN_CONV_W), F32)
        exts[:, 0:8, :] = jnp.zeros((bb, 8, SSD_CONV_W), F32)
        if not has_init:
            sn[...] = jnp.zeros_like(sn)
            mm1[...] = jnp.zeros_like(mm1)
            return
        for s in range(bb):
            for h in range(N_HEAD):
                p, lo = h // 2, (h % 2) * HEAD_D
                sg[s, p, lo:lo + HEAD_D, lo:lo + HEAD_D] = gdn0[s, h]
                sh[s, p, lo:lo + HEAD_D, lo:lo + HEAD_D] = hg0[s, h].T
                sc[s, p, lo:lo + HEAD_D, lo:lo + HEAD_D] = mc0[s, h]
                sn[s, p, :, lo:lo + HEAD_D] = mn0[s, h:h + 1, :]
            for h in range(SSD_H):
                g = h // n_sub
                ss[s, g * SSD_N:(g + 1) * SSD_N, h * SSD_P:(h + 1) * SSD_P] = ssd0[s, h]
        mm1[...] = mm0[...]
        extg[:, 5:8, :] = cg0[...]
        exts[:, 5:8, :] = cs0[...]

    P = dict(sp=sp_ref[...], cwg=cwg_ref[...], cbg=cbg_ref[...], cws=cws_ref[...], cbs=cbs_ref[...],
             vec=vec_ref[...], lg=lg_ref[...])
    K = {name: r[...] for name, r in kref.items()}
    stage1, outs = [], []
    lr = pm_ref.shape[1]
    for s in range(bb):
        if lr == L:
            pm = pm_ref[s]
        else:
            pmpad[s, 0:lr, :] = pm_ref[s]
            pmpad[s, lr:L, :] = jnp.zeros((L - lr, MIX_W), F32)
            pm = pmpad[s]
        extg[s, 8:8 + L, :] = pm[:, 0:GDN_CONV_W]
        exts[s, 8:8 + L, :] = pm[:, SSD_XBC_OFF:SSD_XBC_OFF + SSD_CONV_W]
        st = dict(gdn=sg[s], hg=sh[s], ssd=ss[s], mlc=sc[s], mln=sn[s], mlm=mm1[s])
        outs.append({})
        stage1.append(_mixer_chunk(pm, extg.at[s], exts.at[s], st, P, K, offs, L, nv, J, layer, outs[s]))

    def stage2(seqs):
        return [t for group in zip(*[outs[s]['tasks'] for s in seqs]) for t in group]

    ga, gb = list(range(0, bb // 2)), list(range(bb // 2, bb))
    _run_interleaved([stage1[s] for s in ga])
    _run_interleaved(stage2(ga) + [stage1[s] for s in gb])
    _run_interleaved(stage2(gb) + [outs[s]['finish']() for s in ga])
    _run_interleaved([outs[s]['finish']() for s in gb])
    for s in range(bb):
        branches, new = outs[s]['result']
        br_ref[s] = branches[0:lr].astype(br_ref.dtype)
        for p in range(2):
            sg[s, p] = new['gdn'][p]
            sh[s, p] = new['hg'][p]
            sc[s, p] = new['mlc'][p]
            sn[s, p] = new['mln'][p]
        for h in range(N_HEAD):
            mm1[s, :, h:h + 1] = new['mlm'][h]
        ss[s] = new['ssd']
        tail_g = extg[s, 8 + nv - 3:8 + nv, :]
        tail_s = exts[s, 8 + nv - 3:8 + nv, :]
        extg[s, 5:8, :] = tail_g
        exts[s, 5:8, :] = tail_s
        cg1[s] = tail_g
        cs1[s] = tail_s

    @pl.when(c == nc - 1)
    def _fin():
        for s in range(bb):
            for h in range(N_HEAD):
                p, lo = h // 2, (h % 2) * HEAD_D
                gdn1[s, h] = sg[s, p, lo:lo + HEAD_D, lo:lo + HEAD_D]
                hg1[s, h] = sh[s, p, lo:lo + HEAD_D, lo:lo + HEAD_D].T
                mc1[s, h] = sc[s, p, lo:lo + HEAD_D, lo:lo + HEAD_D]
                mn1[s, h:h + 1, :] = sn[s, p, :, lo:lo + HEAD_D]
            for h in range(SSD_H):
                g = h // n_sub
                ssd1[s, h] = ss[s, g * SSD_N:(g + 1) * SSD_N, h * SSD_P:(h + 1) * SSD_P]


def _full_spec(a):
    nd = a.ndim
    return pl.BlockSpec(a.shape, lambda b, c, _nd=nd: (0,) * _nd)


def _mixer_call(proj, state_shapes, states_in, prev_out, params, L, nv, layer, bb):
    bg, t, _ = proj.shape
    lr = nv
    nchunk = t // lr
    assert bg % bb == 0 and t % lr == 0
    consts, offs, J = _mixer_consts(L, nv)
    const_arrays = [jnp.asarray(consts[n], BF16 if n in _BF16_CONSTS else F32) for n in _CONST_ORDER]

    def st_spec(shape):
        nd = len(shape)
        return pl.BlockSpec((None, bb) + tuple(shape[2:]), lambda b, c, _nd=nd: (layer, b) + (0,) * (_nd - 2))

    has_init = states_in is not None
    n_alias = 0 if prev_out is None else len(prev_out)
    inputs = [proj] + (list(states_in) if has_init else []) + list(params) + const_arrays + list(prev_out or [])
    in_specs = ([pl.BlockSpec((bb, lr, MIX_W), lambda b, c: (b, c, 0))]
                + ([st_spec(s) for s in state_shapes] if has_init else [])
                + [_full_spec(a) for a in params]
                + [_full_spec(a) for a in const_arrays]
                + [pl.BlockSpec(memory_space=pl.ANY)] * n_alias)
    out_shape = ([jax.ShapeDtypeStruct((bg, t, N_BRANCH * BRANCH_W), _row_dtype(lr))]
                 + [jax.ShapeDtypeStruct(s, F32) for s in state_shapes])
    out_specs = ([pl.BlockSpec((bb, lr, N_BRANCH * BRANCH_W), lambda b, c: (b, c, 0))]
                 + [st_spec(s) for s in state_shapes])
    first_alias = len(inputs) - n_alias
    pair = (bb, 2, LANE, LANE)
    scratch = [pltpu.VMEM(pair, F32), pltpu.VMEM(pair, F32), pltpu.VMEM(pair, F32),
               pltpu.VMEM((bb, 2, 1, LANE), F32), pltpu.VMEM((bb, SSD_G * SSD_N, SSD_H * SSD_P), F32),
               pltpu.VMEM((bb, 8 + L, GDN_CONV_W), F32), pltpu.VMEM((bb, 8 + L, SSD_CONV_W), F32),
               pltpu.VMEM((bb, L, MIX_W) if lr < L else (1, 8, LANE), F32)]
    outs = pl.pallas_call(
        functools.partial(_mixer_kernel, L, nv, J, layer, bb, offs, has_init, n_alias),
        grid=(bg // bb, nchunk),
        in_specs=in_specs, out_specs=out_specs, out_shape=out_shape, scratch_shapes=scratch,
        input_output_aliases={first_alias + k: 1 + k for k in range(n_alias)},
        compiler_params=pltpu.CompilerParams(dimension_semantics=("parallel", "arbitrary"),
                                             vmem_limit_bytes=VMEM_LIMIT),
        name=f"mixer_L{L}",
    )(*inputs)
    return outs[0], list(outs[1:])


def _ada_kernel(c_ref, w_ref, b_ref, o_ref):
    o_ref[...] = _mm(_silu(c_ref[...]), w_ref[...]) + b_ref[...]


def _ada_call(c_all, ada_w, ada_b):
    rows = c_all.shape[0]
    n = ada_w.shape[-1]
    tn = 1536
    return pl.pallas_call(
        _ada_kernel,
        grid=(DEPTH, n // tn),
        in_specs=[pl.BlockSpec((rows, D_MODEL), lambda l, j: (0, 0)),
                  pl.BlockSpec((None, D_MODEL, tn), lambda l, j: (l, 0, j)),
                  pl.BlockSpec((None, 1, tn), lambda l, j: (l, 0, j))],
        out_specs=pl.BlockSpec((None, rows, tn), lambda l, j: (l, 0, j)),
        out_shape=jax.ShapeDtypeStruct((DEPTH, rows, n), F32),
        compiler_params=pltpu.CompilerParams(dimension_semantics=("arbitrary", "arbitrary"),
                                             vmem_limit_bytes=VMEM_LIMIT),
        name="ada",
    )(c_all, ada_w, ada_b.reshape(DEPTH, 1, n))


PREP_ROWS = MIX_W - SMALL_OFF


def _prep_w_in_kernel(w_ref, s0_ref, s1_ref, s2_ref, o_ref):
    j = pl.program_id(1)
    j_small = SMALL_OFF // PREP_ROWS

    @pl.when(j != j_small)
    def _copy():
        o_ref[...] = w_ref[0].astype(BF16)

    @pl.when(j == j_small)
    def _small():
        o_ref[...] = jnp.zeros(o_ref.shape, BF16)
        o_ref[0:8, :] = s0_ref[0].astype(BF16)
        o_ref[8:16, :] = s1_ref[0].astype(BF16)
        o_ref[16:24, :] = s2_ref[0].astype(BF16)


def _prep_src_row(j):
    r = PREP_ROWS
    return jnp.where(j < 1024 // r, r * j,
                     jnp.where(j < 2816 // r, r * j + 8,
                               jnp.where(j < SMALL_OFF // r, r * j + 16,
                                         jnp.where(j < MIX_W // r, 0, r * j - MIX_W + 3864))))


def _prep_w_in_call(w_in):
    wt = jnp.swapaxes(w_in, 1, 2)
    depth, _, d = wt.shape

    def rows(n, index):
        return pl.BlockSpec((pl.Element(1), pl.Element(n), pl.Element(d)), index)

    return pl.pallas_call(
        _prep_w_in_kernel,
        grid=(depth, PROJ_W // PREP_ROWS),
        in_specs=[rows(PREP_ROWS, lambda l, j: (l, pl.multiple_of(_prep_src_row(j), 8), 0)),
                  rows(8, lambda l, j: (l, 1024, 0)), rows(8, lambda l, j: (l, 2824, 0)),
                  rows(8, lambda l, j: (l, 3856, 0))],
        out_specs=pl.BlockSpec((None, PREP_ROWS, d), lambda l, j: (l, j, 0)),
        out_shape=jax.ShapeDtypeStruct((depth, PROJ_W, d), BF16),
        compiler_params=pltpu.CompilerParams(dimension_semantics=("arbitrary", "arbitrary"),
                                             vmem_limit_bytes=VMEM_LIMIT),
        name="prep_w_in",
    )(wt, wt, wt, wt)


def _rms_mod(x, nw, sc, sh):
    ms = jnp.mean(x * x, axis=-1, keepdims=True)
    return x * lax.rsqrt(ms + EPS) * nw * (1.0 + sc) + sh


def _tok_spec(gblk, width):
    g, r = gblk
    return pl.BlockSpec((g, r, width), lambda i, j: (i, j, 0))


def _mod_spec(gblk, layer, k, mods):
    rows = 1 if mods.shape[2] == 1 else gblk[1]
    return pl.BlockSpec((None, gblk[0], rows, D_MODEL), lambda i, j: (layer, i, j if rows > 1 else 0, k))


def _layer_spec(shape, layer):
    nd = len(shape)
    return pl.BlockSpec((None,) + tuple(shape[1:]), lambda i, j: (layer,) + (0,) * (nd - 1),
                        pipeline_mode=pl.Buffered(1))


def _dense_params():
    return pltpu.CompilerParams(dimension_semantics=("arbitrary", "arbitrary"), vmem_limit_bytes=VMEM_LIMIT)


def _inproj_kernel(x_ref, nw_ref, sc_ref, sh_ref, w_ref, pm_ref, gate_ref):
    g, r, _ = x_ref.shape
    h = _rms_mod(x_ref[...], nw_ref[...], sc_ref[...], sh_ref[...]).astype(BF16).reshape(g * r, D_MODEL)
    pm_ref[...] = _mm(h, w_ref[0:MIX_W, :], _NT).reshape(g, r, MIX_W)
    gates = _sigmoid(_mm(h, w_ref[MIX_W:PROJ_W, :], _NT))
    gate_ref[...] = gates.astype(gate_ref.dtype).reshape(g, r, GATE_W)


def _inproj_call(x, layer, nw, mods, w, gblk):
    bg, t, _ = x.shape
    return pl.pallas_call(
        _inproj_kernel,
        grid=(bg // gblk[0], t // gblk[1]),
        in_specs=[_tok_spec(gblk, D_MODEL), _layer_spec(nw.shape, layer),
                  _mod_spec(gblk, layer, 1, mods), _mod_spec(gblk, layer, 0, mods), _layer_spec(w.shape, layer)],
        out_specs=[_tok_spec(gblk, MIX_W), _tok_spec(gblk, GATE_W)],
        out_shape=[jax.ShapeDtypeStruct((bg, t, MIX_W), F32),
                   jax.ShapeDtypeStruct((bg, t, GATE_W), _row_dtype(gblk[1]))],
        compiler_params=_dense_params(), name="inproj",
    )(x, nw, mods, mods, w)


def _merge_ffn_kernel(final, br_ref, gate_ref, x_ref, gt1_ref, nw_ref, sc_ref, sh_ref, gt2_ref,
                      wb_ref, wo_ref, w1_ref, w2_ref, fw_ref, o_ref):
    g, r, _ = x_ref.shape
    br = br_ref[...].reshape(g * r, N_BRANCH * BRANCH_W)
    merged = None
    for n in range(N_BRANCH):
        up = _mm(br[:, n * BRANCH_W:(n + 1) * BRANCH_W], wb_ref[n])
        gate = gate_ref[:, :, n * D_MODEL:(n + 1) * D_MODEL].astype(F32).reshape(g * r, D_MODEL)
        t = gate * up
        merged = t if merged is None else merged + t
    x1 = x_ref[...] + gt1_ref[...] * _mm(merged, wo_ref[...]).reshape(g, r, D_MODEL)
    h = _rms_mod(x1, nw_ref[...], sc_ref[...], sh_ref[...]).astype(BF16).reshape(g * r, D_MODEL)
    a = _mm(h, w1_ref[:, 0:D_FF])
    b = _mm(h, w1_ref[:, D_FF:2 * D_FF])
    x2 = x1 + gt2_ref[...] * _mm(_silu(a) * b, w2_ref[...]).reshape(g, r, D_MODEL)
    if final:
        ms = jnp.mean(x2 * x2, axis=-1, keepdims=True)
        x2 = x2 * lax.rsqrt(ms + EPS) * fw_ref[...]
    o_ref[...] = x2


def _merge_ffn_call(br, gates, x, layer, nw, mods, wb, wo, w1, w2, fw, gblk, final):
    bg, t, _ = x.shape
    return pl.pallas_call(
        functools.partial(_merge_ffn_kernel, final),
        grid=(bg // gblk[0], t // gblk[1]),
        in_specs=[_tok_spec(gblk, N_BRANCH * BRANCH_W), _tok_spec(gblk, GATE_W), _tok_spec(gblk, D_MODEL),
                  _mod_spec(gblk, layer, 2, mods), _layer_spec(nw.shape, layer),
                  _mod_spec(gblk, layer, 4, mods), _mod_spec(gblk, layer, 3, mods), _mod_spec(gblk, layer, 5, mods),
                  _layer_spec(wb.shape, layer), _layer_spec(wo.shape, layer),
                  _layer_spec(w1.shape, layer), _layer_spec(w2.shape, layer),
                  pl.BlockSpec(fw.shape, lambda i, j: (0, 0), pipeline_mode=pl.Buffered(1))],
        out_specs=_tok_spec(gblk, D_MODEL),
        out_shape=jax.ShapeDtypeStruct((bg, t, D_MODEL), F32),
        compiler_params=_dense_params(), name="merge_ffn",
    )(br, gates, x, mods, nw, mods, mods, mods, wb, wo, w1, w2, fw)


def _small_params(gdn_dt_bias, ssd_dt_bias, ml_b_i, ml_b_f, gdn_a_log, ssd_a_log):
    z4 = jnp.zeros((4,), F32)
    bias = jnp.concatenate([z4, gdn_dt_bias, ssd_dt_bias, ml_b_i, ml_b_f, jnp.zeros((LANE - 24,), F32)])
    alog = jnp.concatenate([z4, gdn_a_log, ssd_a_log, jnp.zeros((LANE - 16,), F32)])
    return jnp.concatenate([bias[None], alog[None], jnp.zeros((6, LANE), F32)], axis=0)


def _layer_params(l, gdn_conv_w, gdn_conv_b, gdn_a_log, gdn_dt_bias, gdn_norm_w, hg_lb_logits, hg_norm_w,
                  ssd_conv_w, ssd_conv_b, ssd_a_log, ssd_dt_bias, ssd_d, ssd_norm_w, ml_b_i, ml_b_f, ml_norm_w):
    sp = _small_params(gdn_dt_bias[l], ssd_dt_bias[l], ml_b_i[l], ml_b_f[l], gdn_a_log[l], ssd_a_log[l])
    vec = jnp.stack([jnp.tile(gdn_norm_w[l], N_HEAD), jnp.tile(hg_norm_w[l], N_HEAD), ssd_norm_w[l],
                     jnp.tile(ml_norm_w[l], N_HEAD), jnp.repeat(ssd_d[l], SSD_P)]
                    + [jnp.zeros((256,), F32)] * 3)
    return [sp, gdn_conv_w[l], gdn_conv_b[l][None], ssd_conv_w[l], ssd_conv_b[l][None], vec,
            hg_lb_logits.astype(F32)]


def _trunk(x, mods, state_shapes, states_in, L, nv, gblk, gblk_ffn, bb, W, mixer_params):
    g, r, _ = x.shape
    seqs = (g, r) if nv == L else (g * r // nv, nv)
    new_states = None
    for l in range(DEPTH):
        pm, gates = _inproj_call(x, l, W['norm1'], mods, W['w_in'], gblk)
        br, new_states = _mixer_call(pm.reshape(seqs + (MIX_W,)), state_shapes, states_in, new_states,
                                     mixer_params[l], L, nv, l, bb)
        x = _merge_ffn_call(br.reshape(g, r, -1), gates, x, l, W['norm2'], mods, W['w_branch'], W['w_out'],
                            W['ffn_w_in'], W['ffn_w_out'], W['final'], gblk_ffn, final=(l == DEPTH - 1))
    return x, new_states


def kernel(x_prompt, x_sample, c_prompt, c_sample, state_gdn, state_gdn_conv, state_hgrn, state_ssd, state_ssd_conv, state_mlstm_c, state_mlstm_n, state_mlstm_m, ada_w, ada_b, norm1_w, norm2_w, w_in, gdn_conv_w, gdn_conv_b, gdn_a_log, gdn_dt_bias, gdn_norm_w, hg_lb_logits, hg_norm_w, ssd_conv_w, ssd_conv_b, ssd_a_log, ssd_dt_bias, ssd_d, ssd_norm_w, ml_b_i, ml_b_f, ml_norm_w, w_branch, w_out, ffn_w_in, ffn_w_out, final_norm_w):
    bp, tp, _ = x_prompt.shape
    bs, ts, _ = x_sample.shape
    assert tp % CHUNK == 0 and LS % ts == 0 and bs % (LS // ts) == 0

    W = dict(w_in=_prep_w_in_call(w_in), w_branch=w_branch.astype(BF16), w_out=w_out.astype(BF16),
             ffn_w_in=ffn_w_in.astype(BF16), ffn_w_out=ffn_w_out.astype(BF16),
             norm1=norm1_w[:, None, :], norm2=norm2_w[:, None, :], final=final_norm_w[None])
    mixer_params = [_layer_params(l, gdn_conv_w, gdn_conv_b, gdn_a_log, gdn_dt_bias, gdn_norm_w, hg_lb_logits,
                                  hg_norm_w, ssd_conv_w, ssd_conv_b, ssd_a_log, ssd_dt_bias, ssd_d, ssd_norm_w,
                                  ml_b_i, ml_b_f, ml_norm_w) for l in range(DEPTH)]

    mods = _ada_call(jnp.concatenate([c_prompt, c_sample], axis=0), ada_w, ada_b)
    mods_p = mods[:, :bp].reshape(DEPTH, bp, 1, 6 * D_MODEL)
    grp = LS // ts
    mods_s = jnp.repeat(mods[:, bp:], ts, axis=1).reshape(DEPTH, bs // grp, LS, 6 * D_MODEL)
    sample_states = [state_gdn, state_gdn_conv, state_hgrn, state_ssd, state_ssd_conv,
                     state_mlstm_c, state_mlstm_n, state_mlstm_m.reshape(DEPTH, bs, 1, N_HEAD)]
    shapes_s = [s.shape for s in sample_states]
    shapes_p = [(DEPTH, bp) + tuple(s[2:]) for s in shapes_s]

    y_p, new_p = _trunk(x_prompt, mods_p, shapes_p, None, CHUNK, CHUNK, (1, 256), (1, 512), 8, W, mixer_params)
    y_s, new_s = _trunk(x_sample.reshape(bs // grp, LS, D_MODEL), mods_s, shapes_s, sample_states, LS, ts,
                        (32, LS), (32, LS), 8, W, mixer_params)
    new_p[7] = new_p[7].reshape(DEPTH, bp, N_HEAD)
    new_s[7] = new_s[7].reshape(DEPTH, bs, N_HEAD)
    return (y_p, y_s.reshape(bs, ts, D_MODEL)) + tuple(new_p) + tuple(new_s)
```

```python
import functools
import math

import numpy as np
import jax
import jax.numpy as jnp
from jax import lax
from jax.experimental import pallas as pl
from jax.experimental.pallas import tpu as pltpu

F32 = jnp.float32
BF16 = jnp.bfloat16

D_MODEL = 1024
DEPTH = 2
N_BRANCH = 4
BRANCH_W = 256
N_HEAD = 4
HEAD_D = 64
SSD_H = 8
SSD_P = 32
SSD_N = 64
SSD_G = 2
CONV_K = 4
CHUNK = 64
LS = 8
D_FF = 2816
EPS = 1e-6
NEG = -1e30
GDN_CONV_W = 768
SSD_CONV_W = 512
SSD_XBC_OFF = 2304
MIX_W = 4096
GATE_W = 4096
PROJ_W = MIX_W + GATE_W
SMALL_OFF = 3840
LANE = 128
VMEM_LIMIT = 56 * 1024 * 1024

ST_BETA, ST_GA, ST_DT, ST_MI, ST_MF = 0, 4, 8, 16, 20

_NN = (((1,), (0,)), ((), ()))
_NT = (((1,), (1,)), ((), ()))
_TN = (((0,), (0,)), ((), ()))


def _mm(a, b, dims=_NN):
    return lax.dot_general(a.astype(BF16), b.astype(BF16), dims, preferred_element_type=F32)


def _split(x, n):
    parts, r = [], x
    for i in range(n):
        p = r.astype(BF16)
        parts.append(p)
        if i < n - 1:
            r = r - p.astype(F32)
    return parts


def _mm01(x, m, n):
    out = None
    for p in _split(x, n):
        t = lax.dot_general(p, m.astype(BF16), _NN, preferred_element_type=F32)
        out = t if out is None else out + t
    return out


def _m01m(m, x, n):
    out = None
    for p in _split(x, n):
        t = lax.dot_general(m, p, _NN, preferred_element_type=F32)
        out = t if out is None else out + t
    return out


def _sigmoid(x):
    return jax.nn.sigmoid(x)


def _silu(x):
    return x * jax.nn.sigmoid(x)


def _softplus(x):
    return jnp.maximum(x, 0.0) + jnp.log(1.0 + jnp.exp(-jnp.abs(x)))


def _logsig(x):
    return jnp.minimum(x, 0.0) - jnp.log(1.0 + jnp.exp(-jnp.abs(x)))


def _row_dtype(rows):
    return BF16 if rows % 16 == 0 else F32


def _expand(base, heads, width):
    e = np.zeros((LANE, heads * width), np.float32)
    for h in range(heads):
        e[base + h, h * width:(h + 1) * width] = 1.0
    return e


def _cat_segments(segs):
    cols, off, pos, seen = [], {}, 0, {}
    for name, m in segs:
        key = m.tobytes() + bytes(str(m.shape), 'ascii')
        if key in seen:
            off[name] = seen[key]
            continue
        w = m.shape[1]
        wp = -(-w // LANE) * LANE
        mp = np.zeros((m.shape[0], wp), np.float32)
        mp[:, :w] = m
        cols.append(mp)
        off[name] = seen[key] = (pos, w)
        pos += wp
    return np.concatenate(cols, axis=1), off


@functools.lru_cache(maxsize=None)
def _mixer_consts(L, nv):
    J = int(round(math.log2(L)))
    assert 1 << J == L
    f = np.float32
    r = np.arange(L)
    tri = (r[None, :] <= r[:, None])
    rev = (r[:, None] < r[None, :]) & (r[None, :] <= nv - 1)
    strict = (r[None, :] < r[:, None])
    eye = np.eye(L, dtype=bool)
    lvl, ab = [], []
    for j in range(J):
        bnd = ((r >> (j + 1)) << (j + 1)) + (1 << j) - 1
        low = ((r >> j) & 1) == 1
        a = low[:, None] & (bnd[:, None] < r[None, :]) & (r[None, :] <= r[:, None])
        b = (~low)[:, None] & (r[:, None] < r[None, :]) & (r[None, :] <= bnd[:, None])
        ab.append(a | b)
        same = (r[:, None] >> (j + 1)) == (r[None, :] >> (j + 1))
        lvl.append(same & low[:, None] & (~low)[None, :])

    def tile(m, n):
        return np.tile(m.astype(f), (1, n))

    pr = np.repeat(np.arange(2), L)
    lr2 = np.tile(r, 2)
    ch = np.arange(LANE)
    ch2 = np.arange(2 * LANE)
    c = dict(
        trirev=np.concatenate([tri, rev], axis=0).astype(f),
        mhg=np.concatenate(ab + [tri, rev], axis=0).astype(f),
        ones_ll=np.ones((L, L), f), tri=tri.astype(f),
        tril2=tile(tri, 2), strict2=tile(strict, 2), tril4=tile(tri, 4),
        eye4=tile(eye, 4), supper16=tile(strict, 16),
        e_dr=np.concatenate([_expand(ST_GA, 4, L), _expand(ST_DT, SSD_H, L), _expand(ST_MF, 4, L)], axis=1),
        hmaskp=np.stack([tile(eye, 2)] + [tile(m, 2) for m in lvl]),
        eyep=np.eye(2 * L, dtype=f),
        lvlp=np.stack([(pr[:, None] == pr[None, :]) & m[lr2[:, None], lr2[None, :]] for m in lvl]).astype(f),
        stkp=(pr[:, None] == ch[None, :] // HEAD_D).astype(f),
        bdp=(ch[:, None] // HEAD_D == ch[None, :] // HEAD_D).astype(f),
        pm2=np.stack([ch < HEAD_D, ch >= HEAD_D]).astype(f),
        stk4s=(np.repeat(np.arange(4), L)[:, None] == ch[None, :] // SSD_P).astype(f),
        gbd=(ch[:, None] // SSD_N == ch2[None, :] // LANE).astype(f),
        rowvalid=(r[:, None] <= nv - 1).astype(f) * np.ones((1, LANE), f),
        bd64=(ch2[:, None] // 64 == ch2[None, :] // 64).astype(f),
        bd128=(ch2[:, None] // 128 == ch2[None, :] // 128).astype(f),
    )
    c['e_act'], off_act = _cat_segments([('b_r', _expand(ST_BETA, 4, L)), ('b_c', _expand(ST_BETA, 4, HEAD_D)),
                                         ('dt_c', _expand(ST_DT, SSD_H, SSD_P)),
                                         ('i_r', _expand(ST_MI, 4, L)), ('i_c', _expand(ST_MI, 4, HEAD_D))])
    c['e_cs'], off_cs = _cat_segments([('g_c', _expand(ST_GA, 4, HEAD_D)), ('s_c', _expand(ST_DT, SSD_H, SSD_P)),
                                       ('m_r', _expand(ST_MF, 4, L)), ('m_c', _expand(ST_MF, 4, HEAD_D))])
    c['e_rev'], off_rev = _cat_segments([('g_c', _expand(ST_GA, 4, HEAD_D)), ('s_c', _expand(ST_DT, SSD_H, SSD_P)),
                                         ('m_c', _expand(ST_MF, 4, HEAD_D))])
    return c, dict(act=off_act, cs=off_cs, rev=off_rev), J


_CONST_ORDER = ('trirev', 'mhg', 'ones_ll', 'tri', 'tril2', 'strict2', 'tril4', 'eye4', 'supper16', 'e_dr', 'hmaskp',
                'eyep', 'lvlp', 'stkp', 'bdp', 'pm2', 'stk4s', 'gbd', 'rowvalid', 'bd64', 'bd128', 'e_act', 'e_cs',
                'e_rev')
_BF16_CONSTS = ('trirev', 'mhg', 'ones_ll', 'tri', 'supper16', 'e_dr', 'lvlp', 'stkp', 'stk4s', 'bd64', 'bd128',
                'e_act', 'e_cs', 'e_rev')


def _conv_silu(ext, w, b, L):
    y = b
    for i in range(CONV_K):
        y = y + ext[5 + i:5 + i + L, :] * w[i:i + 1, :]
    return _silu(y)


def _run_interleaved(tasks):
    tasks = list(tasks)
    while tasks:
        alive = []
        for t in tasks:
            try:
                next(t)
                alive.append(t)
            except StopIteration:
                pass
        tasks = alive


def _mixer_chunk(pm, ext_g, ext_s, st, P, K, offs, L, nv, J, layer, out):
    lane = lax.broadcasted_iota(jnp.int32, (1, LANE), 1)
    rowvalid = K['rowvalid'][:, 0:1]
    tril2 = K['tril2'] > 0.0
    tril4 = K['tril4'] > 0.0
    stkp, bdp = K['stkp'], K['bdp']
    upper_c = lane >= HEAD_D
    upper_r = lax.broadcasted_iota(jnp.int32, (1, 2 * L), 1) >= L

    def seg(x, which, name):
        o, w = offs[which][name]
        return x[:, o:o + w]

    def pair_c(x, p):
        return x[:, p * LANE:(p + 1) * LANE]

    def pair_r(x, p):
        return x[:, p * 2 * L:(p + 1) * 2 * L]

    def stack(x):
        xb = x.astype(BF16)
        return jnp.concatenate([xb, xb], axis=0) * stkp

    z = pm[:, SMALL_OFF:SMALL_OFF + LANE] + P['sp'][0:1, :]
    act = jnp.where(lane < ST_GA, _sigmoid(z),
                    jnp.where(lane < ST_MI, _softplus(z), jnp.where(lane < ST_MF, z, _logsig(z))))
    neg_a = -jnp.exp(P['sp'][1:2, :])
    dec_in = jnp.where((lane >= ST_GA) & (lane < ST_MI), neg_a * act,
                       jnp.where((lane >= ST_MF) & (lane < ST_MF + 4), act, 0.0))
    cr = _m01m(K['trirev'], dec_in, 2)
    yield
    cs, rev = cr[:L], cr[L:]
    ea = _mm01(act, K['e_act'], 2)
    ec = _mm01(cs, K['e_cs'], 2)
    er = _mm01(rev, K['e_rev'], 2)
    yield
    m_r = seg(ec, 'cs', 'm_r')
    g_c, s_c, m_c = seg(ec, 'cs', 'g_c'), seg(ec, 'cs', 's_c'), seg(ec, 'cs', 'm_c')
    b_r, b_c, dt_c = seg(ea, 'act', 'b_r'), seg(ea, 'act', 'b_c'), seg(ea, 'act', 'dt_c')
    i_r, i_c = seg(ea, 'act', 'i_r'), seg(ea, 'act', 'i_c')
    rg_c, rs_c, rm_c = seg(er, 'rev', 'g_c'), seg(er, 'rev', 's_c'), seg(er, 'rev', 'm_c')
    ym = _mm(dec_in, K['e_dr']).astype(BF16) * K['supper16']
    yield
    d_all = lax.dot_general(K['tri'], ym, _NN, preferred_element_type=F32)
    d_g, d_s, d_m = d_all[:, 0:4 * L], d_all[:, 4 * L:12 * L], d_all[:, 12 * L:16 * L]
    row_i = _m01m(K['ones_ll'], i_r * K['eye4'], 2)

    res = dict(o_gdn=[None, None], gdn=[None, None], o_hg=[None, None], hg=[None, None],
               o_ml=[None, None], mlc=[None, None], mln=[None, None], mlm=[None] * N_HEAD)

    qkv = _conv_silu(ext_g, P['cwg'], P['cbg'], L)
    q, k, v = qkv[:, 0:256], qkv[:, 256:512], qkv[:, 512:768]
    ss = _mm(jnp.concatenate([q * q, k * k], axis=0), K['bd64'])
    yield
    q = q * (lax.rsqrt(ss[:L] + EPS) * (HEAD_D ** -0.5))
    k = k * lax.rsqrt(ss[L:] + EPS)

    def gdn_task(p):
        qt, kt, vt = pair_c(q, p), pair_c(k, p), pair_c(v, p)
        beta_c, gc = pair_c(b_c, p), pair_c(g_c, p)
        eg = jnp.exp(gc)
        dec = jnp.where(tril2, jnp.exp(jnp.where(tril2, pair_r(d_g, p), 0.0)), 0.0)
        kkqk = _mm(jnp.concatenate([kt, qt], axis=0), stack(kt), _NT)
        yield
        n = (kkqk[:L] * dec * pair_r(b_r, p) * K['strict2']).astype(BF16)
        nbd = jnp.concatenate([n, n], axis=0)
        t = K['eyep'] - (nbd * K['lvlp'][0]).astype(F32)
        for j in range(1, J):
            x = _mm(t, nbd * K['lvlp'][j])
            yield
            t = t - _mm(x, t)
            yield
        sv = _mm(t, stack(vt * beta_c))
        sk = _mm(t, stack(kt * (beta_c * eg)))
        yield
        s_p = st['gdn'][p]
        r = _mm(jnp.concatenate([sk[:L] + sk[L:], qt * eg], axis=0), s_p)
        yield
        u = sv[:L] + sv[L:] - r[:L]
        res['o_gdn'][p] = r[L:] + _mm(kkqk[L:] * dec, stack(u))
        kw = kt * (jnp.exp(pair_c(rg_c, p)) * rowvalid)
        res['gdn'][p] = s_p * jnp.exp(gc[nv - 1:nv, :]) + bdp * _mm(kw, u, _TN)
        yield

    lg = P['lg']
    mx = lg[0:1, :]
    for i in range(1, DEPTH):
        mx = jnp.maximum(mx, lg[i:i + 1, :])
    ex = [jnp.exp(lg[i:i + 1, :] - mx) for i in range(DEPTH)]
    tot = ex[0]
    for i in range(1, DEPTH):
        tot = tot + ex[i]
    sm = [e / tot for e in ex]
    cum = sm[0]
    for i in range(1, layer + 1):
        cum = cum + sm[i]
    lb = cum - sm[0]
    lb_pos = lb > 0
    log_lb = jnp.log(jnp.where(lb_pos, lb, 1.0))
    hq = _silu(pm[:, 1024:1280])
    fz = pm[:, 1280:1536]
    hv = pm[:, 1536:1792]
    ls = _logsig(fz)
    t2 = jnp.log1p(-lb) + ls
    la = jnp.maximum(log_lb, t2) + jnp.log1p(jnp.exp(-jnp.abs(log_lb - t2)))
    logf = jnp.where(lb_pos, la, ls)
    kg = (1.0 - lb) * _sigmoid(-fz)
    ey = jnp.exp(_m01m(K['mhg'], logf, 2))
    yield
    eg_h = ey[J * L:(J + 1) * L]
    eyb = ey[0:J * L].astype(BF16)
    hqb = hq.astype(BF16)
    qe = hq * eg_h
    kwr = kg * (ey[(J + 1) * L:(J + 2) * L] * rowvalid)

    def hg_task(p):
        q_p, k_st = pair_c(hqb, p), stack(pair_c(kg, p))
        a_p = K['hmaskp'][0] * _mm(q_p, k_st, _NT)
        yield
        for j in range(J):
            e_j = pair_c(eyb[j * L:(j + 1) * L], p)
            a_p = a_p + K['hmaskp'][j + 1] * _mm(q_p * e_j, k_st * jnp.concatenate([e_j, e_j], axis=0), _NT)
            yield
        st_p = st['hg'][p]
        res['o_hg'][p] = _mm(pair_c(qe, p), st_p, _NT) + _mm(a_p, stack(pair_c(hv, p)))
        yield
        res['hg'][p] = (st_p * pair_c(eg_h[nv - 1:nv, :], p)
                        + bdp * _mm(pair_c(hv, p), pair_c(kwr, p), _TN))
        yield

    xbc = _conv_silu(ext_s, P['cws'], P['cbs'], L)
    xs, bs, cc = xbc[:, 0:256], xbc[:, 256:384], xbc[:, 384:512]
    vs = xs * dt_c
    bsb = bs.astype(BF16)
    bs4 = jnp.concatenate([bsb] * 4, axis=0)

    def ssd_task():
        o_intra = []
        for g in range(SSD_G):
            cbw = _mm(cc * K['pm2'][g:g + 1, :], bs4, _NT)
            dec = jnp.where(tril4, jnp.exp(jnp.where(tril4, d_s[:, g * 4 * L:(g + 1) * 4 * L], 0.0)), 0.0)
            vb = pair_c(vs, g).astype(BF16)
            o_intra.append(_mm(cbw * dec, jnp.concatenate([vb] * 4, axis=0) * K['stk4s']))
            yield
        s_all = st['ssd']
        res['o_ssd'] = jnp.exp(s_c) * _mm(cc, s_all) + jnp.concatenate(o_intra, axis=1)
        res['ssd'] = (s_all * jnp.exp(s_c[nv - 1:nv, :])
                      + K['gbd'] * _mm(bs * rowvalid, vs * jnp.exp(rs_c), _TN))
        yield

    mq = pm[:, 2816:3072] * (HEAD_D ** -0.5)
    mk = pm[:, 3072:3328]
    mv = pm[:, 3328:3584]

    def ml_task(p, delay):
        for _ in range(delay):
            yield
        qt, kt, vt = pair_c(mq, p), pair_c(mk, p), pair_c(mv, p)
        bm_r, bm_c = pair_r(m_r, p), pair_c(m_c, p)
        mp0 = st['mlm'][:, 2 * p:2 * p + 1]
        mp1 = st['mlm'][:, 2 * p + 1:2 * p + 2]
        logw = jnp.where(tril2, pair_r(d_m, p) + pair_r(row_i, p), NEG)
        mx0 = jnp.max(jnp.where(upper_r, NEG, logw), axis=-1, keepdims=True)
        mx1 = jnp.max(jnp.where(upper_r, logw, NEG), axis=-1, keepdims=True)
        l0_r = bm_r + jnp.where(upper_r, mp1, mp0)
        l0_c = bm_c + jnp.where(upper_c, mp1, mp0)
        mt_r = jnp.maximum(l0_r, jnp.where(upper_r, mx1, mx0))
        mt_c = jnp.maximum(l0_c, jnp.where(upper_c, mx1, mx0))
        w = jnp.where(tril2, jnp.exp(logw - mt_r), 0.0)
        w0 = jnp.exp(l0_c - mt_c)
        qk = _mm(qt, stack(kt), _NT) * w
        yield
        c_p = st['mlc'][p]
        n_p = st['mln'][p]
        num = w0 * _mm(qt, c_p) + _mm(qk, stack(vt))
        den = w0 * _mm01(qt * n_p, bdp, 2) + _mm01(qk, stkp, 2)
        res['o_ml'][p] = num * (1.0 / jnp.maximum(jnp.abs(den), jnp.exp(-mt_c)))
        yield
        m_l = mt_c[nv - 1:nv, :]
        wl0 = jnp.exp(bm_c[nv - 1:nv, :] + jnp.where(upper_c, mp1, mp0) - m_l)
        kwl = kt * (jnp.exp(pair_c(rm_c, p) + pair_c(i_c, p) - m_l) * rowvalid)
        res['mlc'][p] = c_p * wl0 + bdp * _mm(kwl, vt, _TN)
        res['mln'][p] = n_p * wl0 + jnp.sum(kwl, axis=0, keepdims=True)
        res['mlm'][2 * p] = m_l[:, 0:1]
        res['mlm'][2 * p + 1] = m_l[:, HEAD_D:HEAD_D + 1]
        yield

    def finish():
        o_gdn = jnp.concatenate(res['o_gdn'], axis=1)
        o_hg = jnp.concatenate(res['o_hg'], axis=1)
        hh = jnp.concatenate(res['o_ml'], axis=1)
        ms = _mm(jnp.concatenate([o_gdn * o_gdn, o_hg * o_hg, hh * hh], axis=0), K['bd64']) * (1.0 / HEAD_D)
        yield
        out_a = o_gdn * lax.rsqrt(ms[0:L] + EPS) * P['vec'][0:1, :] * _silu(pm[:, 768:1024])
        out_b = o_hg * lax.rsqrt(ms[L:2 * L] + EPS) * P['vec'][1:2, :] * _silu(pm[:, 1792:2048])
        out_d = hh * lax.rsqrt(ms[2 * L:3 * L] + EPS) * P['vec'][3:4, :] * _sigmoid(pm[:, 3584:3840])
        ys = (res['o_ssd'] + P['vec'][4:5, :] * xs) * _silu(pm[:, 2048:2304])
        out_c = ys * lax.rsqrt(_mm(ys * ys, K['bd128']) * (1.0 / (2 * HEAD_D)) + EPS) * P['vec'][2:3, :]
        branches = jnp.concatenate([out_a, out_b, out_c, out_d], axis=1)
        new = dict(gdn=res['gdn'], hg=res['hg'], ssd=res['ssd'], mlc=res['mlc'], mln=res['mln'], mlm=res['mlm'])
        out['result'] = (branches, new)
        yield

    out['tasks'] = ([gdn_task(p) for p in range(2)] + [hg_task(p) for p in range(2)]
                    + [ssd_task()] + [ml_task(p, 3 + 2 * p) for p in range(2)])
    out['finish'] = finish


def _mixer_kernel(L, nv, J, layer, bb, offs, has_init, n_alias, *refs):
    it = iter(refs)
    pm_ref = next(it)
    if has_init:
        gdn0, cg0, hg0, ssd0, cs0, mc0, mn0, mm0 = (next(it) for _ in range(8))
    sp_ref, cwg_ref, cbg_ref, cws_ref, cbs_ref, vec_ref, lg_ref = (next(it) for _ in range(7))
    kref = {name: next(it) for name in _CONST_ORDER}
    for _ in range(n_alias):
        next(it)
    br_ref = next(it)
    gdn1, cg1, hg1, ssd1, cs1, mc1, mn1, mm1 = (next(it) for _ in range(8))
    sg, sh, sc, sn, ss, extg, exts, pmpad = (next(it) for _ in range(8))

    c = pl.program_id(1)
    nc = pl.num_programs(1)
    n_sub = SSD_H // SSD_G

    @pl.when(c == 0)
    def _init():
        sg[...] = jnp.zeros_like(sg)
        sh[...] = jnp.zeros_like(sh)
        sc[...] = jnp.zeros_like(sc)
        ss[...] = jnp.zeros_like(ss)
        extg[:, 0:8, :] = jnp.zeros((bb, 8, GDN_CONV_W), F32)
        exts[:, 0:8, :] = jnp.zeros((bb, 8, SSD_CONV_W), F32)
        if not has_init:
            sn[...] = jnp.zeros_like(sn)
            mm1[...] = jnp.zeros_like(mm1)
            return
        for s in range(bb):
            for h in range(N_HEAD):
                p, lo = h // 2, (h % 2) * HEAD_D
                sg[s, p, lo:lo + HEAD_D, lo:lo + HEAD_D] = gdn0[s, h]
                sh[s, p, lo:lo + HEAD_D, lo:lo + HEAD_D] = hg0[s, h].T
                sc[s, p, lo:lo + HEAD_D, lo:lo + HEAD_D] = mc0[s, h]
                sn[s, p, :, lo:lo + HEAD_D] = mn0[s, h:h + 1, :]
            for h in range(SSD_H):
                g = h // n_sub
                ss[s, g * SSD_N:(g + 1) * SSD_N, h * SSD_P:(h + 1) * SSD_P] = ssd0[s, h]
        mm1[...] = mm0[...]
        extg[:, 5:8, :] = cg0[...]
        exts[:, 5:8, :] = cs0[...]

    P = dict(sp=sp_ref[...], cwg=cwg_ref[...], cbg=cbg_ref[...], cws=cws_ref[...], cbs=cbs_ref[...],
             vec=vec_ref[...], lg=lg_ref[...])
    K = {name: r[...] for name, r in kref.items()}
    stage1, outs = [], []
    spg = L // nv
    for s in range(bb):
        grp, off = s // spg, (s % spg) * nv
        if nv == L:
            pm = pm_ref[s]
        else:
            pmpad[s, 0:nv, :] = pm_ref[grp, off:off + nv, :]
            pmpad[s, nv:L, :] = jnp.zeros((L - nv, MIX_W), F32)
            pm = pmpad[s]
        extg[s, 8:8 + L, :] = pm[:, 0:GDN_CONV_W]
        exts[s, 8:8 + L, :] = pm[:, SSD_XBC_OFF:SSD_XBC_OFF + SSD_CONV_W]
        st = dict(gdn=sg[s], hg=sh[s], ssd=ss[s], mlc=sc[s], mln=sn[s], mlm=mm1[s])
        outs.append({})
        stage1.append(_mixer_chunk(pm, extg.at[s], exts.at[s], st, P, K, offs, L, nv, J, layer, outs[s]))

    def stage2(seqs):
        return [t for group in zip(*[outs[s]['tasks'] for s in seqs]) for t in group]

    ga, gb = list(range(0, bb // 2)), list(range(bb // 2, bb))
    _run_interleaved([stage1[s] for s in ga])
    _run_interleaved(stage2(ga) + [stage1[s] for s in gb])
    _run_interleaved(stage2(gb) + [outs[s]['finish']() for s in ga])
    _run_interleaved([outs[s]['finish']() for s in gb])
    for s in range(bb):
        branches, new = outs[s]['result']
        grp, off = s // spg, (s % spg) * nv
        br_ref[grp, off:off + nv, :] = branches[0:nv].astype(br_ref.dtype)
        for p in range(2):
            sg[s, p] = new['gdn'][p]
            sh[s, p] = new['hg'][p]
            sc[s, p] = new['mlc'][p]
            sn[s, p] = new['mln'][p]
        for h in range(N_HEAD):
            mm1[s, :, h:h + 1] = new['mlm'][h]
        ss[s] = new['ssd']
        tail_g = extg[s, 8 + nv - 3:8 + nv, :]
        tail_s = exts[s, 8 + nv - 3:8 + nv, :]
        extg[s, 5:8, :] = tail_g
        exts[s, 5:8, :] = tail_s
        cg1[s] = tail_g
        cs1[s] = tail_s

    @pl.when(c == nc - 1)
    def _fin():
        for s in range(bb):
            for h in range(N_HEAD):
                p, lo = h // 2, (h % 2) * HEAD_D
                gdn1[s, h] = sg[s, p, lo:lo + HEAD_D, lo:lo + HEAD_D]
                hg1[s, h] = sh[s, p, lo:lo + HEAD_D, lo:lo + HEAD_D].T
                mc1[s, h] = sc[s, p, lo:lo + HEAD_D, lo:lo + HEAD_D]
                mn1[s, h:h + 1, :] = sn[s, p, :, lo:lo + HEAD_D]
            for h in range(SSD_H):
                g = h // n_sub
                ssd1[s, h] = ss[s, g * SSD_N:(g + 1) * SSD_N, h * SSD_P:(h + 1) * SSD_P]


def _full_spec(a):
    nd = a.ndim
    return pl.BlockSpec(a.shape, lambda b, c, _nd=nd: (0,) * _nd)


def _mixer_call(proj, state_shapes, states_in, prev_out, params, L, nv, layer, bb):
    g, t, _ = proj.shape
    spg = L // nv if nv < L else 1
    nchunk = t // L
    assert (g * spg) % bb == 0 and bb % spg == 0 and t % L == 0
    consts, offs, J = _mixer_consts(L, nv)
    const_arrays = [jnp.asarray(consts[n], BF16 if n in _BF16_CONSTS else F32) for n in _CONST_ORDER]

    def st_spec(shape):
        nd = len(shape)
        return pl.BlockSpec((None, bb) + tuple(shape[2:]), lambda b, c, _nd=nd: (layer, b) + (0,) * (_nd - 2))

    has_init = states_in is not None
    n_alias = 0 if prev_out is None else len(prev_out)
    inputs = [proj] + (list(states_in) if has_init else []) + list(params) + const_arrays + list(prev_out or [])
    in_specs = ([pl.BlockSpec((bb // spg, L, MIX_W), lambda b, c: (b, c, 0))]
                + ([st_spec(s) for s in state_shapes] if has_init else [])
                + [_full_spec(a) for a in params]
                + [_full_spec(a) for a in const_arrays]
                + [pl.BlockSpec(memory_space=pl.ANY)] * n_alias)
    out_shape = ([jax.ShapeDtypeStruct((g, t, N_BRANCH * BRANCH_W), _row_dtype(L))]
                 + [jax.ShapeDtypeStruct(s, F32) for s in state_shapes])
    out_specs = ([pl.BlockSpec((bb // spg, L, N_BRANCH * BRANCH_W), lambda b, c: (b, c, 0))]
                 + [st_spec(s) for s in state_shapes])
    first_alias = len(inputs) - n_alias
    pair = (bb, 2, LANE, LANE)
    scratch = [pltpu.VMEM(pair, F32), pltpu.VMEM(pair, F32), pltpu.VMEM(pair, F32),
               pltpu.VMEM((bb, 2, 1, LANE), F32), pltpu.VMEM((bb, SSD_G * SSD_N, SSD_H * SSD_P), F32),
               pltpu.VMEM((bb, 8 + L, GDN_CONV_W), F32), pltpu.VMEM((bb, 8 + L, SSD_CONV_W), F32),
               pltpu.VMEM((bb, L, MIX_W) if nv < L else (1, 8, LANE), F32)]
    outs = pl.pallas_call(
        functools.partial(_mixer_kernel, L, nv, J, layer, bb, offs, has_init, n_alias),
        grid=(g * spg // bb, nchunk),
        in_specs=in_specs, out_specs=out_specs, out_shape=out_shape, scratch_shapes=scratch,
        input_output_aliases={first_alias + k: 1 + k for k in range(n_alias)},
        compiler_params=pltpu.CompilerParams(dimension_semantics=("parallel", "arbitrary"),
                                             vmem_limit_bytes=VMEM_LIMIT),
        name=f"mixer_L{L}",
    )(*inputs)
    return outs[0], list(outs[1:])


def _ada_kernel(c_ref, w_ref, b_ref, o_ref):
    o_ref[...] = _mm(_silu(c_ref[...]), w_ref[...]) + b_ref[...]


def _ada_call(c_all, ada_w, ada_b):
    rows = c_all.shape[0]
    n = ada_w.shape[-1]
    tn = 1536
    return pl.pallas_call(
        _ada_kernel,
        grid=(DEPTH, n // tn),
        in_specs=[pl.BlockSpec((rows, D_MODEL), lambda l, j: (0, 0)),
                  pl.BlockSpec((None, D_MODEL, tn), lambda l, j: (l, 0, j)),
                  pl.BlockSpec((None, 1, tn), lambda l, j: (l, 0, j))],
        out_specs=pl.BlockSpec((None, rows, tn), lambda l, j: (l, 0, j)),
        out_shape=jax.ShapeDtypeStruct((DEPTH, rows, n), F32),
        compiler_params=pltpu.CompilerParams(dimension_semantics=("arbitrary", "arbitrary"),
                                             vmem_limit_bytes=VMEM_LIMIT),
        name="ada",
    )(c_all, ada_w, ada_b.reshape(DEPTH, 1, n))


PREP_ROWS = MIX_W - SMALL_OFF


def _prep_w_in_kernel(w_ref, s0_ref, s1_ref, s2_ref, o_ref):
    j = pl.program_id(1)
    j_small = SMALL_OFF // PREP_ROWS

    @pl.when(j != j_small)
    def _copy():
        o_ref[...] = w_ref[0].astype(BF16)

    @pl.when(j == j_small)
    def _small():
        o_ref[...] = jnp.zeros(o_ref.shape, BF16)
        o_ref[0:8, :] = s0_ref[0].astype(BF16)
        o_ref[8:16, :] = s1_ref[0].astype(BF16)
        o_ref[16:24, :] = s2_ref[0].astype(BF16)


def _prep_src_row(j):
    r = PREP_ROWS
    return jnp.where(j < 1024 // r, r * j,
                     jnp.where(j < 2816 // r, r * j + 8,
                               jnp.where(j < SMALL_OFF // r, r * j + 16,
                                         jnp.where(j < MIX_W // r, 0, r * j - MIX_W + 3864))))


def _prep_w_in_call(w_in):
    wt = jnp.swapaxes(w_in, 1, 2)
    depth, _, d = wt.shape

    def rows(n, index):
        return pl.BlockSpec((pl.Element(1), pl.Element(n), pl.Element(d)), index)

    return pl.pallas_call(
        _prep_w_in_kernel,
        grid=(depth, PROJ_W // PREP_ROWS),
        in_specs=[rows(PREP_ROWS, lambda l, j: (l, pl.multiple_of(_prep_src_row(j), 8), 0)),
                  rows(8, lambda l, j: (l, 1024, 0)), rows(8, lambda l, j: (l, 2824, 0)),
                  rows(8, lambda l, j: (l, 3856, 0))],
        out_specs=pl.BlockSpec((None, PREP_ROWS, d), lambda l, j: (l, j, 0)),
        out_shape=jax.ShapeDtypeStruct((depth, PROJ_W, d), BF16),
        compiler_params=pltpu.CompilerParams(dimension_semantics=("arbitrary", "arbitrary"),
                                             vmem_limit_bytes=VMEM_LIMIT),
        name="prep_w_in",
    )(wt, wt, wt, wt)


def _rms_mod(x, nw, sc, sh):
    ms = jnp.mean(x * x, axis=-1, keepdims=True)
    return x * lax.rsqrt(ms + EPS) * nw * (1.0 + sc) + sh


def _tok_spec(gblk, width):
    g, r = gblk
    return pl.BlockSpec((g, r, width), lambda i, j: (i, j, 0))


def _mod_spec(gblk, layer, k, mods):
    return pl.BlockSpec((None, gblk[0], mods.shape[2], D_MODEL), lambda i, j: (layer, i, 0, k))


def _mod_rows(m, r):
    spg = m.shape[1]
    if spg == 1:
        return m
    seq = lax.broadcasted_iota(jnp.int32, (1, r, 1), 1) // (r // spg)
    out = m[:, 0:1, :]
    for j in range(1, spg):
        out = jnp.where(seq == j, m[:, j:j + 1, :], out)
    return out


def _layer_spec(shape, layer):
    nd = len(shape)
    return pl.BlockSpec((None,) + tuple(shape[1:]), lambda i, j: (layer,) + (0,) * (nd - 1),
                        pipeline_mode=pl.Buffered(1))


def _dense_params():
    return pltpu.CompilerParams(dimension_semantics=("arbitrary", "arbitrary"), vmem_limit_bytes=VMEM_LIMIT)


def _inproj_kernel(x_ref, nw_ref, sc_ref, sh_ref, w_ref, pm_ref, gate_ref):
    g, r, _ = x_ref.shape
    h = _rms_mod(x_ref[...], nw_ref[...], _mod_rows(sc_ref[...], r), _mod_rows(sh_ref[...], r))
    h = h.astype(BF16).reshape(g * r, D_MODEL)
    pm_ref[...] = _mm(h, w_ref[0:MIX_W, :], _NT).reshape(g, r, MIX_W)
    gates = _sigmoid(_mm(h, w_ref[MIX_W:PROJ_W, :], _NT))
    gate_ref[...] = gates.astype(gate_ref.dtype).reshape(g, r, GATE_W)


def _inproj_call(x, layer, nw, mods, w, gblk):
    bg, t, _ = x.shape
    return pl.pallas_call(
        _inproj_kernel,
        grid=(bg // gblk[0], t // gblk[1]),
        in_specs=[_tok_spec(gblk, D_MODEL), _layer_spec(nw.shape, layer),
                  _mod_spec(gblk, layer, 1, mods), _mod_spec(gblk, layer, 0, mods), _layer_spec(w.shape, layer)],
        out_specs=[_tok_spec(gblk, MIX_W), _tok_spec(gblk, GATE_W)],
        out_shape=[jax.ShapeDtypeStruct((bg, t, MIX_W), F32),
                   jax.ShapeDtypeStruct((bg, t, GATE_W), _row_dtype(gblk[1]))],
        compiler_params=_dense_params(), name="inproj",
    )(x, nw, mods, mods, w)


def _merge_ffn_kernel(final, br_ref, gate_ref, x_ref, gt1_ref, nw_ref, sc_ref, sh_ref, gt2_ref,
                      wb_ref, wo_ref, w1_ref, w2_ref, fw_ref, o_ref):
    g, r, _ = x_ref.shape
    br = br_ref[...].reshape(g * r, N_BRANCH * BRANCH_W)
    merged = None
    for n in range(N_BRANCH):
        up = _mm(br[:, n * BRANCH_W:(n + 1) * BRANCH_W], wb_ref[n])
        gate = gate_ref[:, :, n * D_MODEL:(n + 1) * D_MODEL].astype(F32).reshape(g * r, D_MODEL)
        t = gate * up
        merged = t if merged is None else merged + t
    x1 = x_ref[...] + _mod_rows(gt1_ref[...], r) * _mm(merged, wo_ref[...]).reshape(g, r, D_MODEL)
    h = _rms_mod(x1, nw_ref[...], _mod_rows(sc_ref[...], r), _mod_rows(sh_ref[...], r))
    h = h.astype(BF16).reshape(g * r, D_MODEL)
    a = _mm(h, w1_ref[:, 0:D_FF])
    b = _mm(h, w1_ref[:, D_FF:2 * D_FF])
    x2 = x1 + _mod_rows(gt2_ref[...], r) * _mm(_silu(a) * b, w2_ref[...]).reshape(g, r, D_MODEL)
    if final:
        ms = jnp.mean(x2 * x2, axis=-1, keepdims=True)
        x2 = x2 * lax.rsqrt(ms + EPS) * fw_ref[...]
    o_ref[...] = x2


def _merge_ffn_call(br, gates, x, layer, nw, mods, wb, wo, w1, w2, fw, gblk, final):
    bg, t, _ = x.shape
    return pl.pallas_call(
        functools.partial(_merge_ffn_kernel, final),
        grid=(bg // gblk[0], t // gblk[1]),
        in_specs=[_tok_spec(gblk, N_BRANCH * BRANCH_W), _tok_spec(gblk, GATE_W), _tok_spec(gblk, D_MODEL),
                  _mod_spec(gblk, layer, 2, mods), _layer_spec(nw.shape, layer),
                  _mod_spec(gblk, layer, 4, mods), _mod_spec(gblk, layer, 3, mods), _mod_spec(gblk, layer, 5, mods),
                  _layer_spec(wb.shape, layer), _layer_spec(wo.shape, layer),
                  _layer_spec(w1.shape, layer), _layer_spec(w2.shape, layer),
                  pl.BlockSpec(fw.shape, lambda i, j: (0, 0), pipeline_mode=pl.Buffered(1))],
        out_specs=_tok_spec(gblk, D_MODEL),
        out_shape=jax.ShapeDtypeStruct((bg, t, D_MODEL), F32),
        compiler_params=_dense_params(), name="merge_ffn",
    )(br, gates, x, mods, nw, mods, mods, mods, wb, wo, w1, w2, fw)


def _small_params(gdn_dt_bias, ssd_dt_bias, ml_b_i, ml_b_f, gdn_a_log, ssd_a_log):
    z4 = jnp.zeros((4,), F32)
    bias = jnp.concatenate([z4, gdn_dt_bias, ssd_dt_bias, ml_b_i, ml_b_f, jnp.zeros((LANE - 24,), F32)])
    alog = jnp.concatenate([z4, gdn_a_log, ssd_a_log, jnp.zeros((LANE - 16,), F32)])
    return jnp.concatenate([bias[None], alog[None], jnp.zeros((6, LANE), F32)], axis=0)


def _layer_params(l, gdn_conv_w, gdn_conv_b, gdn_a_log, gdn_dt_bias, gdn_norm_w, hg_lb_logits, hg_norm_w,
                  ssd_conv_w, ssd_conv_b, ssd_a_log, ssd_dt_bias, ssd_d, ssd_norm_w, ml_b_i, ml_b_f, ml_norm_w):
    sp = _small_params(gdn_dt_bias[l], ssd_dt_bias[l], ml_b_i[l], ml_b_f[l], gdn_a_log[l], ssd_a_log[l])
    vec = jnp.stack([jnp.tile(gdn_norm_w[l], N_HEAD), jnp.tile(hg_norm_w[l], N_HEAD), ssd_norm_w[l],
                     jnp.tile(ml_norm_w[l], N_HEAD), jnp.repeat(ssd_d[l], SSD_P)]
                    + [jnp.zeros((256,), F32)] * 3)
    return [sp, gdn_conv_w[l], gdn_conv_b[l][None], ssd_conv_w[l], ssd_conv_b[l][None], vec,
            hg_lb_logits.astype(F32)]


def _trunk(x, mods, state_shapes, states_in, L, nv, gblk, gblk_ffn, bb, W, mixer_params):
    new_states = None
    for l in range(DEPTH):
        pm, gates = _inproj_call(x, l, W['norm1'], mods, W['w_in'], gblk)
        br, new_states = _mixer_call(pm, state_shapes, states_in, new_states, mixer_params[l], L, nv, l, bb)
        x = _merge_ffn_call(br, gates, x, l, W['norm2'], mods, W['w_branch'], W['w_out'],
                            W['ffn_w_in'], W['ffn_w_out'], W['final'], gblk_ffn, final=(l == DEPTH - 1))
    return x, new_states


def kernel(x_prompt, x_sample, c_prompt, c_sample, state_gdn, state_gdn_conv, state_hgrn, state_ssd, state_ssd_conv, state_mlstm_c, state_mlstm_n, state_mlstm_m, ada_w, ada_b, norm1_w, norm2_w, w_in, gdn_conv_w, gdn_conv_b, gdn_a_log, gdn_dt_bias, gdn_norm_w, hg_lb_logits, hg_norm_w, ssd_conv_w, ssd_conv_b, ssd_a_log, ssd_dt_bias, ssd_d, ssd_norm_w, ml_b_i, ml_b_f, ml_norm_w, w_branch, w_out, ffn_w_in, ffn_w_out, final_norm_w):
    bp, tp, _ = x_prompt.shape
    bs, ts, _ = x_sample.shape
    assert tp % CHUNK == 0 and LS % ts == 0 and bs % (LS // ts) == 0

    W = dict(w_in=_prep_w_in_call(w_in), w_branch=w_branch.astype(BF16), w_out=w_out.astype(BF16),
             ffn_w_in=ffn_w_in.astype(BF16), ffn_w_out=ffn_w_out.astype(BF16),
             norm1=norm1_w[:, None, :], norm2=norm2_w[:, None, :], final=final_norm_w[None])
    mixer_params = [_layer_params(l, gdn_conv_w, gdn_conv_b, gdn_a_log, gdn_dt_bias, gdn_norm_w, hg_lb_logits,
                                  hg_norm_w, ssd_conv_w, ssd_conv_b, ssd_a_log, ssd_dt_bias, ssd_d, ssd_norm_w,
                                  ml_b_i, ml_b_f, ml_norm_w) for l in range(DEPTH)]

    mods = _ada_call(jnp.concatenate([c_prompt, c_sample], axis=0), ada_w, ada_b)
    mods_p = mods[:, :bp].reshape(DEPTH, bp, 1, 6 * D_MODEL)
    grp = LS // ts
    mods_s = mods[:, bp:].reshape(DEPTH, bs // grp, grp, 6 * D_MODEL)
    sample_states = [state_gdn, state_gdn_conv, state_hgrn, state_ssd, state_ssd_conv,
                     state_mlstm_c, state_mlstm_n, state_mlstm_m.reshape(DEPTH, bs, 1, N_HEAD)]
    shapes_s = [s.shape for s in sample_states]
    shapes_p = [(DEPTH, bp) + tuple(s[2:]) for s in shapes_s]

    y_p, new_p = _trunk(x_prompt, mods_p, shapes_p, None, CHUNK, CHUNK, (1, 256), (1, 512), 8, W, mixer_params)
    y_s, new_s = _trunk(x_sample.reshape(bs // grp, LS, D_MODEL), mods_s, shapes_s, sample_states, LS, ts,
                        (32, LS), (32, LS), 8, W, mixer_params)
    new_p[7] = new_p[7].reshape(DEPTH, bp, N_HEAD)
    new_s[7] = new_s[7].reshape(DEPTH, bs, N_HEAD)
    return (y_p, y_s.reshape(bs, ts, D_MODEL)) + tuple(new_p) + tuple(new_s)
```

```python
import functools
import math

import numpy as np
import jax
import jax.numpy as jnp
from jax import lax
from jax.experimental import pallas as pl
from jax.experimental.pallas import tpu as pltpu

F32 = jnp.float32
BF16 = jnp.bfloat16

D_MODEL = 1024
DEPTH = 2
N_BRANCH = 4
BRANCH_W = 256
N_HEAD = 4
HEAD_D = 64
SSD_H = 8
SSD_P = 32
SSD_N = 64
SSD_G = 2
CONV_K = 4
CHUNK = 64
LS = 8
FLOW_DELAY = {CHUNK: 2, LS: 3}
D_FF = 2816
EPS = 1e-6
NEG = -1e30
GDN_CONV_W = 768
SSD_CONV_W = 512
SSD_XBC_OFF = 2304
MIX_W = 4096
GATE_W = 4096
PROJ_W = MIX_W + GATE_W
SMALL_OFF = 3840
LANE = 128
VMEM_LIMIT = 56 * 1024 * 1024

ST_BETA, ST_GA, ST_DT, ST_MI, ST_MF = 0, 4, 8, 16, 20

_NN = (((1,), (0,)), ((), ()))
_NT = (((1,), (1,)), ((), ()))
_TN = (((0,), (0,)), ((), ()))


def _mm(a, b, dims=_NN):
    return lax.dot_general(a.astype(BF16), b.astype(BF16), dims, preferred_element_type=F32)


def _split(x, n):
    parts, r = [], x
    for i in range(n):
        p = r.astype(BF16)
        parts.append(p)
        if i < n - 1:
            r = r - p.astype(F32)
    return parts


def _mm01(x, m, n):
    out = None
    for p in _split(x, n):
        t = lax.dot_general(p, m.astype(BF16), _NN, preferred_element_type=F32)
        out = t if out is None else out + t
    return out


def _m01m(m, x, n):
    out = None
    for p in _split(x, n):
        t = lax.dot_general(m, p, _NN, preferred_element_type=F32)
        out = t if out is None else out + t
    return out


def _sigmoid(x):
    return jax.nn.sigmoid(x)


def _silu(x):
    return x * jax.nn.sigmoid(x)


def _softplus(x):
    return jnp.maximum(x, 0.0) + jnp.log(1.0 + jnp.exp(-jnp.abs(x)))


def _logsig(x):
    return jnp.minimum(x, 0.0) - jnp.log(1.0 + jnp.exp(-jnp.abs(x)))


def _row_dtype(rows):
    return BF16 if rows % 16 == 0 else F32


def _expand(base, heads, width):
    e = np.zeros((LANE, heads * width), np.float32)
    for h in range(heads):
        e[base + h, h * width:(h + 1) * width] = 1.0
    return e


def _cat_segments(segs):
    cols, off, pos, seen = [], {}, 0, {}
    for name, m in segs:
        key = m.tobytes() + bytes(str(m.shape), 'ascii')
        if key in seen:
            off[name] = seen[key]
            continue
        w = m.shape[1]
        wp = -(-w // LANE) * LANE
        mp = np.zeros((m.shape[0], wp), np.float32)
        mp[:, :w] = m
        cols.append(mp)
        off[name] = seen[key] = (pos, w)
        pos += wp
    return np.concatenate(cols, axis=1), off


@functools.lru_cache(maxsize=None)
def _mixer_consts(L, nv):
    J = int(round(math.log2(L)))
    assert 1 << J == L
    f = np.float32
    r = np.arange(L)
    tri = (r[None, :] <= r[:, None])
    rev = (r[:, None] < r[None, :]) & (r[None, :] <= nv - 1)
    strict = (r[None, :] < r[:, None])
    eye = np.eye(L, dtype=bool)
    lvl, ab = [], []
    for j in range(J):
        bnd = ((r >> (j + 1)) << (j + 1)) + (1 << j) - 1
        low = ((r >> j) & 1) == 1
        a = low[:, None] & (bnd[:, None] < r[None, :]) & (r[None, :] <= r[:, None])
        b = (~low)[:, None] & (r[:, None] < r[None, :]) & (r[None, :] <= bnd[:, None])
        ab.append(a | b)
        same = (r[:, None] >> (j + 1)) == (r[None, :] >> (j + 1))
        lvl.append(same & low[:, None] & (~low)[None, :])

    def tile(m, n):
        return np.tile(m.astype(f), (1, n))

    pr = np.repeat(np.arange(2), L)
    lr2 = np.tile(r, 2)
    ch = np.arange(LANE)
    ch2 = np.arange(2 * LANE)
    c = dict(
        trirev=np.concatenate([tri, rev], axis=0).astype(f),
        mhg=np.concatenate(ab + [tri, rev], axis=0).astype(f),
        ones_ll=np.ones((L, L), f), tri=tri.astype(f),
        tril2=tile(tri, 2), strict2=tile(strict, 2), tril4=tile(tri, 4),
        eye4=tile(eye, 4), supper16=tile(strict, 16),
        e_dr=np.concatenate([_expand(ST_GA, 4, L), _expand(ST_DT, SSD_H, L), _expand(ST_MF, 4, L)], axis=1),
        hmaskp=np.stack([tile(eye, 2)] + [tile(m, 2) for m in lvl]),
        eyep=np.eye(2 * L, dtype=f),
        lvlp=np.stack([(pr[:, None] == pr[None, :]) & m[lr2[:, None], lr2[None, :]] for m in lvl]).astype(f),
        stkp=(pr[:, None] == ch[None, :] // HEAD_D).astype(f),
        bdp=(ch[:, None] // HEAD_D == ch[None, :] // HEAD_D).astype(f),
        pm2=np.stack([ch < HEAD_D, ch >= HEAD_D]).astype(f),
        stk4s=(np.repeat(np.arange(4), L)[:, None] == ch[None, :] // SSD_P).astype(f),
        gbd=(ch[:, None] // SSD_N == ch2[None, :] // LANE).astype(f),
        rowvalid=(r[:, None] <= nv - 1).astype(f) * np.ones((1, LANE), f),
        bd64=(ch2[:, None] // 64 == ch2[None, :] // 64).astype(f),
        bd128=(ch2[:, None] // 128 == ch2[None, :] // 128).astype(f),
    )
    c['e_act'], off_act = _cat_segments([('b_r', _expand(ST_BETA, 4, L)), ('b_c', _expand(ST_BETA, 4, HEAD_D)),
                                         ('dt_c', _expand(ST_DT, SSD_H, SSD_P)),
                                         ('i_r', _expand(ST_MI, 4, L)), ('i_c', _expand(ST_MI, 4, HEAD_D))])
    c['e_cs'], off_cs = _cat_segments([('g_c', _expand(ST_GA, 4, HEAD_D)), ('s_c', _expand(ST_DT, SSD_H, SSD_P)),
                                       ('m_r', _expand(ST_MF, 4, L)), ('m_c', _expand(ST_MF, 4, HEAD_D))])
    c['e_rev'], off_rev = _cat_segments([('g_c', _expand(ST_GA, 4, HEAD_D)), ('s_c', _expand(ST_DT, SSD_H, SSD_P)),
                                         ('m_c', _expand(ST_MF, 4, HEAD_D))])
    return c, dict(act=off_act, cs=off_cs, rev=off_rev), J


_CONST_ORDER = ('trirev', 'mhg', 'ones_ll', 'tri', 'tril2', 'strict2', 'tril4', 'eye4', 'supper16', 'e_dr', 'hmaskp',
                'eyep', 'lvlp', 'stkp', 'bdp', 'pm2', 'stk4s', 'gbd', 'rowvalid', 'bd64', 'bd128', 'e_act', 'e_cs',
                'e_rev')
_BF16_CONSTS = ('trirev', 'mhg', 'ones_ll', 'tri', 'supper16', 'e_dr', 'lvlp', 'stkp', 'stk4s', 'bd64', 'bd128',
                'e_act', 'e_cs', 'e_rev')


def _conv_silu(ext, w, b, L):
    y = b
    for i in range(CONV_K):
        y = y + ext[5 + i:5 + i + L, :] * w[i:i + 1, :]
    return _silu(y)


def _run_interleaved(tasks):
    tasks = list(tasks)
    while tasks:
        alive = []
        for t in tasks:
            try:
                next(t)
                alive.append(t)
            except StopIteration:
                pass
        tasks = alive


def _mixer_chunk(pm, ext_g, ext_s, st, P, K, offs, L, nv, J, layer, out):
    lane = lax.broadcasted_iota(jnp.int32, (1, LANE), 1)
    rowvalid = K['rowvalid'][:, 0:1]
    tril2 = K['tril2'] > 0.0
    tril4 = K['tril4'] > 0.0
    stkp, bdp = K['stkp'], K['bdp']
    upper_c = lane >= HEAD_D
    upper_r = lax.broadcasted_iota(jnp.int32, (1, 2 * L), 1) >= L

    def seg(x, which, name):
        o, w = offs[which][name]
        return x[:, o:o + w]

    def pair_c(x, p):
        return x[:, p * LANE:(p + 1) * LANE]

    def pair_r(x, p):
        return x[:, p * 2 * L:(p + 1) * 2 * L]

    def stack(x):
        xb = x.astype(BF16)
        return jnp.concatenate([xb, xb], axis=0) * stkp

    z = pm[:, SMALL_OFF:SMALL_OFF + LANE] + P['sp'][0:1, :]
    act = jnp.where(lane < ST_GA, _sigmoid(z),
                    jnp.where(lane < ST_MI, _softplus(z), jnp.where(lane < ST_MF, z, _logsig(z))))
    neg_a = -jnp.exp(P['sp'][1:2, :])
    dec_in = jnp.where((lane >= ST_GA) & (lane < ST_MI), neg_a * act,
                       jnp.where((lane >= ST_MF) & (lane < ST_MF + 4), act, 0.0))
    cr = _m01m(K['trirev'], dec_in, 2)
    yield
    cs, rev = cr[:L], cr[L:]
    ea = _mm01(act, K['e_act'], 2)
    ec = _mm01(cs, K['e_cs'], 2)
    er = _mm01(rev, K['e_rev'], 2)
    yield
    m_r = seg(ec, 'cs', 'm_r')
    g_c, s_c, m_c = seg(ec, 'cs', 'g_c'), seg(ec, 'cs', 's_c'), seg(ec, 'cs', 'm_c')
    b_r, b_c, dt_c = seg(ea, 'act', 'b_r'), seg(ea, 'act', 'b_c'), seg(ea, 'act', 'dt_c')
    i_r, i_c = seg(ea, 'act', 'i_r'), seg(ea, 'act', 'i_c')
    rg_c, rs_c, rm_c = seg(er, 'rev', 'g_c'), seg(er, 'rev', 's_c'), seg(er, 'rev', 'm_c')
    ym = _mm(dec_in, K['e_dr']).astype(BF16) * K['supper16']
    yield
    d_all = lax.dot_general(K['tri'], ym, _NN, preferred_element_type=F32)
    d_g, d_s, d_m = d_all[:, 0:4 * L], d_all[:, 4 * L:12 * L], d_all[:, 12 * L:16 * L]
    row_i = _m01m(K['ones_ll'], i_r * K['eye4'], 2)

    res = dict(o_gdn=[None, None], gdn=[None, None], o_hg=[None, None], hg=[None, None],
               o_ml=[None, None], mlc=[None, None], mln=[None, None], mlm=[None] * N_HEAD)

    qkv = _conv_silu(ext_g, P['cwg'], P['cbg'], L)
    q, k, v = qkv[:, 0:256], qkv[:, 256:512], qkv[:, 512:768]
    ss = _mm(jnp.concatenate([q * q, k * k], axis=0), K['bd64'])
    yield
    q = q * (lax.rsqrt(ss[:L] + EPS) * (HEAD_D ** -0.5))
    k = k * lax.rsqrt(ss[L:] + EPS)

    def gdn_task(p):
        qt, kt, vt = pair_c(q, p), pair_c(k, p), pair_c(v, p)
        beta_c, gc = pair_c(b_c, p), pair_c(g_c, p)
        eg = jnp.exp(gc)
        dec = jnp.where(tril2, jnp.exp(jnp.where(tril2, pair_r(d_g, p), 0.0)), 0.0)
        kkqk = _mm(jnp.concatenate([kt, qt], axis=0), stack(kt), _NT)
        yield
        n = (kkqk[:L] * dec * pair_r(b_r, p) * K['strict2']).astype(BF16)
        nbd = jnp.concatenate([n, n], axis=0)
        t = K['eyep'] - (nbd * K['lvlp'][0]).astype(F32)
        for j in range(1, J):
            x = _mm(t, nbd * K['lvlp'][j])
            yield
            t = t - _mm(x, t)
            yield
        sv = _mm(t, stack(vt * beta_c))
        sk = _mm(t, stack(kt * (beta_c * eg)))
        yield
        s_p = st['gdn'][p]
        r = _mm(jnp.concatenate([sk[:L] + sk[L:], qt * eg], axis=0), s_p)
        yield
        u = sv[:L] + sv[L:] - r[:L]
        res['o_gdn'][p] = r[L:] + _mm(kkqk[L:] * dec, stack(u))
        kw = kt * (jnp.exp(pair_c(rg_c, p)) * rowvalid)
        res['gdn'][p] = s_p * jnp.exp(gc[nv - 1:nv, :]) + bdp * _mm(kw, u, _TN)
        yield

    lg = P['lg']
    mx = lg[0:1, :]
    for i in range(1, DEPTH):
        mx = jnp.maximum(mx, lg[i:i + 1, :])
    ex = [jnp.exp(lg[i:i + 1, :] - mx) for i in range(DEPTH)]
    tot = ex[0]
    for i in range(1, DEPTH):
        tot = tot + ex[i]
    sm = [e / tot for e in ex]
    cum = sm[0]
    for i in range(1, layer + 1):
        cum = cum + sm[i]
    lb = cum - sm[0]
    lb_pos = lb > 0
    log_lb = jnp.log(jnp.where(lb_pos, lb, 1.0))
    hq = _silu(pm[:, 1024:1280])
    fz = pm[:, 1280:1536]
    hv = pm[:, 1536:1792]
    ls = _logsig(fz)
    t2 = jnp.log1p(-lb) + ls
    la = jnp.maximum(log_lb, t2) + jnp.log1p(jnp.exp(-jnp.abs(log_lb - t2)))
    logf = jnp.where(lb_pos, la, ls)
    kg = (1.0 - lb) * _sigmoid(-fz)
    ey = jnp.exp(_m01m(K['mhg'], logf, 2))
    yield
    eg_h = ey[J * L:(J + 1) * L]
    eyb = ey[0:J * L].astype(BF16)
    hqb = hq.astype(BF16)
    qe = hq * eg_h
    kwr = kg * (ey[(J + 1) * L:(J + 2) * L] * rowvalid)

    def hg_task(p):
        q_p, k_st = pair_c(hqb, p), stack(pair_c(kg, p))
        a_p = K['hmaskp'][0] * _mm(q_p, k_st, _NT)
        yield
        for j in range(J):
            e_j = pair_c(eyb[j * L:(j + 1) * L], p)
            a_p = a_p + K['hmaskp'][j + 1] * _mm(q_p * e_j, k_st * jnp.concatenate([e_j, e_j], axis=0), _NT)
            yield
        st_p = st['hg'][p]
        res['o_hg'][p] = _mm(pair_c(qe, p), st_p, _NT) + _mm(a_p, stack(pair_c(hv, p)))
        yield
        res['hg'][p] = (st_p * pair_c(eg_h[nv - 1:nv, :], p)
                        + bdp * _mm(pair_c(hv, p), pair_c(kwr, p), _TN))
        yield

    xbc = _conv_silu(ext_s, P['cws'], P['cbs'], L)
    xs, bs, cc = xbc[:, 0:256], xbc[:, 256:384], xbc[:, 384:512]
    vs = xs * dt_c
    bsb = bs.astype(BF16)
    bs4 = jnp.concatenate([bsb] * 4, axis=0)

    def ssd_task():
        o_intra = []
        for g in range(SSD_G):
            cbw = _mm(cc * K['pm2'][g:g + 1, :], bs4, _NT)
            dec = jnp.where(tril4, jnp.exp(jnp.where(tril4, d_s[:, g * 4 * L:(g + 1) * 4 * L], 0.0)), 0.0)
            vb = pair_c(vs, g).astype(BF16)
            o_intra.append(_mm(cbw * dec, jnp.concatenate([vb] * 4, axis=0) * K['stk4s']))
            yield
        s_all = st['ssd']
        res['o_ssd'] = jnp.exp(s_c) * _mm(cc, s_all) + jnp.concatenate(o_intra, axis=1)
        res['ssd'] = (s_all * jnp.exp(s_c[nv - 1:nv, :])
                      + K['gbd'] * _mm(bs * rowvalid, vs * jnp.exp(rs_c), _TN))
        yield

    mq = pm[:, 2816:3072] * (HEAD_D ** -0.5)
    mk = pm[:, 3072:3328]
    mv = pm[:, 3328:3584]

    def ml_task(p, delay):
        for _ in range(delay):
            yield
        qt, kt, vt = pair_c(mq, p), pair_c(mk, p), pair_c(mv, p)
        bm_r, bm_c = pair_r(m_r, p), pair_c(m_c, p)
        mp0 = st['mlm'][:, 2 * p:2 * p + 1]
        mp1 = st['mlm'][:, 2 * p + 1:2 * p + 2]
        logw = jnp.where(tril2, pair_r(d_m, p) + pair_r(row_i, p), NEG)
        mx0 = jnp.max(jnp.where(upper_r, NEG, logw), axis=-1, keepdims=True)
        mx1 = jnp.max(jnp.where(upper_r, logw, NEG), axis=-1, keepdims=True)
        l0_r = bm_r + jnp.where(upper_r, mp1, mp0)
        l0_c = bm_c + jnp.where(upper_c, mp1, mp0)
        mt_r = jnp.maximum(l0_r, jnp.where(upper_r, mx1, mx0))
        mt_c = jnp.maximum(l0_c, jnp.where(upper_c, mx1, mx0))
        w = jnp.where(tril2, jnp.exp(logw - mt_r), 0.0)
        w0 = jnp.exp(l0_c - mt_c)
        qk = _mm(qt, stack(kt), _NT) * w
        yield
        c_p = st['mlc'][p]
        n_p = st['mln'][p]
        num = w0 * _mm(qt, c_p) + _mm(qk, stack(vt))
        den = w0 * _mm01(qt * n_p, bdp, 2) + _mm01(qk, stkp, 2)
        res['o_ml'][p] = num * (1.0 / jnp.maximum(jnp.abs(den), jnp.exp(-mt_c)))
        yield
        m_l = mt_c[nv - 1:nv, :]
        wl0 = jnp.exp(bm_c[nv - 1:nv, :] + jnp.where(upper_c, mp1, mp0) - m_l)
        kwl = kt * (jnp.exp(pair_c(rm_c, p) + pair_c(i_c, p) - m_l) * rowvalid)
        res['mlc'][p] = c_p * wl0 + bdp * _mm(kwl, vt, _TN)
        res['mln'][p] = n_p * wl0 + jnp.sum(kwl, axis=0, keepdims=True)
        res['mlm'][2 * p] = m_l[:, 0:1]
        res['mlm'][2 * p + 1] = m_l[:, HEAD_D:HEAD_D + 1]
        yield

    def finish():
        o_gdn = jnp.concatenate(res['o_gdn'], axis=1)
        o_hg = jnp.concatenate(res['o_hg'], axis=1)
        hh = jnp.concatenate(res['o_ml'], axis=1)
        ms = _mm(jnp.concatenate([o_gdn * o_gdn, o_hg * o_hg, hh * hh], axis=0), K['bd64']) * (1.0 / HEAD_D)
        yield
        out_a = o_gdn * lax.rsqrt(ms[0:L] + EPS) * P['vec'][0:1, :] * _silu(pm[:, 768:1024])
        out_b = o_hg * lax.rsqrt(ms[L:2 * L] + EPS) * P['vec'][1:2, :] * _silu(pm[:, 1792:2048])
        out_d = hh * lax.rsqrt(ms[2 * L:3 * L] + EPS) * P['vec'][3:4, :] * _sigmoid(pm[:, 3584:3840])
        ys = (res['o_ssd'] + P['vec'][4:5, :] * xs) * _silu(pm[:, 2048:2304])
        out_c = ys * lax.rsqrt(_mm(ys * ys, K['bd128']) * (1.0 / (2 * HEAD_D)) + EPS) * P['vec'][2:3, :]
        branches = jnp.concatenate([out_a, out_b, out_c, out_d], axis=1)
        new = dict(gdn=res['gdn'], hg=res['hg'], ssd=res['ssd'], mlc=res['mlc'], mln=res['mln'], mlm=res['mlm'])
        out['result'] = (branches, new)
        yield

    out['tasks'] = ([gdn_task(p) for p in range(2)] + [hg_task(p) for p in range(2)]
                    + [ssd_task()] + [ml_task(p, 3 + 2 * p) for p in range(2)])
    out['finish'] = finish


def _mixer_kernel(L, nv, J, layer, bb, offs, has_init, n_alias, *refs):
    it = iter(refs)
    pm_ref = next(it)
    if has_init:
        gdn0, cg0, hg0, ssd0, cs0, mc0, mn0, mm0 = (next(it) for _ in range(8))
    sp_ref, cwg_ref, cbg_ref, cws_ref, cbs_ref, vec_ref, lg_ref = (next(it) for _ in range(7))
    kref = {name: next(it) for name in _CONST_ORDER}
    for _ in range(n_alias):
        next(it)
    br_ref = next(it)
    gdn1, cg1, hg1, ssd1, cs1, mc1, mn1, mm1 = (next(it) for _ in range(8))
    sg, sh, sc, sn, ss, extg, exts, pmpad = (next(it) for _ in range(8))

    c = pl.program_id(1)
    nc = pl.num_programs(1)
    n_sub = SSD_H // SSD_G

    @pl.when(c == 0)
    def _init():
        sg[...] = jnp.zeros_like(sg)
        sh[...] = jnp.zeros_like(sh)
        sc[...] = jnp.zeros_like(sc)
        ss[...] = jnp.zeros_like(ss)
        extg[:, 0:8, :] = jnp.zeros((bb, 8, GDN_CONV_W), F32)
        exts[:, 0:8, :] = jnp.zeros((bb, 8, SSD_CONV_W), F32)
        if not has_init:
            sn[...] = jnp.zeros_like(sn)
            mm1[...] = jnp.zeros_like(mm1)
            return
        for s in range(bb):
            for h in range(N_HEAD):
                p, lo = h // 2, (h % 2) * HEAD_D
                sg[s, p, lo:lo + HEAD_D, lo:lo + HEAD_D] = gdn0[s, h]
                sh[s, p, lo:lo + HEAD_D, lo:lo + HEAD_D] = hg0[s, h].T
                sc[s, p, lo:lo + HEAD_D, lo:lo + HEAD_D] = mc0[s, h]
                sn[s, p, :, lo:lo + HEAD_D] = mn0[s, h:h + 1, :]
            for h in range(SSD_H):
                g = h // n_sub
                ss[s, g * SSD_N:(g + 1) * SSD_N, h * SSD_P:(h + 1) * SSD_P] = ssd0[s, h]
        mm1[...] = mm0[...]
        extg[:, 5:8, :] = cg0[...]
        exts[:, 5:8, :] = cs0[...]

    P = dict(sp=sp_ref[...], cwg=cwg_ref[...], cbg=cbg_ref[...], cws=cws_ref[...], cbs=cbs_ref[...],
             vec=vec_ref[...], lg=lg_ref[...])
    K = {name: r[...] for name, r in kref.items()}
    stage1, outs = [], []
    spg = L // nv
    for s in range(bb):
        grp, off = s // spg, (s % spg) * nv
        if nv == L:
            pm = pm_ref[s]
        else:
            pmpad[s, 0:nv, :] = pm_ref[grp, off:off + nv, :]
            pmpad[s, nv:L, :] = jnp.zeros((L - nv, MIX_W), F32)
            pm = pmpad[s]
        extg[s, 8:8 + L, :] = pm[:, 0:GDN_CONV_W]
        exts[s, 8:8 + L, :] = pm[:, SSD_XBC_OFF:SSD_XBC_OFF + SSD_CONV_W]
        st = dict(gdn=sg[s], hg=sh[s], ssd=ss[s], mlc=sc[s], mln=sn[s], mlm=mm1[s])
        outs.append({})
        stage1.append(_mixer_chunk(pm, extg.at[s], exts.at[s], st, P, K, offs, L, nv, J, layer, outs[s]))

    def flow(s):
        for _ in range(FLOW_DELAY[L] * s):
            yield
        yield from stage1[s]
        tasks = list(outs[s]['tasks'])
        while tasks:
            alive = []
            for t in tasks:
                try:
                    next(t)
                    alive.append(t)
                except StopIteration:
                    pass
            tasks = alive
            yield
        yield from outs[s]['finish']()

    _run_interleaved([flow(s) for s in range(bb)])
    for s in range(bb):
        branches, new = outs[s]['result']
        grp, off = s // spg, (s % spg) * nv
        br_ref[grp, off:off + nv, :] = branches[0:nv].astype(br_ref.dtype)
        for p in range(2):
            sg[s, p] = new['gdn'][p]
            sh[s, p] = new['hg'][p]
            sc[s, p] = new['mlc'][p]
            sn[s, p] = new['mln'][p]
        for h in range(N_HEAD):
            mm1[s, :, h:h + 1] = new['mlm'][h]
        ss[s] = new['ssd']
        tail_g = extg[s, 8 + nv - 3:8 + nv, :]
        tail_s = exts[s, 8 + nv - 3:8 + nv, :]
        extg[s, 5:8, :] = tail_g
        exts[s, 5:8, :] = tail_s
        cg1[s] = tail_g
        cs1[s] = tail_s

    @pl.when(c == nc - 1)
    def _fin():
        for s in range(bb):
            for h in range(N_HEAD):
                p, lo = h // 2, (h % 2) * HEAD_D
                gdn1[s, h] = sg[s, p, lo:lo + HEAD_D, lo:lo + HEAD_D]
                hg1[s, h] = sh[s, p, lo:lo + HEAD_D, lo:lo + HEAD_D].T
                mc1[s, h] = sc[s, p, lo:lo + HEAD_D, lo:lo + HEAD_D]
                mn1[s, h:h + 1, :] = sn[s, p, :, lo:lo + HEAD_D]
            for h in range(SSD_H):
                g = h // n_sub
                ssd1[s, h] = ss[s, g * SSD_N:(g + 1) * SSD_N, h * SSD_P:(h + 1) * SSD_P]


def _full_spec(a):
    nd = a.ndim
    return pl.BlockSpec(a.shape, lambda b, c, _nd=nd: (0,) * _nd)


def _mixer_call(proj, state_shapes, states_in, prev_out, params, L, nv, layer, bb):
    g, t, _ = proj.shape
    spg = L // nv if nv < L else 1
    nchunk = t // L
    assert (g * spg) % bb == 0 and bb % spg == 0 and t % L == 0
    consts, offs, J = _mixer_consts(L, nv)
    const_arrays = [jnp.asarray(consts[n], BF16 if n in _BF16_CONSTS else F32) for n in _CONST_ORDER]

    def st_spec(shape):
        nd = len(shape)
        return pl.BlockSpec((None, bb) + tuple(shape[2:]), lambda b, c, _nd=nd: (layer, b) + (0,) * (_nd - 2))

    has_init = states_in is not None
    n_alias = 0 if prev_out is None else len(prev_out)
    inputs = [proj] + (list(states_in) if has_init else []) + list(params) + const_arrays + list(prev_out or [])
    in_specs = ([pl.BlockSpec((bb // spg, L, MIX_W), lambda b, c: (b, c, 0))]
                + ([st_spec(s) for s in state_shapes] if has_init else [])
                + [_full_spec(a) for a in params]
                + [_full_spec(a) for a in const_arrays]
                + [pl.BlockSpec(memory_space=pl.ANY)] * n_alias)
    out_shape = ([jax.ShapeDtypeStruct((g, t, N_BRANCH * BRANCH_W), _row_dtype(L))]
                 + [jax.ShapeDtypeStruct(s, F32) for s in state_shapes])
    out_specs = ([pl.BlockSpec((bb // spg, L, N_BRANCH * BRANCH_W), lambda b, c: (b, c, 0))]
                 + [st_spec(s) for s in state_shapes])
    first_alias = len(inputs) - n_alias
    pair = (bb, 2, LANE, LANE)
    scratch = [pltpu.VMEM(pair, F32), pltpu.VMEM(pair, F32), pltpu.VMEM(pair, F32),
               pltpu.VMEM((bb, 2, 1, LANE), F32), pltpu.VMEM((bb, SSD_G * SSD_N, SSD_H * SSD_P), F32),
               pltpu.VMEM((bb, 8 + L, GDN_CONV_W), F32), pltpu.VMEM((bb, 8 + L, SSD_CONV_W), F32),
               pltpu.VMEM((bb, L, MIX_W) if nv < L else (1, 8, LANE), F32)]
    outs = pl.pallas_call(
        functools.partial(_mixer_kernel, L, nv, J, layer, bb, offs, has_init, n_alias),
        grid=(g * spg // bb, nchunk),
        in_specs=in_specs, out_specs=out_specs, out_shape=out_shape, scratch_shapes=scratch,
        input_output_aliases={first_alias + k: 1 + k for k in range(n_alias)},
        compiler_params=pltpu.CompilerParams(dimension_semantics=("parallel", "arbitrary"),
                                             vmem_limit_bytes=VMEM_LIMIT),
        name=f"mixer_L{L}",
    )(*inputs)
    return outs[0], list(outs[1:])


def _ada_kernel(c_ref, w_ref, b_ref, o_ref):
    o_ref[...] = _mm(_silu(c_ref[...]), w_ref[...]) + b_ref[...]


def _ada_call(c_all, ada_w, ada_b):
    rows = c_all.shape[0]
    n = ada_w.shape[-1]
    tn = 1536
    return pl.pallas_call(
        _ada_kernel,
        grid=(DEPTH, n // tn),
        in_specs=[pl.BlockSpec((rows, D_MODEL), lambda l, j: (0, 0)),
                  pl.BlockSpec((None, D_MODEL, tn), lambda l, j: (l, 0, j)),
                  pl.BlockSpec((None, 1, tn), lambda l, j: (l, 0, j))],
        out_specs=pl.BlockSpec((None, rows, tn), lambda l, j: (l, 0, j)),
        out_shape=jax.ShapeDtypeStruct((DEPTH, rows, n), F32),
        compiler_params=pltpu.CompilerParams(dimension_semantics=("arbitrary", "arbitrary"),
                                             vmem_limit_bytes=VMEM_LIMIT),
        name="ada",
    )(c_all, ada_w, ada_b.reshape(DEPTH, 1, n))


PREP_ROWS = MIX_W - SMALL_OFF


def _prep_w_in_kernel(w_ref, s0_ref, s1_ref, s2_ref, o_ref):
    j = pl.program_id(1)
    j_small = SMALL_OFF // PREP_ROWS

    @pl.when(j != j_small)
    def _copy():
        o_ref[...] = w_ref[0].astype(BF16)

    @pl.when(j == j_small)
    def _small():
        o_ref[...] = jnp.zeros(o_ref.shape, BF16)
        o_ref[0:8, :] = s0_ref[0].astype(BF16)
        o_ref[8:16, :] = s1_ref[0].astype(BF16)
        o_ref[16:24, :] = s2_ref[0].astype(BF16)


def _prep_src_row(j):
    r = PREP_ROWS
    return jnp.where(j < 1024 // r, r * j,
                     jnp.where(j < 2816 // r, r * j + 8,
                               jnp.where(j < SMALL_OFF // r, r * j + 16,
                                         jnp.where(j < MIX_W // r, 0, r * j - MIX_W + 3864))))


def _prep_w_in_call(w_in):
    wt = jnp.swapaxes(w_in, 1, 2)
    depth, _, d = wt.shape

    def rows(n, index):
        return pl.BlockSpec((pl.Element(1), pl.Element(n), pl.Element(d)), index)

    return pl.pallas_call(
        _prep_w_in_kernel,
        grid=(depth, PROJ_W // PREP_ROWS),
        in_specs=[rows(PREP_ROWS, lambda l, j: (l, pl.multiple_of(_prep_src_row(j), 8), 0)),
                  rows(8, lambda l, j: (l, 1024, 0)), rows(8, lambda l, j: (l, 2824, 0)),
                  rows(8, lambda l, j: (l, 3856, 0))],
        out_specs=pl.BlockSpec((None, PREP_ROWS, d), lambda l, j: (l, j, 0)),
        out_shape=jax.ShapeDtypeStruct((depth, PROJ_W, d), BF16),
        compiler_params=pltpu.CompilerParams(dimension_semantics=("arbitrary", "arbitrary"),
                                             vmem_limit_bytes=VMEM_LIMIT),
        name="prep_w_in",
    )(wt, wt, wt, wt)


def _rms_mod(x, nw, sc, sh):
    ms = jnp.mean(x * x, axis=-1, keepdims=True)
    return x * lax.rsqrt(ms + EPS) * nw * (1.0 + sc) + sh


def _tok_spec(gblk, width):
    g, r = gblk
    return pl.BlockSpec((g, r, width), lambda i, j: (i, j, 0))


def _mod_spec(gblk, layer, k, mods):
    return pl.BlockSpec((None, gblk[0], mods.shape[2], D_MODEL), lambda i, j: (layer, i, 0, k))


def _mod_rows(m, r):
    spg = m.shape[1]
    if spg == 1:
        return m
    seq = lax.broadcasted_iota(jnp.int32, (1, r, 1), 1) // (r // spg)
    out = m[:, 0:1, :]
    for j in range(1, spg):
        out = jnp.where(seq == j, m[:, j:j + 1, :], out)
    return out


def _layer_spec(shape, layer):
    nd = len(shape)
    return pl.BlockSpec((None,) + tuple(shape[1:]), lambda i, j: (layer,) + (0,) * (nd - 1),
                        pipeline_mode=pl.Buffered(1))


def _dense_params():
    return pltpu.CompilerParams(dimension_semantics=("arbitrary", "arbitrary"), vmem_limit_bytes=VMEM_LIMIT)


def _inproj_kernel(x_ref, nw_ref, sc_ref, sh_ref, w_ref, pm_ref, gate_ref):
    g, r, _ = x_ref.shape
    h = _rms_mod(x_ref[...], nw_ref[...], _mod_rows(sc_ref[...], r), _mod_rows(sh_ref[...], r))
    h = h.astype(BF16).reshape(g * r, D_MODEL)
    pm_ref[...] = _mm(h, w_ref[0:MIX_W, :], _NT).reshape(g, r, MIX_W)
    gates = _sigmoid(_mm(h, w_ref[MIX_W:PROJ_W, :], _NT))
    gate_ref[...] = gates.astype(gate_ref.dtype).reshape(g, r, GATE_W)


def _inproj_call(x, layer, nw, mods, w, gblk):
    bg, t, _ = x.shape
    return pl.pallas_call(
        _inproj_kernel,
        grid=(bg // gblk[0], t // gblk[1]),
        in_specs=[_tok_spec(gblk, D_MODEL), _layer_spec(nw.shape, layer),
                  _mod_spec(gblk, layer, 1, mods), _mod_spec(gblk, layer, 0, mods), _layer_spec(w.shape, layer)],
        out_specs=[_tok_spec(gblk, MIX_W), _tok_spec(gblk, GATE_W)],
        out_shape=[jax.ShapeDtypeStruct((bg, t, MIX_W), F32),
                   jax.ShapeDtypeStruct((bg, t, GATE_W), _row_dtype(gblk[1]))],
        compiler_params=_dense_params(), name="inproj",
    )(x, nw, mods, mods, w)


def _merge_ffn_kernel(final, br_ref, gate_ref, x_ref, gt1_ref, nw_ref, sc_ref, sh_ref, gt2_ref,
                      wb_ref, wo_ref, w1_ref, w2_ref, fw_ref, o_ref):
    g, r, _ = x_ref.shape
    br = br_ref[...].reshape(g * r, N_BRANCH * BRANCH_W)
    merged = None
    for n in range(N_BRANCH):
        up = _mm(br[:, n * BRANCH_W:(n + 1) * BRANCH_W], wb_ref[n])
        gate = gate_ref[:, :, n * D_MODEL:(n + 1) * D_MODEL].astype(F32).reshape(g * r, D_MODEL)
        t = gate * up
        merged = t if merged is None else merged + t
    x1 = x_ref[...] + _mod_rows(gt1_ref[...], r) * _mm(merged, wo_ref[...]).reshape(g, r, D_MODEL)
    h = _rms_mod(x1, nw_ref[...], _mod_rows(sc_ref[...], r), _mod_rows(sh_ref[...], r))
    h = h.astype(BF16).reshape(g * r, D_MODEL)
    a = _mm(h, w1_ref[:, 0:D_FF])
    b = _mm(h, w1_ref[:, D_FF:2 * D_FF])
    x2 = x1 + _mod_rows(gt2_ref[...], r) * _mm(_silu(a) * b, w2_ref[...]).reshape(g, r, D_MODEL)
    if final:
        ms = jnp.mean(x2 * x2, axis=-1, keepdims=True)
        x2 = x2 * lax.rsqrt(ms + EPS) * fw_ref[...]
    o_ref[...] = x2


def _merge_ffn_call(br, gates, x, layer, nw, mods, wb, wo, w1, w2, fw, gblk, final):
    bg, t, _ = x.shape
    return pl.pallas_call(
        functools.partial(_merge_ffn_kernel, final),
        grid=(bg // gblk[0], t // gblk[1]),
        in_specs=[_tok_spec(gblk, N_BRANCH * BRANCH_W), _tok_spec(gblk, GATE_W), _tok_spec(gblk, D_MODEL),
                  _mod_spec(gblk, layer, 2, mods), _layer_spec(nw.shape, layer),
                  _mod_spec(gblk, layer, 4, mods), _mod_spec(gblk, layer, 3, mods), _mod_spec(gblk, layer, 5, mods),
                  _layer_spec(wb.shape, layer), _layer_spec(wo.shape, layer),
                  _layer_spec(w1.shape, layer), _layer_spec(w2.shape, layer),
                  pl.BlockSpec(fw.shape, lambda i, j: (0, 0), pipeline_mode=pl.Buffered(1))],
        out_specs=_tok_spec(gblk, D_MODEL),
        out_shape=jax.ShapeDtypeStruct((bg, t, D_MODEL), F32),
        compiler_params=_dense_params(), name="merge_ffn",
    )(br, gates, x, mods, nw, mods, mods, mods, wb, wo, w1, w2, fw)


def _small_params(gdn_dt_bias, ssd_dt_bias, ml_b_i, ml_b_f, gdn_a_log, ssd_a_log):
    z4 = jnp.zeros((4,), F32)
    bias = jnp.concatenate([z4, gdn_dt_bias, ssd_dt_bias, ml_b_i, ml_b_f, jnp.zeros((LANE - 24,), F32)])
    alog = jnp.concatenate([z4, gdn_a_log, ssd_a_log, jnp.zeros((LANE - 16,), F32)])
    return jnp.concatenate([bias[None], alog[None], jnp.zeros((6, LANE), F32)], axis=0)


def _layer_params(l, gdn_conv_w, gdn_conv_b, gdn_a_log, gdn_dt_bias, gdn_norm_w, hg_lb_logits, hg_norm_w,
                  ssd_conv_w, ssd_conv_b, ssd_a_log, ssd_dt_bias, ssd_d, ssd_norm_w, ml_b_i, ml_b_f, ml_norm_w):
    sp = _small_params(gdn_dt_bias[l], ssd_dt_bias[l], ml_b_i[l], ml_b_f[l], gdn_a_log[l], ssd_a_log[l])
    vec = jnp.stack([jnp.tile(gdn_norm_w[l], N_HEAD), jnp.tile(hg_norm_w[l], N_HEAD), ssd_norm_w[l],
                     jnp.tile(ml_norm_w[l], N_HEAD), jnp.repeat(ssd_d[l], SSD_P)]
                    + [jnp.zeros((256,), F32)] * 3)
    return [sp, gdn_conv_w[l], gdn_conv_b[l][None], ssd_conv_w[l], ssd_conv_b[l][None], vec,
            hg_lb_logits.astype(F32)]


def _trunk(x, mods, state_shapes, states_in, L, nv, gblk, gblk_ffn, bb, W, mixer_params):
    new_states = None
    for l in range(DEPTH):
        pm, gates = _inproj_call(x, l, W['norm1'], mods, W['w_in'], gblk)
        br, new_states = _mixer_call(pm, state_shapes, states_in, new_states, mixer_params[l], L, nv, l, bb)
        x = _merge_ffn_call(br, gates, x, l, W['norm2'], mods, W['w_branch'], W['w_out'],
                            W['ffn_w_in'], W['ffn_w_out'], W['final'], gblk_ffn, final=(l == DEPTH - 1))
    return x, new_states


def kernel(x_prompt, x_sample, c_prompt, c_sample, state_gdn, state_gdn_conv, state_hgrn, state_ssd, state_ssd_conv, state_mlstm_c, state_mlstm_n, state_mlstm_m, ada_w, ada_b, norm1_w, norm2_w, w_in, gdn_conv_w, gdn_conv_b, gdn_a_log, gdn_dt_bias, gdn_norm_w, hg_lb_logits, hg_norm_w, ssd_conv_w, ssd_conv_b, ssd_a_log, ssd_dt_bias, ssd_d, ssd_norm_w, ml_b_i, ml_b_f, ml_norm_w, w_branch, w_out, ffn_w_in, ffn_w_out, final_norm_w):
    bp, tp, _ = x_prompt.shape
    bs, ts, _ = x_sample.shape
    assert tp % CHUNK == 0 and LS % ts == 0 and bs % (LS // ts) == 0

    W = dict(w_in=_prep_w_in_call(w_in), w_branch=w_branch.astype(BF16), w_out=w_out.astype(BF16),
             ffn_w_in=ffn_w_in.astype(BF16), ffn_w_out=ffn_w_out.astype(BF16),
             norm1=norm1_w[:, None, :], norm2=norm2_w[:, None, :], final=final_norm_w[None])
    mixer_params = [_layer_params(l, gdn_conv_w, gdn_conv_b, gdn_a_log, gdn_dt_bias, gdn_norm_w, hg_lb_logits,
                                  hg_norm_w, ssd_conv_w, ssd_conv_b, ssd_a_log, ssd_dt_bias, ssd_d, ssd_norm_w,
                                  ml_b_i, ml_b_f, ml_norm_w) for l in range(DEPTH)]

    mods = _ada_call(jnp.concatenate([c_prompt, c_sample], axis=0), ada_w, ada_b)
    mods_p = mods[:, :bp].reshape(DEPTH, bp, 1, 6 * D_MODEL)
    grp = LS // ts
    mods_s = mods[:, bp:].reshape(DEPTH, bs // grp, grp, 6 * D_MODEL)
    sample_states = [state_gdn, state_gdn_conv, state_hgrn, state_ssd, state_ssd_conv,
                     state_mlstm_c, state_mlstm_n, state_mlstm_m.reshape(DEPTH, bs, 1, N_HEAD)]
    shapes_s = [s.shape for s in sample_states]
    shapes_p = [(DEPTH, bp) + tuple(s[2:]) for s in shapes_s]

    y_p, new_p = _trunk(x_prompt, mods_p, shapes_p, None, CHUNK, CHUNK, (1, 256), (1, 512), 8, W, mixer_params)
    y_s, new_s = _trunk(x_sample.reshape(bs // grp, LS, D_MODEL), mods_s, shapes_s, sample_states, LS, ts,
                        (32, LS), (32, LS), 8, W, mixer_params)
    new_p[7] = new_p[7].reshape(DEPTH, bp, N_HEAD)
    new_s[7] = new_s[7].reshape(DEPTH, bs, N_HEAD)
    return (y_p, y_s.reshape(bs, ts, D_MODEL)) + tuple(new_p) + tuple(new_s)
```

```python
import functools
import math

import numpy as np
import jax
import jax.numpy as jnp
from jax import lax
from jax.experimental import pallas as pl
from jax.experimental.pallas import tpu as pltpu

F32 = jnp.float32
BF16 = jnp.bfloat16

D_MODEL = 1024
DEPTH = 2
N_BRANCH = 4
BRANCH_W = 256
N_HEAD = 4
HEAD_D = 64
SSD_H = 8
SSD_P = 32
SSD_N = 64
SSD_G = 2
CONV_K = 4
CHUNK = 64
LS = 8
FLOW_DELAY = {CHUNK: 2, LS: 3}
D_FF = 2816
EPS = 1e-6
NEG = -1e30
GDN_CONV_W = 768
SSD_CONV_W = 512
SSD_XBC_OFF = 2304
MIX_W = 4096
GATE_W = 4096
PROJ_W = MIX_W + GATE_W
SMALL_OFF = 3840
LANE = 128
VMEM_LIMIT = 56 * 1024 * 1024

ST_BETA, ST_GA, ST_DT, ST_MI, ST_MF = 0, 4, 8, 16, 20

_NN = (((1,), (0,)), ((), ()))
_NT = (((1,), (1,)), ((), ()))
_TN = (((0,), (0,)), ((), ()))


def _mm(a, b, dims=_NN):
    return lax.dot_general(a.astype(BF16), b.astype(BF16), dims, preferred_element_type=F32)


def _split(x, n):
    parts, r = [], x
    for i in range(n):
        p = r.astype(BF16)
        parts.append(p)
        if i < n - 1:
            r = r - p.astype(F32)
    return parts


def _mm01(x, m, n):
    out = None
    for p in _split(x, n):
        t = lax.dot_general(p, m.astype(BF16), _NN, preferred_element_type=F32)
        out = t if out is None else out + t
    return out


def _m01m(m, x, n):
    out = None
    for p in _split(x, n):
        t = lax.dot_general(m, p, _NN, preferred_element_type=F32)
        out = t if out is None else out + t
    return out


def _sigmoid(x):
    return jax.nn.sigmoid(x)


def _silu(x):
    return x * jax.nn.sigmoid(x)


def _softplus(x):
    return jnp.maximum(x, 0.0) + jnp.log(1.0 + jnp.exp(-jnp.abs(x)))


def _logsig(x):
    return jnp.minimum(x, 0.0) - jnp.log(1.0 + jnp.exp(-jnp.abs(x)))


def _row_dtype(rows):
    return BF16 if rows % 16 == 0 else F32


def _expand(base, heads, width):
    e = np.zeros((LANE, heads * width), np.float32)
    for h in range(heads):
        e[base + h, h * width:(h + 1) * width] = 1.0
    return e


def _cat_segments(segs):
    cols, off, pos, seen = [], {}, 0, {}
    for name, m in segs:
        key = m.tobytes() + bytes(str(m.shape), 'ascii')
        if key in seen:
            off[name] = seen[key]
            continue
        w = m.shape[1]
        wp = -(-w // LANE) * LANE
        mp = np.zeros((m.shape[0], wp), np.float32)
        mp[:, :w] = m
        cols.append(mp)
        off[name] = seen[key] = (pos, w)
        pos += wp
    return np.concatenate(cols, axis=1), off


@functools.lru_cache(maxsize=None)
def _mixer_consts(L, nv):
    J = int(round(math.log2(L)))
    assert 1 << J == L
    f = np.float32
    r = np.arange(L)
    tri = (r[None, :] <= r[:, None])
    rev = (r[:, None] < r[None, :]) & (r[None, :] <= nv - 1)
    strict = (r[None, :] < r[:, None])
    eye = np.eye(L, dtype=bool)
    lvl, ab = [], []
    for j in range(J):
        bnd = ((r >> (j + 1)) << (j + 1)) + (1 << j) - 1
        low = ((r >> j) & 1) == 1
        a = low[:, None] & (bnd[:, None] < r[None, :]) & (r[None, :] <= r[:, None])
        b = (~low)[:, None] & (r[:, None] < r[None, :]) & (r[None, :] <= bnd[:, None])
        ab.append(a | b)
        same = (r[:, None] >> (j + 1)) == (r[None, :] >> (j + 1))
        lvl.append(same & low[:, None] & (~low)[None, :])

    def tile(m, n):
        return np.tile(m.astype(f), (1, n))

    pr = np.repeat(np.arange(2), L)
    lr2 = np.tile(r, 2)
    ch = np.arange(LANE)
    ch2 = np.arange(2 * LANE)
    c = dict(
        trirev=np.concatenate([tri, rev], axis=0).astype(f),
        mhg=np.concatenate(ab + [tri, rev], axis=0).astype(f),
        ones_ll=np.ones((L, L), f), tri=tri.astype(f),
        tril2=tile(tri, 2), strict2=tile(strict, 2), tril4=tile(tri, 4),
        eye4=tile(eye, 4), supper16=tile(strict, 16),
        e_dr=np.concatenate([_expand(ST_GA, 4, L), _expand(ST_DT, SSD_H, L), _expand(ST_MF, 4, L)], axis=1),
        hmaskp=np.stack([tile(eye, 2)] + [tile(m, 2) for m in lvl]),
        eyep=np.eye(2 * L, dtype=f),
        lvlp=np.stack([(pr[:, None] == pr[None, :]) & m[lr2[:, None], lr2[None, :]] for m in lvl]).astype(f),
        stkp=(pr[:, None] == ch[None, :] // HEAD_D).astype(f),
        bdp=(ch[:, None] // HEAD_D == ch[None, :] // HEAD_D).astype(f),
        pm2=np.stack([ch < HEAD_D, ch >= HEAD_D]).astype(f),
        stk4s=(np.repeat(np.arange(4), L)[:, None] == ch[None, :] // SSD_P).astype(f),
        gbd=(ch[:, None] // SSD_N == ch2[None, :] // LANE).astype(f),
        rowvalid=(r[:, None] <= nv - 1).astype(f) * np.ones((1, LANE), f),
        bd64=(ch2[:, None] // 64 == ch2[None, :] // 64).astype(f),
        bd128=(ch2[:, None] // 128 == ch2[None, :] // 128).astype(f),
    )
    c['e_act'], off_act = _cat_segments([('b_r', _expand(ST_BETA, 4, L)), ('b_c', _expand(ST_BETA, 4, HEAD_D)),
                                         ('dt_c', _expand(ST_DT, SSD_H, SSD_P)),
                                         ('i_r', _expand(ST_MI, 4, L)), ('i_c', _expand(ST_MI, 4, HEAD_D))])
    c['e_cs'], off_cs = _cat_segments([('g_c', _expand(ST_GA, 4, HEAD_D)), ('s_c', _expand(ST_DT, SSD_H, SSD_P)),
                                       ('m_r', _expand(ST_MF, 4, L)), ('m_c', _expand(ST_MF, 4, HEAD_D))])
    c['e_rev'], off_rev = _cat_segments([('g_c', _expand(ST_GA, 4, HEAD_D)), ('s_c', _expand(ST_DT, SSD_H, SSD_P)),
                                         ('m_c', _expand(ST_MF, 4, HEAD_D))])
    return c, dict(act=off_act, cs=off_cs, rev=off_rev), J


_CONST_ORDER = ('trirev', 'mhg', 'ones_ll', 'tri', 'tril2', 'strict2', 'tril4', 'eye4', 'supper16', 'e_dr', 'hmaskp',
                'eyep', 'lvlp', 'stkp', 'bdp', 'pm2', 'stk4s', 'gbd', 'rowvalid', 'bd64', 'bd128', 'e_act', 'e_cs',
                'e_rev')
_BF16_CONSTS = ('trirev', 'mhg', 'ones_ll', 'tri', 'supper16', 'e_dr', 'lvlp', 'stkp', 'stk4s', 'bd64', 'bd128',
                'e_act', 'e_cs', 'e_rev')


def _conv_silu(ext, w, b, L):
    y = b
    for i in range(CONV_K):
        y = y + ext[5 + i:5 + i + L, :] * w[i:i + 1, :]
    return _silu(y)


def _run_interleaved(tasks):
    tasks = list(tasks)
    while tasks:
        alive = []
        for t in tasks:
            try:
                next(t)
                alive.append(t)
            except StopIteration:
                pass
        tasks = alive


def _mixer_chunk(pm, ext_g, ext_s, st, P, K, offs, L, nv, J, layer, out):
    lane = lax.broadcasted_iota(jnp.int32, (1, LANE), 1)
    rowvalid = K['rowvalid'][:, 0:1]
    tril2 = K['tril2'] > 0.0
    stkp, bdp = K['stkp'], K['bdp']
    upper_c = lane >= HEAD_D
    upper_r = lax.broadcasted_iota(jnp.int32, (1, 2 * L), 1) >= L

    def seg(x, which, name):
        o, w = offs[which][name]
        return x[:, o:o + w]

    def pair_c(x, p):
        return x[:, p * LANE:(p + 1) * LANE]

    def pair_r(x, p):
        return x[:, p * 2 * L:(p + 1) * 2 * L]

    def stack(x):
        xb = x.astype(BF16)
        return jnp.concatenate([xb, xb], axis=0) * stkp

    z = pm[:, SMALL_OFF:SMALL_OFF + LANE] + P['sp'][0:1, :]
    act = jnp.where(lane < ST_GA, _sigmoid(z),
                    jnp.where(lane < ST_MI, _softplus(z), jnp.where(lane < ST_MF, z, _logsig(z))))
    neg_a = -jnp.exp(P['sp'][1:2, :])
    dec_in = jnp.where((lane >= ST_GA) & (lane < ST_MI), neg_a * act,
                       jnp.where((lane >= ST_MF) & (lane < ST_MF + 4), act, 0.0))
    cr = _m01m(K['trirev'], dec_in, 2)
    yield
    cs, rev = cr[:L], cr[L:]
    ea = _mm01(act, K['e_act'], 2)
    ec = _mm01(cs, K['e_cs'], 2)
    er = _mm01(rev, K['e_rev'], 2)
    yield
    m_r = seg(ec, 'cs', 'm_r')
    g_c, s_c, m_c = seg(ec, 'cs', 'g_c'), seg(ec, 'cs', 's_c'), seg(ec, 'cs', 'm_c')
    b_r, b_c, dt_c = seg(ea, 'act', 'b_r'), seg(ea, 'act', 'b_c'), seg(ea, 'act', 'dt_c')
    i_r, i_c = seg(ea, 'act', 'i_r'), seg(ea, 'act', 'i_c')
    rg_c, rs_c, rm_c = seg(er, 'rev', 'g_c'), seg(er, 'rev', 's_c'), seg(er, 'rev', 'm_c')
    ym = _mm(dec_in, K['e_dr']).astype(BF16) * K['supper16']
    yield
    d_all = lax.dot_general(K['tri'], ym, _NN, preferred_element_type=F32)
    d_g, d_s, d_m = d_all[:, 0:4 * L], d_all[:, 4 * L:12 * L], d_all[:, 12 * L:16 * L]
    row_i = _m01m(K['ones_ll'], i_r * K['eye4'], 2)

    res = dict(o_gdn=[None, None], gdn=[None, None], o_hg=[None, None], hg=[None, None],
               o_ml=[None, None], mlc=[None, None], mln=[None, None], mlm=[None] * N_HEAD)

    qkv = _conv_silu(ext_g, P['cwg'], P['cbg'], L)
    q, k, v = qkv[:, 0:256], qkv[:, 256:512], qkv[:, 512:768]
    ss = _mm(jnp.concatenate([q * q, k * k], axis=0), K['bd64'])
    yield
    q = q * (lax.rsqrt(ss[:L] + EPS) * (HEAD_D ** -0.5))
    k = k * lax.rsqrt(ss[L:] + EPS)

    def gdn_task(p):
        qt, kt, vt = pair_c(q, p), pair_c(k, p), pair_c(v, p)
        beta_c, gc = pair_c(b_c, p), pair_c(g_c, p)
        eg = jnp.exp(gc)
        dec = jnp.exp(pair_r(d_g, p)) * K['tril2']
        kkqk = _mm(jnp.concatenate([kt, qt], axis=0), stack(kt), _NT)
        yield
        n = (kkqk[:L] * dec * pair_r(b_r, p) * K['strict2']).astype(BF16)
        nbd = jnp.concatenate([n, n], axis=0)
        t = K['eyep'] - (nbd * K['lvlp'][0]).astype(F32)
        for j in range(1, J):
            x = _mm(t, nbd * K['lvlp'][j])
            yield
            t = t - _mm(x, t)
            yield
        sv = _mm(t, stack(vt * beta_c))
        sk = _mm(t, stack(kt * (beta_c * eg)))
        yield
        s_p = st['gdn'][p]
        r = _mm(jnp.concatenate([sk[:L] + sk[L:], qt * eg], axis=0), s_p)
        yield
        u = sv[:L] + sv[L:] - r[:L]
        res['o_gdn'][p] = r[L:] + _mm(kkqk[L:] * dec, stack(u))
        kw = kt * (jnp.exp(pair_c(rg_c, p)) * rowvalid)
        res['gdn'][p] = s_p * jnp.exp(gc[nv - 1:nv, :]) + bdp * _mm(kw, u, _TN)
        yield

    lg = P['lg']
    mx = lg[0:1, :]
    for i in range(1, DEPTH):
        mx = jnp.maximum(mx, lg[i:i + 1, :])
    ex = [jnp.exp(lg[i:i + 1, :] - mx) for i in range(DEPTH)]
    tot = ex[0]
    for i in range(1, DEPTH):
        tot = tot + ex[i]
    sm = [e / tot for e in ex]
    cum = sm[0]
    for i in range(1, layer + 1):
        cum = cum + sm[i]
    lb = cum - sm[0]
    lb_pos = lb > 0
    log_lb = jnp.log(jnp.where(lb_pos, lb, 1.0))
    hq = _silu(pm[:, 1024:1280])
    fz = pm[:, 1280:1536]
    hv = pm[:, 1536:1792]
    ls = _logsig(fz)
    t2 = jnp.log1p(-lb) + ls
    la = jnp.maximum(log_lb, t2) + jnp.log1p(jnp.exp(-jnp.abs(log_lb - t2)))
    logf = jnp.where(lb_pos, la, ls)
    kg = (1.0 - lb) * _sigmoid(-fz)
    ey = jnp.exp(_m01m(K['mhg'], logf, 2))
    yield
    eg_h = ey[J * L:(J + 1) * L]
    eyb = ey[0:J * L].astype(BF16)
    hqb = hq.astype(BF16)
    qe = hq * eg_h
    kwr = kg * (ey[(J + 1) * L:(J + 2) * L] * rowvalid)

    def hg_task(p):
        q_p, k_st = pair_c(hqb, p), stack(pair_c(kg, p))
        a_p = K['hmaskp'][0] * _mm(q_p, k_st, _NT)
        yield
        for j in range(J):
            e_j = pair_c(eyb[j * L:(j + 1) * L], p)
            a_p = a_p + K['hmaskp'][j + 1] * _mm(q_p * e_j, k_st * jnp.concatenate([e_j, e_j], axis=0), _NT)
            yield
        st_p = st['hg'][p]
        res['o_hg'][p] = _mm(pair_c(qe, p), st_p, _NT) + _mm(a_p, stack(pair_c(hv, p)))
        yield
        res['hg'][p] = (st_p * pair_c(eg_h[nv - 1:nv, :], p)
                        + bdp * _mm(pair_c(hv, p), pair_c(kwr, p), _TN))
        yield

    xbc = _conv_silu(ext_s, P['cws'], P['cbs'], L)
    xs, bs, cc = xbc[:, 0:256], xbc[:, 256:384], xbc[:, 384:512]
    vs = xs * dt_c
    bsb = bs.astype(BF16)
    bs4 = jnp.concatenate([bsb] * 4, axis=0)

    def ssd_task():
        o_intra = []
        for g in range(SSD_G):
            cbw = _mm(cc * K['pm2'][g:g + 1, :], bs4, _NT)
            dec = jnp.exp(d_s[:, g * 4 * L:(g + 1) * 4 * L]) * K['tril4']
            vb = pair_c(vs, g).astype(BF16)
            o_intra.append(_mm(cbw * dec, jnp.concatenate([vb] * 4, axis=0) * K['stk4s']))
            yield
        s_all = st['ssd']
        res['o_ssd'] = jnp.exp(s_c) * _mm(cc, s_all) + jnp.concatenate(o_intra, axis=1)
        res['ssd'] = (s_all * jnp.exp(s_c[nv - 1:nv, :])
                      + K['gbd'] * _mm(bs * rowvalid, vs * jnp.exp(rs_c), _TN))
        yield

    mq = pm[:, 2816:3072] * (HEAD_D ** -0.5)
    mk = pm[:, 3072:3328]
    mv = pm[:, 3328:3584]

    def ml_task(p, delay):
        for _ in range(delay):
            yield
        qt, kt, vt = pair_c(mq, p), pair_c(mk, p), pair_c(mv, p)
        bm_r, bm_c = pair_r(m_r, p), pair_c(m_c, p)
        mp0 = st['mlm'][:, 2 * p:2 * p + 1]
        mp1 = st['mlm'][:, 2 * p + 1:2 * p + 2]
        logw = jnp.where(tril2, pair_r(d_m, p) + pair_r(row_i, p), NEG)
        mx0 = jnp.max(jnp.where(upper_r, NEG, logw), axis=-1, keepdims=True)
        mx1 = jnp.max(jnp.where(upper_r, logw, NEG), axis=-1, keepdims=True)
        l0_r = bm_r + jnp.where(upper_r, mp1, mp0)
        l0_c = bm_c + jnp.where(upper_c, mp1, mp0)
        mt_r = jnp.maximum(l0_r, jnp.where(upper_r, mx1, mx0))
        mt_c = jnp.maximum(l0_c, jnp.where(upper_c, mx1, mx0))
        w = jnp.exp(logw - mt_r)
        w0 = jnp.exp(l0_c - mt_c)
        qk = _mm(qt, stack(kt), _NT) * w
        yield
        c_p = st['mlc'][p]
        n_p = st['mln'][p]
        num = w0 * _mm(qt, c_p) + _mm(qk, stack(vt))
        den = w0 * _mm01(qt * n_p, bdp, 2) + _mm01(qk, stkp, 2)
        res['o_ml'][p] = num * (1.0 / jnp.maximum(jnp.abs(den), jnp.exp(-mt_c)))
        yield
        m_l = mt_c[nv - 1:nv, :]
        wl0 = jnp.exp(bm_c[nv - 1:nv, :] + jnp.where(upper_c, mp1, mp0) - m_l)
        kwl = kt * (jnp.exp(pair_c(rm_c, p) + pair_c(i_c, p) - m_l) * rowvalid)
        res['mlc'][p] = c_p * wl0 + bdp * _mm(kwl, vt, _TN)
        res['mln'][p] = n_p * wl0 + jnp.sum(kwl, axis=0, keepdims=True)
        res['mlm'][2 * p] = m_l[:, 0:1]
        res['mlm'][2 * p + 1] = m_l[:, HEAD_D:HEAD_D + 1]
        yield

    def finish():
        o_gdn = jnp.concatenate(res['o_gdn'], axis=1)
        o_hg = jnp.concatenate(res['o_hg'], axis=1)
        hh = jnp.concatenate(res['o_ml'], axis=1)
        ms = _mm(jnp.concatenate([o_gdn * o_gdn, o_hg * o_hg, hh * hh], axis=0), K['bd64']) * (1.0 / HEAD_D)
        yield
        out_a = o_gdn * lax.rsqrt(ms[0:L] + EPS) * P['vec'][0:1, :] * _silu(pm[:, 768:1024])
        out_b = o_hg * lax.rsqrt(ms[L:2 * L] + EPS) * P['vec'][1:2, :] * _silu(pm[:, 1792:2048])
        out_d = hh * lax.rsqrt(ms[2 * L:3 * L] + EPS) * P['vec'][3:4, :] * _sigmoid(pm[:, 3584:3840])
        ys = (res['o_ssd'] + P['vec'][4:5, :] * xs) * _silu(pm[:, 2048:2304])
        out_c = ys * lax.rsqrt(_mm(ys * ys, K['bd128']) * (1.0 / (2 * HEAD_D)) + EPS) * P['vec'][2:3, :]
        branches = jnp.concatenate([out_a, out_b, out_c, out_d], axis=1)
        new = dict(gdn=res['gdn'], hg=res['hg'], ssd=res['ssd'], mlc=res['mlc'], mln=res['mln'], mlm=res['mlm'])
        out['result'] = (branches, new)
        yield

    out['tasks'] = ([gdn_task(p) for p in range(2)] + [hg_task(p) for p in range(2)]
                    + [ssd_task()] + [ml_task(p, 3 + 2 * p) for p in range(2)])
    out['finish'] = finish


def _mixer_kernel(L, nv, J, layer, bb, offs, has_init, n_alias, *refs):
    it = iter(refs)
    pm_ref = next(it)
    if has_init:
        gdn0, cg0, hg0, ssd0, cs0, mc0, mn0, mm0 = (next(it) for _ in range(8))
    sp_ref, cwg_ref, cbg_ref, cws_ref, cbs_ref, vec_ref, lg_ref = (next(it) for _ in range(7))
    kref = {name: next(it) for name in _CONST_ORDER}
    for _ in range(n_alias):
        next(it)
    br_ref = next(it)
    gdn1, cg1, hg1, ssd1, cs1, mc1, mn1, mm1 = (next(it) for _ in range(8))
    sg, sh, sc, sn, ss, extg, exts, pmpad = (next(it) for _ in range(8))

    c = pl.program_id(1)
    nc = pl.num_programs(1)
    n_sub = SSD_H // SSD_G

    @pl.when(c == 0)
    def _init():
        sg[...] = jnp.zeros_like(sg)
        sh[...] = jnp.zeros_like(sh)
        sc[...] = jnp.zeros_like(sc)
        ss[...] = jnp.zeros_like(ss)
        extg[:, 0:8, :] = jnp.zeros((bb, 8, GDN_CONV_W), F32)
        exts[:, 0:8, :] = jnp.zeros((bb, 8, SSD_CONV_W), F32)
        if not has_init:
            sn[...] = jnp.zeros_like(sn)
            mm1[...] = jnp.zeros_like(mm1)
            return
        for s in range(bb):
            for h in range(N_HEAD):
                p, lo = h // 2, (h % 2) * HEAD_D
                sg[s, p, lo:lo + HEAD_D, lo:lo + HEAD_D] = gdn0[s, h]
                sh[s, p, lo:lo + HEAD_D, lo:lo + HEAD_D] = hg0[s, h].T
                sc[s, p, lo:lo + HEAD_D, lo:lo + HEAD_D] = mc0[s, h]
                sn[s, p, :, lo:lo + HEAD_D] = mn0[s, h:h + 1, :]
            for h in range(SSD_H):
                g = h // n_sub
                ss[s, g * SSD_N:(g + 1) * SSD_N, h * SSD_P:(h + 1) * SSD_P] = ssd0[s, h]
        mm1[...] = mm0[...]
        extg[:, 5:8, :] = cg0[...]
        exts[:, 5:8, :] = cs0[...]

    P = dict(sp=sp_ref[...], cwg=cwg_ref[...], cbg=cbg_ref[...], cws=cws_ref[...], cbs=cbs_ref[...],
             vec=vec_ref[...], lg=lg_ref[...])
    K = {name: r[...] for name, r in kref.items()}
    stage1, outs = [], []
    spg = L // nv
    for s in range(bb):
        grp, off = s // spg, (s % spg) * nv
        if nv == L:
            pm = pm_ref[s]
        else:
            pmpad[s, 0:nv, :] = pm_ref[grp, off:off + nv, :]
            pmpad[s, nv:L, :] = jnp.zeros((L - nv, MIX_W), F32)
            pm = pmpad[s]
        extg[s, 8:8 + L, :] = pm[:, 0:GDN_CONV_W]
        exts[s, 8:8 + L, :] = pm[:, SSD_XBC_OFF:SSD_XBC_OFF + SSD_CONV_W]
        st = dict(gdn=sg[s], hg=sh[s], ssd=ss[s], mlc=sc[s], mln=sn[s], mlm=mm1[s])
        outs.append({})
        stage1.append(_mixer_chunk(pm, extg.at[s], exts.at[s], st, P, K, offs, L, nv, J, layer, outs[s]))

    def flow(s):
        for _ in range(FLOW_DELAY[L] * s):
            yield
        yield from stage1[s]
        tasks = list(outs[s]['tasks'])
        while tasks:
            alive = []
            for t in tasks:
                try:
                    next(t)
                    alive.append(t)
                except StopIteration:
                    pass
            tasks = alive
            yield
        yield from outs[s]['finish']()

    _run_interleaved([flow(s) for s in range(bb)])
    for s in range(bb):
        branches, new = outs[s]['result']
        grp, off = s // spg, (s % spg) * nv
        br_ref[grp, off:off + nv, :] = branches[0:nv].astype(br_ref.dtype)
        for p in range(2):
            sg[s, p] = new['gdn'][p]
            sh[s, p] = new['hg'][p]
            sc[s, p] = new['mlc'][p]
            sn[s, p] = new['mln'][p]
        for h in range(N_HEAD):
            mm1[s, :, h:h + 1] = new['mlm'][h]
        ss[s] = new['ssd']
        tail_g = extg[s, 8 + nv - 3:8 + nv, :]
        tail_s = exts[s, 8 + nv - 3:8 + nv, :]
        extg[s, 5:8, :] = tail_g
        exts[s, 5:8, :] = tail_s
        cg1[s] = tail_g
        cs1[s] = tail_s

    @pl.when(c == nc - 1)
    def _fin():
        for s in range(bb):
            for h in range(N_HEAD):
                p, lo = h // 2, (h % 2) * HEAD_D
                gdn1[s, h] = sg[s, p, lo:lo + HEAD_D, lo:lo + HEAD_D]
                hg1[s, h] = sh[s, p, lo:lo + HEAD_D, lo:lo + HEAD_D].T
                mc1[s, h] = sc[s, p, lo:lo + HEAD_D, lo:lo + HEAD_D]
                mn1[s, h:h + 1, :] = sn[s, p, :, lo:lo + HEAD_D]
            for h in range(SSD_H):
                g = h // n_sub
                ssd1[s, h] = ss[s, g * SSD_N:(g + 1) * SSD_N, h * SSD_P:(h + 1) * SSD_P]


def _full_spec(a):
    nd = a.ndim
    return pl.BlockSpec(a.shape, lambda b, c, _nd=nd: (0,) * _nd)


def _mixer_call(proj, state_shapes, states_in, prev_out, params, L, nv, layer, bb):
    g, t, _ = proj.shape
    spg = L // nv if nv < L else 1
    nchunk = t // L
    assert (g * spg) % bb == 0 and bb % spg == 0 and t % L == 0
    consts, offs, J = _mixer_consts(L, nv)
    const_arrays = [jnp.asarray(consts[n], BF16 if n in _BF16_CONSTS else F32) for n in _CONST_ORDER]

    def st_spec(shape):
        nd = len(shape)
        return pl.BlockSpec((None, bb) + tuple(shape[2:]), lambda b, c, _nd=nd: (layer, b) + (0,) * (_nd - 2))

    has_init = states_in is not None
    n_alias = 0 if prev_out is None else len(prev_out)
    inputs = [proj] + (list(states_in) if has_init else []) + list(params) + const_arrays + list(prev_out or [])
    in_specs = ([pl.BlockSpec((bb // spg, L, MIX_W), lambda b, c: (b, c, 0))]
                + ([st_spec(s) for s in state_shapes] if has_init else [])
                + [_full_spec(a) for a in params]
                + [_full_spec(a) for a in const_arrays]
                + [pl.BlockSpec(memory_space=pl.ANY)] * n_alias)
    out_shape = ([jax.ShapeDtypeStruct((g, t, N_BRANCH * BRANCH_W), _row_dtype(L))]
                 + [jax.ShapeDtypeStruct(s, F32) for s in state_shapes])
    out_specs = ([pl.BlockSpec((bb // spg, L, N_BRANCH * BRANCH_W), lambda b, c: (b, c, 0))]
                 + [st_spec(s) for s in state_shapes])
    first_alias = len(inputs) - n_alias
    pair = (bb, 2, LANE, LANE)
    scratch = [pltpu.VMEM(pair, F32), pltpu.VMEM(pair, F32), pltpu.VMEM(pair, F32),
               pltpu.VMEM((bb, 2, 1, LANE), F32), pltpu.VMEM((bb, SSD_G * SSD_N, SSD_H * SSD_P), F32),
               pltpu.VMEM((bb, 8 + L, GDN_CONV_W), F32), pltpu.VMEM((bb, 8 + L, SSD_CONV_W), F32),
               pltpu.VMEM((bb, L, MIX_W) if nv < L else (1, 8, LANE), F32)]
    outs = pl.pallas_call(
        functools.partial(_mixer_kernel, L, nv, J, layer, bb, offs, has_init, n_alias),
        grid=(g * spg // bb, nchunk),
        in_specs=in_specs, out_specs=out_specs, out_shape=out_shape, scratch_shapes=scratch,
        input_output_aliases={first_alias + k: 1 + k for k in range(n_alias)},
        compiler_params=pltpu.CompilerParams(dimension_semantics=("parallel", "arbitrary"),
                                             vmem_limit_bytes=VMEM_LIMIT),
        name=f"mixer_L{L}",
    )(*inputs)
    return outs[0], list(outs[1:])


def _ada_kernel(c_ref, w_ref, b_ref, o_ref):
    o_ref[...] = _mm(_silu(c_ref[...]), w_ref[...]) + b_ref[...]


def _ada_call(c_all, ada_w, ada_b):
    rows = c_all.shape[0]
    n = ada_w.shape[-1]
    tn = 1536
    return pl.pallas_call(
        _ada_kernel,
        grid=(DEPTH, n // tn),
        in_specs=[pl.BlockSpec((rows, D_MODEL), lambda l, j: (0, 0)),
                  pl.BlockSpec((None, D_MODEL, tn), lambda l, j: (l, 0, j)),
                  pl.BlockSpec((None, 1, tn), lambda l, j: (l, 0, j))],
        out_specs=pl.BlockSpec((None, rows, tn), lambda l, j: (l, 0, j)),
        out_shape=jax.ShapeDtypeStruct((DEPTH, rows, n), F32),
        compiler_params=pltpu.CompilerParams(dimension_semantics=("arbitrary", "arbitrary"),
                                             vmem_limit_bytes=VMEM_LIMIT),
        name="ada",
    )(c_all, ada_w, ada_b.reshape(DEPTH, 1, n))


PREP_ROWS = MIX_W - SMALL_OFF


def _prep_w_in_kernel(w_ref, s0_ref, s1_ref, s2_ref, o_ref):
    j = pl.program_id(1)
    j_small = SMALL_OFF // PREP_ROWS

    @pl.when(j != j_small)
    def _copy():
        o_ref[...] = w_ref[0].astype(BF16)

    @pl.when(j == j_small)
    def _small():
        o_ref[...] = jnp.zeros(o_ref.shape, BF16)
        o_ref[0:8, :] = s0_ref[0].astype(BF16)
        o_ref[8:16, :] = s1_ref[0].astype(BF16)
        o_ref[16:24, :] = s2_ref[0].astype(BF16)


def _prep_src_row(j):
    r = PREP_ROWS
    return jnp.where(j < 1024 // r, r * j,
                     jnp.where(j < 2816 // r, r * j + 8,
                               jnp.where(j < SMALL_OFF // r, r * j + 16,
                                         jnp.where(j < MIX_W // r, 0, r * j - MIX_W + 3864))))


def _prep_w_in_call(w_in):
    wt = jnp.swapaxes(w_in, 1, 2)
    depth, _, d = wt.shape

    def rows(n, index):
        return pl.BlockSpec((pl.Element(1), pl.Element(n), pl.Element(d)), index)

    return pl.pallas_call(
        _prep_w_in_kernel,
        grid=(depth, PROJ_W // PREP_ROWS),
        in_specs=[rows(PREP_ROWS, lambda l, j: (l, pl.multiple_of(_prep_src_row(j), 8), 0)),
                  rows(8, lambda l, j: (l, 1024, 0)), rows(8, lambda l, j: (l, 2824, 0)),
                  rows(8, lambda l, j: (l, 3856, 0))],
        out_specs=pl.BlockSpec((None, PREP_ROWS, d), lambda l, j: (l, j, 0)),
        out_shape=jax.ShapeDtypeStruct((depth, PROJ_W, d), BF16),
        compiler_params=pltpu.CompilerParams(dimension_semantics=("arbitrary", "arbitrary"),
                                             vmem_limit_bytes=VMEM_LIMIT),
        name="prep_w_in",
    )(wt, wt, wt, wt)


def _rms_mod(x, nw, sc, sh):
    ms = jnp.mean(x * x, axis=-1, keepdims=True)
    return x * lax.rsqrt(ms + EPS) * nw * (1.0 + sc) + sh


def _tok_spec(gblk, width):
    g, r = gblk
    return pl.BlockSpec((g, r, width), lambda i, j: (i, j, 0))


def _mod_spec(gblk, layer, k, mods):
    return pl.BlockSpec((None, gblk[0], mods.shape[2], D_MODEL), lambda i, j: (layer, i, 0, k))


def _mod_rows(m, r):
    spg = m.shape[1]
    if spg == 1:
        return m
    seq = lax.broadcasted_iota(jnp.int32, (1, r, 1), 1) // (r // spg)
    out = m[:, 0:1, :]
    for j in range(1, spg):
        out = jnp.where(seq == j, m[:, j:j + 1, :], out)
    return out


def _layer_spec(shape, layer):
    nd = len(shape)
    return pl.BlockSpec((None,) + tuple(shape[1:]), lambda i, j: (layer,) + (0,) * (nd - 1),
                        pipeline_mode=pl.Buffered(1))


def _dense_params():
    return pltpu.CompilerParams(dimension_semantics=("arbitrary", "arbitrary"), vmem_limit_bytes=VMEM_LIMIT)


def _inproj_kernel(x_ref, nw_ref, sc_ref, sh_ref, w_ref, pm_ref, gate_ref):
    g, r, _ = x_ref.shape
    h = _rms_mod(x_ref[...], nw_ref[...], _mod_rows(sc_ref[...], r), _mod_rows(sh_ref[...], r))
    h = h.astype(BF16).reshape(g * r, D_MODEL)
    pm_ref[...] = _mm(h, w_ref[0:MIX_W, :], _NT).reshape(g, r, MIX_W)
    gates = _sigmoid(_mm(h, w_ref[MIX_W:PROJ_W, :], _NT))
    gate_ref[...] = gates.astype(gate_ref.dtype).reshape(g, r, GATE_W)


def _inproj_call(x, layer, nw, mods, w, gblk):
    bg, t, _ = x.shape
    return pl.pallas_call(
        _inproj_kernel,
        grid=(bg // gblk[0], t // gblk[1]),
        in_specs=[_tok_spec(gblk, D_MODEL), _layer_spec(nw.shape, layer),
                  _mod_spec(gblk, layer, 1, mods), _mod_spec(gblk, layer, 0, mods), _layer_spec(w.shape, layer)],
        out_specs=[_tok_spec(gblk, MIX_W), _tok_spec(gblk, GATE_W)],
        out_shape=[jax.ShapeDtypeStruct((bg, t, MIX_W), F32),
                   jax.ShapeDtypeStruct((bg, t, GATE_W), _row_dtype(gblk[1]))],
        compiler_params=_dense_params(), name="inproj",
    )(x, nw, mods, mods, w)


def _merge_ffn_kernel(final, br_ref, gate_ref, x_ref, gt1_ref, nw_ref, sc_ref, sh_ref, gt2_ref,
                      wb_ref, wo_ref, w1_ref, w2_ref, fw_ref, o_ref):
    g, r, _ = x_ref.shape
    br = br_ref[...].reshape(g * r, N_BRANCH * BRANCH_W)
    merged = None
    for n in range(N_BRANCH):
        up = _mm(br[:, n * BRANCH_W:(n + 1) * BRANCH_W], wb_ref[n])
        gate = gate_ref[:, :, n * D_MODEL:(n + 1) * D_MODEL].astype(F32).reshape(g * r, D_MODEL)
        t = gate * up
        merged = t if merged is None else merged + t
    x1 = x_ref[...] + _mod_rows(gt1_ref[...], r) * _mm(merged, wo_ref[...]).reshape(g, r, D_MODEL)
    h = _rms_mod(x1, nw_ref[...], _mod_rows(sc_ref[...], r), _mod_rows(sh_ref[...], r))
    h = h.astype(BF16).reshape(g * r, D_MODEL)
    a = _mm(h, w1_ref[:, 0:D_FF])
    b = _mm(h, w1_ref[:, D_FF:2 * D_FF])
    x2 = x1 + _mod_rows(gt2_ref[...], r) * _mm(_silu(a) * b, w2_ref[...]).reshape(g, r, D_MODEL)
    if final:
        ms = jnp.mean(x2 * x2, axis=-1, keepdims=True)
        x2 = x2 * lax.rsqrt(ms + EPS) * fw_ref[...]
    o_ref[...] = x2


def _merge_ffn_call(br, gates, x, layer, nw, mods, wb, wo, w1, w2, fw, gblk, final):
    bg, t, _ = x.shape
    return pl.pallas_call(
        functools.partial(_merge_ffn_kernel, final),
        grid=(bg // gblk[0], t // gblk[1]),
        in_specs=[_tok_spec(gblk, N_BRANCH * BRANCH_W), _tok_spec(gblk, GATE_W), _tok_spec(gblk, D_MODEL),
                  _mod_spec(gblk, layer, 2, mods), _layer_spec(nw.shape, layer),
                  _mod_spec(gblk, layer, 4, mods), _mod_spec(gblk, layer, 3, mods), _mod_spec(gblk, layer, 5, mods),
                  _layer_spec(wb.shape, layer), _layer_spec(wo.shape, layer),
                  _layer_spec(w1.shape, layer), _layer_spec(w2.shape, layer),
                  pl.BlockSpec(fw.shape, lambda i, j: (0, 0), pipeline_mode=pl.Buffered(1))],
        out_specs=_tok_spec(gblk, D_MODEL),
        out_shape=jax.ShapeDtypeStruct((bg, t, D_MODEL), F32),
        compiler_params=_dense_params(), name="merge_ffn",
    )(br, gates, x, mods, nw, mods, mods, mods, wb, wo, w1, w2, fw)


def _small_params(gdn_dt_bias, ssd_dt_bias, ml_b_i, ml_b_f, gdn_a_log, ssd_a_log):
    z4 = jnp.zeros((4,), F32)
    bias = jnp.concatenate([z4, gdn_dt_bias, ssd_dt_bias, ml_b_i, ml_b_f, jnp.zeros((LANE - 24,), F32)])
    alog = jnp.concatenate([z4, gdn_a_log, ssd_a_log, jnp.zeros((LANE - 16,), F32)])
    return jnp.concatenate([bias[None], alog[None], jnp.zeros((6, LANE), F32)], axis=0)


def _layer_params(l, gdn_conv_w, gdn_conv_b, gdn_a_log, gdn_dt_bias, gdn_norm_w, hg_lb_logits, hg_norm_w,
                  ssd_conv_w, ssd_conv_b, ssd_a_log, ssd_dt_bias, ssd_d, ssd_norm_w, ml_b_i, ml_b_f, ml_norm_w):
    sp = _small_params(gdn_dt_bias[l], ssd_dt_bias[l], ml_b_i[l], ml_b_f[l], gdn_a_log[l], ssd_a_log[l])
    vec = jnp.stack([jnp.tile(gdn_norm_w[l], N_HEAD), jnp.tile(hg_norm_w[l], N_HEAD), ssd_norm_w[l],
                     jnp.tile(ml_norm_w[l], N_HEAD), jnp.repeat(ssd_d[l], SSD_P)]
                    + [jnp.zeros((256,), F32)] * 3)
    return [sp, gdn_conv_w[l], gdn_conv_b[l][None], ssd_conv_w[l], ssd_conv_b[l][None], vec,
            hg_lb_logits.astype(F32)]


def _trunk(x, mods, state_shapes, states_in, L, nv, gblk, gblk_ffn, bb, W, mixer_params):
    new_states = None
    for l in range(DEPTH):
        pm, gates = _inproj_call(x, l, W['norm1'], mods, W['w_in'], gblk)
        br, new_states = _mixer_call(pm, state_shapes, states_in, new_states, mixer_params[l], L, nv, l, bb)
        x = _merge_ffn_call(br, gates, x, l, W['norm2'], mods, W['w_branch'], W['w_out'],
                            W['ffn_w_in'], W['ffn_w_out'], W['final'], gblk_ffn, final=(l == DEPTH - 1))
    return x, new_states


def kernel(x_prompt, x_sample, c_prompt, c_sample, state_gdn, state_gdn_conv, state_hgrn, state_ssd, state_ssd_conv, state_mlstm_c, state_mlstm_n, state_mlstm_m, ada_w, ada_b, norm1_w, norm2_w, w_in, gdn_conv_w, gdn_conv_b, gdn_a_log, gdn_dt_bias, gdn_norm_w, hg_lb_logits, hg_norm_w, ssd_conv_w, ssd_conv_b, ssd_a_log, ssd_dt_bias, ssd_d, ssd_norm_w, ml_b_i, ml_b_f, ml_norm_w, w_branch, w_out, ffn_w_in, ffn_w_out, final_norm_w):
    bp, tp, _ = x_prompt.shape
    bs, ts, _ = x_sample.shape
    assert tp % CHUNK == 0 and LS % ts == 0 and bs % (LS // ts) == 0

    W = dict(w_in=_prep_w_in_call(w_in), w_branch=w_branch.astype(BF16), w_out=w_out.astype(BF16),
             ffn_w_in=ffn_w_in.astype(BF16), ffn_w_out=ffn_w_out.astype(BF16),
             norm1=norm1_w[:, None, :], norm2=norm2_w[:, None, :], final=final_norm_w[None])
    mixer_params = [_layer_params(l, gdn_conv_w, gdn_conv_b, gdn_a_log, gdn_dt_bias, gdn_norm_w, hg_lb_logits,
                                  hg_norm_w, ssd_conv_w, ssd_conv_b, ssd_a_log, ssd_dt_bias, ssd_d, ssd_norm_w,
                                  ml_b_i, ml_b_f, ml_norm_w) for l in range(DEPTH)]

    mods = _ada_call(jnp.concatenate([c_prompt, c_sample], axis=0), ada_w, ada_b)
    mods_p = mods[:, :bp].reshape(DEPTH, bp, 1, 6 * D_MODEL)
    grp = LS // ts
    mods_s = mods[:, bp:].reshape(DEPTH, bs // grp, grp, 6 * D_MODEL)
    sample_states = [state_gdn, state_gdn_conv, state_hgrn, state_ssd, state_ssd_conv,
                     state_mlstm_c, state_mlstm_n, state_mlstm_m.reshape(DEPTH, bs, 1, N_HEAD)]
    shapes_s = [s.shape for s in sample_states]
    shapes_p = [(DEPTH, bp) + tuple(s[2:]) for s in shapes_s]

    y_p, new_p = _trunk(x_prompt, mods_p, shapes_p, None, CHUNK, CHUNK, (1, 256), (1, 512), 8, W, mixer_params)
    y_s, new_s = _trunk(x_sample.reshape(bs // grp, LS, D_MODEL), mods_s, shapes_s, sample_states, LS, ts,
                        (32, LS), (32, LS), 8, W, mixer_params)
    new_p[7] = new_p[7].reshape(DEPTH, bp, N_HEAD)
    new_s[7] = new_s[7].reshape(DEPTH, bs, N_HEAD)
    return (y_p, y_s.reshape(bs, ts, D_MODEL)) + tuple(new_p) + tuple(new_s)
```

```python
import functools
import math

import numpy as np
import jax
import jax.numpy as jnp
from jax import lax
from jax.experimental import pallas as pl
from jax.experimental.pallas import tpu as pltpu

F32 = jnp.float32
BF16 = jnp.bfloat16

D_MODEL = 1024
DEPTH = 2
N_BRANCH = 4
BRANCH_W = 256
N_HEAD = 4
HEAD_D = 64
SSD_H = 8
SSD_P = 32
SSD_N = 64
SSD_G = 2
CONV_K = 4
CHUNK = 64
LS = 8
FLOW_DELAY = {CHUNK: 2, LS: 3}
D_FF = 2816
EPS = 1e-6
NEG = -1e30
GDN_CONV_W = 768
SSD_CONV_W = 512
SSD_XBC_OFF = 2304
MIX_W = 4096
GATE_W = 4096
PROJ_W = MIX_W + GATE_W
SMALL_OFF = 3840
LANE = 128
VMEM_LIMIT = 56 * 1024 * 1024

ST_BETA, ST_GA, ST_DT, ST_MI, ST_MF = 0, 4, 8, 16, 20

_NN = (((1,), (0,)), ((), ()))
_NT = (((1,), (1,)), ((), ()))
_TN = (((0,), (0,)), ((), ()))


def _mm(a, b, dims=_NN):
    return lax.dot_general(a.astype(BF16), b.astype(BF16), dims, preferred_element_type=F32)


def _split(x, n):
    parts, r = [], x
    for i in range(n):
        p = r.astype(BF16)
        parts.append(p)
        if i < n - 1:
            r = r - p.astype(F32)
    return parts


N_PIECE = 2


def _mm01(x, m2):
    pieces = _split(x, N_PIECE)
    k = x.shape[1]
    if x.shape[0] % 16 == 0:
        return lax.dot_general(jnp.concatenate(pieces, axis=1), m2.astype(BF16), _NN, preferred_element_type=F32)
    out = None
    for i, p in enumerate(pieces):
        t = lax.dot_general(p, m2[i * k:(i + 1) * k, :].astype(BF16), _NN, preferred_element_type=F32)
        out = t if out is None else out + t
    return out


def _m01m(m2, x):
    pieces = _split(x, N_PIECE)
    k = x.shape[0]
    if k % 16 == 0:
        return lax.dot_general(m2, jnp.concatenate(pieces, axis=0), _NN, preferred_element_type=F32)
    out = None
    for i, p in enumerate(pieces):
        t = lax.dot_general(m2[:, i * k:(i + 1) * k], p, _NN, preferred_element_type=F32)
        out = t if out is None else out + t
    return out


def _sigmoid(x):
    return jax.nn.sigmoid(x)


def _silu(x):
    return x * jax.nn.sigmoid(x)


def _softplus(x):
    return jnp.maximum(x, 0.0) + jnp.log(1.0 + jnp.exp(-jnp.abs(x)))


def _logsig(x):
    return jnp.minimum(x, 0.0) - jnp.log(1.0 + jnp.exp(-jnp.abs(x)))


def _row_dtype(rows):
    return BF16 if rows % 16 == 0 else F32


def _expand(base, heads, width):
    e = np.zeros((LANE, heads * width), np.float32)
    for h in range(heads):
        e[base + h, h * width:(h + 1) * width] = 1.0
    return e


def _cat_segments(segs):
    cols, off, pos, seen = [], {}, 0, {}
    for name, m in segs:
        key = m.tobytes() + bytes(str(m.shape), 'ascii')
        if key in seen:
            off[name] = seen[key]
            continue
        w = m.shape[1]
        wp = -(-w // LANE) * LANE
        mp = np.zeros((m.shape[0], wp), np.float32)
        mp[:, :w] = m
        cols.append(mp)
        off[name] = seen[key] = (pos, w)
        pos += wp
    return np.concatenate(cols, axis=1), off


@functools.lru_cache(maxsize=None)
def _mixer_consts(L, nv):
    J = int(round(math.log2(L)))
    assert 1 << J == L
    f = np.float32
    r = np.arange(L)
    tri = (r[None, :] <= r[:, None])
    rev = (r[:, None] < r[None, :]) & (r[None, :] <= nv - 1)
    strict = (r[None, :] < r[:, None])
    eye = np.eye(L, dtype=bool)
    lvl, ab = [], []
    for j in range(J):
        bnd = ((r >> (j + 1)) << (j + 1)) + (1 << j) - 1
        low = ((r >> j) & 1) == 1
        a = low[:, None] & (bnd[:, None] < r[None, :]) & (r[None, :] <= r[:, None])
        b = (~low)[:, None] & (r[:, None] < r[None, :]) & (r[None, :] <= bnd[:, None])
        ab.append(a | b)
        same = (r[:, None] >> (j + 1)) == (r[None, :] >> (j + 1))
        lvl.append(same & low[:, None] & (~low)[None, :])

    def tile(m, n):
        return np.tile(m.astype(f), (1, n))

    pr = np.repeat(np.arange(2), L)
    lr2 = np.tile(r, 2)
    ch = np.arange(LANE)
    ch2 = np.arange(2 * LANE)
    c = dict(
        trirev=np.concatenate([tri, rev], axis=0).astype(f),
        mhg=np.concatenate(ab + [tri, rev], axis=0).astype(f),
        ones_ll=np.ones((L, L), f), tri=tri.astype(f),
        tril2=tile(tri, 2), strict2=tile(strict, 2), tril4=tile(tri, 4),
        eye4=tile(eye, 4), supper16=tile(strict, 16),
        e_dr=np.concatenate([_expand(ST_GA, 4, L), _expand(ST_DT, SSD_H, L), _expand(ST_MF, 4, L)], axis=1),
        hmaskp=np.stack([tile(eye, 2)] + [tile(m, 2) for m in lvl]),
        eyep=np.eye(2 * L, dtype=f),
        lvlp=np.stack([(pr[:, None] == pr[None, :]) & m[lr2[:, None], lr2[None, :]] for m in lvl]).astype(f),
        stkp=(pr[:, None] == ch[None, :] // HEAD_D).astype(f),
        bdp=(ch[:, None] // HEAD_D == ch[None, :] // HEAD_D).astype(f),
        pm2=np.stack([ch < HEAD_D, ch >= HEAD_D]).astype(f),
        stk4s=(np.repeat(np.arange(4), L)[:, None] == ch[None, :] // SSD_P).astype(f),
        gbd=(ch[:, None] // SSD_N == ch2[None, :] // LANE).astype(f),
        rowvalid=(r[:, None] <= nv - 1).astype(f) * np.ones((1, LANE), f),
        bd64=(ch2[:, None] // 64 == ch2[None, :] // 64).astype(f),
        bd128=(ch2[:, None] // 128 == ch2[None, :] // 128).astype(f),
    )
    c['e_act'], off_act = _cat_segments([('b_r', _expand(ST_BETA, 4, L)), ('b_c', _expand(ST_BETA, 4, HEAD_D)),
                                         ('dt_c', _expand(ST_DT, SSD_H, SSD_P)),
                                         ('i_r', _expand(ST_MI, 4, L)), ('i_c', _expand(ST_MI, 4, HEAD_D))])
    c['e_cs'], off_cs = _cat_segments([('g_c', _expand(ST_GA, 4, HEAD_D)), ('s_c', _expand(ST_DT, SSD_H, SSD_P)),
                                       ('m_r', _expand(ST_MF, 4, L)), ('m_c', _expand(ST_MF, 4, HEAD_D))])
    c['e_rev'], off_rev = _cat_segments([('g_c', _expand(ST_GA, 4, HEAD_D)), ('s_c', _expand(ST_DT, SSD_H, SSD_P)),
                                         ('m_c', _expand(ST_MF, 4, HEAD_D))])
    c['bdp2'], c['stkp2'] = c['bdp'], c['stkp']
    for name in ('trirev', 'mhg', 'ones_ll'):
        c[name] = np.tile(c[name], (1, N_PIECE))
    for name in ('e_act', 'e_cs', 'e_rev', 'bdp2', 'stkp2'):
        c[name] = np.tile(c[name], (N_PIECE, 1))
    return c, dict(act=off_act, cs=off_cs, rev=off_rev), J


_CONST_ORDER = ('trirev', 'mhg', 'ones_ll', 'tri', 'tril2', 'strict2', 'tril4', 'eye4', 'supper16', 'e_dr', 'hmaskp',
                'eyep', 'lvlp', 'stkp', 'bdp', 'pm2', 'stk4s', 'gbd', 'rowvalid', 'bd64', 'bd128', 'e_act', 'e_cs',
                'e_rev', 'bdp2', 'stkp2')
_BF16_CONSTS = ('trirev', 'mhg', 'ones_ll', 'tri', 'supper16', 'e_dr', 'lvlp', 'stkp', 'stk4s', 'bd64', 'bd128',
                'e_act', 'e_cs', 'e_rev', 'bdp2', 'stkp2')


def _conv_silu(ext, w, b, L):
    y = b
    for i in range(CONV_K):
        y = y + ext[5 + i:5 + i + L, :] * w[i:i + 1, :]
    return _silu(y)


def _run_interleaved(tasks):
    tasks = list(tasks)
    while tasks:
        alive = []
        for t in tasks:
            try:
                next(t)
                alive.append(t)
            except StopIteration:
                pass
        tasks = alive


def _mixer_chunk(pm, ext_g, ext_s, st, P, K, offs, L, nv, J, layer, out):
    lane = lax.broadcasted_iota(jnp.int32, (1, LANE), 1)
    rowvalid = K['rowvalid'][:, 0:1]
    tril2 = K['tril2'] > 0.0
    stkp, bdp = K['stkp'], K['bdp']
    upper_c = lane >= HEAD_D
    upper_r = lax.broadcasted_iota(jnp.int32, (1, 2 * L), 1) >= L

    def seg(x, which, name):
        o, w = offs[which][name]
        return x[:, o:o + w]

    def pair_c(x, p):
        return x[:, p * LANE:(p + 1) * LANE]

    def pair_r(x, p):
        return x[:, p * 2 * L:(p + 1) * 2 * L]

    def stack(x):
        xb = x.astype(BF16)
        return jnp.concatenate([xb, xb], axis=0) * stkp

    z = pm[:, SMALL_OFF:SMALL_OFF + LANE] + P['sp'][0:1, :]
    act = jnp.where(lane < ST_GA, _sigmoid(z),
                    jnp.where(lane < ST_MI, _softplus(z), jnp.where(lane < ST_MF, z, _logsig(z))))
    neg_a = -jnp.exp(P['sp'][1:2, :])
    dec_in = jnp.where((lane >= ST_GA) & (lane < ST_MI), neg_a * act,
                       jnp.where((lane >= ST_MF) & (lane < ST_MF + 4), act, 0.0))
    cr = _m01m(K['trirev'], dec_in)
    yield
    cs, rev = cr[:L], cr[L:]
    ea = _mm01(act, K['e_act'])
    ec = _mm01(cs, K['e_cs'])
    er = _mm01(rev, K['e_rev'])
    yield
    m_r = seg(ec, 'cs', 'm_r')
    g_c, s_c, m_c = seg(ec, 'cs', 'g_c'), seg(ec, 'cs', 's_c'), seg(ec, 'cs', 'm_c')
    b_r, b_c, dt_c = seg(ea, 'act', 'b_r'), seg(ea, 'act', 'b_c'), seg(ea, 'act', 'dt_c')
    i_r, i_c = seg(ea, 'act', 'i_r'), seg(ea, 'act', 'i_c')
    rg_c, rs_c, rm_c = seg(er, 'rev', 'g_c'), seg(er, 'rev', 's_c'), seg(er, 'rev', 'm_c')
    ym = _mm(dec_in, K['e_dr']).astype(BF16) * K['supper16']
    yield
    d_all = lax.dot_general(K['tri'], ym, _NN, preferred_element_type=F32)
    d_g, d_s, d_m = d_all[:, 0:4 * L], d_all[:, 4 * L:12 * L], d_all[:, 12 * L:16 * L]
    row_i = _m01m(K['ones_ll'], i_r * K['eye4'])

    res = dict(o_gdn=[None, None], gdn=[None, None], o_hg=[None, None], hg=[None, None],
               o_ml=[None, None], mlc=[None, None], mln=[None, None], mlm=[None] * N_HEAD)

    qkv = _conv_silu(ext_g, P['cwg'], P['cbg'], L)
    q, k, v = qkv[:, 0:256], qkv[:, 256:512], qkv[:, 512:768]
    ss = _mm(jnp.concatenate([q * q, k * k], axis=0), K['bd64'])
    yield
    q = q * (lax.rsqrt(ss[:L] + EPS) * (HEAD_D ** -0.5))
    k = k * lax.rsqrt(ss[L:] + EPS)

    def gdn_task(p):
        qt, kt, vt = pair_c(q, p), pair_c(k, p), pair_c(v, p)
        beta_c, gc = pair_c(b_c, p), pair_c(g_c, p)
        eg = jnp.exp(gc)
        dec = jnp.exp(pair_r(d_g, p)) * K['tril2']
        kkqk = _mm(jnp.concatenate([kt, qt], axis=0), stack(kt), _NT)
        yield
        n = (kkqk[:L] * dec * pair_r(b_r, p) * K['strict2']).astype(BF16)
        nbd = jnp.concatenate([n, n], axis=0)
        t = K['eyep'] - (nbd * K['lvlp'][0]).astype(F32)
        for j in range(1, J):
            x = _mm(t, nbd * K['lvlp'][j])
            yield
            t = t - _mm(x, t)
            yield
        sol = _mm(t, jnp.concatenate([stack(vt * beta_c), stack(kt * (beta_c * eg))], axis=1))
        yield
        sol = sol[:L] + sol[L:]
        s_p = st['gdn'][p]
        u = sol[:, 0:LANE] - _mm(sol[:, LANE:2 * LANE], s_p)
        yield
        qo = jnp.concatenate([(qt * eg).astype(BF16), (kkqk[L:] * dec).astype(BF16)], axis=1)
        res['o_gdn'][p] = _mm(qo, jnp.concatenate([s_p.astype(BF16), stack(u)], axis=0))
        kw = kt * (jnp.exp(pair_c(rg_c, p)) * rowvalid)
        res['gdn'][p] = s_p * jnp.exp(gc[nv - 1:nv, :]) + bdp * _mm(kw, u, _TN)
        yield

    lg = P['lg']
    mx = lg[0:1, :]
    for i in range(1, DEPTH):
        mx = jnp.maximum(mx, lg[i:i + 1, :])
    ex = [jnp.exp(lg[i:i + 1, :] - mx) for i in range(DEPTH)]
    tot = ex[0]
    for i in range(1, DEPTH):
        tot = tot + ex[i]
    sm = [e / tot for e in ex]
    cum = sm[0]
    for i in range(1, layer + 1):
        cum = cum + sm[i]
    lb = cum - sm[0]
    lb_pos = lb > 0
    log_lb = jnp.log(jnp.where(lb_pos, lb, 1.0))
    hq = _silu(pm[:, 1024:1280])
    fz = pm[:, 1280:1536]
    hv = pm[:, 1536:1792]
    ls = _logsig(fz)
    t2 = jnp.log1p(-lb) + ls
    la = jnp.maximum(log_lb, t2) + jnp.log1p(jnp.exp(-jnp.abs(log_lb - t2)))
    logf = jnp.where(lb_pos, la, ls)
    kg = (1.0 - lb) * _sigmoid(-fz)
    ey = jnp.exp(_m01m(K['mhg'], logf))
    yield
    eg_h = ey[J * L:(J + 1) * L]
    eyb = ey[0:J * L].astype(BF16)
    hqb = hq.astype(BF16)
    qe = hq * eg_h
    kwr = kg * (ey[(J + 1) * L:(J + 2) * L] * rowvalid)

    def hg_task(p):
        q_p, k_st = pair_c(hqb, p), stack(pair_c(kg, p))
        a_p = K['hmaskp'][0] * _mm(q_p, k_st, _NT)
        yield
        for j in range(J):
            e_j = pair_c(eyb[j * L:(j + 1) * L], p)
            a_p = a_p + K['hmaskp'][j + 1] * _mm(q_p * e_j, k_st * jnp.concatenate([e_j, e_j], axis=0), _NT)
            yield
        st_p = st['hg'][p]
        res['o_hg'][p] = _mm(pair_c(qe, p), st_p, _NT) + _mm(a_p, stack(pair_c(hv, p)))
        yield
        res['hg'][p] = (st_p * pair_c(eg_h[nv - 1:nv, :], p)
                        + bdp * _mm(pair_c(hv, p), pair_c(kwr, p), _TN))
        yield

    xbc = _conv_silu(ext_s, P['cws'], P['cbs'], L)
    xs, bs, cc = xbc[:, 0:256], xbc[:, 256:384], xbc[:, 384:512]
    vs = xs * dt_c
    bsb = bs.astype(BF16)
    bs4 = jnp.concatenate([bsb] * 4, axis=0)

    def ssd_task():
        o_intra = []
        for g in range(SSD_G):
            cbw = _mm(cc * K['pm2'][g:g + 1, :], bs4, _NT)
            dec = jnp.exp(d_s[:, g * 4 * L:(g + 1) * 4 * L]) * K['tril4']
            vb = pair_c(vs, g).astype(BF16)
            o_intra.append(_mm(cbw * dec, jnp.concatenate([vb] * 4, axis=0) * K['stk4s']))
            yield
        s_all = st['ssd']
        res['o_ssd'] = jnp.exp(s_c) * _mm(cc, s_all) + jnp.concatenate(o_intra, axis=1)
        res['ssd'] = (s_all * jnp.exp(s_c[nv - 1:nv, :])
                      + K['gbd'] * _mm(bs * rowvalid, vs * jnp.exp(rs_c), _TN))
        yield

    mq = pm[:, 2816:3072] * (HEAD_D ** -0.5)
    mk = pm[:, 3072:3328]
    mv = pm[:, 3328:3584]

    def ml_task(p, delay):
        for _ in range(delay):
            yield
        qt, kt, vt = pair_c(mq, p), pair_c(mk, p), pair_c(mv, p)
        bm_r, bm_c = pair_r(m_r, p), pair_c(m_c, p)
        mp0 = st['mlm'][:, 2 * p:2 * p + 1]
        mp1 = st['mlm'][:, 2 * p + 1:2 * p + 2]
        logw = jnp.where(tril2, pair_r(d_m, p) + pair_r(row_i, p), NEG)
        mx0 = jnp.max(jnp.where(upper_r, NEG, logw), axis=-1, keepdims=True)
        mx1 = jnp.max(jnp.where(upper_r, logw, NEG), axis=-1, keepdims=True)
        l0_r = bm_r + jnp.where(upper_r, mp1, mp0)
        l0_c = bm_c + jnp.where(upper_c, mp1, mp0)
        mt_r = jnp.maximum(l0_r, jnp.where(upper_r, mx1, mx0))
        mt_c = jnp.maximum(l0_c, jnp.where(upper_c, mx1, mx0))
        w = jnp.exp(logw - mt_r)
        w0 = jnp.exp(l0_c - mt_c)
        qk = _mm(qt, stack(kt), _NT) * w
        yield
        c_p = st['mlc'][p]
        n_p = st['mln'][p]
        qn = jnp.concatenate([(w0 * qt).astype(BF16), qk.astype(BF16)], axis=1)
        num = _mm(qn, jnp.concatenate([c_p.astype(BF16), stack(vt)], axis=0))
        den = w0 * _mm01(qt * n_p, K['bdp2']) + _mm01(qk, K['stkp2'])
        res['o_ml'][p] = num * (1.0 / jnp.maximum(jnp.abs(den), jnp.exp(-mt_c)))
        yield
        m_l = mt_c[nv - 1:nv, :]
        wl0 = jnp.exp(bm_c[nv - 1:nv, :] + jnp.where(upper_c, mp1, mp0) - m_l)
        kwl = kt * (jnp.exp(pair_c(rm_c, p) + pair_c(i_c, p) - m_l) * rowvalid)
        res['mlc'][p] = c_p * wl0 + bdp * _mm(kwl, vt, _TN)
        res['mln'][p] = n_p * wl0 + jnp.sum(kwl, axis=0, keepdims=True)
        res['mlm'][2 * p] = m_l[:, 0:1]
        res['mlm'][2 * p + 1] = m_l[:, HEAD_D:HEAD_D + 1]
        yield

    def finish():
        o_gdn = jnp.concatenate(res['o_gdn'], axis=1)
        o_hg = jnp.concatenate(res['o_hg'], axis=1)
        hh = jnp.concatenate(res['o_ml'], axis=1)
        ms = _mm(jnp.concatenate([o_gdn * o_gdn, o_hg * o_hg, hh * hh], axis=0), K['bd64']) * (1.0 / HEAD_D)
        yield
        out_a = o_gdn * lax.rsqrt(ms[0:L] + EPS) * P['vec'][0:1, :] * _silu(pm[:, 768:1024])
        out_b = o_hg * lax.rsqrt(ms[L:2 * L] + EPS) * P['vec'][1:2, :] * _silu(pm[:, 1792:2048])
        out_d = hh * lax.rsqrt(ms[2 * L:3 * L] + EPS) * P['vec'][3:4, :] * _sigmoid(pm[:, 3584:3840])
        ys = (res['o_ssd'] + P['vec'][4:5, :] * xs) * _silu(pm[:, 2048:2304])
        out_c = ys * lax.rsqrt(_mm(ys * ys, K['bd128']) * (1.0 / (2 * HEAD_D)) + EPS) * P['vec'][2:3, :]
        branches = jnp.concatenate([out_a, out_b, out_c, out_d], axis=1)
        new = dict(gdn=res['gdn'], hg=res['hg'], ssd=res['ssd'], mlc=res['mlc'], mln=res['mln'], mlm=res['mlm'])
        out['result'] = (branches, new)
        yield

    out['tasks'] = ([gdn_task(p) for p in range(2)] + [hg_task(p) for p in range(2)]
                    + [ssd_task()] + [ml_task(p, 3 + 2 * p) for p in range(2)])
    out['finish'] = finish


def _mixer_kernel(L, nv, J, layer, bb, offs, has_init, n_alias, *refs):
    it = iter(refs)
    pm_ref = next(it)
    if has_init:
        gdn0, cg0, hg0, ssd0, cs0, mc0, mn0, mm0 = (next(it) for _ in range(8))
    sp_ref, cwg_ref, cbg_ref, cws_ref, cbs_ref, vec_ref, lg_ref = (next(it) for _ in range(7))
    kref = {name: next(it) for name in _CONST_ORDER}
    for _ in range(n_alias):
        next(it)
    br_ref = next(it)
    gdn1, cg1, hg1, ssd1, cs1, mc1, mn1, mm1 = (next(it) for _ in range(8))
    sg, sh, sc, sn, ss, extg, exts, pmpad = (next(it) for _ in range(8))

    c = pl.program_id(1)
    nc = pl.num_programs(1)
    n_sub = SSD_H // SSD_G

    @pl.when(c == 0)
    def _init():
        sg[...] = jnp.zeros_like(sg)
        sh[...] = jnp.zeros_like(sh)
        sc[...] = jnp.zeros_like(sc)
        ss[...] = jnp.zeros_like(ss)
        extg[:, 0:8, :] = jnp.zeros((bb, 8, GDN_CONV_W), F32)
        exts[:, 0:8, :] = jnp.zeros((bb, 8, SSD_CONV_W), F32)
        if not has_init:
            sn[...] = jnp.zeros_like(sn)
            mm1[...] = jnp.zeros_like(mm1)
            return
        for s in range(bb):
            for h in range(N_HEAD):
                p, lo = h // 2, (h % 2) * HEAD_D
                sg[s, p, lo:lo + HEAD_D, lo:lo + HEAD_D] = gdn0[s, h]
                sh[s, p, lo:lo + HEAD_D, lo:lo + HEAD_D] = hg0[s, h].T
                sc[s, p, lo:lo + HEAD_D, lo:lo + HEAD_D] = mc0[s, h]
                sn[s, p, :, lo:lo + HEAD_D] = mn0[s, h:h + 1, :]
            for h in range(SSD_H):
                g = h // n_sub
                ss[s, g * SSD_N:(g + 1) * SSD_N, h * SSD_P:(h + 1) * SSD_P] = ssd0[s, h]
        mm1[...] = mm0[...]
        extg[:, 5:8, :] = cg0[...]
        exts[:, 5:8, :] = cs0[...]

    P = dict(sp=sp_ref[...], cwg=cwg_ref[...], cbg=cbg_ref[...], cws=cws_ref[...], cbs=cbs_ref[...],
             vec=vec_ref[...], lg=lg_ref[...])
    K = {name: r[...] for name, r in kref.items()}
    stage1, outs = [], []
    spg = L // nv
    for s in range(bb):
        grp, off = s // spg, (s % spg) * nv
        if nv == L:
            pm = pm_ref[s]
        else:
            pmpad[s, 0:nv, :] = pm_ref[grp, off:off + nv, :]
            pmpad[s, nv:L, :] = jnp.zeros((L - nv, MIX_W), F32)
            pm = pmpad[s]
        extg[s, 8:8 + L, :] = pm[:, 0:GDN_CONV_W]
        exts[s, 8:8 + L, :] = pm[:, SSD_XBC_OFF:SSD_XBC_OFF + SSD_CONV_W]
        st = dict(gdn=sg[s], hg=sh[s], ssd=ss[s], mlc=sc[s], mln=sn[s], mlm=mm1[s])
        outs.append({})
        stage1.append(_mixer_chunk(pm, extg.at[s], exts.at[s], st, P, K, offs, L, nv, J, layer, outs[s]))

    def flow(s):
        for _ in range(FLOW_DELAY[L] * s):
            yield
        yield from stage1[s]
        tasks = list(outs[s]['tasks'])
        while tasks:
            alive = []
            for t in tasks:
                try:
                    next(t)
                    alive.append(t)
                except StopIteration:
                    pass
            tasks = alive
            yield
        yield from outs[s]['finish']()

    _run_interleaved([flow(s) for s in range(bb)])
    for s in range(bb):
        branches, new = outs[s]['result']
        grp, off = s // spg, (s % spg) * nv
        br_ref[grp, off:off + nv, :] = branches[0:nv].astype(br_ref.dtype)
        for p in range(2):
            sg[s, p] = new['gdn'][p]
            sh[s, p] = new['hg'][p]
            sc[s, p] = new['mlc'][p]
            sn[s, p] = new['mln'][p]
        for h in range(N_HEAD):
            mm1[s, :, h:h + 1] = new['mlm'][h]
        ss[s] = new['ssd']
        tail_g = extg[s, 8 + nv - 3:8 + nv, :]
        tail_s = exts[s, 8 + nv - 3:8 + nv, :]
        extg[s, 5:8, :] = tail_g
        exts[s, 5:8, :] = tail_s
        cg1[s] = tail_g
        cs1[s] = tail_s

    @pl.when(c == nc - 1)
    def _fin():
        for s in range(bb):
            for h in range(N_HEAD):
                p, lo = h // 2, (h % 2) * HEAD_D
                gdn1[s, h] = sg[s, p, lo:lo + HEAD_D, lo:lo + HEAD_D]
                hg1[s, h] = sh[s, p, lo:lo + HEAD_D, lo:lo + HEAD_D].T
                mc1[s, h] = sc[s, p, lo:lo + HEAD_D, lo:lo + HEAD_D]
                mn1[s, h:h + 1, :] = sn[s, p, :, lo:lo + HEAD_D]
            for h in range(SSD_H):
                g = h // n_sub
                ssd1[s, h] = ss[s, g * SSD_N:(g + 1) * SSD_N, h * SSD_P:(h + 1) * SSD_P]


def _full_spec(a):
    nd = a.ndim
    return pl.BlockSpec(a.shape, lambda b, c, _nd=nd: (0,) * _nd)


def _mixer_call(proj, state_shapes, states_in, prev_out, params, L, nv, layer, bb):
    g, t, _ = proj.shape
    spg = L // nv if nv < L else 1
    nchunk = t // L
    assert (g * spg) % bb == 0 and bb % spg == 0 and t % L == 0
    consts, offs, J = _mixer_consts(L, nv)
    const_arrays = [jnp.asarray(consts[n], BF16 if n in _BF16_CONSTS else F32) for n in _CONST_ORDER]

    def st_spec(shape):
        nd = len(shape)
        return pl.BlockSpec((None, bb) + tuple(shape[2:]), lambda b, c, _nd=nd: (layer, b) + (0,) * (_nd - 2))

    has_init = states_in is not None
    n_alias = 0 if prev_out is None else len(prev_out)
    inputs = [proj] + (list(states_in) if has_init else []) + list(params) + const_arrays + list(prev_out or [])
    in_specs = ([pl.BlockSpec((bb // spg, L, MIX_W), lambda b, c: (b, c, 0))]
                + ([st_spec(s) for s in state_shapes] if has_init else [])
                + [_full_spec(a) for a in params]
                + [_full_spec(a) for a in const_arrays]
                + [pl.BlockSpec(memory_space=pl.ANY)] * n_alias)
    out_shape = ([jax.ShapeDtypeStruct((g, t, N_BRANCH * BRANCH_W), _row_dtype(L))]
                 + [jax.ShapeDtypeStruct(s, F32) for s in state_shapes])
    out_specs = ([pl.BlockSpec((bb // spg, L, N_BRANCH * BRANCH_W), lambda b, c: (b, c, 0))]
                 + [st_spec(s) for s in state_shapes])
    first_alias = len(inputs) - n_alias
    pair = (bb, 2, LANE, LANE)
    scratch = [pltpu.VMEM(pair, F32), pltpu.VMEM(pair, F32), pltpu.VMEM(pair, F32),
               pltpu.VMEM((bb, 2, 1, LANE), F32), pltpu.VMEM((bb, SSD_G * SSD_N, SSD_H * SSD_P), F32),
               pltpu.VMEM((bb, 8 + L, GDN_CONV_W), F32), pltpu.VMEM((bb, 8 + L, SSD_CONV_W), F32),
               pltpu.VMEM((bb, L, MIX_W) if nv < L else (1, 8, LANE), F32)]
    outs = pl.pallas_call(
        functools.partial(_mixer_kernel, L, nv, J, layer, bb, offs, has_init, n_alias),
        grid=(g * spg // bb, nchunk),
        in_specs=in_specs, out_specs=out_specs, out_shape=out_shape, scratch_shapes=scratch,
        input_output_aliases={first_alias + k: 1 + k for k in range(n_alias)},
        compiler_params=pltpu.CompilerParams(dimension_semantics=("parallel", "arbitrary"),
                                             vmem_limit_bytes=VMEM_LIMIT),
        name=f"mixer_L{L}",
    )(*inputs)
    return outs[0], list(outs[1:])


def _ada_kernel(c_ref, w_ref, b_ref, o_ref):
    o_ref[...] = _mm(_silu(c_ref[...]), w_ref[...]) + b_ref[...]


def _ada_call(c_all, ada_w, ada_b):
    rows = c_all.shape[0]
    n = ada_w.shape[-1]
    tn = 1536
    return pl.pallas_call(
        _ada_kernel,
        grid=(DEPTH, n // tn),
        in_specs=[pl.BlockSpec((rows, D_MODEL), lambda l, j: (0, 0)),
                  pl.BlockSpec((None, D_MODEL, tn), lambda l, j: (l, 0, j)),
                  pl.BlockSpec((None, 1, tn), lambda l, j: (l, 0, j))],
        out_specs=pl.BlockSpec((None, rows, tn), lambda l, j: (l, 0, j)),
        out_shape=jax.ShapeDtypeStruct((DEPTH, rows, n), F32),
        compiler_params=pltpu.CompilerParams(dimension_semantics=("arbitrary", "arbitrary"),
                                             vmem_limit_bytes=VMEM_LIMIT),
        name="ada",
    )(c_all, ada_w, ada_b.reshape(DEPTH, 1, n))


PREP_ROWS = MIX_W - SMALL_OFF


def _prep_w_in_kernel(w_ref, s0_ref, s1_ref, s2_ref, o_ref):
    j = pl.program_id(1)
    j_small = SMALL_OFF // PREP_ROWS

    @pl.when(j != j_small)
    def _copy():
        o_ref[...] = w_ref[0].astype(BF16)

    @pl.when(j == j_small)
    def _small():
        o_ref[...] = jnp.zeros(o_ref.shape, BF16)
        o_ref[0:8, :] = s0_ref[0].astype(BF16)
        o_ref[8:16, :] = s1_ref[0].astype(BF16)
        o_ref[16:24, :] = s2_ref[0].astype(BF16)


def _prep_src_row(j):
    r = PREP_ROWS
    return jnp.where(j < 1024 // r, r * j,
                     jnp.where(j < 2816 // r, r * j + 8,
                               jnp.where(j < SMALL_OFF // r, r * j + 16,
                                         jnp.where(j < MIX_W // r, 0, r * j - MIX_W + 3864))))


def _prep_w_in_call(w_in):
    wt = jnp.swapaxes(w_in, 1, 2)
    depth, _, d = wt.shape

    def rows(n, index):
        return pl.BlockSpec((pl.Element(1), pl.Element(n), pl.Element(d)), index)

    return pl.pallas_call(
        _prep_w_in_kernel,
        grid=(depth, PROJ_W // PREP_ROWS),
        in_specs=[rows(PREP_ROWS, lambda l, j: (l, pl.multiple_of(_prep_src_row(j), 8), 0)),
                  rows(8, lambda l, j: (l, 1024, 0)), rows(8, lambda l, j: (l, 2824, 0)),
                  rows(8, lambda l, j: (l, 3856, 0))],
        out_specs=pl.BlockSpec((None, PREP_ROWS, d), lambda l, j: (l, j, 0)),
        out_shape=jax.ShapeDtypeStruct((depth, PROJ_W, d), BF16),
        compiler_params=pltpu.CompilerParams(dimension_semantics=("arbitrary", "arbitrary"),
                                             vmem_limit_bytes=VMEM_LIMIT),
        name="prep_w_in",
    )(wt, wt, wt, wt)


def _rms_mod(x, nw, sc, sh):
    ms = jnp.mean(x * x, axis=-1, keepdims=True)
    return x * lax.rsqrt(ms + EPS) * nw * (1.0 + sc) + sh


def _tok_spec(gblk, width):
    g, r = gblk
    return pl.BlockSpec((g, r, width), lambda i, j: (i, j, 0))


def _mod_spec(gblk, layer, k, mods):
    return pl.BlockSpec((None, gblk[0], mods.shape[2], D_MODEL), lambda i, j: (layer, i, 0, k))


def _mod_rows(m, r):
    spg = m.shape[1]
    if spg == 1:
        return m
    seq = lax.broadcasted_iota(jnp.int32, (1, r, 1), 1) // (r // spg)
    out = m[:, 0:1, :]
    for j in range(1, spg):
        out = jnp.where(seq == j, m[:, j:j + 1, :], out)
    return out


def _layer_spec(shape, layer):
    nd = len(shape)
    return pl.BlockSpec((None,) + tuple(shape[1:]), lambda i, j: (layer,) + (0,) * (nd - 1),
                        pipeline_mode=pl.Buffered(1))


def _dense_params():
    return pltpu.CompilerParams(dimension_semantics=("arbitrary", "arbitrary"), vmem_limit_bytes=VMEM_LIMIT)


def _inproj_kernel(x_ref, nw_ref, sc_ref, sh_ref, w_ref, pm_ref, gate_ref):
    g, r, _ = x_ref.shape
    h = _rms_mod(x_ref[...], nw_ref[...], _mod_rows(sc_ref[...], r), _mod_rows(sh_ref[...], r))
    h = h.astype(BF16).reshape(g * r, D_MODEL)
    pm_ref[...] = _mm(h, w_ref[0:MIX_W, :], _NT).reshape(g, r, MIX_W)
    gates = _sigmoid(_mm(h, w_ref[MIX_W:PROJ_W, :], _NT))
    gate_ref[...] = gates.astype(gate_ref.dtype).reshape(g, r, GATE_W)


def _inproj_call(x, layer, nw, mods, w, gblk):
    bg, t, _ = x.shape
    return pl.pallas_call(
        _inproj_kernel,
        grid=(bg // gblk[0], t // gblk[1]),
        in_specs=[_tok_spec(gblk, D_MODEL), _layer_spec(nw.shape, layer),
                  _mod_spec(gblk, layer, 1, mods), _mod_spec(gblk, layer, 0, mods), _layer_spec(w.shape, layer)],
        out_specs=[_tok_spec(gblk, MIX_W), _tok_spec(gblk, GATE_W)],
        out_shape=[jax.ShapeDtypeStruct((bg, t, MIX_W), F32),
                   jax.ShapeDtypeStruct((bg, t, GATE_W), _row_dtype(gblk[1]))],
        compiler_params=_dense_params(), name="inproj",
    )(x, nw, mods, mods, w)


def _merge_ffn_kernel(final, br_ref, gate_ref, x_ref, gt1_ref, nw_ref, sc_ref, sh_ref, gt2_ref,
                      wb_ref, wo_ref, w1_ref, w2_ref, fw_ref, o_ref):
    g, r, _ = x_ref.shape
    br = br_ref[...].reshape(g * r, N_BRANCH * BRANCH_W)
    merged = None
    for n in range(N_BRANCH):
        up = _mm(br[:, n * BRANCH_W:(n + 1) * BRANCH_W], wb_ref[n])
        gate = gate_ref[:, :, n * D_MODEL:(n + 1) * D_MODEL].astype(F32).reshape(g * r, D_MODEL)
        t = gate * up
        merged = t if merged is None else merged + t
    x1 = x_ref[...] + _mod_rows(gt1_ref[...], r) * _mm(merged, wo_ref[...]).reshape(g, r, D_MODEL)
    h = _rms_mod(x1, nw_ref[...], _mod_rows(sc_ref[...], r), _mod_rows(sh_ref[...], r))
    h = h.astype(BF16).reshape(g * r, D_MODEL)
    a = _mm(h, w1_ref[:, 0:D_FF])
    b = _mm(h, w1_ref[:, D_FF:2 * D_FF])
    x2 = x1 + _mod_rows(gt2_ref[...], r) * _mm(_silu(a) * b, w2_ref[...]).reshape(g, r, D_MODEL)
    if final:
        ms = jnp.mean(x2 * x2, axis=-1, keepdims=True)
        x2 = x2 * lax.rsqrt(ms + EPS) * fw_ref[...]
    o_ref[...] = x2


def _merge_ffn_call(br, gates, x, layer, nw, mods, wb, wo, w1, w2, fw, gblk, final):
    bg, t, _ = x.shape
    return pl.pallas_call(
        functools.partial(_merge_ffn_kernel, final),
        grid=(bg // gblk[0], t // gblk[1]),
        in_specs=[_tok_spec(gblk, N_BRANCH * BRANCH_W), _tok_spec(gblk, GATE_W), _tok_spec(gblk, D_MODEL),
                  _mod_spec(gblk, layer, 2, mods), _layer_spec(nw.shape, layer),
                  _mod_spec(gblk, layer, 4, mods), _mod_spec(gblk, layer, 3, mods), _mod_spec(gblk, layer, 5, mods),
                  _layer_spec(wb.shape, layer), _layer_spec(wo.shape, layer),
                  _layer_spec(w1.shape, layer), _layer_spec(w2.shape, layer),
                  pl.BlockSpec(fw.shape, lambda i, j: (0, 0), pipeline_mode=pl.Buffered(1))],
        out_specs=_tok_spec(gblk, D_MODEL),
        out_shape=jax.ShapeDtypeStruct((bg, t, D_MODEL), F32),
        compiler_params=_dense_params(), name="merge_ffn",
    )(br, gates, x, mods, nw, mods, mods, mods, wb, wo, w1, w2, fw)


def _small_params(gdn_dt_bias, ssd_dt_bias, ml_b_i, ml_b_f, gdn_a_log, ssd_a_log):
    z4 = jnp.zeros((4,), F32)
    bias = jnp.concatenate([z4, gdn_dt_bias, ssd_dt_bias, ml_b_i, ml_b_f, jnp.zeros((LANE - 24,), F32)])
    alog = jnp.concatenate([z4, gdn_a_log, ssd_a_log, jnp.zeros((LANE - 16,), F32)])
    return jnp.concatenate([bias[None], alog[None], jnp.zeros((6, LANE), F32)], axis=0)


def _layer_params(l, gdn_conv_w, gdn_conv_b, gdn_a_log, gdn_dt_bias, gdn_norm_w, hg_lb_logits, hg_norm_w,
                  ssd_conv_w, ssd_conv_b, ssd_a_log, ssd_dt_bias, ssd_d, ssd_norm_w, ml_b_i, ml_b_f, ml_norm_w):
    sp = _small_params(gdn_dt_bias[l], ssd_dt_bias[l], ml_b_i[l], ml_b_f[l], gdn_a_log[l], ssd_a_log[l])
    vec = jnp.stack([jnp.tile(gdn_norm_w[l], N_HEAD), jnp.tile(hg_norm_w[l], N_HEAD), ssd_norm_w[l],
                     jnp.tile(ml_norm_w[l], N_HEAD), jnp.repeat(ssd_d[l], SSD_P)]
                    + [jnp.zeros((256,), F32)] * 3)
    return [sp, gdn_conv_w[l], gdn_conv_b[l][None], ssd_conv_w[l], ssd_conv_b[l][None], vec,
            hg_lb_logits.astype(F32)]


def _trunk(x, mods, state_shapes, states_in, L, nv, gblk, gblk_ffn, bb, W, mixer_params):
    new_states = None
    for l in range(DEPTH):
        pm, gates = _inproj_call(x, l, W['norm1'], mods, W['w_in'], gblk)
        br, new_states = _mixer_call(pm, state_shapes, states_in, new_states, mixer_params[l], L, nv, l, bb)
        x = _merge_ffn_call(br, gates, x, l, W['norm2'], mods, W['w_branch'], W['w_out'],
                            W['ffn_w_in'], W['ffn_w_out'], W['final'], gblk_ffn, final=(l == DEPTH - 1))
    return x, new_states


def kernel(x_prompt, x_sample, c_prompt, c_sample, state_gdn, state_gdn_conv, state_hgrn, state_ssd, state_ssd_conv, state_mlstm_c, state_mlstm_n, state_mlstm_m, ada_w, ada_b, norm1_w, norm2_w, w_in, gdn_conv_w, gdn_conv_b, gdn_a_log, gdn_dt_bias, gdn_norm_w, hg_lb_logits, hg_norm_w, ssd_conv_w, ssd_conv_b, ssd_a_log, ssd_dt_bias, ssd_d, ssd_norm_w, ml_b_i, ml_b_f, ml_norm_w, w_branch, w_out, ffn_w_in, ffn_w_out, final_norm_w):
    bp, tp, _ = x_prompt.shape
    bs, ts, _ = x_sample.shape
    assert tp % CHUNK == 0 and LS % ts == 0 and bs % (LS // ts) == 0

    W = dict(w_in=_prep_w_in_call(w_in), w_branch=w_branch.astype(BF16), w_out=w_out.astype(BF16),
             ffn_w_in=ffn_w_in.astype(BF16), ffn_w_out=ffn_w_out.astype(BF16),
             norm1=norm1_w[:, None, :], norm2=norm2_w[:, None, :], final=final_norm_w[None])
    mixer_params = [_layer_params(l, gdn_conv_w, gdn_conv_b, gdn_a_log, gdn_dt_bias, gdn_norm_w, hg_lb_logits,
                                  hg_norm_w, ssd_conv_w, ssd_conv_b, ssd_a_log, ssd_dt_bias, ssd_d, ssd_norm_w,
                                  ml_b_i, ml_b_f, ml_norm_w) for l in range(DEPTH)]

    mods = _ada_call(jnp.concatenate([c_prompt, c_sample], axis=0), ada_w, ada_b)
    mods_p = mods[:, :bp].reshape(DEPTH, bp, 1, 6 * D_MODEL)
    grp = LS // ts
    mods_s = mods[:, bp:].reshape(DEPTH, bs // grp, grp, 6 * D_MODEL)
    sample_states = [state_gdn, state_gdn_conv, state_hgrn, state_ssd, state_ssd_conv,
                     state_mlstm_c, state_mlstm_n, state_mlstm_m.reshape(DEPTH, bs, 1, N_HEAD)]
    shapes_s = [s.shape for s in sample_states]
    shapes_p = [(DEPTH, bp) + tuple(s[2:]) for s in shapes_s]

    y_p, new_p = _trunk(x_prompt, mods_p, shapes_p, None, CHUNK, CHUNK, (1, 256), (1, 512), 8, W, mixer_params)
    y_s, new_s = _trunk(x_sample.reshape(bs // grp, LS, D_MODEL), mods_s, shapes_s, sample_states, LS, ts,
                        (32, LS), (32, LS), 8, W, mixer_params)
    new_p[7] = new_p[7].reshape(DEPTH, bp, N_HEAD)
    new_s[7] = new_s[7].reshape(DEPTH, bs, N_HEAD)
    return (y_p, y_s.reshape(bs, ts, D_MODEL)) + tuple(new_p) + tuple(new_s)
```

```python
import functools
import math

import numpy as np
import jax
import jax.numpy as jnp
from jax import lax
from jax.experimental import pallas as pl
from jax.experimental.pallas import tpu as pltpu

F32 = jnp.float32
BF16 = jnp.bfloat16

D_MODEL = 1024
DEPTH = 2
N_BRANCH = 4
BRANCH_W = 256
N_HEAD = 4
HEAD_D = 64
SSD_H = 8
SSD_P = 32
SSD_N = 64
SSD_G = 2
CONV_K = 4
CHUNK = 64
LS = 8
FLOW_DELAY = {CHUNK: 2, LS: 3}
D_FF = 2816
EPS = 1e-6
NEG = -1e30
GDN_CONV_W = 768
SSD_CONV_W = 512
SSD_XBC_OFF = 2304
MIX_W = 4096
GATE_W = 4096
PROJ_W = MIX_W + GATE_W
SMALL_OFF = 3840
LANE = 128
VMEM_LIMIT = 56 * 1024 * 1024

ST_BETA, ST_GA, ST_DT, ST_MI, ST_MF = 0, 4, 8, 16, 20

_NN = (((1,), (0,)), ((), ()))
_NT = (((1,), (1,)), ((), ()))
_TN = (((0,), (0,)), ((), ()))


def _mm(a, b, dims=_NN):
    return lax.dot_general(a.astype(BF16), b.astype(BF16), dims, preferred_element_type=F32)


def _split(x, n):
    parts, r = [], x
    for i in range(n):
        p = r.astype(BF16)
        parts.append(p)
        if i < n - 1:
            r = r - p.astype(F32)
    return parts


N_PIECE = 2


def _mm01(x, m2):
    pieces = _split(x, N_PIECE)
    k = x.shape[1]
    if x.shape[0] % 16 == 0:
        return lax.dot_general(jnp.concatenate(pieces, axis=1), m2.astype(BF16), _NN, preferred_element_type=F32)
    out = None
    for i, p in enumerate(pieces):
        t = lax.dot_general(p, m2[i * k:(i + 1) * k, :].astype(BF16), _NN, preferred_element_type=F32)
        out = t if out is None else out + t
    return out


def _m01m(m2, x):
    pieces = _split(x, N_PIECE)
    k = x.shape[0]
    if k % 16 == 0:
        return lax.dot_general(m2, jnp.concatenate(pieces, axis=0), _NN, preferred_element_type=F32)
    out = None
    for i, p in enumerate(pieces):
        t = lax.dot_general(m2[:, i * k:(i + 1) * k], p, _NN, preferred_element_type=F32)
        out = t if out is None else out + t
    return out


def _sigmoid(x):
    return jax.nn.sigmoid(x)


def _silu(x):
    return x * jax.nn.sigmoid(x)


def _softplus(x):
    return jnp.maximum(x, 0.0) + jnp.log(1.0 + jnp.exp(-jnp.abs(x)))


def _logsig(x):
    return jnp.minimum(x, 0.0) - jnp.log(1.0 + jnp.exp(-jnp.abs(x)))


def _row_dtype(rows):
    return BF16 if rows % 16 == 0 else F32


def _expand(base, heads, width):
    e = np.zeros((LANE, heads * width), np.float32)
    for h in range(heads):
        e[base + h, h * width:(h + 1) * width] = 1.0
    return e


def _cat_segments(segs):
    cols, off, pos, seen = [], {}, 0, {}
    for name, m in segs:
        key = m.tobytes() + bytes(str(m.shape), 'ascii')
        if key in seen:
            off[name] = seen[key]
            continue
        w = m.shape[1]
        wp = -(-w // LANE) * LANE
        mp = np.zeros((m.shape[0], wp), np.float32)
        mp[:, :w] = m
        cols.append(mp)
        off[name] = seen[key] = (pos, w)
        pos += wp
    return np.concatenate(cols, axis=1), off


@functools.lru_cache(maxsize=None)
def _mixer_consts(L, nv):
    J = int(round(math.log2(L)))
    assert 1 << J == L
    f = np.float32
    r = np.arange(L)
    tri = (r[None, :] <= r[:, None])
    rev = (r[:, None] < r[None, :]) & (r[None, :] <= nv - 1)
    strict = (r[None, :] < r[:, None])
    eye = np.eye(L, dtype=bool)
    lvl, ab = [], []
    for j in range(J):
        bnd = ((r >> (j + 1)) << (j + 1)) + (1 << j) - 1
        low = ((r >> j) & 1) == 1
        a = low[:, None] & (bnd[:, None] < r[None, :]) & (r[None, :] <= r[:, None])
        b = (~low)[:, None] & (r[:, None] < r[None, :]) & (r[None, :] <= bnd[:, None])
        ab.append(a | b)
        same = (r[:, None] >> (j + 1)) == (r[None, :] >> (j + 1))
        lvl.append(same & low[:, None] & (~low)[None, :])

    def tile(m, n):
        return np.tile(m.astype(f), (1, n))

    pr = np.repeat(np.arange(2), L)
    lr2 = np.tile(r, 2)
    ch = np.arange(LANE)
    ch2 = np.arange(2 * LANE)
    c = dict(
        trirev=np.concatenate([tri, rev], axis=0).astype(f),
        mhg=np.concatenate(ab + [tri, rev], axis=0).astype(f),
        ones_ll=np.ones((L, L), f), tri=tri.astype(f),
        tril2=tile(tri, 2), strict2=tile(strict, 2), tril4=tile(tri, 4),
        eye4=tile(eye, 4), supper16=tile(strict, 16),
        e_dr=np.concatenate([_expand(ST_GA, 4, L), _expand(ST_DT, SSD_H, L), _expand(ST_MF, 4, L)], axis=1),
        hmaskp=np.stack([tile(eye, 2)] + [tile(m, 2) for m in lvl]),
        eyep=np.eye(2 * L, dtype=f),
        lvlp=np.stack([(pr[:, None] == pr[None, :]) & m[lr2[:, None], lr2[None, :]] for m in lvl]).astype(f),
        stkp=(pr[:, None] == ch[None, :] // HEAD_D).astype(f),
        bdp=(ch[:, None] // HEAD_D == ch[None, :] // HEAD_D).astype(f),
        pm2=np.stack([ch < HEAD_D, ch >= HEAD_D]).astype(f),
        stk4s=(np.repeat(np.arange(4), L)[:, None] == ch[None, :] // SSD_P).astype(f),
        gbd=(ch[:, None] // SSD_N == ch2[None, :] // LANE).astype(f),
        rowvalid=(r[:, None] <= nv - 1).astype(f) * np.ones((1, LANE), f),
        bd64=(ch2[:, None] // 64 == ch2[None, :] // 64).astype(f),
        bd128=(ch2[:, None] // 128 == ch2[None, :] // 128).astype(f),
    )
    c['e_act'], off_act = _cat_segments([('b_r', _expand(ST_BETA, 4, L)), ('b_c', _expand(ST_BETA, 4, HEAD_D)),
                                         ('dt_c', _expand(ST_DT, SSD_H, SSD_P)),
                                         ('i_r', _expand(ST_MI, 4, L)), ('i_c', _expand(ST_MI, 4, HEAD_D))])
    c['e_cs'], off_cs = _cat_segments([('g_c', _expand(ST_GA, 4, HEAD_D)), ('s_c', _expand(ST_DT, SSD_H, SSD_P)),
                                       ('m_r', _expand(ST_MF, 4, L)), ('m_c', _expand(ST_MF, 4, HEAD_D))])
    c['e_rev'], off_rev = _cat_segments([('g_c', _expand(ST_GA, 4, HEAD_D)), ('s_c', _expand(ST_DT, SSD_H, SSD_P)),
                                         ('m_c', _expand(ST_MF, 4, HEAD_D))])
    c['bdp2'], c['stkp2'] = c['bdp'], c['stkp']
    for name in ('trirev', 'mhg', 'ones_ll'):
        c[name] = np.tile(c[name], (1, N_PIECE))
    for name in ('e_act', 'e_cs', 'e_rev', 'bdp2', 'stkp2'):
        c[name] = np.tile(c[name], (N_PIECE, 1))
    return c, dict(act=off_act, cs=off_cs, rev=off_rev), J


_CONST_ORDER = ('trirev', 'mhg', 'ones_ll', 'tri', 'tril2', 'strict2', 'tril4', 'eye4', 'supper16', 'e_dr', 'hmaskp',
                'eyep', 'lvlp', 'stkp', 'bdp', 'pm2', 'stk4s', 'gbd', 'rowvalid', 'bd64', 'bd128', 'e_act', 'e_cs',
                'e_rev', 'bdp2', 'stkp2')
_BF16_CONSTS = ('trirev', 'mhg', 'ones_ll', 'tri', 'supper16', 'e_dr', 'lvlp', 'stkp', 'stk4s', 'bd64', 'bd128',
                'e_act', 'e_cs', 'e_rev', 'bdp2', 'stkp2')


def _conv_silu(ext, w, b, L):
    y = b
    for i in range(CONV_K):
        y = y + ext[5 + i:5 + i + L, :] * w[i:i + 1, :]
    return _silu(y)


def _run_interleaved(tasks):
    tasks = list(tasks)
    while tasks:
        alive = []
        for t in tasks:
            try:
                next(t)
                alive.append(t)
            except StopIteration:
                pass
        tasks = alive


def _mixer_chunk(pm, ext_g, ext_s, st, P, K, offs, L, nv, J, layer, out):
    lane = lax.broadcasted_iota(jnp.int32, (1, LANE), 1)
    rowvalid = K['rowvalid'][:, 0:1]
    tril2 = K['tril2'] > 0.0
    stkp, bdp = K['stkp'], K['bdp']
    upper_c = lane >= HEAD_D
    upper_r = lax.broadcasted_iota(jnp.int32, (1, 2 * L), 1) >= L

    def seg(x, which, name):
        o, w = offs[which][name]
        return x[:, o:o + w]

    def pair_c(x, p):
        return x[:, p * LANE:(p + 1) * LANE]

    def pair_r(x, p):
        return x[:, p * 2 * L:(p + 1) * 2 * L]

    def stack(x):
        xb = x.astype(BF16)
        return jnp.concatenate([xb, xb], axis=0) * stkp

    z = pm[:, SMALL_OFF:SMALL_OFF + LANE] + P['sp'][0:1, :]
    act = jnp.where(lane < ST_GA, _sigmoid(z),
                    jnp.where(lane < ST_MI, _softplus(z), jnp.where(lane < ST_MF, z, _logsig(z))))
    neg_a = -jnp.exp(P['sp'][1:2, :])
    dec_in = jnp.where((lane >= ST_GA) & (lane < ST_MI), neg_a * act,
                       jnp.where((lane >= ST_MF) & (lane < ST_MF + 4), act, 0.0))
    cr = _m01m(K['trirev'], dec_in)
    yield
    cs, rev = cr[:L], cr[L:]
    ea = _mm01(act, K['e_act'])
    ec = _mm01(cs, K['e_cs'])
    er = _mm01(rev, K['e_rev'])
    yield
    m_r = seg(ec, 'cs', 'm_r')
    g_c, s_c, m_c = seg(ec, 'cs', 'g_c'), seg(ec, 'cs', 's_c'), seg(ec, 'cs', 'm_c')
    b_r, b_c, dt_c = seg(ea, 'act', 'b_r'), seg(ea, 'act', 'b_c'), seg(ea, 'act', 'dt_c')
    i_r, i_c = seg(ea, 'act', 'i_r'), seg(ea, 'act', 'i_c')
    rg_c, rs_c, rm_c = seg(er, 'rev', 'g_c'), seg(er, 'rev', 's_c'), seg(er, 'rev', 'm_c')
    ym = _mm(dec_in, K['e_dr']).astype(BF16) * K['supper16']
    yield
    d_all = lax.dot_general(K['tri'], ym, _NN, preferred_element_type=F32)
    d_g, d_s, d_m = d_all[:, 0:4 * L], d_all[:, 4 * L:12 * L], d_all[:, 12 * L:16 * L]
    row_i = _m01m(K['ones_ll'], i_r * K['eye4'])

    res = dict(o_gdn=[None, None], gdn=[None, None], o_hg=[None, None], hg=[None, None],
               o_ml=[None, None], mlc=[None, None], mln=[None, None], mlm=[None] * N_HEAD)

    qkv = _conv_silu(ext_g, P['cwg'], P['cbg'], L)
    q, k, v = qkv[:, 0:256], qkv[:, 256:512], qkv[:, 512:768]
    ss = _mm(jnp.concatenate([q * q, k * k], axis=0), K['bd64'])
    yield
    q = q * (lax.rsqrt(ss[:L] + EPS) * (HEAD_D ** -0.5))
    k = k * lax.rsqrt(ss[L:] + EPS)

    def gdn_task(p):
        qt, kt, vt = pair_c(q, p), pair_c(k, p), pair_c(v, p)
        beta_c, gc = pair_c(b_c, p), pair_c(g_c, p)
        eg = jnp.exp(gc)
        dec = jnp.exp(pair_r(d_g, p)) * K['tril2']
        kkqk = _mm(jnp.concatenate([kt, qt], axis=0), stack(kt), _NT)
        yield
        n = (kkqk[:L] * dec * pair_r(b_r, p) * K['strict2']).astype(BF16)
        nbd = jnp.concatenate([n, n], axis=0)
        t = K['eyep'] - (nbd * K['lvlp'][0]).astype(F32)
        for j in range(1, J):
            b = 1 << j
            if b % 16:
                x = _mm(t, nbd * K['lvlp'][j])
                yield
                t = t - _mm(x, t)
                yield
                continue
            starts = [h * L + s + b for h in range(2) for s in range(0, L, 2 * b)]
            x = _mm(jnp.concatenate([t[r:r + b] for r in starts], axis=0), nbd * K['lvlp'][j])
            yield
            upd = _mm(x, t)
            zero = jnp.zeros((b, 2 * L), F32)
            t = t - jnp.concatenate([blk for i in range(len(starts)) for blk in (zero, upd[i * b:(i + 1) * b])],
                                    axis=0)
            yield
        sol = _mm(t, jnp.concatenate([stack(vt * beta_c), stack(kt * (beta_c * eg))], axis=1))
        yield
        sol = sol[:L] + sol[L:]
        s_p = st['gdn'][p]
        u = sol[:, 0:LANE] - _mm(sol[:, LANE:2 * LANE], s_p)
        yield
        qo = jnp.concatenate([(qt * eg).astype(BF16), (kkqk[L:] * dec).astype(BF16)], axis=1)
        res['o_gdn'][p] = _mm(qo, jnp.concatenate([s_p.astype(BF16), stack(u)], axis=0))
        kw = kt * (jnp.exp(pair_c(rg_c, p)) * rowvalid)
        res['gdn'][p] = s_p * jnp.exp(gc[nv - 1:nv, :]) + bdp * _mm(kw, u, _TN)
        yield

    lg = P['lg']
    mx = lg[0:1, :]
    for i in range(1, DEPTH):
        mx = jnp.maximum(mx, lg[i:i + 1, :])
    ex = [jnp.exp(lg[i:i + 1, :] - mx) for i in range(DEPTH)]
    tot = ex[0]
    for i in range(1, DEPTH):
        tot = tot + ex[i]
    sm = [e / tot for e in ex]
    cum = sm[0]
    for i in range(1, layer + 1):
        cum = cum + sm[i]
    lb = cum - sm[0]
    lb_pos = lb > 0
    log_lb = jnp.log(jnp.where(lb_pos, lb, 1.0))
    hq = _silu(pm[:, 1024:1280])
    fz = pm[:, 1280:1536]
    hv = pm[:, 1536:1792]
    ls = _logsig(fz)
    t2 = jnp.log1p(-lb) + ls
    la = jnp.maximum(log_lb, t2) + jnp.log1p(jnp.exp(-jnp.abs(log_lb - t2)))
    logf = jnp.where(lb_pos, la, ls)
    kg = (1.0 - lb) * _sigmoid(-fz)
    ey = jnp.exp(_m01m(K['mhg'], logf))
    yield
    eg_h = ey[J * L:(J + 1) * L]
    eyb = ey[0:J * L].astype(BF16)
    hqb = hq.astype(BF16)
    qe = hq * eg_h
    kwr = kg * (ey[(J + 1) * L:(J + 2) * L] * rowvalid)

    def hg_task(p):
        q_p, k_st = pair_c(hqb, p), stack(pair_c(kg, p))
        a_p = K['hmaskp'][0] * _mm(q_p, k_st, _NT)
        yield
        for j in range(J):
            e_j = pair_c(eyb[j * L:(j + 1) * L], p)
            a_p = a_p + K['hmaskp'][j + 1] * _mm(q_p * e_j, k_st * jnp.concatenate([e_j, e_j], axis=0), _NT)
            yield
        st_p = st['hg'][p]
        res['o_hg'][p] = _mm(pair_c(qe, p), st_p, _NT) + _mm(a_p, stack(pair_c(hv, p)))
        yield
        res['hg'][p] = (st_p * pair_c(eg_h[nv - 1:nv, :], p)
                        + bdp * _mm(pair_c(hv, p), pair_c(kwr, p), _TN))
        yield

    xbc = _conv_silu(ext_s, P['cws'], P['cbs'], L)
    xs, bs, cc = xbc[:, 0:256], xbc[:, 256:384], xbc[:, 384:512]
    vs = xs * dt_c
    bsb = bs.astype(BF16)
    bs4 = jnp.concatenate([bsb] * 4, axis=0)

    def ssd_task():
        o_intra = []
        for g in range(SSD_G):
            cbw = _mm(cc * K['pm2'][g:g + 1, :], bs4, _NT)
            dec = jnp.exp(d_s[:, g * 4 * L:(g + 1) * 4 * L]) * K['tril4']
            vb = pair_c(vs, g).astype(BF16)
            o_intra.append(_mm(cbw * dec, jnp.concatenate([vb] * 4, axis=0) * K['stk4s']))
            yield
        s_all = st['ssd']
        res['o_ssd'] = jnp.exp(s_c) * _mm(cc, s_all) + jnp.concatenate(o_intra, axis=1)
        res['ssd'] = (s_all * jnp.exp(s_c[nv - 1:nv, :])
                      + K['gbd'] * _mm(bs * rowvalid, vs * jnp.exp(rs_c), _TN))
        yield

    mq = pm[:, 2816:3072] * (HEAD_D ** -0.5)
    mk = pm[:, 3072:3328]
    mv = pm[:, 3328:3584]

    def ml_task(p, delay):
        for _ in range(delay):
            yield
        qt, kt, vt = pair_c(mq, p), pair_c(mk, p), pair_c(mv, p)
        bm_r, bm_c = pair_r(m_r, p), pair_c(m_c, p)
        mp0 = st['mlm'][:, 2 * p:2 * p + 1]
        mp1 = st['mlm'][:, 2 * p + 1:2 * p + 2]
        logw = jnp.where(tril2, pair_r(d_m, p) + pair_r(row_i, p), NEG)
        mx0 = jnp.max(jnp.where(upper_r, NEG, logw), axis=-1, keepdims=True)
        mx1 = jnp.max(jnp.where(upper_r, logw, NEG), axis=-1, keepdims=True)
        l0_r = bm_r + jnp.where(upper_r, mp1, mp0)
        l0_c = bm_c + jnp.where(upper_c, mp1, mp0)
        mt_r = jnp.maximum(l0_r, jnp.where(upper_r, mx1, mx0))
        mt_c = jnp.maximum(l0_c, jnp.where(upper_c, mx1, mx0))
        w = jnp.exp(logw - mt_r)
        w0 = jnp.exp(l0_c - mt_c)
        qk = _mm(qt, stack(kt), _NT) * w
        yield
        c_p = st['mlc'][p]
        n_p = st['mln'][p]
        qn = jnp.concatenate([(w0 * qt).astype(BF16), qk.astype(BF16)], axis=1)
        num = _mm(qn, jnp.concatenate([c_p.astype(BF16), stack(vt)], axis=0))
        den = w0 * _mm01(qt * n_p, K['bdp2']) + _mm01(qk, K['stkp2'])
        res['o_ml'][p] = num * (1.0 / jnp.maximum(jnp.abs(den), jnp.exp(-mt_c)))
        yield
        m_l = mt_c[nv - 1:nv, :]
        wl0 = jnp.exp(bm_c[nv - 1:nv, :] + jnp.where(upper_c, mp1, mp0) - m_l)
        kwl = kt * (jnp.exp(pair_c(rm_c, p) + pair_c(i_c, p) - m_l) * rowvalid)
        res['mlc'][p] = c_p * wl0 + bdp * _mm(kwl, vt, _TN)
        res['mln'][p] = n_p * wl0 + jnp.sum(kwl, axis=0, keepdims=True)
        res['mlm'][2 * p] = m_l[:, 0:1]
        res['mlm'][2 * p + 1] = m_l[:, HEAD_D:HEAD_D + 1]
        yield

    def finish():
        o_gdn = jnp.concatenate(res['o_gdn'], axis=1)
        o_hg = jnp.concatenate(res['o_hg'], axis=1)
        hh = jnp.concatenate(res['o_ml'], axis=1)
        ms = _mm(jnp.concatenate([o_gdn * o_gdn, o_hg * o_hg, hh * hh], axis=0), K['bd64']) * (1.0 / HEAD_D)
        yield
        out_a = o_gdn * lax.rsqrt(ms[0:L] + EPS) * P['vec'][0:1, :] * _silu(pm[:, 768:1024])
        out_b = o_hg * lax.rsqrt(ms[L:2 * L] + EPS) * P['vec'][1:2, :] * _silu(pm[:, 1792:2048])
        out_d = hh * lax.rsqrt(ms[2 * L:3 * L] + EPS) * P['vec'][3:4, :] * _sigmoid(pm[:, 3584:3840])
        ys = (res['o_ssd'] + P['vec'][4:5, :] * xs) * _silu(pm[:, 2048:2304])
        out_c = ys * lax.rsqrt(_mm(ys * ys, K['bd128']) * (1.0 / (2 * HEAD_D)) + EPS) * P['vec'][2:3, :]
        branches = jnp.concatenate([out_a, out_b, out_c, out_d], axis=1)
        new = dict(gdn=res['gdn'], hg=res['hg'], ssd=res['ssd'], mlc=res['mlc'], mln=res['mln'], mlm=res['mlm'])
        out['result'] = (branches, new)
        yield

    out['tasks'] = ([gdn_task(p) for p in range(2)] + [hg_task(p) for p in range(2)]
                    + [ssd_task()] + [ml_task(p, 3 + 2 * p) for p in range(2)])
    out['finish'] = finish


def _mixer_kernel(L, nv, J, layer, bb, offs, has_init, n_alias, *refs):
    it = iter(refs)
    pm_ref = next(it)
    if has_init:
        gdn0, cg0, hg0, ssd0, cs0, mc0, mn0, mm0 = (next(it) for _ in range(8))
    sp_ref, cwg_ref, cbg_ref, cws_ref, cbs_ref, vec_ref, lg_ref = (next(it) for _ in range(7))
    kref = {name: next(it) for name in _CONST_ORDER}
    for _ in range(n_alias):
        next(it)
    br_ref = next(it)
    gdn1, cg1, hg1, ssd1, cs1, mc1, mn1, mm1 = (next(it) for _ in range(8))
    sg, sh, sc, sn, ss, extg, exts, pmpad = (next(it) for _ in range(8))

    c = pl.program_id(1)
    nc = pl.num_programs(1)
    n_sub = SSD_H // SSD_G

    @pl.when(c == 0)
    def _init():
        sg[...] = jnp.zeros_like(sg)
        sh[...] = jnp.zeros_like(sh)
        sc[...] = jnp.zeros_like(sc)
        ss[...] = jnp.zeros_like(ss)
        extg[:, 0:8, :] = jnp.zeros((bb, 8, GDN_CONV_W), F32)
        exts[:, 0:8, :] = jnp.zeros((bb, 8, SSD_CONV_W), F32)
        if not has_init:
            sn[...] = jnp.zeros_like(sn)
            mm1[...] = jnp.zeros_like(mm1)
            return
        for s in range(bb):
            for h in range(N_HEAD):
                p, lo = h // 2, (h % 2) * HEAD_D
                sg[s, p, lo:lo + HEAD_D, lo:lo + HEAD_D] = gdn0[s, h]
                sh[s, p, lo:lo + HEAD_D, lo:lo + HEAD_D] = hg0[s, h].T
                sc[s, p, lo:lo + HEAD_D, lo:lo + HEAD_D] = mc0[s, h]
                sn[s, p, :, lo:lo + HEAD_D] = mn0[s, h:h + 1, :]
            for h in range(SSD_H):
                g = h // n_sub
                ss[s, g * SSD_N:(g + 1) * SSD_N, h * SSD_P:(h + 1) * SSD_P] = ssd0[s, h]
        mm1[...] = mm0[...]
        extg[:, 5:8, :] = cg0[...]
        exts[:, 5:8, :] = cs0[...]

    P = dict(sp=sp_ref[...], cwg=cwg_ref[...], cbg=cbg_ref[...], cws=cws_ref[...], cbs=cbs_ref[...],
             vec=vec_ref[...], lg=lg_ref[...])
    K = {name: r[...] for name, r in kref.items()}
    stage1, outs = [], []
    spg = L // nv
    for s in range(bb):
        grp, off = s // spg, (s % spg) * nv
        if nv == L:
            pm = pm_ref[s]
        else:
            pmpad[s, 0:nv, :] = pm_ref[grp, off:off + nv, :]
            pmpad[s, nv:L, :] = jnp.zeros((L - nv, MIX_W), F32)
            pm = pmpad[s]
        extg[s, 8:8 + L, :] = pm[:, 0:GDN_CONV_W]
        exts[s, 8:8 + L, :] = pm[:, SSD_XBC_OFF:SSD_XBC_OFF + SSD_CONV_W]
        st = dict(gdn=sg[s], hg=sh[s], ssd=ss[s], mlc=sc[s], mln=sn[s], mlm=mm1[s])
        outs.append({})
        stage1.append(_mixer_chunk(pm, extg.at[s], exts.at[s], st, P, K, offs, L, nv, J, layer, outs[s]))

    def flow(s):
        for _ in range(FLOW_DELAY[L] * s):
            yield
        yield from stage1[s]
        tasks = list(outs[s]['tasks'])
        while tasks:
            alive = []
            for t in tasks:
                try:
                    next(t)
                    alive.append(t)
                except StopIteration:
                    pass
            tasks = alive
            yield
        yield from outs[s]['finish']()

    _run_interleaved([flow(s) for s in range(bb)])
    for s in range(bb):
        branches, new = outs[s]['result']
        grp, off = s // spg, (s % spg) * nv
        br_ref[grp, off:off + nv, :] = branches[0:nv].astype(br_ref.dtype)
        for p in range(2):
            sg[s, p] = new['gdn'][p]
            sh[s, p] = new['hg'][p]
            sc[s, p] = new['mlc'][p]
            sn[s, p] = new['mln'][p]
        for h in range(N_HEAD):
            mm1[s, :, h:h + 1] = new['mlm'][h]
        ss[s] = new['ssd']
        tail_g = extg[s, 8 + nv - 3:8 + nv, :]
        tail_s = exts[s, 8 + nv - 3:8 + nv, :]
        extg[s, 5:8, :] = tail_g
        exts[s, 5:8, :] = tail_s
        cg1[s] = tail_g
        cs1[s] = tail_s

    @pl.when(c == nc - 1)
    def _fin():
        for s in range(bb):
            for h in range(N_HEAD):
                p, lo = h // 2, (h % 2) * HEAD_D
                gdn1[s, h] = sg[s, p, lo:lo + HEAD_D, lo:lo + HEAD_D]
                hg1[s, h] = sh[s, p, lo:lo + HEAD_D, lo:lo + HEAD_D].T
                mc1[s, h] = sc[s, p, lo:lo + HEAD_D, lo:lo + HEAD_D]
                mn1[s, h:h + 1, :] = sn[s, p, :, lo:lo + HEAD_D]
            for h in range(SSD_H):
                g = h // n_sub
                ssd1[s, h] = ss[s, g * SSD_N:(g + 1) * SSD_N, h * SSD_P:(h + 1) * SSD_P]


def _full_spec(a):
    nd = a.ndim
    return pl.BlockSpec(a.shape, lambda b, c, _nd=nd: (0,) * _nd)


def _mixer_call(proj, state_shapes, states_in, prev_out, params, L, nv, layer, bb):
    g, t, _ = proj.shape
    spg = L // nv if nv < L else 1
    nchunk = t // L
    assert (g * spg) % bb == 0 and bb % spg == 0 and t % L == 0
    consts, offs, J = _mixer_consts(L, nv)
    const_arrays = [jnp.asarray(consts[n], BF16 if n in _BF16_CONSTS else F32) for n in _CONST_ORDER]

    def st_spec(shape):
        nd = len(shape)
        return pl.BlockSpec((None, bb) + tuple(shape[2:]), lambda b, c, _nd=nd: (layer, b) + (0,) * (_nd - 2))

    has_init = states_in is not None
    n_alias = 0 if prev_out is None else len(prev_out)
    inputs = [proj] + (list(states_in) if has_init else []) + list(params) + const_arrays + list(prev_out or [])
    in_specs = ([pl.BlockSpec((bb // spg, L, MIX_W), lambda b, c: (b, c, 0))]
                + ([st_spec(s) for s in state_shapes] if has_init else [])
                + [_full_spec(a) for a in params]
                + [_full_spec(a) for a in const_arrays]
                + [pl.BlockSpec(memory_space=pl.ANY)] * n_alias)
    out_shape = ([jax.ShapeDtypeStruct((g, t, N_BRANCH * BRANCH_W), _row_dtype(L))]
                 + [jax.ShapeDtypeStruct(s, F32) for s in state_shapes])
    out_specs = ([pl.BlockSpec((bb // spg, L, N_BRANCH * BRANCH_W), lambda b, c: (b, c, 0))]
                 + [st_spec(s) for s in state_shapes])
    first_alias = len(inputs) - n_alias
    pair = (bb, 2, LANE, LANE)
    scratch = [pltpu.VMEM(pair, F32), pltpu.VMEM(pair, F32), pltpu.VMEM(pair, F32),
               pltpu.VMEM((bb, 2, 1, LANE), F32), pltpu.VMEM((bb, SSD_G * SSD_N, SSD_H * SSD_P), F32),
               pltpu.VMEM((bb, 8 + L, GDN_CONV_W), F32), pltpu.VMEM((bb, 8 + L, SSD_CONV_W), F32),
               pltpu.VMEM((bb, L, MIX_W) if nv < L else (1, 8, LANE), F32)]
    outs = pl.pallas_call(
        functools.partial(_mixer_kernel, L, nv, J, layer, bb, offs, has_init, n_alias),
        grid=(g * spg // bb, nchunk),
        in_specs=in_specs, out_specs=out_specs, out_shape=out_shape, scratch_shapes=scratch,
        input_output_aliases={first_alias + k: 1 + k for k in range(n_alias)},
        compiler_params=pltpu.CompilerParams(dimension_semantics=("parallel", "arbitrary"),
                                             vmem_limit_bytes=VMEM_LIMIT),
        name=f"mixer_L{L}",
    )(*inputs)
    return outs[0], list(outs[1:])


def _ada_kernel(c_ref, w_ref, b_ref, o_ref):
    o_ref[...] = _mm(_silu(c_ref[...]), w_ref[...]) + b_ref[...]


def _ada_call(c_all, ada_w, ada_b):
    rows = c_all.shape[0]
    n = ada_w.shape[-1]
    tn = 1536
    return pl.pallas_call(
        _ada_kernel,
        grid=(DEPTH, n // tn),
        in_specs=[pl.BlockSpec((rows, D_MODEL), lambda l, j: (0, 0)),
                  pl.BlockSpec((None, D_MODEL, tn), lambda l, j: (l, 0, j)),
                  pl.BlockSpec((None, 1, tn), lambda l, j: (l, 0, j))],
        out_specs=pl.BlockSpec((None, rows, tn), lambda l, j: (l, 0, j)),
        out_shape=jax.ShapeDtypeStruct((DEPTH, rows, n), F32),
        compiler_params=pltpu.CompilerParams(dimension_semantics=("arbitrary", "arbitrary"),
                                             vmem_limit_bytes=VMEM_LIMIT),
        name="ada",
    )(c_all, ada_w, ada_b.reshape(DEPTH, 1, n))


PREP_ROWS = MIX_W - SMALL_OFF


def _prep_w_in_kernel(w_ref, s0_ref, s1_ref, s2_ref, o_ref):
    j = pl.program_id(1)
    j_small = SMALL_OFF // PREP_ROWS

    @pl.when(j != j_small)
    def _copy():
        o_ref[...] = w_ref[0].astype(BF16)

    @pl.when(j == j_small)
    def _small():
        o_ref[...] = jnp.zeros(o_ref.shape, BF16)
        o_ref[0:8, :] = s0_ref[0].astype(BF16)
        o_ref[8:16, :] = s1_ref[0].astype(BF16)
        o_ref[16:24, :] = s2_ref[0].astype(BF16)


def _prep_src_row(j):
    r = PREP_ROWS
    return jnp.where(j < 1024 // r, r * j,
                     jnp.where(j < 2816 // r, r * j + 8,
                               jnp.where(j < SMALL_OFF // r, r * j + 16,
                                         jnp.where(j < MIX_W // r, 0, r * j - MIX_W + 3864))))


def _prep_w_in_call(w_in):
    wt = jnp.swapaxes(w_in, 1, 2)
    depth, _, d = wt.shape

    def rows(n, index):
        return pl.BlockSpec((pl.Element(1), pl.Element(n), pl.Element(d)), index)

    return pl.pallas_call(
        _prep_w_in_kernel,
        grid=(depth, PROJ_W // PREP_ROWS),
        in_specs=[rows(PREP_ROWS, lambda l, j: (l, pl.multiple_of(_prep_src_row(j), 8), 0)),
                  rows(8, lambda l, j: (l, 1024, 0)), rows(8, lambda l, j: (l, 2824, 0)),
                  rows(8, lambda l, j: (l, 3856, 0))],
        out_specs=pl.BlockSpec((None, PREP_ROWS, d), lambda l, j: (l, j, 0)),
        out_shape=jax.ShapeDtypeStruct((depth, PROJ_W, d), BF16),
        compiler_params=pltpu.CompilerParams(dimension_semantics=("arbitrary", "arbitrary"),
                                             vmem_limit_bytes=VMEM_LIMIT),
        name="prep_w_in",
    )(wt, wt, wt, wt)


def _rms_mod(x, nw, sc, sh):
    ms = jnp.mean(x * x, axis=-1, keepdims=True)
    return x * lax.rsqrt(ms + EPS) * nw * (1.0 + sc) + sh


def _tok_spec(gblk, width):
    g, r = gblk
    return pl.BlockSpec((g, r, width), lambda i, j: (i, j, 0))


def _mod_spec(gblk, layer, k, mods):
    return pl.BlockSpec((None, gblk[0], mods.shape[2], D_MODEL), lambda i, j: (layer, i, 0, k))


def _mod_rows(m, r):
    spg = m.shape[1]
    if spg == 1:
        return m
    seq = lax.broadcasted_iota(jnp.int32, (1, r, 1), 1) // (r // spg)
    out = m[:, 0:1, :]
    for j in range(1, spg):
        out = jnp.where(seq == j, m[:, j:j + 1, :], out)
    return out


def _layer_spec(shape, layer):
    nd = len(shape)
    return pl.BlockSpec((None,) + tuple(shape[1:]), lambda i, j: (layer,) + (0,) * (nd - 1),
                        pipeline_mode=pl.Buffered(1))


def _dense_params():
    return pltpu.CompilerParams(dimension_semantics=("arbitrary", "arbitrary"), vmem_limit_bytes=VMEM_LIMIT)


def _inproj_kernel(x_ref, nw_ref, sc_ref, sh_ref, w_ref, pm_ref, gate_ref):
    g, r, _ = x_ref.shape
    h = _rms_mod(x_ref[...], nw_ref[...], _mod_rows(sc_ref[...], r), _mod_rows(sh_ref[...], r))
    h = h.astype(BF16).reshape(g * r, D_MODEL)
    pm_ref[...] = _mm(h, w_ref[0:MIX_W, :], _NT).reshape(g, r, MIX_W)
    gates = _sigmoid(_mm(h, w_ref[MIX_W:PROJ_W, :], _NT))
    gate_ref[...] = gates.astype(gate_ref.dtype).reshape(g, r, GATE_W)


def _inproj_call(x, layer, nw, mods, w, gblk):
    bg, t, _ = x.shape
    return pl.pallas_call(
        _inproj_kernel,
        grid=(bg // gblk[0], t // gblk[1]),
        in_specs=[_tok_spec(gblk, D_MODEL), _layer_spec(nw.shape, layer),
                  _mod_spec(gblk, layer, 1, mods), _mod_spec(gblk, layer, 0, mods), _layer_spec(w.shape, layer)],
        out_specs=[_tok_spec(gblk, MIX_W), _tok_spec(gblk, GATE_W)],
        out_shape=[jax.ShapeDtypeStruct((bg, t, MIX_W), F32),
                   jax.ShapeDtypeStruct((bg, t, GATE_W), _row_dtype(gblk[1]))],
        compiler_params=_dense_params(), name="inproj",
    )(x, nw, mods, mods, w)


def _merge_ffn_kernel(final, br_ref, gate_ref, x_ref, gt1_ref, nw_ref, sc_ref, sh_ref, gt2_ref,
                      wb_ref, wo_ref, w1_ref, w2_ref, fw_ref, o_ref):
    g, r, _ = x_ref.shape
    br = br_ref[...].reshape(g * r, N_BRANCH * BRANCH_W)
    merged = None
    for n in range(N_BRANCH):
        up = _mm(br[:, n * BRANCH_W:(n + 1) * BRANCH_W], wb_ref[n])
        gate = gate_ref[:, :, n * D_MODEL:(n + 1) * D_MODEL].astype(F32).reshape(g * r, D_MODEL)
        t = gate * up
        merged = t if merged is None else merged + t
    x1 = x_ref[...] + _mod_rows(gt1_ref[...], r) * _mm(merged, wo_ref[...]).reshape(g, r, D_MODEL)
    h = _rms_mod(x1, nw_ref[...], _mod_rows(sc_ref[...], r), _mod_rows(sh_ref[...], r))
    h = h.astype(BF16).reshape(g * r, D_MODEL)
    a = _mm(h, w1_ref[:, 0:D_FF])
    b = _mm(h, w1_ref[:, D_FF:2 * D_FF])
    x2 = x1 + _mod_rows(gt2_ref[...], r) * _mm(_silu(a) * b, w2_ref[...]).reshape(g, r, D_MODEL)
    if final:
        ms = jnp.mean(x2 * x2, axis=-1, keepdims=True)
        x2 = x2 * lax.rsqrt(ms + EPS) * fw_ref[...]
    o_ref[...] = x2


def _merge_ffn_call(br, gates, x, layer, nw, mods, wb, wo, w1, w2, fw, gblk, final):
    bg, t, _ = x.shape
    return pl.pallas_call(
        functools.partial(_merge_ffn_kernel, final),
        grid=(bg // gblk[0], t // gblk[1]),
        in_specs=[_tok_spec(gblk, N_BRANCH * BRANCH_W), _tok_spec(gblk, GATE_W), _tok_spec(gblk, D_MODEL),
                  _mod_spec(gblk, layer, 2, mods), _layer_spec(nw.shape, layer),
                  _mod_spec(gblk, layer, 4, mods), _mod_spec(gblk, layer, 3, mods), _mod_spec(gblk, layer, 5, mods),
                  _layer_spec(wb.shape, layer), _layer_spec(wo.shape, layer),
                  _layer_spec(w1.shape, layer), _layer_spec(w2.shape, layer),
                  pl.BlockSpec(fw.shape, lambda i, j: (0, 0), pipeline_mode=pl.Buffered(1))],
        out_specs=_tok_spec(gblk, D_MODEL),
        out_shape=jax.ShapeDtypeStruct((bg, t, D_MODEL), F32),
        compiler_params=_dense_params(), name="merge_ffn",
    )(br, gates, x, mods, nw, mods, mods, mods, wb, wo, w1, w2, fw)


def _small_params(gdn_dt_bias, ssd_dt_bias, ml_b_i, ml_b_f, gdn_a_log, ssd_a_log):
    z4 = jnp.zeros((4,), F32)
    bias = jnp.concatenate([z4, gdn_dt_bias, ssd_dt_bias, ml_b_i, ml_b_f, jnp.zeros((LANE - 24,), F32)])
    alog = jnp.concatenate([z4, gdn_a_log, ssd_a_log, jnp.zeros((LANE - 16,), F32)])
    return jnp.concatenate([bias[None], alog[None], jnp.zeros((6, LANE), F32)], axis=0)


def _layer_params(l, gdn_conv_w, gdn_conv_b, gdn_a_log, gdn_dt_bias, gdn_norm_w, hg_lb_logits, hg_norm_w,
                  ssd_conv_w, ssd_conv_b, ssd_a_log, ssd_dt_bias, ssd_d, ssd_norm_w, ml_b_i, ml_b_f, ml_norm_w):
    sp = _small_params(gdn_dt_bias[l], ssd_dt_bias[l], ml_b_i[l], ml_b_f[l], gdn_a_log[l], ssd_a_log[l])
    vec = jnp.stack([jnp.tile(gdn_norm_w[l], N_HEAD), jnp.tile(hg_norm_w[l], N_HEAD), ssd_norm_w[l],
                     jnp.tile(ml_norm_w[l], N_HEAD), jnp.repeat(ssd_d[l], SSD_P)]
                    + [jnp.zeros((256,), F32)] * 3)
    return [sp, gdn_conv_w[l], gdn_conv_b[l][None], ssd_conv_w[l], ssd_conv_b[l][None], vec,
            hg_lb_logits.astype(F32)]


def _trunk(x, mods, state_shapes, states_in, L, nv, gblk, gblk_ffn, bb, W, mixer_params):
    new_states = None
    for l in range(DEPTH):
        pm, gates = _inproj_call(x, l, W['norm1'], mods, W['w_in'], gblk)
        br, new_states = _mixer_call(pm, state_shapes, states_in, new_states, mixer_params[l], L, nv, l, bb)
        x = _merge_ffn_call(br, gates, x, l, W['norm2'], mods, W['w_branch'], W['w_out'],
                            W['ffn_w_in'], W['ffn_w_out'], W['final'], gblk_ffn, final=(l == DEPTH - 1))
    return x, new_states


def kernel(x_prompt, x_sample, c_prompt, c_sample, state_gdn, state_gdn_conv, state_hgrn, state_ssd, state_ssd_conv, state_mlstm_c, state_mlstm_n, state_mlstm_m, ada_w, ada_b, norm1_w, norm2_w, w_in, gdn_conv_w, gdn_conv_b, gdn_a_log, gdn_dt_bias, gdn_norm_w, hg_lb_logits, hg_norm_w, ssd_conv_w, ssd_conv_b, ssd_a_log, ssd_dt_bias, ssd_d, ssd_norm_w, ml_b_i, ml_b_f, ml_norm_w, w_branch, w_out, ffn_w_in, ffn_w_out, final_norm_w):
    bp, tp, _ = x_prompt.shape
    bs, ts, _ = x_sample.shape
    assert tp % CHUNK == 0 and LS % ts == 0 and bs % (LS // ts) == 0

    W = dict(w_in=_prep_w_in_call(w_in), w_branch=w_branch.astype(BF16), w_out=w_out.astype(BF16),
             ffn_w_in=ffn_w_in.astype(BF16), ffn_w_out=ffn_w_out.astype(BF16),
             norm1=norm1_w[:, None, :], norm2=norm2_w[:, None, :], final=final_norm_w[None])
    mixer_params = [_layer_params(l, gdn_conv_w, gdn_conv_b, gdn_a_log, gdn_dt_bias, gdn_norm_w, hg_lb_logits,
                                  hg_norm_w, ssd_conv_w, ssd_conv_b, ssd_a_log, ssd_dt_bias, ssd_d, ssd_norm_w,
                                  ml_b_i, ml_b_f, ml_norm_w) for l in range(DEPTH)]

    mods = _ada_call(jnp.concatenate([c_prompt, c_sample], axis=0), ada_w, ada_b)
    mods_p = mods[:, :bp].reshape(DEPTH, bp, 1, 6 * D_MODEL)
    grp = LS // ts
    mods_s = mods[:, bp:].reshape(DEPTH, bs // grp, grp, 6 * D_MODEL)
    sample_states = [state_gdn, state_gdn_conv, state_hgrn, state_ssd, state_ssd_conv,
                     state_mlstm_c, state_mlstm_n, state_mlstm_m.reshape(DEPTH, bs, 1, N_HEAD)]
    shapes_s = [s.shape for s in sample_states]
    shapes_p = [(DEPTH, bp) + tuple(s[2:]) for s in shapes_s]

    y_p, new_p = _trunk(x_prompt, mods_p, shapes_p, None, CHUNK, CHUNK, (1, 256), (1, 512), 8, W, mixer_params)
    y_s, new_s = _trunk(x_sample.reshape(bs // grp, LS, D_MODEL), mods_s, shapes_s, sample_states, LS, ts,
                        (32, LS), (32, LS), 8, W, mixer_params)
    new_p[7] = new_p[7].reshape(DEPTH, bp, N_HEAD)
    new_s[7] = new_s[7].reshape(DEPTH, bs, N_HEAD)
    return (y_p, y_s.reshape(bs, ts, D_MODEL)) + tuple(new_p) + tuple(new_s)
```

```python
import functools
import math

import numpy as np
import jax
import jax.numpy as jnp
from jax import lax
from jax.experimental import pallas as pl
from jax.experimental.pallas import tpu as pltpu

F32 = jnp.float32
BF16 = jnp.bfloat16

D_MODEL = 1024
DEPTH = 2
N_BRANCH = 4
BRANCH_W = 256
N_HEAD = 4
HEAD_D = 64
SSD_H = 8
SSD_P = 32
SSD_N = 64
SSD_G = 2
CONV_K = 4
CHUNK = 64
LS = 8
FLOW_DELAY = {CHUNK: 2, LS: 3}
D_FF = 2816
EPS = 1e-6
NEG = -1e30
GDN_CONV_W = 768
SSD_CONV_W = 512
SSD_XBC_OFF = 2304
MIX_W = 4096
GATE_W = 4096
PROJ_W = MIX_W + GATE_W
SMALL_OFF = 3840
LANE = 128
FINE_ROWS = 16
VMEM_LIMIT = 56 * 1024 * 1024

ST_BETA, ST_GA, ST_DT, ST_MI, ST_MF = 0, 4, 8, 16, 20

_NN = (((1,), (0,)), ((), ()))
_NT = (((1,), (1,)), ((), ()))
_TN = (((0,), (0,)), ((), ()))


def _mm(a, b, dims=_NN):
    return lax.dot_general(a.astype(BF16), b.astype(BF16), dims, preferred_element_type=F32)


def _split(x, n):
    parts, r = [], x
    for i in range(n):
        p = r.astype(BF16)
        parts.append(p)
        if i < n - 1:
            r = r - p.astype(F32)
    return parts


N_PIECE = 2


def _mm01(x, m2):
    pieces = _split(x, N_PIECE)
    k = x.shape[1]
    if x.shape[0] % 16 == 0:
        return lax.dot_general(jnp.concatenate(pieces, axis=1), m2.astype(BF16), _NN, preferred_element_type=F32)
    out = None
    for i, p in enumerate(pieces):
        t = lax.dot_general(p, m2[i * k:(i + 1) * k, :].astype(BF16), _NN, preferred_element_type=F32)
        out = t if out is None else out + t
    return out


def _m01m(m2, x):
    pieces = _split(x, N_PIECE)
    k = x.shape[0]
    if k % 16 == 0:
        return lax.dot_general(m2, jnp.concatenate(pieces, axis=0), _NN, preferred_element_type=F32)
    out = None
    for i, p in enumerate(pieces):
        t = lax.dot_general(m2[:, i * k:(i + 1) * k], p, _NN, preferred_element_type=F32)
        out = t if out is None else out + t
    return out


def _sigmoid(x):
    return jax.nn.sigmoid(x)


def _silu(x):
    return x * jax.nn.sigmoid(x)


def _softplus(x):
    return jnp.maximum(x, 0.0) + jnp.log(1.0 + jnp.exp(-jnp.abs(x)))


def _logsig(x):
    return jnp.minimum(x, 0.0) - jnp.log(1.0 + jnp.exp(-jnp.abs(x)))


def _row_dtype(rows):
    return BF16 if rows % 16 == 0 else F32


def _expand(base, heads, width):
    e = np.zeros((LANE, heads * width), np.float32)
    for h in range(heads):
        e[base + h, h * width:(h + 1) * width] = 1.0
    return e


def _cat_segments(segs):
    cols, off, pos, seen = [], {}, 0, {}
    for name, m in segs:
        key = m.tobytes() + bytes(str(m.shape), 'ascii')
        if key in seen:
            off[name] = seen[key]
            continue
        w = m.shape[1]
        wp = -(-w // LANE) * LANE
        mp = np.zeros((m.shape[0], wp), np.float32)
        mp[:, :w] = m
        cols.append(mp)
        off[name] = seen[key] = (pos, w)
        pos += wp
    return np.concatenate(cols, axis=1), off


@functools.lru_cache(maxsize=None)
def _mixer_consts(L, nv):
    J = int(round(math.log2(L)))
    assert 1 << J == L
    f = np.float32
    r = np.arange(L)
    tri = (r[None, :] <= r[:, None])
    rev = (r[:, None] < r[None, :]) & (r[None, :] <= nv - 1)
    strict = (r[None, :] < r[:, None])
    eye = np.eye(L, dtype=bool)
    lvl, ab = [], []
    for j in range(J):
        bnd = ((r >> (j + 1)) << (j + 1)) + (1 << j) - 1
        low = ((r >> j) & 1) == 1
        a = low[:, None] & (bnd[:, None] < r[None, :]) & (r[None, :] <= r[:, None])
        b = (~low)[:, None] & (r[:, None] < r[None, :]) & (r[None, :] <= bnd[:, None])
        ab.append(a | b)
        same = (r[:, None] >> (j + 1)) == (r[None, :] >> (j + 1))
        lvl.append(same & low[:, None] & (~low)[None, :])

    def tile(m, n):
        return np.tile(m.astype(f), (1, n))

    pr = np.repeat(np.arange(2), L)
    lr2 = np.tile(r, 2)
    ch = np.arange(LANE)
    ch2 = np.arange(2 * LANE)
    c = dict(
        trirev=np.concatenate([tri, rev], axis=0).astype(f),
        mhg=np.concatenate(ab + [tri, rev], axis=0).astype(f),
        ones_ll=np.ones((L, L), f), tri=tri.astype(f),
        tril2=tile(tri, 2), strict2=tile(strict, 2), tril4=tile(tri, 4),
        eye4=tile(eye, 4), supper16=tile(strict, 16),
        e_dr=np.concatenate([_expand(ST_GA, 4, L), _expand(ST_DT, SSD_H, L), _expand(ST_MF, 4, L)], axis=1),
        hmaskp=np.stack([tile(eye, 2)] + [tile(m, 2) for m in lvl]),
        eyep=np.eye(2 * L, dtype=f),
        bdf=(np.arange(2 * L)[:, None] // FINE_ROWS == np.arange(2 * L)[None, :] // FINE_ROWS).astype(f),
        lvlp=np.stack([(pr[:, None] == pr[None, :]) & m[lr2[:, None], lr2[None, :]] for m in lvl]).astype(f),
        stkp=(pr[:, None] == ch[None, :] // HEAD_D).astype(f),
        bdp=(ch[:, None] // HEAD_D == ch[None, :] // HEAD_D).astype(f),
        pm2=np.stack([ch < HEAD_D, ch >= HEAD_D]).astype(f),
        stk4s=(np.repeat(np.arange(4), L)[:, None] == ch[None, :] // SSD_P).astype(f),
        gbd=(ch[:, None] // SSD_N == ch2[None, :] // LANE).astype(f),
        rowvalid=(r[:, None] <= nv - 1).astype(f) * np.ones((1, LANE), f),
        bd64=(ch2[:, None] // 64 == ch2[None, :] // 64).astype(f),
        bd128=(ch2[:, None] // 128 == ch2[None, :] // 128).astype(f),
    )
    c['e_act'], off_act = _cat_segments([('b_r', _expand(ST_BETA, 4, L)), ('b_c', _expand(ST_BETA, 4, HEAD_D)),
                                         ('dt_c', _expand(ST_DT, SSD_H, SSD_P)),
                                         ('i_r', _expand(ST_MI, 4, L)), ('i_c', _expand(ST_MI, 4, HEAD_D))])
    c['e_cs'], off_cs = _cat_segments([('g_c', _expand(ST_GA, 4, HEAD_D)), ('s_c', _expand(ST_DT, SSD_H, SSD_P)),
                                       ('m_r', _expand(ST_MF, 4, L)), ('m_c', _expand(ST_MF, 4, HEAD_D))])
    c['e_rev'], off_rev = _cat_segments([('g_c', _expand(ST_GA, 4, HEAD_D)), ('s_c', _expand(ST_DT, SSD_H, SSD_P)),
                                         ('m_c', _expand(ST_MF, 4, HEAD_D))])
    c['bdp2'], c['stkp2'] = c['bdp'], c['stkp']
    for name in ('trirev', 'mhg', 'ones_ll'):
        c[name] = np.tile(c[name], (1, N_PIECE))
    for name in ('e_act', 'e_cs', 'e_rev', 'bdp2', 'stkp2'):
        c[name] = np.tile(c[name], (N_PIECE, 1))
    return c, dict(act=off_act, cs=off_cs, rev=off_rev), J


_CONST_ORDER = ('trirev', 'mhg', 'ones_ll', 'tri', 'tril2', 'strict2', 'tril4', 'eye4', 'supper16', 'e_dr', 'hmaskp',
                'eyep', 'lvlp', 'stkp', 'bdp', 'pm2', 'stk4s', 'gbd', 'rowvalid', 'bd64', 'bd128', 'e_act', 'e_cs',
                'e_rev', 'bdp2', 'stkp2', 'bdf')
_BF16_CONSTS = ('trirev', 'mhg', 'ones_ll', 'tri', 'supper16', 'e_dr', 'lvlp', 'stkp', 'stk4s', 'bd64', 'bd128',
                'e_act', 'e_cs', 'e_rev', 'bdp2', 'stkp2')


def _conv_silu(ext, w, b, L):
    y = b
    for i in range(CONV_K):
        y = y + ext[5 + i:5 + i + L, :] * w[i:i + 1, :]
    return _silu(y)


def _run_interleaved(tasks):
    tasks = list(tasks)
    while tasks:
        alive = []
        for t in tasks:
            try:
                next(t)
                alive.append(t)
            except StopIteration:
                pass
        tasks = alive


def _mixer_chunk(pm, ext_g, ext_s, st, P, K, offs, L, nv, J, layer, out):
    lane = lax.broadcasted_iota(jnp.int32, (1, LANE), 1)
    rowvalid = K['rowvalid'][:, 0:1]
    tril2 = K['tril2'] > 0.0
    stkp, bdp = K['stkp'], K['bdp']
    upper_c = lane >= HEAD_D
    upper_r = lax.broadcasted_iota(jnp.int32, (1, 2 * L), 1) >= L

    def seg(x, which, name):
        o, w = offs[which][name]
        return x[:, o:o + w]

    def pair_c(x, p):
        return x[:, p * LANE:(p + 1) * LANE]

    def pair_r(x, p):
        return x[:, p * 2 * L:(p + 1) * 2 * L]

    def stack(x):
        xb = x.astype(BF16)
        return jnp.concatenate([xb, xb], axis=0) * stkp

    z = pm[:, SMALL_OFF:SMALL_OFF + LANE] + P['sp'][0:1, :]
    act = jnp.where(lane < ST_GA, _sigmoid(z),
                    jnp.where(lane < ST_MI, _softplus(z), jnp.where(lane < ST_MF, z, _logsig(z))))
    neg_a = -jnp.exp(P['sp'][1:2, :])
    dec_in = jnp.where((lane >= ST_GA) & (lane < ST_MI), neg_a * act,
                       jnp.where((lane >= ST_MF) & (lane < ST_MF + 4), act, 0.0))
    cr = _m01m(K['trirev'], dec_in)
    yield
    cs, rev = cr[:L], cr[L:]
    ea = _mm01(act, K['e_act'])
    ec = _mm01(cs, K['e_cs'])
    er = _mm01(rev, K['e_rev'])
    yield
    m_r = seg(ec, 'cs', 'm_r')
    g_c, s_c, m_c = seg(ec, 'cs', 'g_c'), seg(ec, 'cs', 's_c'), seg(ec, 'cs', 'm_c')
    b_r, b_c, dt_c = seg(ea, 'act', 'b_r'), seg(ea, 'act', 'b_c'), seg(ea, 'act', 'dt_c')
    i_r, i_c = seg(ea, 'act', 'i_r'), seg(ea, 'act', 'i_c')
    rg_c, rs_c, rm_c = seg(er, 'rev', 'g_c'), seg(er, 'rev', 's_c'), seg(er, 'rev', 'm_c')
    ym = _mm(dec_in, K['e_dr']).astype(BF16) * K['supper16']
    yield
    d_all = lax.dot_general(K['tri'], ym, _NN, preferred_element_type=F32)
    d_g, d_s, d_m = d_all[:, 0:4 * L], d_all[:, 4 * L:12 * L], d_all[:, 12 * L:16 * L]
    row_i = _m01m(K['ones_ll'], i_r * K['eye4'])

    res = dict(o_gdn=[None, None], gdn=[None, None], o_hg=[None, None], hg=[None, None],
               o_ml=[None, None], mlc=[None, None], mln=[None, None], mlm=[None] * N_HEAD)

    qkv = _conv_silu(ext_g, P['cwg'], P['cbg'], L)
    q, k, v = qkv[:, 0:256], qkv[:, 256:512], qkv[:, 512:768]
    ss = _mm(jnp.concatenate([q * q, k * k], axis=0), K['bd64'])
    yield
    q = q * (lax.rsqrt(ss[:L] + EPS) * (HEAD_D ** -0.5))
    k = k * lax.rsqrt(ss[L:] + EPS)

    def gdn_task(p):
        qt, kt, vt = pair_c(q, p), pair_c(k, p), pair_c(v, p)
        beta_c, gc = pair_c(b_c, p), pair_c(g_c, p)
        eg = jnp.exp(gc)
        dec = jnp.exp(pair_r(d_g, p)) * K['tril2']
        kkqk = _mm(jnp.concatenate([kt, qt], axis=0), stack(kt), _NT)
        yield
        n = (kkqk[:L] * dec * pair_r(b_r, p) * K['strict2']).astype(BF16)
        nbd = jnp.concatenate([n, n], axis=0)
        t = K['eyep'] - (nbd * K['lvlp'][0]).astype(F32)
        n_fine = 2 * L // FINE_ROWS
        tc = t[0:FINE_ROWS]
        for i in range(1, n_fine):
            tc = tc + t[i * FINE_ROWS:(i + 1) * FINE_ROWS]

        def expand(mc):
            return jnp.concatenate([mc] * n_fine, axis=0) * K['bdf'] if n_fine > 1 else mc

        for j in range(1, J):
            b = 1 << j
            if 2 * b <= FINE_ROWS:
                x = _mm(tc, nbd * K['lvlp'][j])
                yield
                tc = tc - _mm(x, expand(tc))
                yield
                if 4 * b > FINE_ROWS or j == J - 1:
                    t = expand(tc)
                continue
            starts = [h * L + s + b for h in range(2) for s in range(0, L, 2 * b)]
            x = _mm(jnp.concatenate([t[r:r + b] for r in starts], axis=0), nbd * K['lvlp'][j])
            yield
            upd = _mm(x, t)
            zero = jnp.zeros((b, 2 * L), F32)
            t = t - jnp.concatenate([blk for i in range(len(starts)) for blk in (zero, upd[i * b:(i + 1) * b])],
                                    axis=0)
            yield
        sol = _mm(t, jnp.concatenate([stack(vt * beta_c), stack(kt * (beta_c * eg))], axis=1))
        yield
        sol = sol[:L] + sol[L:]
        s_p = st['gdn'][p]
        u = sol[:, 0:LANE] - _mm(sol[:, LANE:2 * LANE], s_p)
        yield
        qo = jnp.concatenate([(qt * eg).astype(BF16), (kkqk[L:] * dec).astype(BF16)], axis=1)
        res['o_gdn'][p] = _mm(qo, jnp.concatenate([s_p.astype(BF16), stack(u)], axis=0))
        kw = kt * (jnp.exp(pair_c(rg_c, p)) * rowvalid)
        res['gdn'][p] = s_p * jnp.exp(gc[nv - 1:nv, :]) + bdp * _mm(kw, u, _TN)
        yield

    lg = P['lg']
    mx = lg[0:1, :]
    for i in range(1, DEPTH):
        mx = jnp.maximum(mx, lg[i:i + 1, :])
    ex = [jnp.exp(lg[i:i + 1, :] - mx) for i in range(DEPTH)]
    tot = ex[0]
    for i in range(1, DEPTH):
        tot = tot + ex[i]
    sm = [e / tot for e in ex]
    cum = sm[0]
    for i in range(1, layer + 1):
        cum = cum + sm[i]
    lb = cum - sm[0]
    lb_pos = lb > 0
    log_lb = jnp.log(jnp.where(lb_pos, lb, 1.0))
    hq = _silu(pm[:, 1024:1280])
    fz = pm[:, 1280:1536]
    hv = pm[:, 1536:1792]
    ls = _logsig(fz)
    t2 = jnp.log1p(-lb) + ls
    la = jnp.maximum(log_lb, t2) + jnp.log1p(jnp.exp(-jnp.abs(log_lb - t2)))
    logf = jnp.where(lb_pos, la, ls)
    kg = (1.0 - lb) * _sigmoid(-fz)
    ey = jnp.exp(_m01m(K['mhg'], logf))
    yield
    eg_h = ey[J * L:(J + 1) * L]
    eyb = ey[0:J * L].astype(BF16)
    hqb = hq.astype(BF16)
    qe = hq * eg_h
    kwr = kg * (ey[(J + 1) * L:(J + 2) * L] * rowvalid)

    def hg_task(p):
        q_p, k_st = pair_c(hqb, p), stack(pair_c(kg, p))
        a_p = K['hmaskp'][0] * _mm(q_p, k_st, _NT)
        yield
        for j in range(J):
            e_j = pair_c(eyb[j * L:(j + 1) * L], p)
            a_p = a_p + K['hmaskp'][j + 1] * _mm(q_p * e_j, k_st * jnp.concatenate([e_j, e_j], axis=0), _NT)
            yield
        st_p = st['hg'][p]
        res['o_hg'][p] = _mm(pair_c(qe, p), st_p, _NT) + _mm(a_p, stack(pair_c(hv, p)))
        yield
        res['hg'][p] = (st_p * pair_c(eg_h[nv - 1:nv, :], p)
                        + bdp * _mm(pair_c(hv, p), pair_c(kwr, p), _TN))
        yield

    xbc = _conv_silu(ext_s, P['cws'], P['cbs'], L)
    xs, bs, cc = xbc[:, 0:256], xbc[:, 256:384], xbc[:, 384:512]
    vs = xs * dt_c
    bsb = bs.astype(BF16)
    bs4 = jnp.concatenate([bsb] * 4, axis=0)

    def ssd_task():
        o_intra = []
        for g in range(SSD_G):
            cbw = _mm(cc * K['pm2'][g:g + 1, :], bs4, _NT)
            dec = jnp.exp(d_s[:, g * 4 * L:(g + 1) * 4 * L]) * K['tril4']
            vb = pair_c(vs, g).astype(BF16)
            o_intra.append(_mm(cbw * dec, jnp.concatenate([vb] * 4, axis=0) * K['stk4s']))
            yield
        s_all = st['ssd']
        res['o_ssd'] = jnp.exp(s_c) * _mm(cc, s_all) + jnp.concatenate(o_intra, axis=1)
        res['ssd'] = (s_all * jnp.exp(s_c[nv - 1:nv, :])
                      + K['gbd'] * _mm(bs * rowvalid, vs * jnp.exp(rs_c), _TN))
        yield

    mq = pm[:, 2816:3072] * (HEAD_D ** -0.5)
    mk = pm[:, 3072:3328]
    mv = pm[:, 3328:3584]

    def ml_task(p, delay):
        for _ in range(delay):
            yield
        qt, kt, vt = pair_c(mq, p), pair_c(mk, p), pair_c(mv, p)
        bm_r, bm_c = pair_r(m_r, p), pair_c(m_c, p)
        mp0 = st['mlm'][:, 2 * p:2 * p + 1]
        mp1 = st['mlm'][:, 2 * p + 1:2 * p + 2]
        logw = jnp.where(tril2, pair_r(d_m, p) + pair_r(row_i, p), NEG)
        mx0 = jnp.max(jnp.where(upper_r, NEG, logw), axis=-1, keepdims=True)
        mx1 = jnp.max(jnp.where(upper_r, logw, NEG), axis=-1, keepdims=True)
        l0_r = bm_r + jnp.where(upper_r, mp1, mp0)
        l0_c = bm_c + jnp.where(upper_c, mp1, mp0)
        mt_r = jnp.maximum(l0_r, jnp.where(upper_r, mx1, mx0))
        mt_c = jnp.maximum(l0_c, jnp.where(upper_c, mx1, mx0))
        w = jnp.exp(logw - mt_r)
        w0 = jnp.exp(l0_c - mt_c)
        qk = _mm(qt, stack(kt), _NT) * w
        yield
        c_p = st['mlc'][p]
        n_p = st['mln'][p]
        qn = jnp.concatenate([(w0 * qt).astype(BF16), qk.astype(BF16)], axis=1)
        num = _mm(qn, jnp.concatenate([c_p.astype(BF16), stack(vt)], axis=0))
        den = w0 * _mm01(qt * n_p, K['bdp2']) + _mm01(qk, K['stkp2'])
        res['o_ml'][p] = num * (1.0 / jnp.maximum(jnp.abs(den), jnp.exp(-mt_c)))
        yield
        m_l = mt_c[nv - 1:nv, :]
        wl0 = jnp.exp(bm_c[nv - 1:nv, :] + jnp.where(upper_c, mp1, mp0) - m_l)
        kwl = kt * (jnp.exp(pair_c(rm_c, p) + pair_c(i_c, p) - m_l) * rowvalid)
        res['mlc'][p] = c_p * wl0 + bdp * _mm(kwl, vt, _TN)
        res['mln'][p] = n_p * wl0 + jnp.sum(kwl, axis=0, keepdims=True)
        res['mlm'][2 * p] = m_l[:, 0:1]
        res['mlm'][2 * p + 1] = m_l[:, HEAD_D:HEAD_D + 1]
        yield

    def finish():
        o_gdn = jnp.concatenate(res['o_gdn'], axis=1)
        o_hg = jnp.concatenate(res['o_hg'], axis=1)
        hh = jnp.concatenate(res['o_ml'], axis=1)
        ms = _mm(jnp.concatenate([o_gdn * o_gdn, o_hg * o_hg, hh * hh], axis=0), K['bd64']) * (1.0 / HEAD_D)
        yield
        out_a = o_gdn * lax.rsqrt(ms[0:L] + EPS) * P['vec'][0:1, :] * _silu(pm[:, 768:1024])
        out_b = o_hg * lax.rsqrt(ms[L:2 * L] + EPS) * P['vec'][1:2, :] * _silu(pm[:, 1792:2048])
        out_d = hh * lax.rsqrt(ms[2 * L:3 * L] + EPS) * P['vec'][3:4, :] * _sigmoid(pm[:, 3584:3840])
        ys = (res['o_ssd'] + P['vec'][4:5, :] * xs) * _silu(pm[:, 2048:2304])
        out_c = ys * lax.rsqrt(_mm(ys * ys, K['bd128']) * (1.0 / (2 * HEAD_D)) + EPS) * P['vec'][2:3, :]
        branches = jnp.concatenate([out_a, out_b, out_c, out_d], axis=1)
        new = dict(gdn=res['gdn'], hg=res['hg'], ssd=res['ssd'], mlc=res['mlc'], mln=res['mln'], mlm=res['mlm'])
        out['result'] = (branches, new)
        yield

    out['tasks'] = ([gdn_task(p) for p in range(2)] + [hg_task(p) for p in range(2)]
                    + [ssd_task()] + [ml_task(p, 3 + 2 * p) for p in range(2)])
    out['finish'] = finish


def _mixer_kernel(L, nv, J, layer, bb, offs, has_init, n_alias, *refs):
    it = iter(refs)
    pm_ref = next(it)
    if has_init:
        gdn0, cg0, hg0, ssd0, cs0, mc0, mn0, mm0 = (next(it) for _ in range(8))
    sp_ref, cwg_ref, cbg_ref, cws_ref, cbs_ref, vec_ref, lg_ref = (next(it) for _ in range(7))
    kref = {name: next(it) for name in _CONST_ORDER}
    for _ in range(n_alias):
        next(it)
    br_ref = next(it)
    gdn1, cg1, hg1, ssd1, cs1, mc1, mn1, mm1 = (next(it) for _ in range(8))
    sg, sh, sc, sn, ss, extg, exts, pmpad = (next(it) for _ in range(8))

    c = pl.program_id(1)
    nc = pl.num_programs(1)
    n_sub = SSD_H // SSD_G

    @pl.when(c == 0)
    def _init():
        sg[...] = jnp.zeros_like(sg)
        sh[...] = jnp.zeros_like(sh)
        sc[...] = jnp.zeros_like(sc)
        ss[...] = jnp.zeros_like(ss)
        extg[:, 0:8, :] = jnp.zeros((bb, 8, GDN_CONV_W), F32)
        exts[:, 0:8, :] = jnp.zeros((bb, 8, SSD_CONV_W), F32)
        if not has_init:
            sn[...] = jnp.zeros_like(sn)
            mm1[...] = jnp.zeros_like(mm1)
            return
        for s in range(bb):
            for h in range(N_HEAD):
                p, lo = h // 2, (h % 2) * HEAD_D
                sg[s, p, lo:lo + HEAD_D, lo:lo + HEAD_D] = gdn0[s, h]
                sh[s, p, lo:lo + HEAD_D, lo:lo + HEAD_D] = hg0[s, h].T
                sc[s, p, lo:lo + HEAD_D, lo:lo + HEAD_D] = mc0[s, h]
                sn[s, p, :, lo:lo + HEAD_D] = mn0[s, h:h + 1, :]
            for h in range(SSD_H):
                g = h // n_sub
                ss[s, g * SSD_N:(g + 1) * SSD_N, h * SSD_P:(h + 1) * SSD_P] = ssd0[s, h]
        mm1[...] = mm0[...]
        extg[:, 5:8, :] = cg0[...]
        exts[:, 5:8, :] = cs0[...]

    P = dict(sp=sp_ref[...], cwg=cwg_ref[...], cbg=cbg_ref[...], cws=cws_ref[...], cbs=cbs_ref[...],
             vec=vec_ref[...], lg=lg_ref[...])
    K = {name: r[...] for name, r in kref.items()}
    stage1, outs = [], []
    spg = L // nv
    for s in range(bb):
        grp, off = s // spg, (s % spg) * nv
        if nv == L:
            pm = pm_ref[s]
        else:
            pmpad[s, 0:nv, :] = pm_ref[grp, off:off + nv, :]
            pmpad[s, nv:L, :] = jnp.zeros((L - nv, MIX_W), F32)
            pm = pmpad[s]
        extg[s, 8:8 + L, :] = pm[:, 0:GDN_CONV_W]
        exts[s, 8:8 + L, :] = pm[:, SSD_XBC_OFF:SSD_XBC_OFF + SSD_CONV_W]
        st = dict(gdn=sg[s], hg=sh[s], ssd=ss[s], mlc=sc[s], mln=sn[s], mlm=mm1[s])
        outs.append({})
        stage1.append(_mixer_chunk(pm, extg.at[s], exts.at[s], st, P, K, offs, L, nv, J, layer, outs[s]))

    def flow(s):
        for _ in range(FLOW_DELAY[L] * s):
            yield
        yield from stage1[s]
        tasks = list(outs[s]['tasks'])
        while tasks:
            alive = []
            for t in tasks:
                try:
                    next(t)
                    alive.append(t)
                except StopIteration:
                    pass
            tasks = alive
            yield
        yield from outs[s]['finish']()

    _run_interleaved([flow(s) for s in range(bb)])
    for s in range(bb):
        branches, new = outs[s]['result']
        grp, off = s // spg, (s % spg) * nv
        br_ref[grp, off:off + nv, :] = branches[0:nv].astype(br_ref.dtype)
        for p in range(2):
            sg[s, p] = new['gdn'][p]
            sh[s, p] = new['hg'][p]
            sc[s, p] = new['mlc'][p]
            sn[s, p] = new['mln'][p]
        for h in range(N_HEAD):
            mm1[s, :, h:h + 1] = new['mlm'][h]
        ss[s] = new['ssd']
        tail_g = extg[s, 8 + nv - 3:8 + nv, :]
        tail_s = exts[s, 8 + nv - 3:8 + nv, :]
        extg[s, 5:8, :] = tail_g
        exts[s, 5:8, :] = tail_s
        cg1[s] = tail_g
        cs1[s] = tail_s

    @pl.when(c == nc - 1)
    def _fin():
        for s in range(bb):
            for h in range(N_HEAD):
                p, lo = h // 2, (h % 2) * HEAD_D
                gdn1[s, h] = sg[s, p, lo:lo + HEAD_D, lo:lo + HEAD_D]
                hg1[s, h] = sh[s, p, lo:lo + HEAD_D, lo:lo + HEAD_D].T
                mc1[s, h] = sc[s, p, lo:lo + HEAD_D, lo:lo + HEAD_D]
                mn1[s, h:h + 1, :] = sn[s, p, :, lo:lo + HEAD_D]
            for h in range(SSD_H):
                g = h // n_sub
                ssd1[s, h] = ss[s, g * SSD_N:(g + 1) * SSD_N, h * SSD_P:(h + 1) * SSD_P]


def _full_spec(a):
    nd = a.ndim
    return pl.BlockSpec(a.shape, lambda b, c, _nd=nd: (0,) * _nd)


def _mixer_call(proj, state_shapes, states_in, prev_out, params, L, nv, layer, bb):
    g, t, _ = proj.shape
    spg = L // nv if nv < L else 1
    nchunk = t // L
    assert (g * spg) % bb == 0 and bb % spg == 0 and t % L == 0
    consts, offs, J = _mixer_consts(L, nv)
    const_arrays = [jnp.asarray(consts[n], BF16 if n in _BF16_CONSTS else F32) for n in _CONST_ORDER]

    def st_spec(shape):
        nd = len(shape)
        return pl.BlockSpec((None, bb) + tuple(shape[2:]), lambda b, c, _nd=nd: (layer, b) + (0,) * (_nd - 2))

    has_init = states_in is not None
    n_alias = 0 if prev_out is None else len(prev_out)
    inputs = [proj] + (list(states_in) if has_init else []) + list(params) + const_arrays + list(prev_out or [])
    in_specs = ([pl.BlockSpec((bb // spg, L, MIX_W), lambda b, c: (b, c, 0))]
                + ([st_spec(s) for s in state_shapes] if has_init else [])
                + [_full_spec(a) for a in params]
                + [_full_spec(a) for a in const_arrays]
                + [pl.BlockSpec(memory_space=pl.ANY)] * n_alias)
    out_shape = ([jax.ShapeDtypeStruct((g, t, N_BRANCH * BRANCH_W), _row_dtype(L))]
                 + [jax.ShapeDtypeStruct(s, F32) for s in state_shapes])
    out_specs = ([pl.BlockSpec((bb // spg, L, N_BRANCH * BRANCH_W), lambda b, c: (b, c, 0))]
                 + [st_spec(s) for s in state_shapes])
    first_alias = len(inputs) - n_alias
    pair = (bb, 2, LANE, LANE)
    scratch = [pltpu.VMEM(pair, F32), pltpu.VMEM(pair, F32), pltpu.VMEM(pair, F32),
               pltpu.VMEM((bb, 2, 1, LANE), F32), pltpu.VMEM((bb, SSD_G * SSD_N, SSD_H * SSD_P), F32),
               pltpu.VMEM((bb, 8 + L, GDN_CONV_W), F32), pltpu.VMEM((bb, 8 + L, SSD_CONV_W), F32),
               pltpu.VMEM((bb, L, MIX_W) if nv < L else (1, 8, LANE), F32)]
    outs = pl.pallas_call(
        functools.partial(_mixer_kernel, L, nv, J, layer, bb, offs, has_init, n_alias),
        grid=(g * spg // bb, nchunk),
        in_specs=in_specs, out_specs=out_specs, out_shape=out_shape, scratch_shapes=scratch,
        input_output_aliases={first_alias + k: 1 + k for k in range(n_alias)},
        compiler_params=pltpu.CompilerParams(dimension_semantics=("parallel", "arbitrary"),
                                             vmem_limit_bytes=VMEM_LIMIT),
        name=f"mixer_L{L}",
    )(*inputs)
    return outs[0], list(outs[1:])


def _ada_kernel(c_ref, w_ref, b_ref, o_ref):
    o_ref[...] = _mm(_silu(c_ref[...]), w_ref[...]) + b_ref[...]


def _ada_call(c_all, ada_w, ada_b):
    rows = c_all.shape[0]
    n = ada_w.shape[-1]
    tn = 1536
    return pl.pallas_call(
        _ada_kernel,
        grid=(DEPTH, n // tn),
        in_specs=[pl.BlockSpec((rows, D_MODEL), lambda l, j: (0, 0)),
                  pl.BlockSpec((None, D_MODEL, tn), lambda l, j: (l, 0, j)),
                  pl.BlockSpec((None, 1, tn), lambda l, j: (l, 0, j))],
        out_specs=pl.BlockSpec((None, rows, tn), lambda l, j: (l, 0, j)),
        out_shape=jax.ShapeDtypeStruct((DEPTH, rows, n), F32),
        compiler_params=pltpu.CompilerParams(dimension_semantics=("arbitrary", "arbitrary"),
                                             vmem_limit_bytes=VMEM_LIMIT),
        name="ada",
    )(c_all, ada_w, ada_b.reshape(DEPTH, 1, n))


PREP_ROWS = MIX_W - SMALL_OFF


def _prep_w_in_kernel(w_ref, s0_ref, s1_ref, s2_ref, o_ref):
    j = pl.program_id(1)
    j_small = SMALL_OFF // PREP_ROWS

    @pl.when(j != j_small)
    def _copy():
        o_ref[...] = w_ref[0].astype(BF16)

    @pl.when(j == j_small)
    def _small():
        o_ref[...] = jnp.zeros(o_ref.shape, BF16)
        o_ref[0:8, :] = s0_ref[0].astype(BF16)
        o_ref[8:16, :] = s1_ref[0].astype(BF16)
        o_ref[16:24, :] = s2_ref[0].astype(BF16)


def _prep_src_row(j):
    r = PREP_ROWS
    return jnp.where(j < 1024 // r, r * j,
                     jnp.where(j < 2816 // r, r * j + 8,
                               jnp.where(j < SMALL_OFF // r, r * j + 16,
                                         jnp.where(j < MIX_W // r, 0, r * j - MIX_W + 3864))))


def _prep_w_in_call(w_in):
    wt = jnp.swapaxes(w_in, 1, 2)
    depth, _, d = wt.shape

    def rows(n, index):
        return pl.BlockSpec((pl.Element(1), pl.Element(n), pl.Element(d)), index)

    return pl.pallas_call(
        _prep_w_in_kernel,
        grid=(depth, PROJ_W // PREP_ROWS),
        in_specs=[rows(PREP_ROWS, lambda l, j: (l, pl.multiple_of(_prep_src_row(j), 8), 0)),
                  rows(8, lambda l, j: (l, 1024, 0)), rows(8, lambda l, j: (l, 2824, 0)),
                  rows(8, lambda l, j: (l, 3856, 0))],
        out_specs=pl.BlockSpec((None, PREP_ROWS, d), lambda l, j: (l, j, 0)),
        out_shape=jax.ShapeDtypeStruct((depth, PROJ_W, d), BF16),
        compiler_params=pltpu.CompilerParams(dimension_semantics=("arbitrary", "arbitrary"),
                                             vmem_limit_bytes=VMEM_LIMIT),
        name="prep_w_in",
    )(wt, wt, wt, wt)


def _rms_mod(x, nw, sc, sh):
    ms = jnp.mean(x * x, axis=-1, keepdims=True)
    return x * lax.rsqrt(ms + EPS) * nw * (1.0 + sc) + sh


def _tok_spec(gblk, width):
    g, r = gblk
    return pl.BlockSpec((g, r, width), lambda i, j: (i, j, 0))


def _mod_spec(gblk, layer, k, mods):
    return pl.BlockSpec((None, gblk[0], mods.shape[2], D_MODEL), lambda i, j: (layer, i, 0, k))


def _mod_rows(m, r):
    spg = m.shape[1]
    if spg == 1:
        return m
    seq = lax.broadcasted_iota(jnp.int32, (1, r, 1), 1) // (r // spg)
    out = m[:, 0:1, :]
    for j in range(1, spg):
        out = jnp.where(seq == j, m[:, j:j + 1, :], out)
    return out


def _layer_spec(shape, layer):
    nd = len(shape)
    return pl.BlockSpec((None,) + tuple(shape[1:]), lambda i, j: (layer,) + (0,) * (nd - 1),
                        pipeline_mode=pl.Buffered(1))


def _dense_params():
    return pltpu.CompilerParams(dimension_semantics=("arbitrary", "arbitrary"), vmem_limit_bytes=VMEM_LIMIT)


def _inproj_kernel(x_ref, nw_ref, sc_ref, sh_ref, w_ref, pm_ref, gate_ref):
    g, r, _ = x_ref.shape
    h = _rms_mod(x_ref[...], nw_ref[...], _mod_rows(sc_ref[...], r), _mod_rows(sh_ref[...], r))
    h = h.astype(BF16).reshape(g * r, D_MODEL)
    pm_ref[...] = _mm(h, w_ref[0:MIX_W, :], _NT).reshape(g, r, MIX_W)
    gates = _sigmoid(_mm(h, w_ref[MIX_W:PROJ_W, :], _NT))
    gate_ref[...] = gates.astype(gate_ref.dtype).reshape(g, r, GATE_W)


def _inproj_call(x, layer, nw, mods, w, gblk):
    bg, t, _ = x.shape
    return pl.pallas_call(
        _inproj_kernel,
        grid=(bg // gblk[0], t // gblk[1]),
        in_specs=[_tok_spec(gblk, D_MODEL), _layer_spec(nw.shape, layer),
                  _mod_spec(gblk, layer, 1, mods), _mod_spec(gblk, layer, 0, mods), _layer_spec(w.shape, layer)],
        out_specs=[_tok_spec(gblk, MIX_W), _tok_spec(gblk, GATE_W)],
        out_shape=[jax.ShapeDtypeStruct((bg, t, MIX_W), F32),
                   jax.ShapeDtypeStruct((bg, t, GATE_W), _row_dtype(gblk[1]))],
        compiler_params=_dense_params(), name="inproj",
    )(x, nw, mods, mods, w)


def _merge_ffn_kernel(final, br_ref, gate_ref, x_ref, gt1_ref, nw_ref, sc_ref, sh_ref, gt2_ref,
                      wb_ref, wo_ref, w1_ref, w2_ref, fw_ref, o_ref):
    g, r, _ = x_ref.shape
    br = br_ref[...].reshape(g * r, N_BRANCH * BRANCH_W)
    merged = None
    for n in range(N_BRANCH):
        up = _mm(br[:, n * BRANCH_W:(n + 1) * BRANCH_W], wb_ref[n])
        gate = gate_ref[:, :, n * D_MODEL:(n + 1) * D_MODEL].astype(F32).reshape(g * r, D_MODEL)
        t = gate * up
        merged = t if merged is None else merged + t
    x1 = x_ref[...] + _mod_rows(gt1_ref[...], r) * _mm(merged, wo_ref[...]).reshape(g, r, D_MODEL)
    h = _rms_mod(x1, nw_ref[...], _mod_rows(sc_ref[...], r), _mod_rows(sh_ref[...], r))
    h = h.astype(BF16).reshape(g * r, D_MODEL)
    a = _mm(h, w1_ref[:, 0:D_FF])
    b = _mm(h, w1_ref[:, D_FF:2 * D_FF])
    x2 = x1 + _mod_rows(gt2_ref[...], r) * _mm(_silu(a) * b, w2_ref[...]).reshape(g, r, D_MODEL)
    if final:
        ms = jnp.mean(x2 * x2, axis=-1, keepdims=True)
        x2 = x2 * lax.rsqrt(ms + EPS) * fw_ref[...]
    o_ref[...] = x2


def _merge_ffn_call(br, gates, x, layer, nw, mods, wb, wo, w1, w2, fw, gblk, final):
    bg, t, _ = x.shape
    return pl.pallas_call(
        functools.partial(_merge_ffn_kernel, final),
        grid=(bg // gblk[0], t // gblk[1]),
        in_specs=[_tok_spec(gblk, N_BRANCH * BRANCH_W), _tok_spec(gblk, GATE_W), _tok_spec(gblk, D_MODEL),
                  _mod_spec(gblk, layer, 2, mods), _layer_spec(nw.shape, layer),
                  _mod_spec(gblk, layer, 4, mods), _mod_spec(gblk, layer, 3, mods), _mod_spec(gblk, layer, 5, mods),
                  _layer_spec(wb.shape, layer), _layer_spec(wo.shape, layer),
                  _layer_spec(w1.shape, layer), _layer_spec(w2.shape, layer),
                  pl.BlockSpec(fw.shape, lambda i, j: (0, 0), pipeline_mode=pl.Buffered(1))],
        out_specs=_tok_spec(gblk, D_MODEL),
        out_shape=jax.ShapeDtypeStruct((bg, t, D_MODEL), F32),
        compiler_params=_dense_params(), name="merge_ffn",
    )(br, gates, x, mods, nw, mods, mods, mods, wb, wo, w1, w2, fw)


def _small_params(gdn_dt_bias, ssd_dt_bias, ml_b_i, ml_b_f, gdn_a_log, ssd_a_log):
    z4 = jnp.zeros((4,), F32)
    bias = jnp.concatenate([z4, gdn_dt_bias, ssd_dt_bias, ml_b_i, ml_b_f, jnp.zeros((LANE - 24,), F32)])
    alog = jnp.concatenate([z4, gdn_a_log, ssd_a_log, jnp.zeros((LANE - 16,), F32)])
    return jnp.concatenate([bias[None], alog[None], jnp.zeros((6, LANE), F32)], axis=0)


def _layer_params(l, gdn_conv_w, gdn_conv_b, gdn_a_log, gdn_dt_bias, gdn_norm_w, hg_lb_logits, hg_norm_w,
                  ssd_conv_w, ssd_conv_b, ssd_a_log, ssd_dt_bias, ssd_d, ssd_norm_w, ml_b_i, ml_b_f, ml_norm_w):
    sp = _small_params(gdn_dt_bias[l], ssd_dt_bias[l], ml_b_i[l], ml_b_f[l], gdn_a_log[l], ssd_a_log[l])
    vec = jnp.stack([jnp.tile(gdn_norm_w[l], N_HEAD), jnp.tile(hg_norm_w[l], N_HEAD), ssd_norm_w[l],
                     jnp.tile(ml_norm_w[l], N_HEAD), jnp.repeat(ssd_d[l], SSD_P)]
                    + [jnp.zeros((256,), F32)] * 3)
    return [sp, gdn_conv_w[l], gdn_conv_b[l][None], ssd_conv_w[l], ssd_conv_b[l][None], vec,
            hg_lb_logits.astype(F32)]


def _trunk(x, mods, state_shapes, states_in, L, nv, gblk, gblk_ffn, bb, W, mixer_params):
    new_states = None
    for l in range(DEPTH):
        pm, gates = _inproj_call(x, l, W['norm1'], mods, W['w_in'], gblk)
        br, new_states = _mixer_call(pm, state_shapes, states_in, new_states, mixer_params[l], L, nv, l, bb)
        x = _merge_ffn_call(br, gates, x, l, W['norm2'], mods, W['w_branch'], W['w_out'],
                            W['ffn_w_in'], W['ffn_w_out'], W['final'], gblk_ffn, final=(l == DEPTH - 1))
    return x, new_states


def kernel(x_prompt, x_sample, c_prompt, c_sample, state_gdn, state_gdn_conv, state_hgrn, state_ssd, state_ssd_conv, state_mlstm_c, state_mlstm_n, state_mlstm_m, ada_w, ada_b, norm1_w, norm2_w, w_in, gdn_conv_w, gdn_conv_b, gdn_a_log, gdn_dt_bias, gdn_norm_w, hg_lb_logits, hg_norm_w, ssd_conv_w, ssd_conv_b, ssd_a_log, ssd_dt_bias, ssd_d, ssd_norm_w, ml_b_i, ml_b_f, ml_norm_w, w_branch, w_out, ffn_w_in, ffn_w_out, final_norm_w):
    bp, tp, _ = x_prompt.shape
    bs, ts, _ = x_sample.shape
    assert tp % CHUNK == 0 and LS % ts == 0 and bs % (LS // ts) == 0

    W = dict(w_in=_prep_w_in_call(w_in), w_branch=w_branch.astype(BF16), w_out=w_out.astype(BF16),
             ffn_w_in=ffn_w_in.astype(BF16), ffn_w_out=ffn_w_out.astype(BF16),
             norm1=norm1_w[:, None, :], norm2=norm2_w[:, None, :], final=final_norm_w[None])
    mixer_params = [_layer_params(l, gdn_conv_w, gdn_conv_b, gdn_a_log, gdn_dt_bias, gdn_norm_w, hg_lb_logits,
                                  hg_norm_w, ssd_conv_w, ssd_conv_b, ssd_a_log, ssd_dt_bias, ssd_d, ssd_norm_w,
                                  ml_b_i, ml_b_f, ml_norm_w) for l in range(DEPTH)]

    mods = _ada_call(jnp.concatenate([c_prompt, c_sample], axis=0), ada_w, ada_b)
    mods_p = mods[:, :bp].reshape(DEPTH, bp, 1, 6 * D_MODEL)
    grp = LS // ts
    mods_s = mods[:, bp:].reshape(DEPTH, bs // grp, grp, 6 * D_MODEL)
    sample_states = [state_gdn, state_gdn_conv, state_hgrn, state_ssd, state_ssd_conv,
                     state_mlstm_c, state_mlstm_n, state_mlstm_m.reshape(DEPTH, bs, 1, N_HEAD)]
    shapes_s = [s.shape for s in sample_states]
    shapes_p = [(DEPTH, bp) + tuple(s[2:]) for s in shapes_s]

    y_p, new_p = _trunk(x_prompt, mods_p, shapes_p, None, CHUNK, CHUNK, (1, 256), (1, 512), 8, W, mixer_params)
    y_s, new_s = _trunk(x_sample.reshape(bs // grp, LS, D_MODEL), mods_s, shapes_s, sample_states, LS, ts,
                        (32, LS), (32, LS), 8, W, mixer_params)
    new_p[7] = new_p[7].reshape(DEPTH, bp, N_HEAD)
    new_s[7] = new_s[7].reshape(DEPTH, bs, N_HEAD)
    return (y_p, y_s.reshape(bs, ts, D_MODEL)) + tuple(new_p) + tuple(new_s)
```

```python
import functools
import math

import numpy as np
import jax
import jax.numpy as jnp
from jax import lax
from jax.experimental import pallas as pl
from jax.experimental.pallas import tpu as pltpu

F32 = jnp.float32
BF16 = jnp.bfloat16

D_MODEL = 1024
DEPTH = 2
N_BRANCH = 4
BRANCH_W = 256
N_HEAD = 4
HEAD_D = 64
SSD_H = 8
SSD_P = 32
SSD_N = 64
SSD_G = 2
CONV_K = 4
CHUNK = 64
LS = 8
FLOW_DELAY = {CHUNK: 2, LS: 3}
D_FF = 2816
EPS = 1e-6
NEG = -1e30
GDN_CONV_W = 768
SSD_CONV_W = 512
SSD_XBC_OFF = 2304
MIX_W = 4096
GATE_W = 4096
PROJ_W = MIX_W + GATE_W
SMALL_OFF = 3840
LANE = 128
FINE_ROWS = 16
VMEM_LIMIT = 56 * 1024 * 1024

ST_BETA, ST_GA, ST_DT, ST_MI, ST_MF = 0, 4, 8, 16, 20

_NN = (((1,), (0,)), ((), ()))
_NT = (((1,), (1,)), ((), ()))
_TN = (((0,), (0,)), ((), ()))


def _mm(a, b, dims=_NN):
    return lax.dot_general(a.astype(BF16), b.astype(BF16), dims, preferred_element_type=F32)


def _split(x, n):
    parts, r = [], x
    for i in range(n):
        p = r.astype(BF16)
        parts.append(p)
        if i < n - 1:
            r = r - p.astype(F32)
    return parts


N_PIECE = 2


def _mm01(x, m2):
    pieces = _split(x, N_PIECE)
    k = x.shape[1]
    if x.shape[0] % 16 == 0:
        return lax.dot_general(jnp.concatenate(pieces, axis=1), m2.astype(BF16), _NN, preferred_element_type=F32)
    out = None
    for i, p in enumerate(pieces):
        t = lax.dot_general(p, m2[i * k:(i + 1) * k, :].astype(BF16), _NN, preferred_element_type=F32)
        out = t if out is None else out + t
    return out


def _m01m(m2, x):
    pieces = _split(x, N_PIECE)
    k = x.shape[0]
    if k % 16 == 0:
        return lax.dot_general(m2, jnp.concatenate(pieces, axis=0), _NN, preferred_element_type=F32)
    out = None
    for i, p in enumerate(pieces):
        t = lax.dot_general(m2[:, i * k:(i + 1) * k], p, _NN, preferred_element_type=F32)
        out = t if out is None else out + t
    return out


def _sigmoid(x):
    return jax.nn.sigmoid(x)


def _silu(x):
    return x * jax.nn.sigmoid(x)


def _softplus(x):
    return jnp.maximum(x, 0.0) + jnp.log(1.0 + jnp.exp(-jnp.abs(x)))


def _logsig(x):
    return jnp.minimum(x, 0.0) - jnp.log(1.0 + jnp.exp(-jnp.abs(x)))


def _row_dtype(rows):
    return BF16 if rows % 16 == 0 else F32


def _expand(base, heads, width):
    e = np.zeros((LANE, heads * width), np.float32)
    for h in range(heads):
        e[base + h, h * width:(h + 1) * width] = 1.0
    return e


def _cat_segments(segs):
    cols, off, pos, seen = [], {}, 0, {}
    for name, m in segs:
        key = m.tobytes() + bytes(str(m.shape), 'ascii')
        if key in seen:
            off[name] = seen[key]
            continue
        w = m.shape[1]
        wp = -(-w // LANE) * LANE
        mp = np.zeros((m.shape[0], wp), np.float32)
        mp[:, :w] = m
        cols.append(mp)
        off[name] = seen[key] = (pos, w)
        pos += wp
    return np.concatenate(cols, axis=1), off


@functools.lru_cache(maxsize=None)
def _mixer_consts(L, nv):
    J = int(round(math.log2(L)))
    assert 1 << J == L
    f = np.float32
    r = np.arange(L)
    tri = (r[None, :] <= r[:, None])
    rev = (r[:, None] < r[None, :]) & (r[None, :] <= nv - 1)
    strict = (r[None, :] < r[:, None])
    eye = np.eye(L, dtype=bool)
    lvl, ab = [], []
    for j in range(J):
        bnd = ((r >> (j + 1)) << (j + 1)) + (1 << j) - 1
        low = ((r >> j) & 1) == 1
        a = low[:, None] & (bnd[:, None] < r[None, :]) & (r[None, :] <= r[:, None])
        b = (~low)[:, None] & (r[:, None] < r[None, :]) & (r[None, :] <= bnd[:, None])
        ab.append(a | b)
        same = (r[:, None] >> (j + 1)) == (r[None, :] >> (j + 1))
        lvl.append(same & low[:, None] & (~low)[None, :])

    def tile(m, n):
        return np.tile(m.astype(f), (1, n))

    pr = np.repeat(np.arange(2), L)
    lr2 = np.tile(r, 2)
    ch = np.arange(LANE)
    ch2 = np.arange(2 * LANE)
    c = dict(
        trirev=np.concatenate([tri, rev], axis=0).astype(f),
        mhg=np.concatenate(ab + [tri, rev], axis=0).astype(f),
        ones_ll=np.ones((L, L), f), tri=tri.astype(f),
        tril2=tile(tri, 2), strict2=tile(strict, 2), tril4=tile(tri, 4),
        eye4=tile(eye, 4), supper16=tile(strict, 16),
        e_dr=np.concatenate([_expand(ST_GA, 4, L), _expand(ST_DT, SSD_H, L), _expand(ST_MF, 4, L)], axis=1),
        hmaskp=np.stack([tile(eye, 2)] + [tile(m, 2) for m in lvl]),
        eyep=np.eye(2 * L, dtype=f),
        bdf=(np.arange(2 * L)[:, None] // FINE_ROWS == np.arange(2 * L)[None, :] // FINE_ROWS).astype(f),
        lvlp=np.stack([(pr[:, None] == pr[None, :]) & m[lr2[:, None], lr2[None, :]] for m in lvl]).astype(f),
        stkp=(pr[:, None] == ch[None, :] // HEAD_D).astype(f),
        bdp=(ch[:, None] // HEAD_D == ch[None, :] // HEAD_D).astype(f),
        pm2=np.stack([ch < HEAD_D, ch >= HEAD_D]).astype(f),
        stk4s=(np.repeat(np.arange(4), L)[:, None] == ch[None, :] // SSD_P).astype(f),
        gbd=(ch[:, None] // SSD_N == ch2[None, :] // LANE).astype(f),
        rowvalid=(r[:, None] <= nv - 1).astype(f) * np.ones((1, LANE), f),
        bd64=(ch2[:, None] // 64 == ch2[None, :] // 64).astype(f),
        bd128=(ch2[:, None] // 128 == ch2[None, :] // 128).astype(f),
    )
    c['e_act'], off_act = _cat_segments([('b_r', _expand(ST_BETA, 4, L)), ('b_c', _expand(ST_BETA, 4, HEAD_D)),
                                         ('dt_c', _expand(ST_DT, SSD_H, SSD_P)),
                                         ('i_r', _expand(ST_MI, 4, L)), ('i_c', _expand(ST_MI, 4, HEAD_D))])
    c['e_cs'], off_cs = _cat_segments([('g_c', _expand(ST_GA, 4, HEAD_D)), ('s_c', _expand(ST_DT, SSD_H, SSD_P)),
                                       ('m_r', _expand(ST_MF, 4, L)), ('m_c', _expand(ST_MF, 4, HEAD_D))])
    c['e_rev'], off_rev = _cat_segments([('g_c', _expand(ST_GA, 4, HEAD_D)), ('s_c', _expand(ST_DT, SSD_H, SSD_P)),
                                         ('m_c', _expand(ST_MF, 4, HEAD_D))])
    c['bdp2'], c['stkp2'] = c['bdp'], c['stkp']
    for name in ('trirev', 'mhg', 'ones_ll'):
        c[name] = np.tile(c[name], (1, N_PIECE))
    for name in ('e_act', 'e_cs', 'e_rev', 'bdp2', 'stkp2'):
        c[name] = np.tile(c[name], (N_PIECE, 1))
    return c, dict(act=off_act, cs=off_cs, rev=off_rev), J


_CONST_ORDER = ('trirev', 'mhg', 'ones_ll', 'tri', 'tril2', 'strict2', 'tril4', 'eye4', 'supper16', 'e_dr', 'hmaskp',
                'eyep', 'lvlp', 'stkp', 'bdp', 'pm2', 'stk4s', 'gbd', 'rowvalid', 'bd64', 'bd128', 'e_act', 'e_cs',
                'e_rev', 'bdp2', 'stkp2', 'bdf')
_BF16_CONSTS = ('trirev', 'mhg', 'ones_ll', 'tri', 'supper16', 'e_dr', 'lvlp', 'stkp', 'stk4s', 'bd64', 'bd128',
                'e_act', 'e_cs', 'e_rev', 'bdp2', 'stkp2')


def _conv_silu(ext, w, b, L):
    y = b
    for i in range(CONV_K):
        y = y + ext[5 + i:5 + i + L, :] * w[i:i + 1, :]
    return _silu(y)


def _run_interleaved(tasks):
    tasks = list(tasks)
    while tasks:
        alive = []
        for t in tasks:
            try:
                next(t)
                alive.append(t)
            except StopIteration:
                pass
        tasks = alive


def _mixer_chunk(pm, ext_g, ext_s, st, P, K, offs, L, nv, J, layer, out):
    lane = lax.broadcasted_iota(jnp.int32, (1, LANE), 1)
    rowvalid = K['rowvalid'][:, 0:1]
    tril2 = K['tril2'] > 0.0
    stkp, bdp = K['stkp'], K['bdp']
    upper_c = lane >= HEAD_D
    upper_r = lax.broadcasted_iota(jnp.int32, (1, 2 * L), 1) >= L

    def seg(x, which, name):
        o, w = offs[which][name]
        return x[:, o:o + w]

    def pair_c(x, p):
        return x[:, p * LANE:(p + 1) * LANE]

    def pair_r(x, p):
        return x[:, p * 2 * L:(p + 1) * 2 * L]

    def stack(x):
        xb = x.astype(BF16)
        return jnp.concatenate([xb, xb], axis=0) * stkp

    z = pm[:, SMALL_OFF:SMALL_OFF + LANE] + P['sp'][0:1, :]
    act = jnp.where(lane < ST_GA, _sigmoid(z),
                    jnp.where(lane < ST_MI, _softplus(z), jnp.where(lane < ST_MF, z, _logsig(z))))
    neg_a = -jnp.exp(P['sp'][1:2, :])
    dec_in = jnp.where((lane >= ST_GA) & (lane < ST_MI), neg_a * act,
                       jnp.where((lane >= ST_MF) & (lane < ST_MF + 4), act, 0.0))
    cr = _m01m(K['trirev'], dec_in)
    yield
    cs, rev = cr[:L], cr[L:]
    ea = _mm01(act, K['e_act'])
    ec = _mm01(cs, K['e_cs'])
    er = _mm01(rev, K['e_rev'])
    yield
    m_r = seg(ec, 'cs', 'm_r')
    g_c, s_c, m_c = seg(ec, 'cs', 'g_c'), seg(ec, 'cs', 's_c'), seg(ec, 'cs', 'm_c')
    b_r, b_c, dt_c = seg(ea, 'act', 'b_r'), seg(ea, 'act', 'b_c'), seg(ea, 'act', 'dt_c')
    i_r, i_c = seg(ea, 'act', 'i_r'), seg(ea, 'act', 'i_c')
    rg_c, rs_c, rm_c = seg(er, 'rev', 'g_c'), seg(er, 'rev', 's_c'), seg(er, 'rev', 'm_c')
    ym = _mm(dec_in, K['e_dr']).astype(BF16) * K['supper16']
    yield
    d_all = lax.dot_general(K['tri'], ym, _NN, preferred_element_type=F32)
    d_g, d_s, d_m = d_all[:, 0:4 * L], d_all[:, 4 * L:12 * L], d_all[:, 12 * L:16 * L]
    row_i = _m01m(K['ones_ll'], i_r * K['eye4'])

    res = dict(o_gdn=[None, None], gdn=[None, None], o_hg=[None, None], hg=[None, None],
               o_ml=[None, None], mlc=[None, None], mln=[None, None], mlm=[None] * N_HEAD)

    qkv = _conv_silu(ext_g, P['cwg'], P['cbg'], L)
    q, k, v = qkv[:, 0:256], qkv[:, 256:512], qkv[:, 512:768]
    ss = _mm(jnp.concatenate([q * q, k * k], axis=0), K['bd64'])
    yield
    q = q * (lax.rsqrt(ss[:L] + EPS) * (HEAD_D ** -0.5))
    k = k * lax.rsqrt(ss[L:] + EPS)

    def gdn_task(p):
        qt, kt, vt = pair_c(q, p), pair_c(k, p), pair_c(v, p)
        beta_c, gc = pair_c(b_c, p), pair_c(g_c, p)
        eg = jnp.exp(gc)
        dec = jnp.exp(pair_r(d_g, p)) * K['tril2']
        kkqk = _mm(jnp.concatenate([kt, qt], axis=0), stack(kt), _NT)
        yield
        n = (kkqk[:L] * dec * pair_r(b_r, p) * K['strict2']).astype(BF16)
        nbd = jnp.concatenate([n, n], axis=0)
        t = K['eyep'] - (nbd * K['lvlp'][0]).astype(F32)
        n_fine = 2 * L // FINE_ROWS
        tc = t[0:FINE_ROWS]
        for i in range(1, n_fine):
            tc = tc + t[i * FINE_ROWS:(i + 1) * FINE_ROWS]

        def expand(mc):
            return jnp.concatenate([mc] * n_fine, axis=0) * K['bdf'] if n_fine > 1 else mc

        for j in range(1, J):
            b = 1 << j
            if 2 * b <= FINE_ROWS:
                x = _mm(tc, nbd * K['lvlp'][j])
                yield
                tc = tc - _mm(x, expand(tc))
                yield
                if 4 * b > FINE_ROWS or j == J - 1:
                    t = expand(tc)
                continue
            starts = [h * L + s + b for h in range(2) for s in range(0, L, 2 * b)]
            x = _mm(jnp.concatenate([t[r:r + b] for r in starts], axis=0), nbd * K['lvlp'][j])
            yield
            upd = _mm(x, t)
            zero = jnp.zeros((b, 2 * L), F32)
            t = t - jnp.concatenate([blk for i in range(len(starts)) for blk in (zero, upd[i * b:(i + 1) * b])],
                                    axis=0)
            yield
        sol = _mm(t, jnp.concatenate([stack(vt * beta_c), stack(kt * (beta_c * eg))], axis=1))
        yield
        sol = sol[:L] + sol[L:]
        s_p = st['gdn'][p]
        u = sol[:, 0:LANE] - _mm(sol[:, LANE:2 * LANE], s_p)
        yield
        qo = jnp.concatenate([(qt * eg).astype(BF16), (kkqk[L:] * dec).astype(BF16)], axis=1)
        res['o_gdn'][p] = _mm(qo, jnp.concatenate([s_p.astype(BF16), stack(u)], axis=0))
        kw = kt * (jnp.exp(pair_c(rg_c, p)) * rowvalid)
        res['gdn'][p] = s_p * jnp.exp(gc[nv - 1:nv, :]) + bdp * _mm(kw, u, _TN)
        yield

    lg = P['lg']
    mx = lg[0:1, :]
    for i in range(1, DEPTH):
        mx = jnp.maximum(mx, lg[i:i + 1, :])
    ex = [jnp.exp(lg[i:i + 1, :] - mx) for i in range(DEPTH)]
    tot = ex[0]
    for i in range(1, DEPTH):
        tot = tot + ex[i]
    sm = [e / tot for e in ex]
    cum = sm[0]
    for i in range(1, layer + 1):
        cum = cum + sm[i]
    lb = cum - sm[0]
    lb_pos = lb > 0
    log_lb = jnp.log(jnp.where(lb_pos, lb, 1.0))
    hq = _silu(pm[:, 1024:1280])
    fz = pm[:, 1280:1536]
    hv = pm[:, 1536:1792]
    ls = _logsig(fz)
    t2 = jnp.log1p(-lb) + ls
    la = jnp.maximum(log_lb, t2) + jnp.log1p(jnp.exp(-jnp.abs(log_lb - t2)))
    logf = jnp.where(lb_pos, la, ls)
    kg = (1.0 - lb) * _sigmoid(-fz)
    ey = jnp.exp(_m01m(K['mhg'], logf))
    yield
    eg_h = ey[J * L:(J + 1) * L]
    eyb = ey[0:J * L].astype(BF16)
    hqb = hq.astype(BF16)
    qe = hq * eg_h
    kwr = kg * (ey[(J + 1) * L:(J + 2) * L] * rowvalid)

    def hg_task(p):
        q_p, k_st = pair_c(hqb, p), stack(pair_c(kg, p))
        a_p = K['hmaskp'][0] * _mm(q_p, k_st, _NT)
        yield
        for j in range(J):
            e_j = pair_c(eyb[j * L:(j + 1) * L], p)
            a_p = a_p + K['hmaskp'][j + 1] * _mm(q_p * e_j, k_st * jnp.concatenate([e_j, e_j], axis=0), _NT)
            yield
        st_p = st['hg'][p]
        res['o_hg'][p] = _mm(pair_c(qe, p), st_p, _NT) + _mm(a_p, stack(pair_c(hv, p)))
        yield
        res['hg'][p] = (st_p * pair_c(eg_h[nv - 1:nv, :], p)
                        + bdp * _mm(pair_c(hv, p), pair_c(kwr, p), _TN))
        yield

    xbc = _conv_silu(ext_s, P['cws'], P['cbs'], L)
    xs, bs, cc = xbc[:, 0:256], xbc[:, 256:384], xbc[:, 384:512]
    vs = xs * dt_c
    bsb = bs.astype(BF16)
    bs4 = jnp.concatenate([bsb] * 4, axis=0)

    def ssd_task():
        o_intra = []
        for g in range(SSD_G):
            cbw = _mm(cc * K['pm2'][g:g + 1, :], bs4, _NT)
            dec = jnp.exp(d_s[:, g * 4 * L:(g + 1) * 4 * L]) * K['tril4']
            vb = pair_c(vs, g).astype(BF16)
            o_intra.append(_mm(cbw * dec, jnp.concatenate([vb] * 4, axis=0) * K['stk4s']))
            yield
        s_all = st['ssd']
        res['o_ssd'] = jnp.exp(s_c) * _mm(cc, s_all) + jnp.concatenate(o_intra, axis=1)
        res['ssd'] = (s_all * jnp.exp(s_c[nv - 1:nv, :])
                      + K['gbd'] * _mm(bs * rowvalid, vs * jnp.exp(rs_c), _TN))
        yield

    mq = pm[:, 2816:3072] * (HEAD_D ** -0.5)
    mk = pm[:, 3072:3328]
    mv = pm[:, 3328:3584]

    def ml_task(p, delay):
        for _ in range(delay):
            yield
        qt, kt, vt = pair_c(mq, p), pair_c(mk, p), pair_c(mv, p)
        bm_r, bm_c = pair_r(m_r, p), pair_c(m_c, p)
        mp0 = st['mlm'][:, 2 * p:2 * p + 1]
        mp1 = st['mlm'][:, 2 * p + 1:2 * p + 2]
        logw = jnp.where(tril2, pair_r(d_m, p) + pair_r(row_i, p), NEG)
        mx0 = jnp.max(jnp.where(upper_r, NEG, logw), axis=-1, keepdims=True)
        mx1 = jnp.max(jnp.where(upper_r, logw, NEG), axis=-1, keepdims=True)
        l0_r = bm_r + jnp.where(upper_r, mp1, mp0)
        l0_c = bm_c + jnp.where(upper_c, mp1, mp0)
        mt_r = jnp.maximum(l0_r, jnp.where(upper_r, mx1, mx0))
        mt_c = jnp.maximum(l0_c, jnp.where(upper_c, mx1, mx0))
        w = jnp.exp(logw - mt_r)
        w0 = jnp.exp(l0_c - mt_c)
        qk = _mm(qt, stack(kt), _NT) * w
        yield
        c_p = st['mlc'][p]
        n_p = st['mln'][p]
        qn = jnp.concatenate([(w0 * qt).astype(BF16), qk.astype(BF16)], axis=1)
        num = _mm(qn, jnp.concatenate([c_p.astype(BF16), stack(vt)], axis=0))
        den = w0 * _mm01(qt * n_p, K['bdp2']) + _mm01(qk, K['stkp2'])
        res['o_ml'][p] = num * (1.0 / jnp.maximum(jnp.abs(den), jnp.exp(-mt_c)))
        yield
        m_l = mt_c[nv - 1:nv, :]
        wl0 = jnp.exp(bm_c[nv - 1:nv, :] + jnp.where(upper_c, mp1, mp0) - m_l)
        kwl = kt * (jnp.exp(pair_c(rm_c, p) + pair_c(i_c, p) - m_l) * rowvalid)
        res['mlc'][p] = c_p * wl0 + bdp * _mm(kwl, vt, _TN)
        res['mln'][p] = n_p * wl0 + jnp.sum(kwl, axis=0, keepdims=True)
        res['mlm'][2 * p] = m_l[:, 0:1]
        res['mlm'][2 * p + 1] = m_l[:, HEAD_D:HEAD_D + 1]
        yield

    def finish():
        o_gdn = jnp.concatenate(res['o_gdn'], axis=1)
        o_hg = jnp.concatenate(res['o_hg'], axis=1)
        hh = jnp.concatenate(res['o_ml'], axis=1)
        ms = _mm(jnp.concatenate([o_gdn * o_gdn, o_hg * o_hg, hh * hh], axis=0), K['bd64']) * (1.0 / HEAD_D)
        yield
        out_a = o_gdn * lax.rsqrt(ms[0:L] + EPS) * P['vec'][0:1, :] * _silu(pm[:, 768:1024])
        out_b = o_hg * lax.rsqrt(ms[L:2 * L] + EPS) * P['vec'][1:2, :] * _silu(pm[:, 1792:2048])
        out_d = hh * lax.rsqrt(ms[2 * L:3 * L] + EPS) * P['vec'][3:4, :] * _sigmoid(pm[:, 3584:3840])
        ys = (res['o_ssd'] + P['vec'][4:5, :] * xs) * _silu(pm[:, 2048:2304])
        out_c = ys * lax.rsqrt(_mm(ys * ys, K['bd128']) * (1.0 / (2 * HEAD_D)) + EPS) * P['vec'][2:3, :]
        branches = jnp.concatenate([out_a, out_b, out_c, out_d], axis=1)
        new = dict(gdn=res['gdn'], hg=res['hg'], ssd=res['ssd'], mlc=res['mlc'], mln=res['mln'], mlm=res['mlm'])
        out['result'] = (branches, new)
        yield

    out['tasks'] = ([gdn_task(p) for p in range(2)] + [hg_task(p) for p in range(2)]
                    + [ssd_task()] + [ml_task(p, 3 + 2 * p) for p in range(2)])
    out['finish'] = finish


def _mixer_kernel(L, nv, J, layer, bb, offs, has_init, n_alias, *refs):
    it = iter(refs)
    pm_ref = next(it)
    if has_init:
        gdn0, cg0, hg0, ssd0, cs0, mc0, mn0, mm0 = (next(it) for _ in range(8))
    sp_ref, cwg_ref, cbg_ref, cws_ref, cbs_ref, vec_ref, lg_ref = (next(it) for _ in range(7))
    kref = {name: next(it) for name in _CONST_ORDER}
    for _ in range(n_alias):
        next(it)
    br_ref = next(it)
    gdn1, cg1, hg1, ssd1, cs1, mc1, mn1, mm1 = (next(it) for _ in range(8))
    sg, sh, sc, sn, ss, extg, exts, pmpad = (next(it) for _ in range(8))

    c = pl.program_id(1)
    nc = pl.num_programs(1)
    n_sub = SSD_H // SSD_G

    @pl.when(c == 0)
    def _init():
        sg[...] = jnp.zeros_like(sg)
        sh[...] = jnp.zeros_like(sh)
        sc[...] = jnp.zeros_like(sc)
        ss[...] = jnp.zeros_like(ss)
        extg[:, 0:8, :] = jnp.zeros((bb, 8, GDN_CONV_W), F32)
        exts[:, 0:8, :] = jnp.zeros((bb, 8, SSD_CONV_W), F32)
        if not has_init:
            sn[...] = jnp.zeros_like(sn)
            mm1[...] = jnp.zeros_like(mm1)
            return
        for s in range(bb):
            for h in range(N_HEAD):
                p, lo = h // 2, (h % 2) * HEAD_D
                sg[s, p, lo:lo + HEAD_D, lo:lo + HEAD_D] = gdn0[s, h]
                sh[s, p, lo:lo + HEAD_D, lo:lo + HEAD_D] = hg0[s, h].T
                sc[s, p, lo:lo + HEAD_D, lo:lo + HEAD_D] = mc0[s, h]
                sn[s, p, :, lo:lo + HEAD_D] = mn0[s, h:h + 1, :]
            for h in range(SSD_H):
                g = h // n_sub
                ss[s, g * SSD_N:(g + 1) * SSD_N, h * SSD_P:(h + 1) * SSD_P] = ssd0[s, h]
        mm1[...] = mm0[...]
        extg[:, 5:8, :] = cg0[...]
        exts[:, 5:8, :] = cs0[...]

    P = dict(sp=sp_ref[...], cwg=cwg_ref[...], cbg=cbg_ref[...], cws=cws_ref[...], cbs=cbs_ref[...],
             vec=vec_ref[...], lg=lg_ref[...])
    K = {name: r[...] for name, r in kref.items()}
    stage1, outs = [], []
    spg = L // nv
    for s in range(bb):
        grp, off = s // spg, (s % spg) * nv
        if nv == L:
            pm = pm_ref[s]
        else:
            pmpad[s, 0:nv, :] = pm_ref[grp, off:off + nv, :]
            pmpad[s, nv:L, :] = jnp.zeros((L - nv, MIX_W), F32)
            pm = pmpad[s]
        extg[s, 8:8 + L, :] = pm[:, 0:GDN_CONV_W]
        exts[s, 8:8 + L, :] = pm[:, SSD_XBC_OFF:SSD_XBC_OFF + SSD_CONV_W]
        st = dict(gdn=sg[s], hg=sh[s], ssd=ss[s], mlc=sc[s], mln=sn[s], mlm=mm1[s])
        outs.append({})
        stage1.append(_mixer_chunk(pm, extg.at[s], exts.at[s], st, P, K, offs, L, nv, J, layer, outs[s]))

    def flow(s):
        for _ in range(FLOW_DELAY[L] * s):
            yield
        yield from stage1[s]
        tasks = list(outs[s]['tasks'])
        while tasks:
            alive = []
            for t in tasks:
                try:
                    next(t)
                    alive.append(t)
                except StopIteration:
                    pass
            tasks = alive
            yield
        yield from outs[s]['finish']()

    _run_interleaved([flow(s) for s in range(bb)])
    for s in range(bb):
        branches, new = outs[s]['result']
        grp, off = s // spg, (s % spg) * nv
        br_ref[grp, off:off + nv, :] = branches[0:nv].astype(br_ref.dtype)
        for p in range(2):
            sg[s, p] = new['gdn'][p]
            sh[s, p] = new['hg'][p]
            sc[s, p] = new['mlc'][p]
            sn[s, p] = new['mln'][p]
        for h in range(N_HEAD):
            mm1[s, :, h:h + 1] = new['mlm'][h]
        ss[s] = new['ssd']
        tail_g = extg[s, 8 + nv - 3:8 + nv, :]
        tail_s = exts[s, 8 + nv - 3:8 + nv, :]
        extg[s, 5:8, :] = tail_g
        exts[s, 5:8, :] = tail_s
        cg1[s] = tail_g
        cs1[s] = tail_s

    @pl.when(c == nc - 1)
    def _fin():
        for s in range(bb):
            for h in range(N_HEAD):
                p, lo = h // 2, (h % 2) * HEAD_D
                gdn1[s, h] = sg[s, p, lo:lo + HEAD_D, lo:lo + HEAD_D]
                hg1[s, h] = sh[s, p, lo:lo + HEAD_D, lo:lo + HEAD_D].T
                mc1[s, h] = sc[s, p, lo:lo + HEAD_D, lo:lo + HEAD_D]
                mn1[s, h:h + 1, :] = sn[s, p, :, lo:lo + HEAD_D]
            for h in range(SSD_H):
                g = h // n_sub
                ssd1[s, h] = ss[s, g * SSD_N:(g + 1) * SSD_N, h * SSD_P:(h + 1) * SSD_P]


def _full_spec(a):
    nd = a.ndim
    return pl.BlockSpec(a.shape, lambda b, c, _nd=nd: (0,) * _nd)


def _mixer_call(proj, state_shapes, states_in, prev_out, params, L, nv, layer, bb):
    g, t, _ = proj.shape
    spg = L // nv if nv < L else 1
    nchunk = t // L
    assert (g * spg) % bb == 0 and bb % spg == 0 and t % L == 0
    consts, offs, J = _mixer_consts(L, nv)
    const_arrays = [jnp.asarray(consts[n], BF16 if n in _BF16_CONSTS else F32) for n in _CONST_ORDER]

    def st_spec(shape):
        nd = len(shape)
        return pl.BlockSpec((None, bb) + tuple(shape[2:]), lambda b, c, _nd=nd: (layer, b) + (0,) * (_nd - 2))

    has_init = states_in is not None
    n_alias = 0 if prev_out is None else len(prev_out)
    inputs = [proj] + (list(states_in) if has_init else []) + list(params) + const_arrays + list(prev_out or [])
    in_specs = ([pl.BlockSpec((bb // spg, L, MIX_W), lambda b, c: (b, c, 0))]
                + ([st_spec(s) for s in state_shapes] if has_init else [])
                + [_full_spec(a) for a in params]
                + [_full_spec(a) for a in const_arrays]
                + [pl.BlockSpec(memory_space=pl.ANY)] * n_alias)
    out_shape = ([jax.ShapeDtypeStruct((g, t, N_BRANCH * BRANCH_W), _row_dtype(L))]
                 + [jax.ShapeDtypeStruct(s, F32) for s in state_shapes])
    out_specs = ([pl.BlockSpec((bb // spg, L, N_BRANCH * BRANCH_W), lambda b, c: (b, c, 0))]
                 + [st_spec(s) for s in state_shapes])
    first_alias = len(inputs) - n_alias
    pair = (bb, 2, LANE, LANE)
    scratch = [pltpu.VMEM(pair, F32), pltpu.VMEM(pair, F32), pltpu.VMEM(pair, F32),
               pltpu.VMEM((bb, 2, 1, LANE), F32), pltpu.VMEM((bb, SSD_G * SSD_N, SSD_H * SSD_P), F32),
               pltpu.VMEM((bb, 8 + L, GDN_CONV_W), F32), pltpu.VMEM((bb, 8 + L, SSD_CONV_W), F32),
               pltpu.VMEM((bb, L, MIX_W) if nv < L else (1, 8, LANE), F32)]
    outs = pl.pallas_call(
        functools.partial(_mixer_kernel, L, nv, J, layer, bb, offs, has_init, n_alias),
        grid=(g * spg // bb, nchunk),
        in_specs=in_specs, out_specs=out_specs, out_shape=out_shape, scratch_shapes=scratch,
        input_output_aliases={first_alias + k: 1 + k for k in range(n_alias)},
        compiler_params=pltpu.CompilerParams(dimension_semantics=("parallel", "arbitrary"),
                                             vmem_limit_bytes=VMEM_LIMIT),
        name=f"mixer_L{L}",
    )(*inputs)
    return outs[0], list(outs[1:])


def _ada_kernel(c_ref, w_ref, b_ref, o_ref):
    o_ref[...] = _mm(_silu(c_ref[...]), w_ref[...]) + b_ref[...]


def _ada_call(c_all, ada_w, ada_b):
    rows = c_all.shape[0]
    n = ada_w.shape[-1]
    tn = 3072
    return pl.pallas_call(
        _ada_kernel,
        grid=(DEPTH, n // tn),
        in_specs=[pl.BlockSpec((rows, D_MODEL), lambda l, j: (0, 0)),
                  pl.BlockSpec((None, D_MODEL, tn), lambda l, j: (l, 0, j)),
                  pl.BlockSpec((None, 1, tn), lambda l, j: (l, 0, j))],
        out_specs=pl.BlockSpec((None, rows, tn), lambda l, j: (l, 0, j)),
        out_shape=jax.ShapeDtypeStruct((DEPTH, rows, n), F32),
        compiler_params=pltpu.CompilerParams(dimension_semantics=("arbitrary", "arbitrary"),
                                             vmem_limit_bytes=VMEM_LIMIT),
        name="ada",
    )(c_all, ada_w, ada_b.reshape(DEPTH, 1, n))


PREP_ROWS = MIX_W - SMALL_OFF


def _prep_w_in_kernel(w_ref, s0_ref, s1_ref, s2_ref, o_ref):
    j = pl.program_id(1)
    j_small = SMALL_OFF // PREP_ROWS

    @pl.when(j != j_small)
    def _copy():
        o_ref[...] = w_ref[0].astype(BF16)

    @pl.when(j == j_small)
    def _small():
        o_ref[...] = jnp.zeros(o_ref.shape, BF16)
        o_ref[0:8, :] = s0_ref[0].astype(BF16)
        o_ref[8:16, :] = s1_ref[0].astype(BF16)
        o_ref[16:24, :] = s2_ref[0].astype(BF16)


def _prep_src_row(j):
    r = PREP_ROWS
    return jnp.where(j < 1024 // r, r * j,
                     jnp.where(j < 2816 // r, r * j + 8,
                               jnp.where(j < SMALL_OFF // r, r * j + 16,
                                         jnp.where(j < MIX_W // r, 0, r * j - MIX_W + 3864))))


def _prep_w_in_call(w_in):
    wt = jnp.swapaxes(w_in, 1, 2)
    depth, _, d = wt.shape

    def rows(n, index):
        return pl.BlockSpec((pl.Element(1), pl.Element(n), pl.Element(d)), index)

    return pl.pallas_call(
        _prep_w_in_kernel,
        grid=(depth, PROJ_W // PREP_ROWS),
        in_specs=[rows(PREP_ROWS, lambda l, j: (l, pl.multiple_of(_prep_src_row(j), 8), 0)),
                  rows(8, lambda l, j: (l, 1024, 0)), rows(8, lambda l, j: (l, 2824, 0)),
                  rows(8, lambda l, j: (l, 3856, 0))],
        out_specs=pl.BlockSpec((None, PREP_ROWS, d), lambda l, j: (l, j, 0)),
        out_shape=jax.ShapeDtypeStruct((depth, PROJ_W, d), BF16),
        compiler_params=pltpu.CompilerParams(dimension_semantics=("arbitrary", "arbitrary"),
                                             vmem_limit_bytes=VMEM_LIMIT),
        name="prep_w_in",
    )(wt, wt, wt, wt)


def _rms_mod(x, nw, sc, sh):
    ms = jnp.mean(x * x, axis=-1, keepdims=True)
    return x * lax.rsqrt(ms + EPS) * nw * (1.0 + sc) + sh


def _tok_spec(gblk, width):
    g, r = gblk
    return pl.BlockSpec((g, r, width), lambda i, j: (i, j, 0))


def _mod_spec(gblk, layer, k, mods):
    return pl.BlockSpec((None, gblk[0], mods.shape[2], D_MODEL), lambda i, j: (layer, i, 0, k))


def _mod_rows(m, r):
    spg = m.shape[1]
    if spg == 1:
        return m
    seq = lax.broadcasted_iota(jnp.int32, (1, r, 1), 1) // (r // spg)
    out = m[:, 0:1, :]
    for j in range(1, spg):
        out = jnp.where(seq == j, m[:, j:j + 1, :], out)
    return out


def _layer_spec(shape, layer):
    nd = len(shape)
    return pl.BlockSpec((None,) + tuple(shape[1:]), lambda i, j: (layer,) + (0,) * (nd - 1),
                        pipeline_mode=pl.Buffered(1))


def _dense_params():
    return pltpu.CompilerParams(dimension_semantics=("arbitrary", "arbitrary"), vmem_limit_bytes=VMEM_LIMIT)


def _inproj_kernel(x_ref, nw_ref, sc_ref, sh_ref, w_ref, pm_ref, gate_ref):
    g, r, _ = x_ref.shape
    h = _rms_mod(x_ref[...], nw_ref[...], _mod_rows(sc_ref[...], r), _mod_rows(sh_ref[...], r))
    h = h.astype(BF16).reshape(g * r, D_MODEL)
    pm_ref[...] = _mm(h, w_ref[0:MIX_W, :], _NT).reshape(g, r, MIX_W)
    gates = _sigmoid(_mm(h, w_ref[MIX_W:PROJ_W, :], _NT))
    gate_ref[...] = gates.astype(gate_ref.dtype).reshape(g, r, GATE_W)


def _inproj_call(x, layer, nw, mods, w, gblk):
    bg, t, _ = x.shape
    return pl.pallas_call(
        _inproj_kernel,
        grid=(bg // gblk[0], t // gblk[1]),
        in_specs=[_tok_spec(gblk, D_MODEL), _layer_spec(nw.shape, layer),
                  _mod_spec(gblk, layer, 1, mods), _mod_spec(gblk, layer, 0, mods), _layer_spec(w.shape, layer)],
        out_specs=[_tok_spec(gblk, MIX_W), _tok_spec(gblk, GATE_W)],
        out_shape=[jax.ShapeDtypeStruct((bg, t, MIX_W), F32),
                   jax.ShapeDtypeStruct((bg, t, GATE_W), _row_dtype(gblk[1]))],
        compiler_params=_dense_params(), name="inproj",
    )(x, nw, mods, mods, w)


def _merge_ffn_kernel(final, br_ref, gate_ref, x_ref, gt1_ref, nw_ref, sc_ref, sh_ref, gt2_ref,
                      wb_ref, wo_ref, w1_ref, w2_ref, fw_ref, o_ref):
    g, r, _ = x_ref.shape
    br = br_ref[...].reshape(g * r, N_BRANCH * BRANCH_W)
    merged = None
    for n in range(N_BRANCH):
        up = _mm(br[:, n * BRANCH_W:(n + 1) * BRANCH_W], wb_ref[n])
        gate = gate_ref[:, :, n * D_MODEL:(n + 1) * D_MODEL].astype(F32).reshape(g * r, D_MODEL)
        t = gate * up
        merged = t if merged is None else merged + t
    x1 = x_ref[...] + _mod_rows(gt1_ref[...], r) * _mm(merged, wo_ref[...]).reshape(g, r, D_MODEL)
    h = _rms_mod(x1, nw_ref[...], _mod_rows(sc_ref[...], r), _mod_rows(sh_ref[...], r))
    h = h.astype(BF16).reshape(g * r, D_MODEL)
    a = _mm(h, w1_ref[:, 0:D_FF])
    b = _mm(h, w1_ref[:, D_FF:2 * D_FF])
    x2 = x1 + _mod_rows(gt2_ref[...], r) * _mm(_silu(a) * b, w2_ref[...]).reshape(g, r, D_MODEL)
    if final:
        ms = jnp.mean(x2 * x2, axis=-1, keepdims=True)
        x2 = x2 * lax.rsqrt(ms + EPS) * fw_ref[...]
    o_ref[...] = x2


def _merge_ffn_call(br, gates, x, layer, nw, mods, wb, wo, w1, w2, fw, gblk, final):
    bg, t, _ = x.shape
    return pl.pallas_call(
        functools.partial(_merge_ffn_kernel, final),
        grid=(bg // gblk[0], t // gblk[1]),
        in_specs=[_tok_spec(gblk, N_BRANCH * BRANCH_W), _tok_spec(gblk, GATE_W), _tok_spec(gblk, D_MODEL),
                  _mod_spec(gblk, layer, 2, mods), _layer_spec(nw.shape, layer),
                  _mod_spec(gblk, layer, 4, mods), _mod_spec(gblk, layer, 3, mods), _mod_spec(gblk, layer, 5, mods),
                  _layer_spec(wb.shape, layer), _layer_spec(wo.shape, layer),
                  _layer_spec(w1.shape, layer), _layer_spec(w2.shape, layer),
                  pl.BlockSpec(fw.shape, lambda i, j: (0, 0), pipeline_mode=pl.Buffered(1))],
        out_specs=_tok_spec(gblk, D_MODEL),
        out_shape=jax.ShapeDtypeStruct((bg, t, D_MODEL), F32),
        compiler_params=_dense_params(), name="merge_ffn",
    )(br, gates, x, mods, nw, mods, mods, mods, wb, wo, w1, w2, fw)


def _small_params(gdn_dt_bias, ssd_dt_bias, ml_b_i, ml_b_f, gdn_a_log, ssd_a_log):
    z4 = jnp.zeros((4,), F32)
    bias = jnp.concatenate([z4, gdn_dt_bias, ssd_dt_bias, ml_b_i, ml_b_f, jnp.zeros((LANE - 24,), F32)])
    alog = jnp.concatenate([z4, gdn_a_log, ssd_a_log, jnp.zeros((LANE - 16,), F32)])
    return jnp.concatenate([bias[None], alog[None], jnp.zeros((6, LANE), F32)], axis=0)


def _layer_params(l, gdn_conv_w, gdn_conv_b, gdn_a_log, gdn_dt_bias, gdn_norm_w, hg_lb_logits, hg_norm_w,
                  ssd_conv_w, ssd_conv_b, ssd_a_log, ssd_dt_bias, ssd_d, ssd_norm_w, ml_b_i, ml_b_f, ml_norm_w):
    sp = _small_params(gdn_dt_bias[l], ssd_dt_bias[l], ml_b_i[l], ml_b_f[l], gdn_a_log[l], ssd_a_log[l])
    vec = jnp.stack([jnp.tile(gdn_norm_w[l], N_HEAD), jnp.tile(hg_norm_w[l], N_HEAD), ssd_norm_w[l],
                     jnp.tile(ml_norm_w[l], N_HEAD), jnp.repeat(ssd_d[l], SSD_P)]
                    + [jnp.zeros((256,), F32)] * 3)
    return [sp, gdn_conv_w[l], gdn_conv_b[l][None], ssd_conv_w[l], ssd_conv_b[l][None], vec,
            hg_lb_logits.astype(F32)]


def _trunk(x, mods, state_shapes, states_in, L, nv, gblk, gblk_ffn, bb, W, mixer_params):
    new_states = None
    for l in range(DEPTH):
        pm, gates = _inproj_call(x, l, W['norm1'], mods, W['w_in'], gblk)
        br, new_states = _mixer_call(pm, state_shapes, states_in, new_states, mixer_params[l], L, nv, l, bb)
        x = _merge_ffn_call(br, gates, x, l, W['norm2'], mods, W['w_branch'], W['w_out'],
                            W['ffn_w_in'], W['ffn_w_out'], W['final'], gblk_ffn, final=(l == DEPTH - 1))
    return x, new_states


def kernel(x_prompt, x_sample, c_prompt, c_sample, state_gdn, state_gdn_conv, state_hgrn, state_ssd, state_ssd_conv, state_mlstm_c, state_mlstm_n, state_mlstm_m, ada_w, ada_b, norm1_w, norm2_w, w_in, gdn_conv_w, gdn_conv_b, gdn_a_log, gdn_dt_bias, gdn_norm_w, hg_lb_logits, hg_norm_w, ssd_conv_w, ssd_conv_b, ssd_a_log, ssd_dt_bias, ssd_d, ssd_norm_w, ml_b_i, ml_b_f, ml_norm_w, w_branch, w_out, ffn_w_in, ffn_w_out, final_norm_w):
    bp, tp, _ = x_prompt.shape
    bs, ts, _ = x_sample.shape
    assert tp % CHUNK == 0 and LS % ts == 0 and bs % (LS // ts) == 0

    W = dict(w_in=_prep_w_in_call(w_in), w_branch=w_branch.astype(BF16), w_out=w_out.astype(BF16),
             ffn_w_in=ffn_w_in.astype(BF16), ffn_w_out=ffn_w_out.astype(BF16),
             norm1=norm1_w[:, None, :], norm2=norm2_w[:, None, :], final=final_norm_w[None])
    mixer_params = [_layer_params(l, gdn_conv_w, gdn_conv_b, gdn_a_log, gdn_dt_bias, gdn_norm_w, hg_lb_logits,
                                  hg_norm_w, ssd_conv_w, ssd_conv_b, ssd_a_log, ssd_dt_bias, ssd_d, ssd_norm_w,
                                  ml_b_i, ml_b_f, ml_norm_w) for l in range(DEPTH)]

    mods = _ada_call(jnp.concatenate([c_prompt, c_sample], axis=0), ada_w, ada_b)
    mods_p = mods[:, :bp].reshape(DEPTH, bp, 1, 6 * D_MODEL)
    grp = LS // ts
    mods_s = mods[:, bp:].reshape(DEPTH, bs // grp, grp, 6 * D_MODEL)
    sample_states = [state_gdn, state_gdn_conv, state_hgrn, state_ssd, state_ssd_conv,
                     state_mlstm_c, state_mlstm_n, state_mlstm_m.reshape(DEPTH, bs, 1, N_HEAD)]
    shapes_s = [s.shape for s in sample_states]
    shapes_p = [(DEPTH, bp) + tuple(s[2:]) for s in shapes_s]

    y_p, new_p = _trunk(x_prompt, mods_p, shapes_p, None, CHUNK, CHUNK, (1, 256), (1, 512), 8, W, mixer_params)
    y_s, new_s = _trunk(x_sample.reshape(bs // grp, LS, D_MODEL), mods_s, shapes_s, sample_states, LS, ts,
                        (64, LS), (64, LS), 8, W, mixer_params)
    new_p[7] = new_p[7].reshape(DEPTH, bp, N_HEAD)
    new_s[7] = new_s[7].reshape(DEPTH, bs, N_HEAD)
    return (y_p, y_s.reshape(bs, ts, D_MODEL)) + tuple(new_p) + tuple(new_s)
```
